```python
import math
import jax, jax.numpy as jnp
from jax import lax
import numpy as np

D_MODEL = 1024
BATCH = 8
SEQ = 2048
DEPTH = 2
DEC_BATCH = 128
DEC_SEQ = 4
PAST_LEN = 16384
PAGE_SIZE = 128

MIX_W = D_MODEL
RET_W = (3 * MIX_W) // 8
S5_W = MIX_W // 4
GLA_W = MIX_W - RET_W - S5_W
RET_HEAD_DIM = 64
RET_HEADS = RET_W // RET_HEAD_DIM
ROPE_BASE = 10000.0
S5_CH_PER_GROUP = 16
S5_GROUPS = S5_W // S5_CH_PER_GROUP
S5_STATE = 64
S5_MIN_NEG = 1e-4
GLA_HEADS = 4
GLA_KEY_W = GLA_W // 2
GLA_DK = GLA_KEY_W // GLA_HEADS
GLA_DV = GLA_W // GLA_HEADS
GLA_RANK = 16
GLA_GATE_TEMP = 16.0
IN_COLS = 4 * RET_W + S5_W + 2 * GLA_KEY_W + 2 * GLA_W + GLA_RANK
D_FF = ((8 * D_MODEL // 3 + 127) // 128) * 128
CONV_W = 3
CHUNK = 64
EPS = 1e-6

kernel_name = 'hymba_style_retention_s5_gla_convffn_step'


def _in_split_points():
    sizes = [RET_W, RET_W, RET_W, RET_W, S5_W, GLA_KEY_W, GLA_KEY_W, GLA_W, GLA_W, GLA_RANK]
    pts, acc = [], 0
    for s in sizes[:-1]:
        acc += s
        pts.append(acc)
    return pts


def _rmsnorm(x, g):
    xf = x.astype(jnp.float32)
    y = xf * lax.rsqrt(jnp.mean(xf * xf, axis=-1, keepdims=True) + EPS)
    return (y * g.astype(jnp.float32)).astype(x.dtype)


def _rotary(x, pos):
    half = x.shape[-1] // 2
    inv = ROPE_BASE ** (-jnp.arange(half, dtype=jnp.float32) / half)
    ang = pos.astype(jnp.float32)[:, None] * inv[None, :]
    cos = jnp.cos(ang)[None, :, None, :]
    sin = jnp.sin(ang)[None, :, None, :]
    x1, x2 = x[..., :half], x[..., half:]
    return jnp.concatenate([x1 * cos - x2 * sin, x1 * sin + x2 * cos], axis=-1)


def _retention_chunked(q, k, v, s0):
    bsz, h, L, dk = q.shape
    dv = v.shape[-1]
    c = min(CHUNK, L)
    n = L // c
    log_gamma = jnp.log(1.0 - 2.0 ** (-5.0 - jnp.arange(h, dtype=jnp.float32)))
    q = q.reshape(bsz, h, n, c, dk)
    k = k.reshape(bsz, h, n, c, dk)
    v = v.reshape(bsz, h, n, c, dv)
    i = jnp.arange(c, dtype=jnp.float32)
    diff = i[:, None] - i[None, :]
    decay_mask = jnp.where(diff >= 0, jnp.exp(log_gamma[:, None, None] * jnp.maximum(diff, 0.0)), 0.0)
    scores = jnp.einsum('bhnid,bhnjd->bhnij', q, k) * decay_mask[None, :, None]
    o_intra = jnp.einsum('bhnij,bhnjv->bhniv', scores, v)
    k_to_end = k * jnp.exp(log_gamma[:, None] * (c - 1.0 - i)[None, :])[None, :, None, :, None]
    kv = jnp.einsum('bhnjd,bhnjv->bhndv', k_to_end, v)
    chunk_decay = jnp.exp(log_gamma * c)[None, :, None, None]

    def step(s, kv_c):
        return chunk_decay * s + kv_c, s

    s_last, s_prev = lax.scan(step, s0, jnp.moveaxis(kv, 2, 0))
    s_prev = jnp.moveaxis(s_prev, 0, 2)
    q_from_start = q * jnp.exp(log_gamma[:, None] * (i + 1.0)[None, :])[None, :, None, :, None]
    o_cross = jnp.einsum('bhnid,bhndv->bhniv', q_from_start, s_prev)
    return (o_intra + o_cross).reshape(bsz, h, L, dv), s_last


def _gla_chunked(q, k, v, log_g, s0):
    bsz, h, L, dk = q.shape
    dv = v.shape[-1]
    c = min(CHUNK, L)
    n = L // c
    q = q.reshape(bsz, h, n, c, dk)
    k = k.reshape(bsz, h, n, c, dk)
    log_g = log_g.reshape(bsz, h, n, c, dk)
    v = v.reshape(bsz, h, n, c, dv)
    b_cum = jnp.cumsum(log_g, axis=3)
    b_end = b_cum[:, :, :, -1:, :]
    q_in = q * jnp.exp(b_cum)
    k_in = k * jnp.exp(-b_cum)
    causal = jnp.tril(jnp.ones((c, c), dtype=bool))
    scores = jnp.where(causal, jnp.einsum('bhnid,bhnjd->bhnij', q_in, k_in), 0.0)
    o_intra = jnp.einsum('bhnij,bhnjv->bhniv', scores, v)
    kv = jnp.einsum('bhnjd,bhnjv->bhndv', k * jnp.exp(b_end - b_cum), v)
    decay = jnp.exp(b_end[:, :, :, 0, :])

    def step(s, inp):
        d_c, kv_c = inp
        return d_c[..., None] * s + kv_c, s

    s_last, s_prev = lax.scan(step, s0, (jnp.moveaxis(decay, 2, 0), jnp.moveaxis(kv, 2, 0)))
    s_prev = jnp.moveaxis(s_prev, 0, 2)
    o_cross = jnp.einsum('bhnid,bhndv->bhniv', q_in, s_prev)
    return (o_intra + o_cross).reshape(bsz, h, L, dv), s_last


def _s5_scan(u, h0_re, h0_im, lam_re, lam_im, log_dt, b_re, b_im, c_re, c_im, d):
    f32 = jnp.float32
    bsz, L, _ = u.shape
    ug = u.reshape(bsz, L, S5_GROUPS, S5_CH_PER_GROUP)
    lr = jnp.minimum(lam_re.astype(f32), -S5_MIN_NEG)
    li = lam_im.astype(f32)
    dt = jnp.exp(log_dt.astype(f32))[:, None]
    mag = jnp.exp(lr * dt)
    ar = mag * jnp.cos(li * dt)
    ai = mag * jnp.sin(li * dt)
    den = lr * lr + li * li
    cr = ((ar - 1.0) * lr + ai * li) / den
    ci = (ai * lr - (ar - 1.0) * li) / den
    b_re = b_re.astype(f32)
    b_im = b_im.astype(f32)
    bbar_re = cr[..., None] * b_re - ci[..., None] * b_im
    bbar_im = cr[..., None] * b_im + ci[..., None] * b_re
    xr = jnp.einsum('blgh,gph->blgp', ug, bbar_re)
    xi = jnp.einsum('blgh,gph->blgp', ug, bbar_im)
    ar_b = jnp.broadcast_to(ar, xr.shape)
    ai_b = jnp.broadcast_to(ai, xr.shape)

    def combine(e1, e2):
        a1r, a1i, b1r, b1i = e1
        a2r, a2i, b2r, b2i = e2
        return (a2r * a1r - a2i * a1i,
                a2r * a1i + a2i * a1r,
                a2r * b1r - a2i * b1i + b2r,
                a2r * b1i + a2i * b1r + b2i)

    _, _, hr, hi = lax.associative_scan(combine, (ar_b, ai_b, xr, xi), axis=1)
    t1 = jnp.arange(1, L + 1, dtype=f32)[:, None, None]
    pmag = jnp.exp(lr * dt * t1)
    pph = li * dt * t1
    pr = pmag * jnp.cos(pph)
    pi = pmag * jnp.sin(pph)
    h0r = h0_re.astype(f32)[:, None]
    h0i = h0_im.astype(f32)[:, None]
    hr = hr + pr * h0r - pi * h0i
    hi = hi + pr * h0i + pi * h0r
    y = jnp.einsum('gcp,blgp->blgc', c_re.astype(f32), hr) - jnp.einsum('gcp,blgp->blgc', c_im.astype(f32), hi)
    y = y.reshape(bsz, L, S5_W) + d.astype(f32) * u
    return y, hr[:, -1], hi[:, -1]


def _layer(x, pos, s_ret, s5r, s5i, s_gla, conv_buf,
           norm_mix_g, w_in, ret_norm_g, ret_norm_b,
           s5_lambda_re, s5_lambda_im, s5_log_dt, s5_b_re, s5_b_im, s5_c_re, s5_c_im, s5_d,
           s5_glu_w, s5_glu_b, gla_gate_w, gla_gate_b, gla_norm_g, w_out,
           norm_ffn_g, ffn_w_in, ffn_conv_w, ffn_conv_b, ffn_w_out):
    f32 = jnp.float32
    bsz, L, _ = x.shape
    h = _rmsnorm(x, norm_mix_g)
    proj = (h @ w_in).astype(f32)
    rq, rk, rv, rg, su, gq, gk, gv, gg, glr = jnp.split(proj, _in_split_points(), axis=-1)
    rq = _rotary(rq.reshape(bsz, L, RET_HEADS, RET_HEAD_DIM), pos)
    rk = _rotary(rk.reshape(bsz, L, RET_HEADS, RET_HEAD_DIM), pos) * (RET_HEAD_DIM ** -0.5)
    rv = rv.reshape(bsz, L, RET_HEADS, RET_HEAD_DIM)
    o_ret, s_ret_new = _retention_chunked(rq.transpose(0, 2, 1, 3), rk.transpose(0, 2, 1, 3),
                                          rv.transpose(0, 2, 1, 3), s_ret.astype(f32))
    o_ret = o_ret.transpose(0, 2, 1, 3)
    mu = jnp.mean(o_ret, axis=-1, keepdims=True)
    var = jnp.mean(jnp.square(o_ret - mu), axis=-1, keepdims=True)
    o_ret = ((o_ret - mu) * lax.rsqrt(var + EPS)).reshape(bsz, L, RET_W)
    o_ret = (o_ret * ret_norm_g.astype(f32) + ret_norm_b.astype(f32)) * jax.nn.silu(rg)
    y5, s5r_new, s5i_new = _s5_scan(su, s5r, s5i, s5_lambda_re, s5_lambda_im, s5_log_dt,
                                    s5_b_re, s5_b_im, s5_c_re, s5_c_im, s5_d)
    y5 = jax.nn.gelu(y5)
    o_s5 = y5 * jax.nn.sigmoid(y5 @ s5_glu_w.astype(f32) + s5_glu_b.astype(f32))
    log_g = jax.nn.log_sigmoid(glr @ gla_gate_w.astype(f32) + gla_gate_b.astype(f32)) / GLA_GATE_TEMP
    gq = gq.reshape(bsz, L, GLA_HEADS, GLA_DK).transpose(0, 2, 1, 3) * (GLA_DK ** -0.5)
    gk = gk.reshape(bsz, L, GLA_HEADS, GLA_DK).transpose(0, 2, 1, 3)
    gv = gv.reshape(bsz, L, GLA_HEADS, GLA_DV).transpose(0, 2, 1, 3)
    log_g = log_g.reshape(bsz, L, GLA_HEADS, GLA_DK).transpose(0, 2, 1, 3)
    o_gla, s_gla_new = _gla_chunked(gq, gk, gv, log_g, s_gla.astype(f32))
    o_gla = o_gla.transpose(0, 2, 1, 3)
    o_gla = o_gla * lax.rsqrt(jnp.mean(o_gla * o_gla, axis=-1, keepdims=True) + EPS) * gla_norm_g.astype(f32)
    o_gla = o_gla.reshape(bsz, L, GLA_W) * jax.nn.silu(gg)
    mix = jnp.concatenate([o_ret, o_s5, o_gla], axis=-1).astype(x.dtype)
    x = x + mix @ w_out
    h = _rmsnorm(x, norm_ffn_g)
    up = h @ ffn_w_in
    a, gate = jnp.split(up, 2, axis=-1)
    padded = jnp.concatenate([conv_buf.astype(a.dtype), a], axis=1)
    conv = ffn_conv_b + padded[:, 0:L] * ffn_conv_w[0]
    for t in range(1, CONV_W):
        conv = conv + padded[:, t:t + L] * ffn_conv_w[t]
    conv_new = padded[:, L:]
    x = x + (jax.nn.gelu(conv) * gate) @ ffn_w_out
    return x, s_ret_new, s5r_new, s5i_new, s_gla_new, conv_new


def _run_trunk(x, pos, s_ret, s5r, s5i, s_gla, conv, norm_mix_g, w_in, ret_norm_g, ret_norm_b,
               s5_lambda_re, s5_lambda_im, s5_log_dt, s5_b_re, s5_b_im, s5_c_re, s5_c_im, s5_d,
               s5_glu_w, s5_glu_b, gla_gate_w, gla_gate_b, gla_norm_g, w_out,
               norm_ffn_g, ffn_w_in, ffn_conv_w, ffn_conv_b, ffn_w_out, norm_final_g):
    new_ret, new_s5r, new_s5i, new_gla, new_conv = [], [], [], [], []
    for l in range(DEPTH):
        x, a1, a2, a3, a4, a5 = _layer(
            x, pos, s_ret[l], s5r[l], s5i[l], s_gla[l], conv[l],
            norm_mix_g[l], w_in[l], ret_norm_g[l], ret_norm_b[l],
            s5_lambda_re[l], s5_lambda_im[l], s5_log_dt[l], s5_b_re[l], s5_b_im[l],
            s5_c_re[l], s5_c_im[l], s5_d[l], s5_glu_w[l], s5_glu_b[l],
            gla_gate_w[l], gla_gate_b[l], gla_norm_g[l], w_out[l],
            norm_ffn_g[l], ffn_w_in[l], ffn_conv_w[l], ffn_conv_b[l], ffn_w_out[l])
        new_ret.append(a1)
        new_s5r.append(a2)
        new_s5i.append(a3)
        new_gla.append(a4)
        new_conv.append(a5)
    y = _rmsnorm(x, norm_final_g)
    return (y, jnp.stack(new_ret), jnp.stack(new_s5r), jnp.stack(new_s5i),
            jnp.stack(new_gla), jnp.stack(new_conv))


def setup_inputs(seed: int = 0) -> dict:
    key = jax.random.key(seed)
    ks = jax.random.split(key, 32)
    f32 = jnp.float32

    def nrm(i, shape, scale):
        return jax.random.normal(ks[i], shape, f32) * scale

    G, P, H16 = S5_GROUPS, S5_STATE, S5_CH_PER_GROUP
    return {
        'x_prompt': nrm(0, (BATCH, SEQ, D_MODEL), 1.0),
        'x_sample': nrm(1, (DEC_BATCH, DEC_SEQ, D_MODEL), 1.0),
        'state_ret': nrm(2, (DEPTH, DEC_BATCH, RET_HEADS, RET_HEAD_DIM, RET_HEAD_DIM), 0.3),
        'state_s5_re': nrm(3, (DEPTH, DEC_BATCH, G, P), 0.5),
        'state_s5_im': nrm(4, (DEPTH, DEC_BATCH, G, P), 0.5),
        'state_gla': nrm(5, (DEPTH, DEC_BATCH, GLA_HEADS, GLA_DK, GLA_DV), 0.3),
        'state_ffn_conv': nrm(6, (DEPTH, DEC_BATCH, CONV_W - 1, D_FF), 1.0),
        'norm_mix_g': 1.0 + nrm(7, (DEPTH, D_MODEL), 0.02),
        'w_in': nrm(8, (DEPTH, D_MODEL, IN_COLS), D_MODEL ** -0.5),
        'ret_norm_g': 1.0 + nrm(9, (DEPTH, RET_W), 0.02),
        'ret_norm_b': nrm(10, (DEPTH, RET_W), 0.02),
        's5_lambda_re': -0.5 + nrm(11, (DEPTH, G, P), 0.01),
        's5_lambda_im': jnp.pi * jnp.arange(P, dtype=f32) + nrm(12, (DEPTH, G, P), 0.01),
        's5_log_dt': jax.random.uniform(ks[13], (DEPTH, G), f32, math.log(1e-3), math.log(1e-1)),
        's5_b_re': nrm(14, (DEPTH, G, P, H16), (2.0 * H16) ** -0.5),
        's5_b_im': nrm(15, (DEPTH, G, P, H16), (2.0 * H16) ** -0.5),
        's5_c_re': nrm(16, (DEPTH, G, H16, P), (2.0 * P) ** -0.5),
        's5_c_im': nrm(17, (DEPTH, G, H16, P), (2.0 * P) ** -0.5),
        's5_d': nrm(18, (DEPTH, S5_W), 0.5),
        's5_glu_w': nrm(19, (DEPTH, S5_W, S5_W), S5_W ** -0.5),
        's5_glu_b': nrm(20, (DEPTH, S5_W), 0.02),
        'gla_gate_w': nrm(21, (DEPTH, GLA_RANK, GLA_KEY_W), GLA_RANK ** -0.5),
        'gla_gate_b': nrm(22, (DEPTH, GLA_KEY_W), 0.1),
        'gla_norm_g': 1.0 + nrm(23, (DEPTH, GLA_DV), 0.02),
        'w_out': nrm(24, (DEPTH, MIX_W, D_MODEL), MIX_W ** -0.5),
        'norm_ffn_g': 1.0 + nrm(25, (DEPTH, D_MODEL), 0.02),
        'ffn_w_in': nrm(26, (DEPTH, D_MODEL, 2 * D_FF), D_MODEL ** -0.5),
        'ffn_conv_w': nrm(27, (DEPTH, CONV_W, D_FF), CONV_W ** -0.5),
        'ffn_conv_b': nrm(28, (DEPTH, D_FF), 0.02),
        'ffn_w_out': nrm(29, (DEPTH, D_FF, D_MODEL), D_FF ** -0.5),
        'norm_final_g': 1.0 + nrm(30, (D_MODEL,), 0.02),
    }


def reference(x_prompt, x_sample, state_ret, state_s5_re, state_s5_im, state_gla, state_ffn_conv,
              norm_mix_g, w_in, ret_norm_g, ret_norm_b,
              s5_lambda_re, s5_lambda_im, s5_log_dt, s5_b_re, s5_b_im, s5_c_re, s5_c_im, s5_d,
              s5_glu_w, s5_glu_b, gla_gate_w, gla_gate_b, gla_norm_g, w_out,
              norm_ffn_g, ffn_w_in, ffn_conv_w, ffn_conv_b, ffn_w_out, norm_final_g):
    f32 = jnp.float32
    bp, lp, _ = x_prompt.shape
    bs, ls, _ = x_sample.shape
    pos_prompt = jnp.arange(lp, dtype=jnp.int32)
    pos_sample = PAST_LEN + jnp.arange(ls, dtype=jnp.int32)
    z_ret = jnp.zeros((DEPTH, bp, RET_HEADS, RET_HEAD_DIM, RET_HEAD_DIM), f32)
    z_s5 = jnp.zeros((DEPTH, bp, S5_GROUPS, S5_STATE), f32)
    z_gla = jnp.zeros((DEPTH, bp, GLA_HEADS, GLA_DK, GLA_DV), f32)
    z_conv = jnp.zeros((DEPTH, bp, CONV_W - 1, D_FF), x_prompt.dtype)
    y_prompt, ret_p, s5re_p, s5im_p, gla_p, conv_p = _run_trunk(
        x_prompt, pos_prompt, z_ret, z_s5, z_s5, z_gla, z_conv,
        norm_mix_g, w_in, ret_norm_g, ret_norm_b,
        s5_lambda_re, s5_lambda_im, s5_log_dt, s5_b_re, s5_b_im, s5_c_re, s5_c_im, s5_d,
        s5_glu_w, s5_glu_b, gla_gate_w, gla_gate_b, gla_norm_g, w_out,
        norm_ffn_g, ffn_w_in, ffn_conv_w, ffn_conv_b, ffn_w_out, norm_final_g)
    y_sample, ret_s, s5re_s, s5im_s, gla_s, conv_s = _run_trunk(
        x_sample, pos_sample, state_ret, state_s5_re, state_s5_im, state_gla, state_ffn_conv,
        norm_mix_g, w_in, ret_norm_g, ret_norm_b,
        s5_lambda_re, s5_lambda_im, s5_log_dt, s5_b_re, s5_b_im, s5_c_re, s5_c_im, s5_d,
        s5_glu_w, s5_glu_b, gla_gate_w, gla_gate_b, gla_norm_g, w_out,
        norm_ffn_g, ffn_w_in, ffn_conv_w, ffn_conv_b, ffn_w_out, norm_final_g)
    return (y_prompt, y_sample, ret_p, ret_s, s5re_p, s5re_s, s5im_p, s5im_s, gla_p, gla_s, conv_p, conv_s)
```

```python
import functools

import numpy as np
import jax
import jax.numpy as jnp
from jax import lax
from jax.experimental import pallas as pl
from jax.experimental.pallas import tpu as pltpu

F32, BF16 = jnp.float32, jnp.bfloat16

D_MODEL = 1024
RET_W, S5_W, GLA_W = 384, 256, 384
RET_HEADS, RET_HD = 6, 64
RET_PAIRS = RET_HEADS // 2
S5_GROUPS, S5_CH, S5_STATE = 16, 16, 64
S5_LANES = S5_GROUPS * S5_STATE
S5_MIN_NEG = 1e-4
GLA_HEADS, GLA_DK, GLA_DV, GLA_RANK = 4, 48, 96, 16
GLA_KW = GLA_HEADS * GLA_DK
GLA_GATE_TEMP = 16.0
D_FF = 2816
CONV_W = 3
ROPE_BASE = 10000.0
CHUNK = 64
EPS = 1e-6
PAST_LEN = 16384
IN_COLS = 4 * RET_W + S5_W + 2 * GLA_KW + 2 * GLA_W + GLA_RANK
COL_RET, COL_S5, COL_GLA, COL_LR = 0, 4 * RET_W, 4 * RET_W + S5_W, IN_COLS - GLA_RANK
GLA_COLS = 2 * GLA_KW + 2 * GLA_W

V7X_VMEM_BYTES = 64 * 1024 * 1024
V7X_SUBLANES = 8
VMEM_LIMIT = 58 * 1024 * 1024

ROW_TILE = 512
S5_TIME_TILE = 128
NORM_ROWS = 32
ACT_COLS = 256


def _dot(a, b):
    return jnp.dot(a, b, preferred_element_type=F32)


def _dot_nt(a, b):
    return lax.dot_general(a, b, (((1,), (1,)), ((), ())), preferred_element_type=F32)


def _dot_tn(a, b):
    return lax.dot_general(a, b, (((0,), (0,)), ((), ())), preferred_element_type=F32)


def _hi_lo(x):
    hi = x.astype(BF16)
    return hi, (x - hi.astype(F32)).astype(BF16)


def _dot_exact_rhs(x, m):
    hi, lo = _hi_lo(x)
    return _dot(hi, m) + _dot(lo, m)


def _dot_tn_exact_rhs(x, m):
    hi, lo = _hi_lo(x)
    return _dot_tn(hi, m) + _dot_tn(lo, m)


def _dot_exact_lhs(m, x):
    hi, lo = _hi_lo(x)
    return _dot(m, hi) + _dot(m, lo)


def _dot3(a, b):
    ah, al = _hi_lo(a)
    bh, bl = _hi_lo(b)
    return _dot(ah, bh) + _dot(ah, bl) + _dot(al, bh)


def _sigmoid(x):
    return 1.0 / (1.0 + jnp.exp(-x))


def _gelu_tanh(x):
    return 0.5 * x * (1.0 + jnp.tanh(0.7978845608028654 * (x + 0.044715 * (x * x * x))))


def _rmsnorm_rows(x, g):
    return x * lax.rsqrt(jnp.mean(x * x, axis=-1, keepdims=True) + EPS) * g


def _const_spec(shape, single_buffer=False):
    nd = len(shape)
    if single_buffer:
        return pl.BlockSpec(shape, lambda *_: (0,) * nd, pipeline_mode=pl.Buffered(1))
    return pl.BlockSpec(shape, lambda *_: (0,) * nd)


def _params():
    return pltpu.CompilerParams(vmem_limit_bytes=VMEM_LIMIT)


def _mix_in_body(x_ref, g_ref, w_ref, cos_ref, sin_ref, ret_ref, su_ref, gla_ref, glr_ref):
    h = _rmsnorm_rows(x_ref[...], g_ref[...]).astype(BF16)
    ret = _dot(h, w_ref[:, COL_RET:COL_S5])
    cos = cos_ref[...]
    sin = sin_ref[...]
    lane = lax.broadcasted_iota(jnp.int32, cos.shape, 1)
    first_half = (lane & (RET_HD // 2)) == 0

    def rotary(z):
        swapped = jnp.where(first_half, pltpu.roll(z, RET_W - RET_HD // 2, 1), pltpu.roll(z, RET_HD // 2, 1))
        return z * cos + swapped * sin

    ret_ref[:, 0:RET_W] = rotary(ret[:, 0:RET_W]).astype(BF16)
    ret_ref[:, RET_W:2 * RET_W] = (rotary(ret[:, RET_W:2 * RET_W]) * RET_HD ** -0.5).astype(BF16)
    ret_ref[:, 2 * RET_W:] = ret[:, 2 * RET_W:].astype(BF16)
    su_ref[...] = _dot(h, w_ref[:, COL_S5:COL_GLA]).astype(BF16)
    gla = _dot(h, w_ref[:, COL_GLA:COL_LR])
    glane = lax.broadcasted_iota(jnp.int32, (1, GLA_COLS), 1)
    gla_ref[...] = (gla * jnp.where(glane < GLA_KW, GLA_DK ** -0.5, 1.0)).astype(BF16)
    glr_ref[...] = _dot(h, w_ref[:, COL_LR:IN_COLS])


def _mix_in(x2d, norm_g, w_in_bf, cos_tab, sin_tab, tm):
    rows = x2d.shape[0]
    n_tab = cos_tab.shape[0] // tm
    row_spec = lambda w: pl.BlockSpec((tm, w), lambda i: (i, 0))
    tab_spec = pl.BlockSpec((tm, RET_W), lambda i: (i % n_tab, 0))
    return pl.pallas_call(
        _mix_in_body,
        grid=(rows // tm,),
        in_specs=[row_spec(D_MODEL), _const_spec((1, D_MODEL)), _const_spec((D_MODEL, IN_COLS), True),
                  tab_spec, tab_spec],
        out_specs=[row_spec(4 * RET_W), row_spec(S5_W), row_spec(GLA_COLS), row_spec(GLA_RANK)],
        out_shape=[jax.ShapeDtypeStruct((rows, 4 * RET_W), BF16), jax.ShapeDtypeStruct((rows, S5_W), BF16),
                   jax.ShapeDtypeStruct((rows, GLA_COLS), BF16), jax.ShapeDtypeStruct((rows, GLA_RANK), F32)],
        compiler_params=_params(),
        name="mix_in",
    )(x2d, norm_g.reshape(1, D_MODEL), w_in_bf, cos_tab, sin_tab)


def _chunk_geometry(seq_len):
    lc = min(CHUNK, seq_len)
    assert CHUNK % lc == 0 and seq_len % lc == 0
    rows = np.arange(CHUNK)
    return lc, CHUNK // lc, rows // lc, rows % lc


def _row_select(nseq, lc):
    if nseq == 1:
        return [None]
    row = lax.broadcasted_iota(jnp.int32, (CHUNK, 1), 0)
    return [jnp.where((row >= j * lc) & (row < (j + 1) * lc), 1.0, 0.0) for j in range(nseq)]


def _retention_consts(seq_len):
    lc, nseq, seq, t = _chunk_geometry(seq_len)
    gam = 1.0 - 2.0 ** (-5.0 - np.arange(RET_HEADS))
    same = (seq[:, None] == seq[None, :]) & (t[:, None] >= t[None, :])
    diff = np.maximum(t[:, None] - t[None, :], 0)
    dmask = np.zeros((RET_PAIRS, CHUNK, 2 * CHUNK))
    cdec = np.zeros((RET_PAIRS, 2 * RET_HD, 2 * RET_HD))
    for p in range(RET_PAIRS):
        for s in range(2):
            g = gam[2 * p + s]
            dmask[p, :, s * CHUNK:(s + 1) * CHUNK] = np.where(same, g ** diff, 0.0)
            cdec[p, s * RET_HD:(s + 1) * RET_HD, s * RET_HD:(s + 1) * RET_HD] = g ** lc
    lane_gam = np.repeat(gam, RET_HD)[None, :]
    qdec = lane_gam ** (t[:, None] + 1.0)
    kdec = lane_gam ** (lc - 1.0 - t[:, None])
    bd = (cdec[0] > 0).astype(np.float32)
    head_mask = np.stack([np.arange(2 * RET_HD) < RET_HD, np.arange(2 * RET_HD) >= RET_HD]).astype(np.float32)
    ones_blk = np.kron(np.eye(RET_HEADS), np.ones((RET_HD, RET_HD)))
    f = lambda a: jnp.asarray(a, F32)
    return dict(lc=lc, nseq=nseq, dmask=f(dmask), cdec=f(cdec), qdec=f(qdec), kdec=f(kdec), bd=f(bd),
                head_mask=jnp.asarray(head_mask, BF16), ones_blk=jnp.asarray(ones_blk, BF16))


def _retention_body(*refs, tl, lc, nseq, has_state):
    (ret_ref, dmask_ref, cdec_ref, qdec_ref, kdec_ref, bd_ref, hm_ref, ones_ref, lng_ref, lnb_ref) = refs[:10]
    n_in = 11 if has_state else 10
    o_ref, sout_ref, s_scr, o_scr = refs[n_in:]
    first = pl.program_id(1) == 0
    bd = bd_ref[...]

    @pl.when(first)
    def _():
        if has_state:
            sin_ref = refs[10]
            for j in range(nseq):
                for p in range(RET_PAIRS):
                    s2 = sin_ref[j, p]
                    s_scr[j * RET_PAIRS + p] = jnp.concatenate([s2, s2], axis=1) * bd
        else:
            s_scr[...] = jnp.zeros_like(s_scr)

    m0 = hm_ref[0:1, :]
    m1 = hm_ref[1:2, :]
    sel = _row_select(nseq, lc)

    def chunk(n, carry):
        rows = pl.ds(pl.multiple_of(n * CHUNK, CHUNK), CHUNK)
        q = ret_ref[rows, 0:RET_W]
        k = ret_ref[rows, RET_W:2 * RET_W]
        v = ret_ref[rows, 2 * RET_W:3 * RET_W]
        q_start = q.astype(F32) * qdec_ref[...]
        k_end = k.astype(F32) * kdec_ref[...]
        for p in range(RET_PAIRS):
            lanes = slice(2 * RET_HD * p, 2 * RET_HD * (p + 1))
            qp, kp, vp = q[:, lanes], k[:, lanes], v[:, lanes]
            kk = jnp.concatenate([kp * m0, kp * m1], axis=0)
            vv = jnp.concatenate([vp * m0, vp * m1], axis=0)
            scores = _dot_nt(qp, kk) * dmask_ref[p]
            o = _dot(scores.astype(BF16), vv)
            for j in range(nseq):
                qs, ke = q_start[:, lanes], k_end[:, lanes]
                if sel[j] is not None:
                    qs, ke = qs * sel[j], ke * sel[j]
                s_prev = s_scr[j * RET_PAIRS + p]
                o = o + _dot(qs.astype(BF16), s_prev.astype(BF16))
                kv = _dot_tn(ke.astype(BF16), vp) * bd
                s_scr[j * RET_PAIRS + p] = s_prev * cdec_ref[p] + kv
            o_scr[rows, lanes] = o
        return carry

    lax.fori_loop(0, tl // CHUNK, chunk, 0)

    o = o_scr[...]
    ones_blk = ones_ref[...]
    mu = _dot_exact_rhs(o, ones_blk) * (1.0 / RET_HD)
    cen = o - mu
    var = _dot_exact_rhs(cen * cen, ones_blk) * (1.0 / RET_HD)
    y = cen * lax.rsqrt(var + EPS) * lng_ref[...] + lnb_ref[...]
    gate = ret_ref[:, 3 * RET_W:4 * RET_W].astype(F32)
    o_ref[...] = (y * gate * _sigmoid(gate)).astype(BF16)
    for j in range(nseq):
        for p in range(RET_PAIRS):
            s = s_scr[j * RET_PAIRS + p]
            sout_ref[j, p] = s[:, 0:RET_HD] + s[:, RET_HD:2 * RET_HD]


def _retention(ret2d, ln_g, ln_b, state, batch, seq_len):
    c = _retention_consts(seq_len)
    lc, nseq = c["lc"], c["nseq"]
    tl = CHUNK if nseq > 1 else min(ROW_TILE, seq_len)
    nb, nl = batch // nseq, (nseq * seq_len) // tl
    has_state = state is not None
    state_shape = (batch, RET_PAIRS, 2 * RET_HD, RET_HD)
    state_spec = pl.BlockSpec((nseq,) + state_shape[1:], lambda b, l: (b, 0, 0, 0))
    consts = [c["dmask"], c["cdec"], c["qdec"], c["kdec"], c["bd"], c["head_mask"], c["ones_blk"],
              ln_g.reshape(1, RET_W), ln_b.reshape(1, RET_W)]
    in_specs = [pl.BlockSpec((tl, 4 * RET_W), lambda b, l: (b * nl + l, 0))] + [_const_spec(a.shape) for a in consts]
    args = [ret2d] + consts
    if has_state:
        in_specs.append(state_spec)
        args.append(state.reshape(state_shape))
    o, s_new = pl.pallas_call(
        functools.partial(_retention_body, tl=tl, lc=lc, nseq=nseq, has_state=has_state),
        grid=(nb, nl),
        in_specs=in_specs,
        out_specs=[pl.BlockSpec((tl, RET_W), lambda b, l: (b * nl + l, 0)), state_spec],
        out_shape=[jax.ShapeDtypeStruct((batch * seq_len, RET_W), BF16), jax.ShapeDtypeStruct(state_shape, F32)],
        scratch_shapes=[pltpu.VMEM((nseq * RET_PAIRS, 2 * RET_HD, 2 * RET_HD), F32), pltpu.VMEM((tl, RET_W), F32)],
        compiler_params=_params(),
        name="retention",
    )(*args)
    return o, s_new.reshape(batch, RET_HEADS, RET_HD, RET_HD)


def _gla_consts(seq_len):
    lc, nseq, seq, t = _chunk_geometry(seq_len)
    same_seq = seq[:, None] == seq[None, :]
    causal = same_seq & (t[:, None] >= t[None, :])
    kmask = np.kron(np.eye(GLA_HEADS), np.ones((1, GLA_DK)))
    vmask = np.kron(np.eye(GLA_HEADS), np.ones((1, GLA_DV)))
    bd = np.kron(np.eye(GLA_HEADS), np.ones((GLA_DK, GLA_DV)))
    ones_blk = np.kron(np.eye(GLA_HEADS), np.ones((GLA_DV, GLA_DV)))
    b = lambda a: jnp.asarray(a, BF16)
    return dict(lc=lc, nseq=nseq, tril=b(causal), seq_ones=b(same_seq),
                causal4=jnp.asarray(np.tile(causal, (1, GLA_HEADS)), F32), kmask=b(kmask), vmask=b(vmask),
                bd=jnp.asarray(bd, F32), ones_blk=b(ones_blk), ones_rows=jnp.ones((CHUNK, GLA_W), BF16))


def _gla_body(*refs, tl, lc, nseq, has_state):
    (gla_ref, glr_ref, gw_ref, gb_ref, tril_ref, seq1_ref, causal_ref, kmask_ref, vmask_ref, bd_ref,
     ones_blk_ref, ones_rows_ref, ng_ref) = refs[:13]
    n_in = 14 if has_state else 13
    o_ref, sout_ref, s_scr, o_scr, lg_scr = refs[n_in:]
    first = pl.program_id(1) == 0
    bd = bd_ref[...]

    @pl.when(first)
    def _():
        if has_state:
            sin_ref = refs[13]
            for j in range(nseq):
                s2 = sin_ref[j]
                s_scr[j] = jnp.concatenate([s2] * GLA_HEADS, axis=1) * bd
        else:
            s_scr[...] = jnp.zeros_like(s_scr)

    z = _dot3(glr_ref[...], gw_ref[...]) + gb_ref[...]
    lg_scr[...] = (jnp.minimum(z, 0.0) - jnp.log(1.0 + jnp.exp(-jnp.abs(z)))) * (1.0 / GLA_GATE_TEMP)
    sel = _row_select(nseq, lc)
    q0, k0, v0, g0 = 0, GLA_KW, 2 * GLA_KW, 2 * GLA_KW + GLA_W

    def chunk(n, carry):
        rows = pl.ds(pl.multiple_of(n * CHUNK, CHUNK), CHUNK)
        q = gla_ref[rows, q0:k0].astype(F32)
        k = gla_ref[rows, k0:v0].astype(F32)
        v = gla_ref[rows, v0:g0]
        lg = lg_scr[rows, :]
        b_cum = _dot_exact_lhs(tril_ref[...], lg)
        b_end = _dot_exact_lhs(seq1_ref[...], lg)
        q_in = q * jnp.exp(b_cum)
        k_in = (k * jnp.exp(-b_cum)).astype(BF16)
        k_dec = k * jnp.exp(b_end - b_cum)
        kk = jnp.concatenate([k_in * kmask_ref[h:h + 1, :] for h in range(GLA_HEADS)], axis=0)
        vv = jnp.concatenate([v * vmask_ref[h:h + 1, :] for h in range(GLA_HEADS)], axis=0)
        scores = _dot_nt(q_in.astype(BF16), kk) * causal_ref[...]
        o = _dot(scores.astype(BF16), vv)
        for j in range(nseq):
            qs, kd, lgj = q_in, k_dec, lg
            if sel[j] is not None:
                qs, kd, lgj = qs * sel[j], kd * sel[j], lg * sel[j]
            s_prev = s_scr[j]
            o = o + _dot(qs.astype(BF16), s_prev.astype(BF16))
            kv = _dot_tn(kd.astype(BF16), v) * bd
            decay = jnp.exp(_dot_tn_exact_rhs(lgj, ones_rows_ref[...]))
            s_scr[j] = s_prev * decay + kv
        o_scr[rows, :] = o
        return carry

    lax.fori_loop(0, tl // CHUNK, chunk, 0)

    o = o_scr[...]
    ms = _dot_exact_rhs(o * o, ones_blk_ref[...]) * (1.0 / GLA_DV)
    gate = gla_ref[:, g0:g0 + GLA_W].astype(F32)
    o_ref[...] = (o * lax.rsqrt(ms + EPS) * ng_ref[...] * gate * _sigmoid(gate)).astype(BF16)
    for j in range(nseq):
        s = s_scr[j]
        sout_ref[j] = (s[:, 0:GLA_DV] + s[:, GLA_DV:2 * GLA_DV]
                       + s[:, 2 * GLA_DV:3 * GLA_DV] + s[:, 3 * GLA_DV:4 * GLA_DV])


def _gla(gla2d, glr2d, gate_w, gate_b, norm_g, state, batch, seq_len):
    c = _gla_consts(seq_len)
    lc, nseq = c["lc"], c["nseq"]
    tl = CHUNK if nseq > 1 else min(ROW_TILE, seq_len)
    nb, nl = batch // nseq, (nseq * seq_len) // tl
    has_state = state is not None
    state_shape = (batch, GLA_KW, GLA_DV)
    state_spec = pl.BlockSpec((nseq,) + state_shape[1:], lambda b, l: (b, 0, 0))
    consts = [gate_w, gate_b.reshape(1, GLA_KW), c["tril"], c["seq_ones"], c["causal4"], c["kmask"], c["vmask"],
              c["bd"], c["ones_blk"], c["ones_rows"], jnp.tile(norm_g, GLA_HEADS).reshape(1, GLA_W)]
    row_spec = lambda w: pl.BlockSpec((tl, w), lambda b, l: (b * nl + l, 0))
    in_specs = [row_spec(GLA_COLS), row_spec(GLA_RANK)] + [_const_spec(a.shape) for a in consts]
    args = [gla2d, glr2d] + consts
    if has_state:
        in_specs.append(state_spec)
        args.append(state.reshape(state_shape))
    o, s_new = pl.pallas_call(
        functools.partial(_gla_body, tl=tl, lc=lc, nseq=nseq, has_state=has_state),
        grid=(nb, nl),
        in_specs=in_specs,
        out_specs=[row_spec(GLA_W), state_spec],
        out_shape=[jax.ShapeDtypeStruct((batch * seq_len, GLA_W), BF16), jax.ShapeDtypeStruct(state_shape, F32)],
        scratch_shapes=[pltpu.VMEM((nseq, GLA_KW, GLA_W), F32), pltpu.VMEM((tl, GLA_W), F32),
                        pltpu.VMEM((tl, GLA_KW), F32)],
        compiler_params=_params(),
        name="gla",
    )(*args)
    return o, s_new.reshape(batch, GLA_HEADS, GLA_DK, GLA_DV)


def _s5_discretize(lam_re, lam_im, log_dt, b_re, b_im, c_re, c_im):
    lr = jnp.minimum(lam_re.astype(F32), -S5_MIN_NEG)
    li = lam_im.astype(F32)
    dt = jnp.exp(log_dt.astype(F32))[:, None]
    mag = jnp.exp(lr * dt)
    ar = mag * jnp.cos(li * dt)
    ai = mag * jnp.sin(li * dt)
    den = lr * lr + li * li
    cr = ((ar - 1.0) * lr + ai * li) / den
    ci = (ai * lr - (ar - 1.0) * li) / den
    b_re, b_im = b_re.astype(F32), b_im.astype(F32)
    bbar_re = cr[..., None] * b_re - ci[..., None] * b_im
    bbar_im = cr[..., None] * b_im + ci[..., None] * b_re
    eye = jnp.eye(S5_GROUPS, dtype=F32)
    in_blk = lambda b: jnp.einsum("gpi,gh->gihp", b, eye).reshape(S5_W, S5_LANES)
    out_blk = lambda c: jnp.einsum("gop,gh->gpho", c.astype(F32), eye).reshape(S5_LANES, S5_W)
    bdb = jnp.concatenate([in_blk(bbar_re), in_blk(bbar_im)], axis=1).astype(BF16)
    bdc = jnp.concatenate([out_blk(c_re), -out_blk(c_im)], axis=0).astype(BF16)
    return ar.reshape(1, S5_LANES), ai.reshape(1, S5_LANES), bdb, bdc


def _s5_body(*refs, tl, batch, has_state):
    u_ref, ar_ref, ai_ref, bdb_ref, bdc_ref, d_ref, gw_ref, gb_ref = refs[:8]
    n_in = 10 if has_state else 8
    o_ref, hr_ref, hi_ref, x_scr, h_scr = refs[n_in:]

    @pl.when(pl.program_id(0) == 0)
    def _():
        if has_state:
            h_scr[0] = refs[8][...]
            h_scr[1] = refs[9][...]
        else:
            h_scr[...] = jnp.zeros_like(h_scr)

    u = u_ref[...]
    x_scr[...] = _dot(u, bdb_ref[...])
    ar = jnp.broadcast_to(ar_ref[...], (batch, S5_LANES))
    ai = jnp.broadcast_to(ai_ref[...], (batch, S5_LANES))

    def step(t, carry):
        hr, hi = carry
        rows = pl.ds(pl.multiple_of(t * batch, batch), batch)
        nr = ar * hr - ai * hi + x_scr[rows, 0:S5_LANES]
        ni = ar * hi + ai * hr + x_scr[rows, S5_LANES:2 * S5_LANES]
        x_scr[rows, 0:S5_LANES] = nr
        x_scr[rows, S5_LANES:2 * S5_LANES] = ni
        return nr, ni

    carry = (h_scr[0], h_scr[1])
    if tl <= V7X_SUBLANES:
        for t in range(tl):
            carry = step(t, carry)
    else:
        carry = lax.fori_loop(0, tl, step, carry, unroll=V7X_SUBLANES)
    h_scr[0], h_scr[1] = carry
    hr_ref[...], hi_ref[...] = carry

    y = _dot(x_scr[...].astype(BF16), bdc_ref[...]) + d_ref[...] * u.astype(F32)
    y = _gelu_tanh(y)
    o_ref[...] = (y * _sigmoid(_dot(y.astype(BF16), gw_ref[...]) + gb_ref[...])).astype(BF16)


def _s5(u_tm, disc, d, glu_w_bf, glu_b, state, batch, seq_len):
    ar, ai, bdb, bdc = disc
    tl = min(S5_TIME_TILE, seq_len)
    has_state = state is not None
    consts = [ar, ai, bdb, bdc, d.reshape(1, S5_W), glu_w_bf, glu_b.reshape(1, S5_W)]
    h_spec = pl.BlockSpec((batch, S5_LANES), lambda i: (0, 0))
    in_specs = [pl.BlockSpec((tl * batch, S5_W), lambda i: (i, 0))] + [_const_spec(a.shape) for a in consts]
    args = [u_tm] + consts
    if has_state:
        in_specs += [h_spec, h_spec]
        args += [state[0].reshape(batch, S5_LANES), state[1].reshape(batch, S5_LANES)]
    o, hr, hi = pl.pallas_call(
        functools.partial(_s5_body, tl=tl, batch=batch, has_state=has_state),
        grid=(seq_len // tl,),
        in_specs=in_specs,
        out_specs=[pl.BlockSpec((tl * batch, S5_W), lambda i: (i, 0)), h_spec, h_spec],
        out_shape=[jax.ShapeDtypeStruct((seq_len * batch, S5_W), BF16),
                   jax.ShapeDtypeStruct((batch, S5_LANES), F32), jax.ShapeDtypeStruct((batch, S5_LANES), F32)],
        scratch_shapes=[pltpu.VMEM((tl * batch, 2 * S5_LANES), F32), pltpu.VMEM((2, batch, S5_LANES), F32)],
        compiler_params=_params(),
        name="s5",
    )(*args)
    shape = (batch, S5_GROUPS, S5_STATE)
    return o, hr.reshape(shape), hi.reshape(shape)


def _ffn_body(*refs, tm, seq_rows, final):
    (x_ref, oret_ref, os5_ref, ogla_ref, wout_ref, gffn_ref, win_ref, cw_ref, cb_ref, wo_ref) = refs[:10]
    n = 10
    gfin_ref = None
    if final:
        gfin_ref, n = refs[n], n + 1
    if seq_rows is not None:
        f1_ref, f2_ref = refs[n:n + 2]
        n += 2
    out_ref, conv_ref, x1_scr, h_scr, a_scr, g_scr, act_scr = refs[n:]
    pad = V7X_SUBLANES

    if seq_rows is None:
        @pl.when(pl.program_id(1) == 0)
        def _():
            a_scr[0:pad, :] = jnp.zeros((pad, D_FF), F32)
    else:
        a_scr[0:pad, :] = jnp.zeros((pad, D_FF), F32)

    x1_scr[...] = (x_ref[...] + _dot(oret_ref[...], wout_ref[0:RET_W, :])
                   + _dot(os5_ref[...], wout_ref[RET_W:RET_W + S5_W, :])
                   + _dot(ogla_ref[...], wout_ref[RET_W + S5_W:D_MODEL, :]))

    def norm_rows(src_ref, dst_ref, g, dtype):
        def blk(i, c):
            r = pl.ds(pl.multiple_of(i * NORM_ROWS, NORM_ROWS), NORM_ROWS)
            dst_ref[r, :] = _rmsnorm_rows(src_ref[r, :], g).astype(dtype)
            return c
        lax.fori_loop(0, tm // NORM_ROWS, blk, 0)

    norm_rows(x1_scr, h_scr, gffn_ref[...], BF16)
    h = h_scr[...]
    a_scr[pad:pad + tm, :] = _dot(h, win_ref[:, 0:D_FF])
    g_scr[...] = _dot(h, win_ref[:, D_FF:2 * D_FF])

    def act_rows(i, c):
        r0 = pl.multiple_of(i * NORM_ROWS, NORM_ROWS)
        r = pl.ds(r0, NORM_ROWS)
        for c0 in range(0, D_FF, ACT_COLS):
            cols = slice(c0, c0 + ACT_COLS)
            cur = a_scr[pl.ds(r0 + pad, NORM_ROWS), cols]
            ext = jnp.concatenate([a_scr[pl.ds(r0, pad), cols], cur], axis=0)
            prev1 = ext[pad - 1:pad - 1 + NORM_ROWS]
            prev2 = ext[pad - 2:pad - 2 + NORM_ROWS]
            if seq_rows is not None:
                t = lax.broadcasted_iota(jnp.int32, (NORM_ROWS, ACT_COLS), 0) % seq_rows
                prev1 = jnp.where(t == 0, f1_ref[r, cols], prev1)
                prev2 = jnp.where(t < 2, f2_ref[r, cols], prev2)
            conv = (cb_ref[:, cols] + prev2 * cw_ref[0:1, cols] + prev1 * cw_ref[1:2, cols]
                    + cur * cw_ref[2:3, cols])
            act_scr[r, cols] = (_gelu_tanh(conv) * g_scr[r, cols]).astype(BF16)
        return c

    lax.fori_loop(0, tm // NORM_ROWS, act_rows, 0)

    if final:
        x1_scr[...] = x1_scr[...] + _dot(act_scr[...], wo_ref[...])
        norm_rows(x1_scr, out_ref, gfin_ref[...], F32)
    else:
        out_ref[...] = x1_scr[...] + _dot(act_scr[...], wo_ref[...])

    if seq_rows is None:
        conv_ref[0] = a_scr[pad + tm - (CONV_W - 1):pad + tm, :]
        a_scr[0:pad, :] = a_scr[tm:tm + pad, :]
    else:
        conv_ref[...] = a_scr[pad:pad + tm, :]


def _ffn(x2d, o_ret, o_s5, o_gla, w_out_bf, norm_g, w_in_bf, conv_w, conv_b, w_o_bf, final_g, conv_state,
         batch, seq_len):
    rows = batch * seq_len
    tm = min(ROW_TILE, rows)
    short = conv_state is not None
    final = final_g is not None
    cw = jnp.concatenate([conv_w.astype(F32), jnp.zeros((V7X_SUBLANES - CONV_W, D_FF), F32)], axis=0)
    consts = [w_out_bf, norm_g.reshape(1, D_MODEL), w_in_bf, cw, conv_b.reshape(1, D_FF), w_o_bf]
    big = [True, False, True, False, False, True]
    if final:
        consts.append(final_g.reshape(1, D_MODEL))
        big.append(False)
    if short:
        assert tm % seq_len == 0 and seq_len >= CONV_W - 1 and NORM_ROWS % seq_len == 0
        grid = (rows // tm,)
        imap = lambda i: (i, 0)
        zeros = jnp.zeros((batch, seq_len - 1, D_FF), F32)
        fill1 = jnp.concatenate([conv_state[:, 1:2], zeros], axis=1).reshape(rows, D_FF)
        fill2 = jnp.concatenate([conv_state, zeros[:, 1:]], axis=1).reshape(rows, D_FF)
        extra, extra_specs = [fill1, fill2], [pl.BlockSpec((tm, D_FF), imap)] * 2
        conv_spec = pl.BlockSpec((tm, D_FF), imap)
        conv_shape = jax.ShapeDtypeStruct((rows, D_FF), F32)
    else:
        assert seq_len % tm == 0
        nl = seq_len // tm
        grid = (batch, nl)
        imap = lambda b, l: (b * nl + l, 0)
        extra, extra_specs = [], []
        conv_spec = pl.BlockSpec((1, CONV_W - 1, D_FF), lambda b, l: (b, 0, 0))
        conv_shape = jax.ShapeDtypeStruct((batch, CONV_W - 1, D_FF), F32)
    row_spec = lambda w: pl.BlockSpec((tm, w), imap)
    in_specs = ([row_spec(D_MODEL), row_spec(RET_W), row_spec(S5_W), row_spec(GLA_W)]
                + [_const_spec(a.shape, b) for a, b in zip(consts, big)] + extra_specs)
    out, conv_out = pl.pallas_call(
        functools.partial(_ffn_body, tm=tm, seq_rows=seq_len if short else None, final=final),
        grid=grid,
        in_specs=in_specs,
        out_specs=[row_spec(D_MODEL), conv_spec],
        out_shape=[jax.ShapeDtypeStruct((rows, D_MODEL), F32), conv_shape],
        scratch_shapes=[pltpu.VMEM((tm, D_MODEL), F32), pltpu.VMEM((tm, D_MODEL), BF16),
                        pltpu.VMEM((tm + V7X_SUBLANES, D_FF), F32), pltpu.VMEM((tm, D_FF), F32),
                        pltpu.VMEM((tm, D_FF), BF16)],
        compiler_params=_params(),
        name="ffn",
    )(x2d, o_ret, o_s5, o_gla, *consts, *extra)
    if short:
        conv_out = conv_out.reshape(batch, seq_len, D_FF)[:, seq_len - (CONV_W - 1):]
    return out, conv_out


def _rotary_tables(pos, rows):
    half = RET_HD // 2
    inv = ROPE_BASE ** (-jnp.arange(half, dtype=F32) / half)
    ang = pos.astype(F32)[:, None] * inv[None, :]
    cos = jnp.tile(jnp.concatenate([jnp.cos(ang), jnp.cos(ang)], axis=1), (1, RET_HEADS))
    sin = jnp.tile(jnp.concatenate([-jnp.sin(ang), jnp.sin(ang)], axis=1), (1, RET_HEADS))
    reps = rows // pos.shape[0]
    return jnp.tile(cos, (reps, 1)), jnp.tile(sin, (reps, 1))


def _time_major(a2d, batch, seq_len):
    return a2d.reshape(batch, seq_len, -1).transpose(1, 0, 2).reshape(seq_len * batch, -1)


def _batch_major(a2d, batch, seq_len):
    return a2d.reshape(seq_len, batch, -1).transpose(1, 0, 2).reshape(batch * seq_len, -1)


def _run_group(x, pos, states, layers, final_g):
    batch, seq_len, _ = x.shape
    rows = batch * seq_len
    tm = min(ROW_TILE, rows)
    short = states is not None
    tab_rows = tm if short else seq_len
    cos_tab, sin_tab = _rotary_tables(pos, tab_rows)
    x2d = x.reshape(rows, D_MODEL)
    outs = []
    for li, lp in enumerate(layers):
        st = states[li] if short else None
        ret, su, gla, glr = _mix_in(x2d, lp["norm_mix_g"], lp["w_in"], cos_tab, sin_tab, tm)
        o_ret, s_ret = _retention(ret, lp["ret_norm_g"], lp["ret_norm_b"], st["ret"] if short else None,
                                  batch, seq_len)
        o_s5, s5r, s5i = _s5(_time_major(su, batch, seq_len), lp["s5_disc"], lp["s5_d"], lp["s5_glu_w"],
                             lp["s5_glu_b"], (st["s5r"], st["s5i"]) if short else None, batch, seq_len)
        o_s5 = _batch_major(o_s5, batch, seq_len)
        o_gla, s_gla = _gla(gla, glr, lp["gla_gate_w"], lp["gla_gate_b"], lp["gla_norm_g"],
                            st["gla"] if short else None, batch, seq_len)
        last = li == len(layers) - 1
        x2d, conv_new = _ffn(x2d, o_ret, o_s5, o_gla, lp["w_out"], lp["norm_ffn_g"], lp["ffn_w_in"],
                             lp["ffn_conv_w"], lp["ffn_conv_b"], lp["ffn_w_out"], final_g if last else None,
                             st["conv"] if short else None, batch, seq_len)
        outs.append((s_ret, s5r, s5i, s_gla, conv_new))
    stacked = [jnp.stack([o[i] for o in outs]) for i in range(5)]
    return [x2d.reshape(batch, seq_len, D_MODEL)] + stacked


def kernel(x_prompt, x_sample, state_ret, state_s5_re, state_s5_im, state_gla, state_ffn_conv, norm_mix_g, w_in, ret_norm_g, ret_norm_b, s5_lambda_re, s5_lambda_im, s5_log_dt, s5_b_re, s5_b_im, s5_c_re, s5_c_im, s5_d, s5_glu_w, s5_glu_b, gla_gate_w, gla_gate_b, gla_norm_g, w_out, norm_ffn_g, ffn_w_in, ffn_conv_w, ffn_conv_b, ffn_w_out, norm_final_g):
    depth = w_in.shape[0]
    layers = []
    for l in range(depth):
        layers.append(dict(
            norm_mix_g=norm_mix_g[l], w_in=w_in[l].astype(BF16), ret_norm_g=ret_norm_g[l], ret_norm_b=ret_norm_b[l],
            s5_disc=_s5_discretize(s5_lambda_re[l], s5_lambda_im[l], s5_log_dt[l], s5_b_re[l], s5_b_im[l],
                                   s5_c_re[l], s5_c_im[l]),
            s5_d=s5_d[l], s5_glu_w=s5_glu_w[l].astype(BF16), s5_glu_b=s5_glu_b[l],
            gla_gate_w=gla_gate_w[l], gla_gate_b=gla_gate_b[l], gla_norm_g=gla_norm_g[l],
            w_out=w_out[l].astype(BF16), norm_ffn_g=norm_ffn_g[l], ffn_w_in=ffn_w_in[l].astype(BF16),
            ffn_conv_w=ffn_conv_w[l], ffn_conv_b=ffn_conv_b[l], ffn_w_out=ffn_w_out[l].astype(BF16)))
    bp, lp_, _ = x_prompt.shape
    bs, ls, _ = x_sample.shape
    pos_prompt = jnp.arange(lp_, dtype=jnp.int32)
    pos_sample = PAST_LEN + jnp.arange(ls, dtype=jnp.int32)
    sample_states = [dict(ret=state_ret[l], s5r=state_s5_re[l], s5i=state_s5_im[l], gla=state_gla[l],
                          conv=state_ffn_conv[l]) for l in range(depth)]
    yp, ret_p, s5r_p, s5i_p, gla_p, conv_p = _run_group(x_prompt, pos_prompt, None, layers, norm_final_g)
    ys, ret_s, s5r_s, s5i_s, gla_s, conv_s = _run_group(x_sample, pos_sample, sample_states, layers, norm_final_g)
    return (yp, ys, ret_p, ret_s, s5r_p, s5r_s, s5i_p, s5i_s, gla_p, gla_s, conv_p, conv_s)
```

```python
import functools

import numpy as np
import jax
import jax.numpy as jnp
from jax import lax
from jax.experimental import pallas as pl
from jax.experimental.pallas import tpu as pltpu

F32, BF16 = jnp.float32, jnp.bfloat16

D_MODEL = 1024
RET_W, S5_W, GLA_W = 384, 256, 384
RET_HEADS, RET_HD = 6, 64
RET_PAIRS = RET_HEADS // 2
S5_GROUPS, S5_CH, S5_STATE = 16, 16, 64
S5_LANES = S5_GROUPS * S5_STATE
S5_MIN_NEG = 1e-4
GLA_HEADS, GLA_DK, GLA_DV, GLA_RANK = 4, 48, 96, 16
GLA_KW = GLA_HEADS * GLA_DK
GLA_GATE_TEMP = 16.0
D_FF = 2816
CONV_W = 3
ROPE_BASE = 10000.0
CHUNK = 64
EPS = 1e-6
PAST_LEN = 16384
IN_COLS = 4 * RET_W + S5_W + 2 * GLA_KW + 2 * GLA_W + GLA_RANK
COL_RET, COL_S5, COL_GLA, COL_LR = 0, 4 * RET_W, 4 * RET_W + S5_W, IN_COLS - GLA_RANK
GLA_COLS = 2 * GLA_KW + 2 * GLA_W

V7X_VMEM_BYTES = 64 * 1024 * 1024
V7X_SUBLANES = 8
VMEM_LIMIT = 58 * 1024 * 1024

ROW_TILE = 512
S5_TIME_TILE = 128
ACT_COLS = 256


def _dot(a, b):
    return jnp.dot(a, b, preferred_element_type=F32)


def _dot_nt(a, b):
    return lax.dot_general(a, b, (((1,), (1,)), ((), ())), preferred_element_type=F32)


def _dot_tn(a, b):
    return lax.dot_general(a, b, (((0,), (0,)), ((), ())), preferred_element_type=F32)


def _hi_lo(x):
    hi = x.astype(BF16)
    return hi, (x - hi.astype(F32)).astype(BF16)


def _dot_exact_rhs(x, m):
    hi, lo = _hi_lo(x)
    return _dot(hi, m) + _dot(lo, m)


def _dot_exact_lhs(m, x):
    hi, lo = _hi_lo(x)
    return _dot(m, hi) + _dot(m, lo)


def _dot3(a, b):
    ah, al = _hi_lo(a)
    bh, bl = _hi_lo(b)
    return _dot(ah, bh) + _dot(ah, bl) + _dot(al, bh)


def _sigmoid(x):
    return 1.0 / (1.0 + jnp.exp(-x))


def _gelu_tanh(x):
    return 0.5 * x * (1.0 + jnp.tanh(0.7978845608028654 * (x + 0.044715 * (x * x * x))))


def _rmsnorm_rows(x, g):
    return x * lax.rsqrt(jnp.mean(x * x, axis=-1, keepdims=True) + EPS) * g


def _const_spec(shape, single_buffer=False):
    nd = len(shape)
    if single_buffer:
        return pl.BlockSpec(shape, lambda *_: (0,) * nd, pipeline_mode=pl.Buffered(1))
    return pl.BlockSpec(shape, lambda *_: (0,) * nd)


def _params():
    return pltpu.CompilerParams(vmem_limit_bytes=VMEM_LIMIT)


def _mix_in_body(x_ref, g_ref, w_ref, cos_ref, sin_ref, ret_ref, su_ref, gla_ref, glr_ref):
    h = _rmsnorm_rows(x_ref[...], g_ref[...]).astype(BF16)
    ret = _dot(h, w_ref[:, COL_RET:COL_S5])
    cos = cos_ref[...]
    sin = sin_ref[...]
    lane = lax.broadcasted_iota(jnp.int32, cos.shape, 1)
    first_half = (lane & (RET_HD // 2)) == 0

    def rotary(z):
        swapped = jnp.where(first_half, pltpu.roll(z, RET_W - RET_HD // 2, 1), pltpu.roll(z, RET_HD // 2, 1))
        return z * cos + swapped * sin

    ret_ref[:, 0:RET_W] = rotary(ret[:, 0:RET_W]).astype(BF16)
    ret_ref[:, RET_W:2 * RET_W] = (rotary(ret[:, RET_W:2 * RET_W]) * RET_HD ** -0.5).astype(BF16)
    ret_ref[:, 2 * RET_W:] = ret[:, 2 * RET_W:].astype(BF16)
    su_ref[...] = _dot(h, w_ref[:, COL_S5:COL_GLA]).astype(BF16)
    gla = _dot(h, w_ref[:, COL_GLA:COL_LR])
    glane = lax.broadcasted_iota(jnp.int32, (1, GLA_COLS), 1)
    gla_ref[...] = (gla * jnp.where(glane < GLA_KW, GLA_DK ** -0.5, 1.0)).astype(BF16)
    glr_ref[...] = _dot(h, w_ref[:, COL_LR:IN_COLS])


def _mix_in(x2d, norm_g, w_in_bf, cos_tab, sin_tab, tm):
    rows = x2d.shape[0]
    n_tab = cos_tab.shape[0] // tm
    row_spec = lambda w: pl.BlockSpec((tm, w), lambda i: (i, 0))
    tab_spec = pl.BlockSpec((tm, RET_W), lambda i: (i % n_tab, 0))
    return pl.pallas_call(
        _mix_in_body,
        grid=(rows // tm,),
        in_specs=[row_spec(D_MODEL), _const_spec((1, D_MODEL)), _const_spec((D_MODEL, IN_COLS), True),
                  tab_spec, tab_spec],
        out_specs=[row_spec(4 * RET_W), row_spec(S5_W), row_spec(GLA_COLS), row_spec(GLA_RANK)],
        out_shape=[jax.ShapeDtypeStruct((rows, 4 * RET_W), BF16), jax.ShapeDtypeStruct((rows, S5_W), BF16),
                   jax.ShapeDtypeStruct((rows, GLA_COLS), BF16), jax.ShapeDtypeStruct((rows, GLA_RANK), F32)],
        compiler_params=_params(),
        name="mix_in",
    )(x2d, norm_g.reshape(1, D_MODEL), w_in_bf, cos_tab, sin_tab)


def _chunk_geometry(seq_len):
    lc = min(CHUNK, seq_len)
    assert CHUNK % lc == 0 and seq_len % lc == 0
    rows = np.arange(CHUNK)
    return lc, CHUNK // lc, rows // lc, rows % lc


def _row_select(nseq, lc):
    if nseq == 1:
        return [None]
    row = lax.broadcasted_iota(jnp.int32, (CHUNK, 1), 0)
    return [jnp.where((row >= j * lc) & (row < (j + 1) * lc), 1.0, 0.0).astype(BF16) for j in range(nseq)]


def _retention_consts(seq_len):
    lc, nseq, seq, t = _chunk_geometry(seq_len)
    gam = 1.0 - 2.0 ** (-5.0 - np.arange(RET_HEADS))
    same = (seq[:, None] == seq[None, :]) & (t[:, None] >= t[None, :])
    diff = np.maximum(t[:, None] - t[None, :], 0)
    dmask = np.zeros((RET_PAIRS, CHUNK, 2 * CHUNK))
    cdec = np.zeros((RET_PAIRS, 2 * RET_HD, 2 * RET_HD))
    for p in range(RET_PAIRS):
        for s in range(2):
            g = gam[2 * p + s]
            dmask[p, :, s * CHUNK:(s + 1) * CHUNK] = np.where(same, g ** diff, 0.0)
            cdec[p, s * RET_HD:(s + 1) * RET_HD, s * RET_HD:(s + 1) * RET_HD] = g ** lc
    lane_gam = np.repeat(gam, RET_HD)[None, :]
    qdec = lane_gam ** (t[:, None] + 1.0)
    kdec = lane_gam ** (lc - 1.0 - t[:, None])
    bd = (cdec[0] > 0).astype(np.float32)
    head_mask = np.stack([np.arange(2 * RET_HD) < RET_HD, np.arange(2 * RET_HD) >= RET_HD]).astype(np.float32)
    ones_blk = np.kron(np.eye(RET_HEADS), np.ones((RET_HD, RET_HD)))
    f = lambda a: jnp.asarray(a, F32)
    return dict(lc=lc, nseq=nseq, dmask=f(dmask), cdec=f(cdec), qdec=f(qdec), kdec=f(kdec), bd=f(bd),
                head_mask=jnp.asarray(head_mask, BF16), ones_blk=jnp.asarray(ones_blk, BF16))


def _retention_body(*refs, tl, lc, nseq, has_state):
    (ret_ref, dmask_ref, cdec_ref, qdec_ref, kdec_ref, bd_ref, hm_ref, ones_ref, lng_ref, lnb_ref) = refs[:10]
    n_in = 11 if has_state else 10
    o_ref, sout_ref, s_scr, o_scr = refs[n_in:]
    first = pl.program_id(1) == 0
    bd = bd_ref[...]

    @pl.when(first)
    def _():
        if has_state:
            sin_ref = refs[10]
            for j in range(nseq):
                for p in range(RET_PAIRS):
                    s2 = sin_ref[j, p]
                    s_scr[j * RET_PAIRS + p] = jnp.concatenate([s2, s2], axis=1) * bd
        else:
            s_scr[...] = jnp.zeros_like(s_scr)

    m0 = hm_ref[0:1, :]
    m1 = hm_ref[1:2, :]
    sel = _row_select(nseq, lc)
    q = ret_ref[:, 0:RET_W]
    k = ret_ref[:, RET_W:2 * RET_W]
    v = ret_ref[:, 2 * RET_W:3 * RET_W]
    q_start = (q.astype(F32) * qdec_ref[...]).astype(BF16)
    k_end = (k.astype(F32) * kdec_ref[...]).astype(BF16)
    blocks = [(n, p) for n in range(tl // CHUNK) for p in range(RET_PAIRS)]

    def part(a, n, p):
        return a[n * CHUNK:(n + 1) * CHUNK, 2 * RET_HD * p:2 * RET_HD * (p + 1)]

    scores, o, kv = {}, {}, {}
    for n, p in blocks:
        kp = part(k, n, p)
        kk = jnp.concatenate([kp * m0, kp * m1], axis=0)
        scores[n, p] = (_dot_nt(part(q, n, p), kk) * dmask_ref[p]).astype(BF16)
    for n, p in blocks:
        vp = part(v, n, p)
        o[n, p] = _dot(scores[n, p], jnp.concatenate([vp * m0, vp * m1], axis=0))
    for n, p in blocks:
        for j in range(nseq):
            ke = part(k_end, n, p)
            kv[n, p, j] = _dot_tn(ke if sel[j] is None else ke * sel[j], part(v, n, p)) * bd
    for n, p in blocks:
        for j in range(nseq):
            qs = part(q_start, n, p)
            s_prev = s_scr[j * RET_PAIRS + p]
            o[n, p] = o[n, p] + _dot(qs if sel[j] is None else qs * sel[j], s_prev.astype(BF16))
            s_scr[j * RET_PAIRS + p] = s_prev * cdec_ref[p] + kv[n, p, j]
        o_scr[n * CHUNK:(n + 1) * CHUNK, 2 * RET_HD * p:2 * RET_HD * (p + 1)] = o[n, p]

    o = o_scr[...]
    ones_blk = ones_ref[...]
    mu = _dot_exact_rhs(o, ones_blk) * (1.0 / RET_HD)
    cen = o - mu
    var = _dot_exact_rhs(cen * cen, ones_blk) * (1.0 / RET_HD)
    y = cen * lax.rsqrt(var + EPS) * lng_ref[...] + lnb_ref[...]
    gate = ret_ref[:, 3 * RET_W:4 * RET_W].astype(F32)
    o_ref[...] = (y * gate * _sigmoid(gate)).astype(BF16)
    for j in range(nseq):
        for p in range(RET_PAIRS):
            s = s_scr[j * RET_PAIRS + p]
            sout_ref[j, p] = s[:, 0:RET_HD] + s[:, RET_HD:2 * RET_HD]


def _retention(ret2d, ln_g, ln_b, state, batch, seq_len):
    c = _retention_consts(seq_len)
    lc, nseq = c["lc"], c["nseq"]
    tl = CHUNK if nseq > 1 else min(ROW_TILE, seq_len)
    nb, nl = batch // nseq, (nseq * seq_len) // tl
    has_state = state is not None
    state_shape = (batch, RET_PAIRS, 2 * RET_HD, RET_HD)
    state_spec = pl.BlockSpec((nseq,) + state_shape[1:], lambda b, l: (b, 0, 0, 0))
    tile_rows = lambda a: jnp.tile(a, (tl // CHUNK, 1))
    consts = [c["dmask"], c["cdec"], tile_rows(c["qdec"]), tile_rows(c["kdec"]), c["bd"], c["head_mask"], c["ones_blk"],
              ln_g.reshape(1, RET_W), ln_b.reshape(1, RET_W)]
    in_specs = [pl.BlockSpec((tl, 4 * RET_W), lambda b, l: (b * nl + l, 0))] + [_const_spec(a.shape) for a in consts]
    args = [ret2d] + consts
    if has_state:
        in_specs.append(state_spec)
        args.append(state.reshape(state_shape))
    o, s_new = pl.pallas_call(
        functools.partial(_retention_body, tl=tl, lc=lc, nseq=nseq, has_state=has_state),
        grid=(nb, nl),
        in_specs=in_specs,
        out_specs=[pl.BlockSpec((tl, RET_W), lambda b, l: (b * nl + l, 0)), state_spec],
        out_shape=[jax.ShapeDtypeStruct((batch * seq_len, RET_W), BF16), jax.ShapeDtypeStruct(state_shape, F32)],
        scratch_shapes=[pltpu.VMEM((nseq * RET_PAIRS, 2 * RET_HD, 2 * RET_HD), F32), pltpu.VMEM((tl, RET_W), F32)],
        compiler_params=_params(),
        name="retention",
    )(*args)
    return o, s_new.reshape(batch, RET_HEADS, RET_HD, RET_HD)


def _gla_consts(seq_len):
    lc, nseq, seq, t = _chunk_geometry(seq_len)
    same_seq = seq[:, None] == seq[None, :]
    causal = same_seq & (t[:, None] >= t[None, :])
    kmask = np.kron(np.eye(GLA_HEADS), np.ones((1, GLA_DK)))
    vmask = np.kron(np.eye(GLA_HEADS), np.ones((1, GLA_DV)))
    bd = np.kron(np.eye(GLA_HEADS), np.ones((GLA_DK, GLA_DV)))
    ones_blk = np.kron(np.eye(GLA_HEADS), np.ones((GLA_DV, GLA_DV)))
    b = lambda a: jnp.asarray(a, BF16)
    return dict(lc=lc, nseq=nseq, tril=b(causal), seq_ones=b(same_seq),
                causal4=jnp.asarray(np.tile(causal, (1, GLA_HEADS)), F32), kmask=b(kmask), vmask=b(vmask),
                bd=jnp.asarray(bd, F32), ones_blk=b(ones_blk), ones_rows=jnp.ones((CHUNK, GLA_W), BF16))


def _gla_body(*refs, tl, lc, nseq, has_state):
    (gla_ref, glr_ref, gw_ref, gb_ref, tril_ref, seq1_ref, causal_ref, kmask_ref, vmask_ref, bd_ref,
     ones_blk_ref, ones_rows_ref, ng_ref) = refs[:13]
    n_in = 14 if has_state else 13
    o_ref, sout_ref, s_scr, o_scr = refs[n_in:]
    first = pl.program_id(1) == 0
    bd = bd_ref[...]

    @pl.when(first)
    def _():
        if has_state:
            sin_ref = refs[13]
            for j in range(nseq):
                s2 = sin_ref[j]
                s_scr[j] = jnp.concatenate([s2] * GLA_HEADS, axis=1) * bd
        else:
            s_scr[...] = jnp.zeros_like(s_scr)

    z = _dot3(glr_ref[...], gw_ref[...]) + gb_ref[...]
    lg = (jnp.minimum(z, 0.0) - jnp.log(1.0 + jnp.exp(-jnp.abs(z)))) * (1.0 / GLA_GATE_TEMP)
    sel = _row_select(nseq, lc)
    q0, k0, v0, g0 = 0, GLA_KW, 2 * GLA_KW, 2 * GLA_KW + GLA_W
    b_cum = _dot_exact_lhs(tril_ref[...], lg)
    b_end = _dot_exact_lhs(seq1_ref[...], lg)
    q_in = (gla_ref[:, q0:k0].astype(F32) * jnp.exp(b_cum)).astype(BF16)
    kf = gla_ref[:, k0:v0].astype(F32)
    k_in = (kf * jnp.exp(-b_cum)).astype(BF16)
    k_dec = (kf * jnp.exp(b_end - b_cum)).astype(BF16)
    v = gla_ref[:, v0:g0]
    lg_hi, lg_lo = _hi_lo(lg)
    ones_rows = ones_rows_ref[...]
    chunks = range(tl // CHUNK)

    def rows(a, n):
        return a[n * CHUNK:(n + 1) * CHUNK]

    scores, o, kv, dec = {}, {}, {}, {}
    for n in chunks:
        kn = rows(k_in, n)
        kk = jnp.concatenate([kn * kmask_ref[h:h + 1, :] for h in range(GLA_HEADS)], axis=0)
        scores[n] = (_dot_nt(rows(q_in, n), kk) * causal_ref[...]).astype(BF16)
    for n in chunks:
        vn = rows(v, n)
        vv = jnp.concatenate([vn * vmask_ref[h:h + 1, :] for h in range(GLA_HEADS)], axis=0)
        o[n] = _dot(scores[n], vv)
    for n in chunks:
        for j in range(nseq):
            kd, hi, lo = rows(k_dec, n), rows(lg_hi, n), rows(lg_lo, n)
            if sel[j] is not None:
                kd, hi, lo = kd * sel[j], hi * sel[j], lo * sel[j]
            kv[n, j] = _dot_tn(kd, rows(v, n)) * bd
            dec[n, j] = jnp.exp(_dot_tn(hi, ones_rows) + _dot_tn(lo, ones_rows))
    for n in chunks:
        for j in range(nseq):
            qs = rows(q_in, n)
            s_prev = s_scr[j]
            o[n] = o[n] + _dot(qs if sel[j] is None else qs * sel[j], s_prev.astype(BF16))
            s_scr[j] = s_prev * dec[n, j] + kv[n, j]
        o_scr[n * CHUNK:(n + 1) * CHUNK, :] = o[n]

    o = o_scr[...]
    ms = _dot_exact_rhs(o * o, ones_blk_ref[...]) * (1.0 / GLA_DV)
    gate = gla_ref[:, g0:g0 + GLA_W].astype(F32)
    o_ref[...] = (o * lax.rsqrt(ms + EPS) * ng_ref[...] * gate * _sigmoid(gate)).astype(BF16)
    for j in range(nseq):
        s = s_scr[j]
        sout_ref[j] = (s[:, 0:GLA_DV] + s[:, GLA_DV:2 * GLA_DV]
                       + s[:, 2 * GLA_DV:3 * GLA_DV] + s[:, 3 * GLA_DV:4 * GLA_DV])


def _gla(gla2d, glr2d, gate_w, gate_b, norm_g, state, batch, seq_len):
    c = _gla_consts(seq_len)
    lc, nseq = c["lc"], c["nseq"]
    tl = CHUNK if nseq > 1 else min(ROW_TILE, seq_len)
    nb, nl = batch // nseq, (nseq * seq_len) // tl
    has_state = state is not None
    state_shape = (batch, GLA_KW, GLA_DV)
    state_spec = pl.BlockSpec((nseq,) + state_shape[1:], lambda b, l: (b, 0, 0))
    over_chunks = lambda a: jnp.kron(jnp.eye(tl // CHUNK, dtype=BF16), a)
    consts = [gate_w, gate_b.reshape(1, GLA_KW), over_chunks(c["tril"]), over_chunks(c["seq_ones"]), c["causal4"],
              c["kmask"], c["vmask"],
              c["bd"], c["ones_blk"], c["ones_rows"], jnp.tile(norm_g, GLA_HEADS).reshape(1, GLA_W)]
    row_spec = lambda w: pl.BlockSpec((tl, w), lambda b, l: (b * nl + l, 0))
    in_specs = [row_spec(GLA_COLS), row_spec(GLA_RANK)] + [_const_spec(a.shape) for a in consts]
    args = [gla2d, glr2d] + consts
    if has_state:
        in_specs.append(state_spec)
        args.append(state.reshape(state_shape))
    o, s_new = pl.pallas_call(
        functools.partial(_gla_body, tl=tl, lc=lc, nseq=nseq, has_state=has_state),
        grid=(nb, nl),
        in_specs=in_specs,
        out_specs=[row_spec(GLA_W), state_spec],
        out_shape=[jax.ShapeDtypeStruct((batch * seq_len, GLA_W), BF16), jax.ShapeDtypeStruct(state_shape, F32)],
        scratch_shapes=[pltpu.VMEM((nseq, GLA_KW, GLA_W), F32), pltpu.VMEM((tl, GLA_W), F32)],
        compiler_params=_params(),
        name="gla",
    )(*args)
    return o, s_new.reshape(batch, GLA_HEADS, GLA_DK, GLA_DV)


def _s5_discretize(lam_re, lam_im, log_dt, b_re, b_im, c_re, c_im):
    lr = jnp.minimum(lam_re.astype(F32), -S5_MIN_NEG)
    li = lam_im.astype(F32)
    dt = jnp.exp(log_dt.astype(F32))[:, None]
    mag = jnp.exp(lr * dt)
    ar = mag * jnp.cos(li * dt)
    ai = mag * jnp.sin(li * dt)
    den = lr * lr + li * li
    cr = ((ar - 1.0) * lr + ai * li) / den
    ci = (ai * lr - (ar - 1.0) * li) / den
    b_re, b_im = b_re.astype(F32), b_im.astype(F32)
    bbar_re = cr[..., None] * b_re - ci[..., None] * b_im
    bbar_im = cr[..., None] * b_im + ci[..., None] * b_re
    eye = jnp.eye(S5_GROUPS, dtype=F32)
    in_blk = lambda b: jnp.einsum("gpi,gh->gihp", b, eye).reshape(S5_W, S5_LANES)
    out_blk = lambda c: jnp.einsum("gop,gh->gpho", c.astype(F32), eye).reshape(S5_LANES, S5_W)
    bdb = jnp.concatenate([in_blk(bbar_re), in_blk(bbar_im)], axis=1).astype(BF16)
    bdc = jnp.concatenate([out_blk(c_re), -out_blk(c_im)], axis=0).astype(BF16)
    return ar.reshape(1, S5_LANES), ai.reshape(1, S5_LANES), bdb, bdc


def _s5_body(*refs, tl, batch, has_state):
    u_ref, ar_ref, ai_ref, bdb_ref, bdc_ref, d_ref, gw_ref, gb_ref = refs[:8]
    n_in = 10 if has_state else 8
    o_ref, hr_ref, hi_ref, x_scr, h_scr = refs[n_in:]

    @pl.when(pl.program_id(0) == 0)
    def _():
        if has_state:
            h_scr[0] = refs[8][...]
            h_scr[1] = refs[9][...]
        else:
            h_scr[...] = jnp.zeros_like(h_scr)

    u = u_ref[...]
    x_scr[...] = _dot(u, bdb_ref[...])
    ar = jnp.broadcast_to(ar_ref[...], (batch, S5_LANES))
    ai = jnp.broadcast_to(ai_ref[...], (batch, S5_LANES))

    def step(t, carry):
        hr, hi = carry
        rows = pl.ds(pl.multiple_of(t * batch, batch), batch)
        nr = ar * hr - ai * hi + x_scr[rows, 0:S5_LANES]
        ni = ar * hi + ai * hr + x_scr[rows, S5_LANES:2 * S5_LANES]
        x_scr[rows, 0:S5_LANES] = nr
        x_scr[rows, S5_LANES:2 * S5_LANES] = ni
        return nr, ni

    carry = (h_scr[0], h_scr[1])
    if tl <= V7X_SUBLANES:
        for t in range(tl):
            carry = step(t, carry)
    else:
        carry = lax.fori_loop(0, tl, step, carry, unroll=V7X_SUBLANES)
    h_scr[0], h_scr[1] = carry
    hr_ref[...], hi_ref[...] = carry

    y = _dot(x_scr[...].astype(BF16), bdc_ref[...]) + d_ref[...] * u.astype(F32)
    y = _gelu_tanh(y)
    o_ref[...] = (y * _sigmoid(_dot(y.astype(BF16), gw_ref[...]) + gb_ref[...])).astype(BF16)


def _s5(u_tm, disc, d, glu_w_bf, glu_b, state, batch, seq_len):
    ar, ai, bdb, bdc = disc
    tl = min(S5_TIME_TILE, seq_len)
    has_state = state is not None
    consts = [ar, ai, bdb, bdc, d.reshape(1, S5_W), glu_w_bf, glu_b.reshape(1, S5_W)]
    h_spec = pl.BlockSpec((batch, S5_LANES), lambda i: (0, 0))
    in_specs = [pl.BlockSpec((tl * batch, S5_W), lambda i: (i, 0))] + [_const_spec(a.shape) for a in consts]
    args = [u_tm] + consts
    if has_state:
        in_specs += [h_spec, h_spec]
        args += [state[0].reshape(batch, S5_LANES), state[1].reshape(batch, S5_LANES)]
    o, hr, hi = pl.pallas_call(
        functools.partial(_s5_body, tl=tl, batch=batch, has_state=has_state),
        grid=(seq_len // tl,),
        in_specs=in_specs,
        out_specs=[pl.BlockSpec((tl * batch, S5_W), lambda i: (i, 0)), h_spec, h_spec],
        out_shape=[jax.ShapeDtypeStruct((seq_len * batch, S5_W), BF16),
                   jax.ShapeDtypeStruct((batch, S5_LANES), F32), jax.ShapeDtypeStruct((batch, S5_LANES), F32)],
        scratch_shapes=[pltpu.VMEM((tl * batch, 2 * S5_LANES), F32), pltpu.VMEM((2, batch, S5_LANES), F32)],
        compiler_params=_params(),
        name="s5",
    )(*args)
    shape = (batch, S5_GROUPS, S5_STATE)
    return o, hr.reshape(shape), hi.reshape(shape)


def _ffn_body(*refs, tm, seq_rows, final):
    (x_ref, oret_ref, os5_ref, ogla_ref, wout_ref, gffn_ref, win_ref, cw_ref, cb_ref, wo_ref) = refs[:10]
    n = 10
    gfin_ref = None
    if final:
        gfin_ref, n = refs[n], n + 1
    if seq_rows is not None:
        f1_ref, f2_ref = refs[n:n + 2]
        n += 2
    out_ref, conv_ref, x1_scr, h_scr, carry_scr, act_scr = refs[n:]
    pad = V7X_SUBLANES

    if seq_rows is None:
        @pl.when(pl.program_id(1) == 0)
        def _():
            carry_scr[...] = jnp.zeros_like(carry_scr)
    else:
        carry_scr[...] = jnp.zeros_like(carry_scr)

    x1_scr[...] = (x_ref[...] + _dot(oret_ref[...], wout_ref[0:RET_W, :])
                   + _dot(os5_ref[...], wout_ref[RET_W:RET_W + S5_W, :])
                   + _dot(ogla_ref[...], wout_ref[RET_W + S5_W:D_MODEL, :]))
    h_scr[...] = _rmsnorm_rows(x1_scr[...], gffn_ref[...]).astype(BF16)
    h = h_scr[...]
    if seq_rows is not None:
        t = lax.broadcasted_iota(jnp.int32, (tm, ACT_COLS), 0) % seq_rows

    for c in range(D_FF // ACT_COLS):
        cols = slice(c * ACT_COLS, (c + 1) * ACT_COLS)
        up = _dot(h, win_ref[:, 2 * c * ACT_COLS:2 * (c + 1) * ACT_COLS])
        a_c, gate_c = up[:, 0:ACT_COLS], up[:, ACT_COLS:2 * ACT_COLS]
        ext = jnp.concatenate([carry_scr[:, cols], a_c], axis=0)
        prev1 = ext[pad - 1:pad - 1 + tm]
        prev2 = ext[pad - 2:pad - 2 + tm]
        if seq_rows is not None:
            prev1 = jnp.where(t == 0, f1_ref[:, cols], prev1)
            prev2 = jnp.where(t < 2, f2_ref[:, cols], prev2)
            conv_ref[:, cols] = a_c
        else:
            carry_scr[:, cols] = a_c[tm - pad:tm]
        conv = (cb_ref[:, cols] + prev2 * cw_ref[0:1, cols] + prev1 * cw_ref[1:2, cols]
                + a_c * cw_ref[2:3, cols])
        act_scr[:, cols] = (_gelu_tanh(conv) * gate_c).astype(BF16)

    x2 = x1_scr[...] + _dot(act_scr[...], wo_ref[...])
    out_ref[...] = _rmsnorm_rows(x2, gfin_ref[...]) if final else x2
    if seq_rows is None:
        conv_ref[0] = carry_scr[pad - (CONV_W - 1):pad, :]


def _interleave_up_weights(w_in):
    n = D_FF // ACT_COLS
    w = w_in.astype(BF16)
    a = w[:, 0:D_FF].reshape(D_MODEL, n, ACT_COLS)
    g = w[:, D_FF:2 * D_FF].reshape(D_MODEL, n, ACT_COLS)
    return jnp.stack([a, g], axis=2).reshape(D_MODEL, 2 * D_FF)


def _ffn(x2d, o_ret, o_s5, o_gla, w_out_bf, norm_g, w_in_bf, conv_w, conv_b, w_o_bf, final_g, conv_state,
         batch, seq_len):
    rows = batch * seq_len
    tm = min(ROW_TILE, rows)
    short = conv_state is not None
    final = final_g is not None
    cw = jnp.concatenate([conv_w.astype(F32), jnp.zeros((V7X_SUBLANES - CONV_W, D_FF), F32)], axis=0)
    consts = [w_out_bf, norm_g.reshape(1, D_MODEL), w_in_bf, cw, conv_b.reshape(1, D_FF), w_o_bf]
    big = [True, False, True, False, False, True]
    if final:
        consts.append(final_g.reshape(1, D_MODEL))
        big.append(False)
    if short:
        assert tm % seq_len == 0 and seq_len >= CONV_W - 1
        grid = (rows // tm,)
        imap = lambda i: (i, 0)
        zeros = jnp.zeros((batch, seq_len - 1, D_FF), F32)
        fill1 = jnp.concatenate([conv_state[:, 1:2], zeros], axis=1).reshape(rows, D_FF)
        fill2 = jnp.concatenate([conv_state, zeros[:, 1:]], axis=1).reshape(rows, D_FF)
        extra, extra_specs = [fill1, fill2], [pl.BlockSpec((tm, D_FF), imap)] * 2
        conv_spec = pl.BlockSpec((tm, D_FF), imap)
        conv_shape = jax.ShapeDtypeStruct((rows, D_FF), F32)
    else:
        assert seq_len % tm == 0
        nl = seq_len // tm
        grid = (batch, nl)
        imap = lambda b, l: (b * nl + l, 0)
        extra, extra_specs = [], []
        conv_spec = pl.BlockSpec((1, CONV_W - 1, D_FF), lambda b, l: (b, 0, 0))
        conv_shape = jax.ShapeDtypeStruct((batch, CONV_W - 1, D_FF), F32)
    row_spec = lambda w: pl.BlockSpec((tm, w), imap)
    in_specs = ([row_spec(D_MODEL), row_spec(RET_W), row_spec(S5_W), row_spec(GLA_W)]
                + [_const_spec(a.shape, b) for a, b in zip(consts, big)] + extra_specs)
    out, conv_out = pl.pallas_call(
        functools.partial(_ffn_body, tm=tm, seq_rows=seq_len if short else None, final=final),
        grid=grid,
        in_specs=in_specs,
        out_specs=[row_spec(D_MODEL), conv_spec],
        out_shape=[jax.ShapeDtypeStruct((rows, D_MODEL), F32), conv_shape],
        scratch_shapes=[pltpu.VMEM((tm, D_MODEL), F32), pltpu.VMEM((tm, D_MODEL), BF16),
                        pltpu.VMEM((V7X_SUBLANES, D_FF), F32), pltpu.VMEM((tm, D_FF), BF16)],
        compiler_params=_params(),
        name="ffn",
    )(x2d, o_ret, o_s5, o_gla, *consts, *extra)
    if short:
        conv_out = conv_out.reshape(batch, seq_len, D_FF)[:, seq_len - (CONV_W - 1):]
    return out, conv_out


def _rotary_tables(pos, rows):
    half = RET_HD // 2
    inv = ROPE_BASE ** (-jnp.arange(half, dtype=F32) / half)
    ang = pos.astype(F32)[:, None] * inv[None, :]
    cos = jnp.tile(jnp.concatenate([jnp.cos(ang), jnp.cos(ang)], axis=1), (1, RET_HEADS))
    sin = jnp.tile(jnp.concatenate([-jnp.sin(ang), jnp.sin(ang)], axis=1), (1, RET_HEADS))
    reps = rows // pos.shape[0]
    return jnp.tile(cos, (reps, 1)), jnp.tile(sin, (reps, 1))


def _time_major(a2d, batch, seq_len):
    return a2d.reshape(batch, seq_len, -1).transpose(1, 0, 2).reshape(seq_len * batch, -1)


def _batch_major(a2d, batch, seq_len):
    return a2d.reshape(seq_len, batch, -1).transpose(1, 0, 2).reshape(batch * seq_len, -1)


def _run_group(x, pos, states, layers, final_g):
    batch, seq_len, _ = x.shape
    rows = batch * seq_len
    tm = min(ROW_TILE, rows)
    short = states is not None
    tab_rows = tm if short else seq_len
    cos_tab, sin_tab = _rotary_tables(pos, tab_rows)
    x2d = x.reshape(rows, D_MODEL)
    outs = []
    for li, lp in enumerate(layers):
        st = states[li] if short else None
        ret, su, gla, glr = _mix_in(x2d, lp["norm_mix_g"], lp["w_in"], cos_tab, sin_tab, tm)
        o_ret, s_ret = _retention(ret, lp["ret_norm_g"], lp["ret_norm_b"], st["ret"] if short else None,
                                  batch, seq_len)
        o_s5, s5r, s5i = _s5(_time_major(su, batch, seq_len), lp["s5_disc"], lp["s5_d"], lp["s5_glu_w"],
                             lp["s5_glu_b"], (st["s5r"], st["s5i"]) if short else None, batch, seq_len)
        o_s5 = _batch_major(o_s5, batch, seq_len)
        o_gla, s_gla = _gla(gla, glr, lp["gla_gate_w"], lp["gla_gate_b"], lp["gla_norm_g"],
                            st["gla"] if short else None, batch, seq_len)
        last = li == len(layers) - 1
        x2d, conv_new = _ffn(x2d, o_ret, o_s5, o_gla, lp["w_out"], lp["norm_ffn_g"], lp["ffn_w_in"],
                             lp["ffn_conv_w"], lp["ffn_conv_b"], lp["ffn_w_out"], final_g if last else None,
                             st["conv"] if short else None, batch, seq_len)
        outs.append((s_ret, s5r, s5i, s_gla, conv_new))
    stacked = [jnp.stack([o[i] for o in outs]) for i in range(5)]
    return [x2d.reshape(batch, seq_len, D_MODEL)] + stacked


def kernel(x_prompt, x_sample, state_ret, state_s5_re, state_s5_im, state_gla, state_ffn_conv, norm_mix_g, w_in, ret_norm_g, ret_norm_b, s5_lambda_re, s5_lambda_im, s5_log_dt, s5_b_re, s5_b_im, s5_c_re, s5_c_im, s5_d, s5_glu_w, s5_glu_b, gla_gate_w, gla_gate_b, gla_norm_g, w_out, norm_ffn_g, ffn_w_in, ffn_conv_w, ffn_conv_b, ffn_w_out, norm_final_g):
    depth = w_in.shape[0]
    layers = []
    for l in range(depth):
        layers.append(dict(
            norm_mix_g=norm_mix_g[l], w_in=w_in[l].astype(BF16), ret_norm_g=ret_norm_g[l], ret_norm_b=ret_norm_b[l],
            s5_disc=_s5_discretize(s5_lambda_re[l], s5_lambda_im[l], s5_log_dt[l], s5_b_re[l], s5_b_im[l],
                                   s5_c_re[l], s5_c_im[l]),
            s5_d=s5_d[l], s5_glu_w=s5_glu_w[l].astype(BF16), s5_glu_b=s5_glu_b[l],
            gla_gate_w=gla_gate_w[l], gla_gate_b=gla_gate_b[l], gla_norm_g=gla_norm_g[l],
            w_out=w_out[l].astype(BF16), norm_ffn_g=norm_ffn_g[l], ffn_w_in=_interleave_up_weights(ffn_w_in[l]),
            ffn_conv_w=ffn_conv_w[l], ffn_conv_b=ffn_conv_b[l], ffn_w_out=ffn_w_out[l].astype(BF16)))
    bp, lp_, _ = x_prompt.shape
    bs, ls, _ = x_sample.shape
    pos_prompt = jnp.arange(lp_, dtype=jnp.int32)
    pos_sample = PAST_LEN + jnp.arange(ls, dtype=jnp.int32)
    sample_states = [dict(ret=state_ret[l], s5r=state_s5_re[l], s5i=state_s5_im[l], gla=state_gla[l],
                          conv=state_ffn_conv[l]) for l in range(depth)]
    yp, ret_p, s5r_p, s5i_p, gla_p, conv_p = _run_group(x_prompt, pos_prompt, None, layers, norm_final_g)
    ys, ret_s, s5r_s, s5i_s, gla_s, conv_s = _run_group(x_sample, pos_sample, sample_states, layers, norm_final_g)
    return (yp, ys, ret_p, ret_s, s5r_p, s5r_s, s5i_p, s5i_s, gla_p, gla_s, conv_p, conv_s)
```

```python
import functools

import numpy as np
import jax
import jax.numpy as jnp
from jax import lax
from jax.experimental import pallas as pl
from jax.experimental.pallas import tpu as pltpu

F32, BF16 = jnp.float32, jnp.bfloat16

D_MODEL = 1024
RET_W, S5_W, GLA_W = 384, 256, 384
RET_HEADS, RET_HD = 6, 64
RET_PAIRS = RET_HEADS // 2
S5_GROUPS, S5_CH, S5_STATE = 16, 16, 64
S5_LANES = S5_GROUPS * S5_STATE
S5_MIN_NEG = 1e-4
GLA_HEADS, GLA_DK, GLA_DV, GLA_RANK = 4, 48, 96, 16
GLA_KW = GLA_HEADS * GLA_DK
GLA_GATE_TEMP = 16.0
D_FF = 2816
CONV_W = 3
ROPE_BASE = 10000.0
CHUNK = 64
EPS = 1e-6
PAST_LEN = 16384
IN_COLS = 4 * RET_W + S5_W + 2 * GLA_KW + 2 * GLA_W + GLA_RANK
COL_RET, COL_S5, COL_GLA, COL_LR = 0, 4 * RET_W, 4 * RET_W + S5_W, IN_COLS - GLA_RANK
GLA_COLS = 2 * GLA_KW + 2 * GLA_W

V7X_VMEM_BYTES = 64 * 1024 * 1024
V7X_SUBLANES = 8
V7X_LANES = 128
VMEM_LIMIT = 58 * 1024 * 1024

ROW_TILE = 512
S5_TIME_TILE = 128
ACT_COLS = 256


def _dot(a, b):
    return jnp.dot(a, b, preferred_element_type=F32)


def _dot_nt(a, b):
    return lax.dot_general(a, b, (((1,), (1,)), ((), ())), preferred_element_type=F32)


def _dot_tn(a, b):
    return lax.dot_general(a, b, (((0,), (0,)), ((), ())), preferred_element_type=F32)


def _hi_lo(x):
    hi = x.astype(BF16)
    return hi, (x - hi.astype(F32)).astype(BF16)


def _dot_exact_rhs(x, m):
    hi, lo = _hi_lo(x)
    return _dot(hi, m) + _dot(lo, m)


def _dot_exact_lhs(m, x):
    hi, lo = _hi_lo(x)
    return _dot(m, hi) + _dot(m, lo)


def _dot3(a, b):
    ah, al = _hi_lo(a)
    bh, bl = _hi_lo(b)
    return _dot(ah, bh) + _dot(ah, bl) + _dot(al, bh)


def _sigmoid(x):
    return 1.0 / (1.0 + jnp.exp(-x))


def _gelu_tanh(x):
    return 0.5 * x * (1.0 + jnp.tanh(0.7978845608028654 * (x + 0.044715 * (x * x * x))))


def _rmsnorm_rows(x, g):
    return x * lax.rsqrt(jnp.mean(x * x, axis=-1, keepdims=True) + EPS) * g


def _const_spec(shape, single_buffer=False):
    nd = len(shape)
    if single_buffer:
        return pl.BlockSpec(shape, lambda *_: (0,) * nd, pipeline_mode=pl.Buffered(1))
    return pl.BlockSpec(shape, lambda *_: (0,) * nd)


def _layer_spec(stacked_shape, layer, single_buffer=False):
    nd = len(stacked_shape) - 1
    mode = dict(pipeline_mode=pl.Buffered(1)) if single_buffer else {}
    return pl.BlockSpec((None,) + tuple(stacked_shape[1:]), lambda *_: (layer,) + (0,) * nd, **mode)


def _params():
    return pltpu.CompilerParams(vmem_limit_bytes=VMEM_LIMIT)


def _mix_in_body(x_ref, g_ref, w_ref, cos_ref, sin_ref, ret_ref, su_ref, gla_ref, glr_ref):
    h = _rmsnorm_rows(x_ref[...], g_ref[...]).astype(BF16)
    ret = _dot(h, w_ref[:, COL_RET:COL_S5])
    cos = cos_ref[...]
    sin = sin_ref[...]
    lane = lax.broadcasted_iota(jnp.int32, cos.shape, 1)
    first_half = (lane & (RET_HD // 2)) == 0

    def rotary(z):
        swapped = jnp.where(first_half, pltpu.roll(z, RET_W - RET_HD // 2, 1), pltpu.roll(z, RET_HD // 2, 1))
        return z * cos + swapped * sin

    ret_ref[:, 0:RET_W] = rotary(ret[:, 0:RET_W]).astype(BF16)
    ret_ref[:, RET_W:2 * RET_W] = (rotary(ret[:, RET_W:2 * RET_W]) * RET_HD ** -0.5).astype(BF16)
    ret_ref[:, 2 * RET_W:] = ret[:, 2 * RET_W:].astype(BF16)
    su_ref[...] = _dot(h, w_ref[:, COL_S5:COL_GLA]).astype(BF16)
    gla = _dot(h, w_ref[:, COL_GLA:COL_LR])
    glane = lax.broadcasted_iota(jnp.int32, (1, GLA_COLS), 1)
    gla_ref[...] = (gla * jnp.where(glane < GLA_KW, GLA_DK ** -0.5, 1.0)).astype(BF16)
    glr_ref[...] = _dot(h, w_ref[:, COL_LR:IN_COLS])


def _mix_in(x2d, norm_g, w_in_bf, layer, cos_tab, sin_tab, tm):
    rows = x2d.shape[0]
    n_tab = cos_tab.shape[0] // tm
    row_spec = lambda w: pl.BlockSpec((tm, w), lambda i: (i, 0))
    tab_spec = pl.BlockSpec((tm, RET_W), lambda i: (i % n_tab, 0))
    return pl.pallas_call(
        _mix_in_body,
        grid=(rows // tm,),
        in_specs=[row_spec(D_MODEL), _const_spec((1, D_MODEL)), _layer_spec(w_in_bf.shape, layer, True),
                  tab_spec, tab_spec],
        out_specs=[row_spec(4 * RET_W), row_spec(S5_W), row_spec(GLA_COLS), row_spec(GLA_RANK)],
        out_shape=[jax.ShapeDtypeStruct((rows, 4 * RET_W), BF16), jax.ShapeDtypeStruct((rows, S5_W), BF16),
                   jax.ShapeDtypeStruct((rows, GLA_COLS), BF16), jax.ShapeDtypeStruct((rows, GLA_RANK), F32)],
        compiler_params=_params(),
        name="mix_in",
    )(x2d, norm_g.reshape(1, D_MODEL), w_in_bf, cos_tab, sin_tab)


def _chunk_geometry(seq_len):
    lc = min(CHUNK, seq_len)
    assert CHUNK % lc == 0 and seq_len % lc == 0
    rows = np.arange(CHUNK)
    return lc, CHUNK // lc, rows // lc, rows % lc


def _row_select(nseq, lc):
    if nseq == 1:
        return [None]
    row = lax.broadcasted_iota(jnp.int32, (CHUNK, 1), 0)
    return [jnp.where((row >= j * lc) & (row < (j + 1) * lc), 1.0, 0.0).astype(BF16) for j in range(nseq)]


def _retention_consts(seq_len):
    lc, nseq, seq, t = _chunk_geometry(seq_len)
    gam = 1.0 - 2.0 ** (-5.0 - np.arange(RET_HEADS))
    same = (seq[:, None] == seq[None, :]) & (t[:, None] >= t[None, :])
    diff = np.maximum(t[:, None] - t[None, :], 0)
    dmask = np.zeros((RET_PAIRS, CHUNK, 2 * CHUNK))
    cdec = np.zeros((RET_PAIRS, 2 * RET_HD, 2 * RET_HD))
    for p in range(RET_PAIRS):
        for s in range(2):
            g = gam[2 * p + s]
            dmask[p, :, s * CHUNK:(s + 1) * CHUNK] = np.where(same, g ** diff, 0.0)
            cdec[p, s * RET_HD:(s + 1) * RET_HD, s * RET_HD:(s + 1) * RET_HD] = g ** lc
    lane_gam = np.repeat(gam, RET_HD)[None, :]
    qdec = lane_gam ** (t[:, None] + 1.0)
    kdec = lane_gam ** (lc - 1.0 - t[:, None])
    bd = (cdec[0] > 0).astype(np.float32)
    head_mask = np.stack([np.arange(2 * RET_HD) < RET_HD, np.arange(2 * RET_HD) >= RET_HD]).astype(np.float32)
    ones_blk = np.kron(np.eye(RET_HEADS), np.ones((RET_HD, RET_HD)))
    f = lambda a: jnp.asarray(a, F32)
    return dict(lc=lc, nseq=nseq, dmask=f(dmask), cdec=f(cdec), qdec=f(qdec), kdec=f(kdec), bd=f(bd),
                head_mask=jnp.asarray(head_mask, BF16), ones_blk=jnp.asarray(ones_blk, BF16))


def _retention_body(*refs, tl, lc, nseq, has_state):
    (ret_ref, dmask_ref, cdec_ref, qdec_ref, kdec_ref, bd_ref, hm_ref, ones_ref, lng_ref, lnb_ref) = refs[:10]
    n_in = 11 if has_state else 10
    o_ref, sout_ref, s_scr, o_scr = refs[n_in:]
    first = pl.program_id(1) == 0
    bd = bd_ref[...]

    @pl.when(first)
    def _():
        if has_state:
            sin_ref = refs[10]
            for j in range(nseq):
                for p in range(RET_PAIRS):
                    s2 = jnp.concatenate([sin_ref[j, 2 * p], sin_ref[j, 2 * p + 1]], axis=0)
                    s_scr[j * RET_PAIRS + p] = jnp.concatenate([s2, s2], axis=1) * bd
        else:
            s_scr[...] = jnp.zeros_like(s_scr)

    m0 = hm_ref[0:1, :]
    m1 = hm_ref[1:2, :]
    sel = _row_select(nseq, lc)
    q = ret_ref[:, 0:RET_W]
    k = ret_ref[:, RET_W:2 * RET_W]
    v = ret_ref[:, 2 * RET_W:3 * RET_W]
    q_start = (q.astype(F32) * qdec_ref[...]).astype(BF16)
    k_end = (k.astype(F32) * kdec_ref[...]).astype(BF16)
    blocks = [(n, p) for n in range(tl // CHUNK) for p in range(RET_PAIRS)]

    def part(a, n, p):
        return a[n * CHUNK:(n + 1) * CHUNK, 2 * RET_HD * p:2 * RET_HD * (p + 1)]

    scores, o, kv = {}, {}, {}
    for n, p in blocks:
        kp = part(k, n, p)
        kk = jnp.concatenate([kp * m0, kp * m1], axis=0)
        scores[n, p] = (_dot_nt(part(q, n, p), kk) * dmask_ref[p]).astype(BF16)
    for n, p in blocks:
        vp = part(v, n, p)
        o[n, p] = _dot(scores[n, p], jnp.concatenate([vp * m0, vp * m1], axis=0))
    for n, p in blocks:
        for j in range(nseq):
            ke = part(k_end, n, p)
            kv[n, p, j] = _dot_tn(ke if sel[j] is None else ke * sel[j], part(v, n, p)) * bd
    for n, p in blocks:
        for j in range(nseq):
            qs = part(q_start, n, p)
            s_prev = s_scr[j * RET_PAIRS + p]
            o[n, p] = o[n, p] + _dot(qs if sel[j] is None else qs * sel[j], s_prev.astype(BF16))
            s_scr[j * RET_PAIRS + p] = s_prev * cdec_ref[p] + kv[n, p, j]
        o_scr[n * CHUNK:(n + 1) * CHUNK, 2 * RET_HD * p:2 * RET_HD * (p + 1)] = o[n, p]

    o = o_scr[...]
    ones_blk = ones_ref[...]
    mu = _dot_exact_rhs(o, ones_blk) * (1.0 / RET_HD)
    cen = o - mu
    var = _dot_exact_rhs(cen * cen, ones_blk) * (1.0 / RET_HD)
    y = cen * lax.rsqrt(var + EPS) * lng_ref[...] + lnb_ref[...]
    gate = ret_ref[:, 3 * RET_W:4 * RET_W].astype(F32)
    o_ref[...] = (y * gate * _sigmoid(gate)).astype(BF16)
    for j in range(nseq):
        for p in range(RET_PAIRS):
            s = s_scr[j * RET_PAIRS + p]
            sout_ref[j, 2 * p] = s[0:RET_HD, 0:RET_HD]
            sout_ref[j, 2 * p + 1] = s[RET_HD:2 * RET_HD, RET_HD:2 * RET_HD]


def _retention(ret2d, ln_g, ln_b, state, layer, batch, seq_len):
    c = _retention_consts(seq_len)
    lc, nseq = c["lc"], c["nseq"]
    tl = CHUNK if nseq > 1 else min(ROW_TILE, seq_len)
    nb, nl = batch // nseq, (nseq * seq_len) // tl
    has_state = state is not None
    state_shape = (batch, RET_HEADS, RET_HD, RET_HD)
    state_spec = pl.BlockSpec((nseq,) + state_shape[1:], lambda b, l: (b, 0, 0, 0))
    tile_rows = lambda a: jnp.tile(a, (tl // CHUNK, 1))
    consts = [c["dmask"], c["cdec"], tile_rows(c["qdec"]), tile_rows(c["kdec"]), c["bd"], c["head_mask"], c["ones_blk"],
              ln_g.reshape(1, RET_W), ln_b.reshape(1, RET_W)]
    in_specs = [pl.BlockSpec((tl, 4 * RET_W), lambda b, l: (b * nl + l, 0))] + [_const_spec(a.shape) for a in consts]
    args = [ret2d] + consts
    if has_state:
        in_specs.append(pl.BlockSpec((None, nseq) + state_shape[1:], lambda b, l: (layer, b, 0, 0, 0)))
        args.append(state)
    o, s_new = pl.pallas_call(
        functools.partial(_retention_body, tl=tl, lc=lc, nseq=nseq, has_state=has_state),
        grid=(nb, nl),
        in_specs=in_specs,
        out_specs=[pl.BlockSpec((tl, RET_W), lambda b, l: (b * nl + l, 0)), state_spec],
        out_shape=[jax.ShapeDtypeStruct((batch * seq_len, RET_W), BF16), jax.ShapeDtypeStruct(state_shape, F32)],
        scratch_shapes=[pltpu.VMEM((nseq * RET_PAIRS, 2 * RET_HD, 2 * RET_HD), F32), pltpu.VMEM((tl, RET_W), F32)],
        compiler_params=_params(),
        name="retention",
    )(*args)
    return o, s_new


def _gla_consts(seq_len):
    lc, nseq, seq, t = _chunk_geometry(seq_len)
    same_seq = seq[:, None] == seq[None, :]
    causal = same_seq & (t[:, None] >= t[None, :])
    kmask = np.kron(np.eye(GLA_HEADS), np.ones((1, GLA_DK)))
    vmask = np.kron(np.eye(GLA_HEADS), np.ones((1, GLA_DV)))
    bd = np.kron(np.eye(GLA_HEADS), np.ones((GLA_DK, GLA_DV)))
    ones_blk = np.kron(np.eye(GLA_HEADS), np.ones((GLA_DV, GLA_DV)))
    b = lambda a: jnp.asarray(a, BF16)
    return dict(lc=lc, nseq=nseq, tril=b(causal), seq_ones=b(same_seq),
                causal4=jnp.asarray(np.tile(causal, (1, GLA_HEADS)), F32), kmask=b(kmask), vmask=b(vmask),
                bd=jnp.asarray(bd, F32), ones_blk=b(ones_blk), seq_of_row=seq)


def _gla_body(*refs, tl, lc, nseq, has_state):
    (gla_ref, glr_ref, gw_ref, gb_ref, tril_ref, seq1_ref, causal_ref, kmask_ref, vmask_ref, bd_ref,
     ones_blk_ref, block_ref, ng_ref) = refs[:13]
    n_in = 14 if has_state else 13
    o_ref, sout_ref, s_scr, o_scr = refs[n_in:]
    first = pl.program_id(1) == 0
    bd = bd_ref[...]

    @pl.when(first)
    def _():
        if has_state:
            sin_ref = refs[13]
            for j in range(nseq):
                s2 = jnp.concatenate([sin_ref[j, h] for h in range(GLA_HEADS)], axis=0)
                s_scr[j] = jnp.concatenate([s2] * GLA_HEADS, axis=1) * bd
        else:
            s_scr[...] = jnp.zeros_like(s_scr)

    z = _dot3(glr_ref[...], gw_ref[...]) + gb_ref[...]
    lg = (jnp.minimum(z, 0.0) - jnp.log(1.0 + jnp.exp(-jnp.abs(z)))) * (1.0 / GLA_GATE_TEMP)
    sel = _row_select(nseq, lc)
    q0, k0, v0, g0 = 0, GLA_KW, 2 * GLA_KW, 2 * GLA_KW + GLA_W
    b_cum = _dot_exact_lhs(tril_ref[...], lg)
    b_end = _dot_exact_lhs(seq1_ref[...], lg)
    q_in = (gla_ref[:, q0:k0].astype(F32) * jnp.exp(b_cum)).astype(BF16)
    kf = gla_ref[:, k0:v0].astype(F32)
    k_in = (kf * jnp.exp(-b_cum)).astype(BF16)
    k_dec = (kf * jnp.exp(b_end - b_cum)).astype(BF16)
    v = gla_ref[:, v0:g0]
    lg_hi, lg_lo = _hi_lo(lg)
    block_decay = jnp.exp(_dot_tn(lg_hi, block_ref[...]) + _dot_tn(lg_lo, block_ref[...]))
    chunks = range(tl // CHUNK)

    def rows(a, n):
        return a[n * CHUNK:(n + 1) * CHUNK]

    scores, o, kv, dec = {}, {}, {}, {}
    for n in chunks:
        kn = rows(k_in, n)
        kk = jnp.concatenate([kn * kmask_ref[h:h + 1, :] for h in range(GLA_HEADS)], axis=0)
        scores[n] = (_dot_nt(rows(q_in, n), kk) * causal_ref[...]).astype(BF16)
    for n in chunks:
        vn = rows(v, n)
        vv = jnp.concatenate([vn * vmask_ref[h:h + 1, :] for h in range(GLA_HEADS)], axis=0)
        o[n] = _dot(scores[n], vv)
    for n in chunks:
        for j in range(nseq):
            kd = rows(k_dec, n)
            kv[n, j] = _dot_tn(kd if sel[j] is None else kd * sel[j], rows(v, n)) * bd
            blk = n * nseq + j
            dec[n, j] = jnp.broadcast_to(block_decay[:, blk:blk + 1], (GLA_KW, GLA_W))
    for n in chunks:
        for j in range(nseq):
            qs = rows(q_in, n)
            s_prev = s_scr[j]
            o[n] = o[n] + _dot(qs if sel[j] is None else qs * sel[j], s_prev.astype(BF16))
            s_scr[j] = s_prev * dec[n, j] + kv[n, j]
        o_scr[n * CHUNK:(n + 1) * CHUNK, :] = o[n]

    o = o_scr[...]
    ms = _dot_exact_rhs(o * o, ones_blk_ref[...]) * (1.0 / GLA_DV)
    gate = gla_ref[:, g0:g0 + GLA_W].astype(F32)
    o_ref[...] = (o * lax.rsqrt(ms + EPS) * ng_ref[...] * gate * _sigmoid(gate)).astype(BF16)
    for j in range(nseq):
        s = s_scr[j]
        for h in range(GLA_HEADS):
            sout_ref[j, h] = s[h * GLA_DK:(h + 1) * GLA_DK, h * GLA_DV:(h + 1) * GLA_DV]


def _gla(gla2d, glr2d, gate_w, gate_b, norm_g, state, layer, batch, seq_len):
    c = _gla_consts(seq_len)
    lc, nseq = c["lc"], c["nseq"]
    tl = CHUNK if nseq > 1 else min(ROW_TILE, seq_len)
    nb, nl = batch // nseq, (nseq * seq_len) // tl
    has_state = state is not None
    state_shape = (batch, GLA_HEADS, GLA_DK, GLA_DV)
    state_spec = pl.BlockSpec((nseq,) + state_shape[1:], lambda b, l: (b, 0, 0, 0))
    over_chunks = lambda a: jnp.kron(jnp.eye(tl // CHUNK, dtype=BF16), a)
    block_of_row = (np.arange(tl) // CHUNK) * nseq + np.tile(c["seq_of_row"], tl // CHUNK)
    assert (tl // CHUNK) * nseq <= V7X_LANES
    block = block_of_row[:, None] == np.arange(V7X_LANES)[None, :]
    consts = [gate_w, gate_b.reshape(1, GLA_KW), over_chunks(c["tril"]), over_chunks(c["seq_ones"]), c["causal4"],
              c["kmask"], c["vmask"],
              c["bd"], c["ones_blk"], jnp.asarray(block, BF16), jnp.tile(norm_g, GLA_HEADS).reshape(1, GLA_W)]
    row_spec = lambda w: pl.BlockSpec((tl, w), lambda b, l: (b * nl + l, 0))
    in_specs = [row_spec(GLA_COLS), row_spec(GLA_RANK)] + [_const_spec(a.shape) for a in consts]
    args = [gla2d, glr2d] + consts
    if has_state:
        in_specs.append(pl.BlockSpec((None, nseq) + state_shape[1:], lambda b, l: (layer, b, 0, 0, 0)))
        args.append(state)
    o, s_new = pl.pallas_call(
        functools.partial(_gla_body, tl=tl, lc=lc, nseq=nseq, has_state=has_state),
        grid=(nb, nl),
        in_specs=in_specs,
        out_specs=[row_spec(GLA_W), state_spec],
        out_shape=[jax.ShapeDtypeStruct((batch * seq_len, GLA_W), BF16), jax.ShapeDtypeStruct(state_shape, F32)],
        scratch_shapes=[pltpu.VMEM((nseq, GLA_KW, GLA_W), F32), pltpu.VMEM((tl, GLA_W), F32)],
        compiler_params=_params(),
        name="gla",
    )(*args)
    return o, s_new


def _s5_discretize(lam_re, lam_im, log_dt, b_re, b_im, c_re, c_im):
    lr = jnp.minimum(lam_re.astype(F32), -S5_MIN_NEG)
    li = lam_im.astype(F32)
    dt = jnp.exp(log_dt.astype(F32))[:, None]
    mag = jnp.exp(lr * dt)
    ar = mag * jnp.cos(li * dt)
    ai = mag * jnp.sin(li * dt)
    den = lr * lr + li * li
    cr = ((ar - 1.0) * lr + ai * li) / den
    ci = (ai * lr - (ar - 1.0) * li) / den
    b_re, b_im = b_re.astype(F32), b_im.astype(F32)
    bbar_re = cr[..., None] * b_re - ci[..., None] * b_im
    bbar_im = cr[..., None] * b_im + ci[..., None] * b_re
    eye = jnp.eye(S5_GROUPS, dtype=F32)
    in_blk = lambda b: jnp.einsum("gpi,gh->gihp", b, eye).reshape(S5_W, S5_LANES)
    out_blk = lambda c: jnp.einsum("gop,gh->gpho", c.astype(F32), eye).reshape(S5_LANES, S5_W)
    bdb = jnp.concatenate([in_blk(bbar_re), in_blk(bbar_im)], axis=1).astype(BF16)
    bdc = jnp.concatenate([out_blk(c_re), -out_blk(c_im)], axis=0).astype(BF16)
    return ar.reshape(1, S5_LANES), ai.reshape(1, S5_LANES), bdb, bdc


def _s5_body(*refs, tl, batch, has_state):
    u_ref, ar_ref, ai_ref, bdb_ref, bdc_ref, d_ref, gw_ref, gb_ref = refs[:8]
    n_in = 10 if has_state else 8
    o_ref, hr_ref, hi_ref, x_scr, h_scr = refs[n_in:]

    @pl.when(pl.program_id(0) == 0)
    def _():
        if has_state:
            h_scr[0] = refs[8][...]
            h_scr[1] = refs[9][...]
        else:
            h_scr[...] = jnp.zeros_like(h_scr)

    u = u_ref[...]
    x_scr[...] = _dot(u, bdb_ref[...])
    ar = jnp.broadcast_to(ar_ref[...], (batch, S5_LANES))
    ai = jnp.broadcast_to(ai_ref[...], (batch, S5_LANES))

    def step(t, carry):
        hr, hi = carry
        rows = pl.ds(pl.multiple_of(t * batch, batch), batch)
        nr = ar * hr - ai * hi + x_scr[rows, 0:S5_LANES]
        ni = ar * hi + ai * hr + x_scr[rows, S5_LANES:2 * S5_LANES]
        x_scr[rows, 0:S5_LANES] = nr
        x_scr[rows, S5_LANES:2 * S5_LANES] = ni
        return nr, ni

    carry = (h_scr[0], h_scr[1])
    if tl <= V7X_SUBLANES:
        for t in range(tl):
            carry = step(t, carry)
    else:
        carry = lax.fori_loop(0, tl, step, carry, unroll=V7X_SUBLANES)
    h_scr[0], h_scr[1] = carry
    hr_ref[...], hi_ref[...] = carry

    y = _dot(x_scr[...].astype(BF16), bdc_ref[...]) + d_ref[...] * u.astype(F32)
    y = _gelu_tanh(y)
    o_ref[...] = (y * _sigmoid(_dot(y.astype(BF16), gw_ref[...]) + gb_ref[...])).astype(BF16)


def _s5(u_tm, disc, d, glu_w_bf, glu_b, state, batch, seq_len):
    ar, ai, bdb, bdc = disc
    tl = min(S5_TIME_TILE, seq_len)
    has_state = state is not None
    consts = [ar, ai, bdb, bdc, d.reshape(1, S5_W), glu_w_bf, glu_b.reshape(1, S5_W)]
    h_spec = pl.BlockSpec((batch, S5_LANES), lambda i: (0, 0))
    in_specs = [pl.BlockSpec((tl * batch, S5_W), lambda i: (i, 0))] + [_const_spec(a.shape) for a in consts]
    args = [u_tm] + consts
    if has_state:
        in_specs += [h_spec, h_spec]
        args += [state[0].reshape(batch, S5_LANES), state[1].reshape(batch, S5_LANES)]
    o, hr, hi = pl.pallas_call(
        functools.partial(_s5_body, tl=tl, batch=batch, has_state=has_state),
        grid=(seq_len // tl,),
        in_specs=in_specs,
        out_specs=[pl.BlockSpec((tl * batch, S5_W), lambda i: (i, 0)), h_spec, h_spec],
        out_shape=[jax.ShapeDtypeStruct((seq_len * batch, S5_W), BF16),
                   jax.ShapeDtypeStruct((batch, S5_LANES), F32), jax.ShapeDtypeStruct((batch, S5_LANES), F32)],
        scratch_shapes=[pltpu.VMEM((tl * batch, 2 * S5_LANES), F32), pltpu.VMEM((2, batch, S5_LANES), F32)],
        compiler_params=_params(),
        name="s5",
    )(*args)
    shape = (batch, S5_GROUPS, S5_STATE)
    return o, hr.reshape(shape), hi.reshape(shape)


def _ffn_body(*refs, tm, seq_rows, final):
    (x_ref, oret_ref, os5_ref, ogla_ref, wout_ref, gffn_ref, win_ref, cw_ref, cb_ref, wo_ref) = refs[:10]
    n = 10
    gfin_ref = None
    if final:
        gfin_ref, n = refs[n], n + 1
    if seq_rows is not None:
        f1_ref, f2_ref = refs[n:n + 2]
        n += 2
    out_ref, conv_ref, x1_scr, h_scr, carry_scr, act_scr = refs[n:]
    pad = V7X_SUBLANES

    if seq_rows is None:
        @pl.when(pl.program_id(1) == 0)
        def _():
            carry_scr[...] = jnp.zeros_like(carry_scr)
    else:
        carry_scr[...] = jnp.zeros_like(carry_scr)

    x1_scr[...] = (x_ref[...] + _dot(oret_ref[...], wout_ref[0:RET_W, :])
                   + _dot(os5_ref[...], wout_ref[RET_W:RET_W + S5_W, :])
                   + _dot(ogla_ref[...], wout_ref[RET_W + S5_W:D_MODEL, :]))
    h_scr[...] = _rmsnorm_rows(x1_scr[...], gffn_ref[...]).astype(BF16)
    h = h_scr[...]
    if seq_rows is not None:
        t = lax.broadcasted_iota(jnp.int32, (tm, ACT_COLS), 0) % seq_rows

    for c in range(D_FF // ACT_COLS):
        cols = slice(c * ACT_COLS, (c + 1) * ACT_COLS)
        a_c = _dot(h, win_ref[:, c * ACT_COLS:(c + 1) * ACT_COLS])
        gate_c = _dot(h, win_ref[:, D_FF + c * ACT_COLS:D_FF + (c + 1) * ACT_COLS])
        ext = jnp.concatenate([carry_scr[:, cols], a_c], axis=0)
        prev1 = ext[pad - 1:pad - 1 + tm]
        prev2 = ext[pad - 2:pad - 2 + tm]
        if seq_rows is not None:
            prev1 = jnp.where(t == 0, f1_ref[:, cols], prev1)
            prev2 = jnp.where(t < 2, f2_ref[:, cols], prev2)
            conv_ref[:, cols] = a_c
        else:
            carry_scr[:, cols] = a_c[tm - pad:tm]
        conv = (cb_ref[:, cols] + prev2 * cw_ref[0:1, cols] + prev1 * cw_ref[1:2, cols]
                + a_c * cw_ref[2:3, cols])
        act_scr[:, cols] = (_gelu_tanh(conv) * gate_c).astype(BF16)

    x2 = x1_scr[...] + _dot(act_scr[...], wo_ref[...])
    out_ref[...] = _rmsnorm_rows(x2, gfin_ref[...]) if final else x2
    if seq_rows is None:
        conv_ref[0] = carry_scr[pad - (CONV_W - 1):pad, :]


def _ffn(x2d, o_ret, o_s5, o_gla, w_out_bf, norm_g, w_in_bf, conv_w, conv_b, w_o_bf, layer, final_g, conv_state,
         batch, seq_len):
    rows = batch * seq_len
    tm = min(ROW_TILE, rows)
    short = conv_state is not None
    final = final_g is not None
    cw = jnp.concatenate([conv_w.astype(F32), jnp.zeros((V7X_SUBLANES - CONV_W, D_FF), F32)], axis=0)
    consts = [w_out_bf, norm_g.reshape(1, D_MODEL), w_in_bf, cw, conv_b.reshape(1, D_FF), w_o_bf]
    stacked = [True, False, True, False, False, True]
    if final:
        consts.append(final_g.reshape(1, D_MODEL))
        stacked.append(False)
    if short:
        assert tm % seq_len == 0 and seq_len >= CONV_W - 1
        grid = (rows // tm,)
        imap = lambda i: (i, 0)
        zeros = jnp.zeros((batch, seq_len - 1, D_FF), F32)
        fill1 = jnp.concatenate([conv_state[:, 1:2], zeros], axis=1).reshape(rows, D_FF)
        fill2 = jnp.concatenate([conv_state, zeros[:, 1:]], axis=1).reshape(rows, D_FF)
        extra, extra_specs = [fill1, fill2], [pl.BlockSpec((tm, D_FF), imap)] * 2
        conv_spec = pl.BlockSpec((tm, D_FF), imap)
        conv_shape = jax.ShapeDtypeStruct((rows, D_FF), F32)
    else:
        assert seq_len % tm == 0
        nl = seq_len // tm
        grid = (batch, nl)
        imap = lambda b, l: (b * nl + l, 0)
        extra, extra_specs = [], []
        conv_spec = pl.BlockSpec((1, CONV_W - 1, D_FF), lambda b, l: (b, 0, 0))
        conv_shape = jax.ShapeDtypeStruct((batch, CONV_W - 1, D_FF), F32)
    row_spec = lambda w: pl.BlockSpec((tm, w), imap)
    in_specs = ([row_spec(D_MODEL), row_spec(RET_W), row_spec(S5_W), row_spec(GLA_W)]
                + [_layer_spec(a.shape, layer, True) if s else _const_spec(a.shape) for a, s in zip(consts, stacked)]
                + extra_specs)
    out, conv_out = pl.pallas_call(
        functools.partial(_ffn_body, tm=tm, seq_rows=seq_len if short else None, final=final),
        grid=grid,
        in_specs=in_specs,
        out_specs=[row_spec(D_MODEL), conv_spec],
        out_shape=[jax.ShapeDtypeStruct((rows, D_MODEL), F32), conv_shape],
        scratch_shapes=[pltpu.VMEM((tm, D_MODEL), F32), pltpu.VMEM((tm, D_MODEL), BF16),
                        pltpu.VMEM((V7X_SUBLANES, D_FF), F32), pltpu.VMEM((tm, D_FF), BF16)],
        compiler_params=_params(),
        name="ffn",
    )(x2d, o_ret, o_s5, o_gla, *consts, *extra)
    if short:
        conv_out = conv_out.reshape(batch, seq_len, D_FF)[:, seq_len - (CONV_W - 1):]
    return out, conv_out


def _rotary_tables(pos, rows):
    half = RET_HD // 2
    inv = ROPE_BASE ** (-jnp.arange(half, dtype=F32) / half)
    ang = pos.astype(F32)[:, None] * inv[None, :]
    cos = jnp.tile(jnp.concatenate([jnp.cos(ang), jnp.cos(ang)], axis=1), (1, RET_HEADS))
    sin = jnp.tile(jnp.concatenate([-jnp.sin(ang), jnp.sin(ang)], axis=1), (1, RET_HEADS))
    reps = rows // pos.shape[0]
    return jnp.tile(cos, (reps, 1)), jnp.tile(sin, (reps, 1))


def _time_major(a2d, batch, seq_len):
    return a2d.reshape(batch, seq_len, -1).transpose(1, 0, 2).reshape(seq_len * batch, -1)


def _batch_major(a2d, batch, seq_len):
    return a2d.reshape(seq_len, batch, -1).transpose(1, 0, 2).reshape(batch * seq_len, -1)


def _run_group(x, pos, states, layers, big, final_g):
    batch, seq_len, _ = x.shape
    rows = batch * seq_len
    tm = min(ROW_TILE, rows)
    short = states is not None
    tab_rows = tm if short else seq_len
    cos_tab, sin_tab = _rotary_tables(pos, tab_rows)
    x2d = x.reshape(rows, D_MODEL)
    outs = []
    for li, lp in enumerate(layers):
        ret, su, gla, glr = _mix_in(x2d, lp["norm_mix_g"], big["w_in"], li, cos_tab, sin_tab, tm)
        o_ret, s_ret = _retention(ret, lp["ret_norm_g"], lp["ret_norm_b"], states["ret"] if short else None, li,
                                  batch, seq_len)
        s5_state = (states["s5r"][li], states["s5i"][li]) if short else None
        o_s5, s5r, s5i = _s5(_time_major(su, batch, seq_len), lp["s5_disc"], lp["s5_d"], lp["s5_glu_w"],
                             lp["s5_glu_b"], s5_state, batch, seq_len)
        o_s5 = _batch_major(o_s5, batch, seq_len)
        o_gla, s_gla = _gla(gla, glr, lp["gla_gate_w"], lp["gla_gate_b"], lp["gla_norm_g"],
                            states["gla"] if short else None, li, batch, seq_len)
        last = li == len(layers) - 1
        x2d, conv_new = _ffn(x2d, o_ret, o_s5, o_gla, big["w_out"], lp["norm_ffn_g"], big["ffn_w_in"],
                             lp["ffn_conv_w"], lp["ffn_conv_b"], big["ffn_w_out"], li, final_g if last else None,
                             states["conv"][li] if short else None, batch, seq_len)
        outs.append((s_ret, s5r, s5i, s_gla, conv_new))
    stacked = [jnp.stack([o[i] for o in outs]) for i in range(5)]
    return [x2d.reshape(batch, seq_len, D_MODEL)] + stacked


def kernel(x_prompt, x_sample, state_ret, state_s5_re, state_s5_im, state_gla, state_ffn_conv, norm_mix_g, w_in, ret_norm_g, ret_norm_b, s5_lambda_re, s5_lambda_im, s5_log_dt, s5_b_re, s5_b_im, s5_c_re, s5_c_im, s5_d, s5_glu_w, s5_glu_b, gla_gate_w, gla_gate_b, gla_norm_g, w_out, norm_ffn_g, ffn_w_in, ffn_conv_w, ffn_conv_b, ffn_w_out, norm_final_g):
    depth = w_in.shape[0]
    big = dict(w_in=w_in.astype(BF16), w_out=w_out.astype(BF16), ffn_w_in=ffn_w_in.astype(BF16),
               ffn_w_out=ffn_w_out.astype(BF16))
    layers = []
    for l in range(depth):
        layers.append(dict(
            norm_mix_g=norm_mix_g[l], ret_norm_g=ret_norm_g[l], ret_norm_b=ret_norm_b[l],
            s5_disc=_s5_discretize(s5_lambda_re[l], s5_lambda_im[l], s5_log_dt[l], s5_b_re[l], s5_b_im[l],
                                   s5_c_re[l], s5_c_im[l]),
            s5_d=s5_d[l], s5_glu_w=s5_glu_w[l].astype(BF16), s5_glu_b=s5_glu_b[l],
            gla_gate_w=gla_gate_w[l], gla_gate_b=gla_gate_b[l], gla_norm_g=gla_norm_g[l],
            norm_ffn_g=norm_ffn_g[l], ffn_conv_w=ffn_conv_w[l], ffn_conv_b=ffn_conv_b[l]))
    lp_, ls = x_prompt.shape[1], x_sample.shape[1]
    pos_prompt = jnp.arange(lp_, dtype=jnp.int32)
    pos_sample = PAST_LEN + jnp.arange(ls, dtype=jnp.int32)
    sample_states = dict(ret=state_ret, s5r=state_s5_re, s5i=state_s5_im, gla=state_gla, conv=state_ffn_conv)
    yp, ret_p, s5r_p, s5i_p, gla_p, conv_p = _run_group(x_prompt, pos_prompt, None, layers, big, norm_final_g)
    ys, ret_s, s5r_s, s5i_s, gla_s, conv_s = _run_group(x_sample, pos_sample, sample_states, layers, big,
                                                        norm_final_g)
    return (yp, ys, ret_p, ret_s, s5r_p, s5r_s, s5i_p, s5i_s, gla_p, gla_s, conv_p, conv_s)
```

```python
import functools

import numpy as np
import jax
import jax.numpy as jnp
from jax import lax
from jax.experimental import pallas as pl
from jax.experimental.pallas import tpu as pltpu

F32, BF16 = jnp.float32, jnp.bfloat16

D_MODEL = 1024
RET_W, S5_W, GLA_W = 384, 256, 384
RET_HEADS, RET_HD = 6, 64
RET_PAIRS = RET_HEADS // 2
S5_GROUPS, S5_CH, S5_STATE = 16, 16, 64
S5_LANES = S5_GROUPS * S5_STATE
S5_MIN_NEG = 1e-4
GLA_HEADS, GLA_DK, GLA_DV, GLA_RANK = 4, 48, 96, 16
GLA_KW = GLA_HEADS * GLA_DK
GLA_GATE_TEMP = 16.0
D_FF = 2816
CONV_W = 3
ROPE_BASE = 10000.0
CHUNK = 64
EPS = 1e-6
PAST_LEN = 16384
IN_COLS = 4 * RET_W + S5_W + 2 * GLA_KW + 2 * GLA_W + GLA_RANK
COL_RET, COL_S5, COL_GLA, COL_LR = 0, 4 * RET_W, 4 * RET_W + S5_W, IN_COLS - GLA_RANK
GLA_COLS = 2 * GLA_KW + 2 * GLA_W

V7X_VMEM_BYTES = 64 * 1024 * 1024
V7X_SUBLANES = 8
V7X_LANES = 128
VMEM_LIMIT = 58 * 1024 * 1024

ROW_TILE = 512
S5_TIME_TILE = 128
ACT_COLS = 256


def _dot(a, b):
    return jnp.dot(a, b, preferred_element_type=F32)


def _dot_nt(a, b):
    return lax.dot_general(a, b, (((1,), (1,)), ((), ())), preferred_element_type=F32)


def _dot_tn(a, b):
    return lax.dot_general(a, b, (((0,), (0,)), ((), ())), preferred_element_type=F32)


def _hi_lo(x):
    hi = x.astype(BF16)
    return hi, (x - hi.astype(F32)).astype(BF16)


def _dot_exact_lhs(m, x):
    hi, lo = _hi_lo(x)
    return _dot(m, hi) + _dot(m, lo)


def _dot3(a, b):
    ah, al = _hi_lo(a)
    bh, bl = _hi_lo(b)
    return _dot(ah, bh) + _dot(ah, bl) + _dot(al, bh)


def _sigmoid(x):
    return 1.0 / (1.0 + jnp.exp(-x))


def _gelu_tanh(x):
    return 0.5 * x * (1.0 + jnp.tanh(0.7978845608028654 * (x + 0.044715 * (x * x * x))))


def _rmsnorm_rows(x, g):
    return x * lax.rsqrt(jnp.mean(x * x, axis=-1, keepdims=True) + EPS) * g


def _const_spec(shape, single_buffer=False):
    nd = len(shape)
    if single_buffer:
        return pl.BlockSpec(shape, lambda *_: (0,) * nd, pipeline_mode=pl.Buffered(1))
    return pl.BlockSpec(shape, lambda *_: (0,) * nd)


def _layer_spec(stacked_shape, layer, single_buffer=False):
    nd = len(stacked_shape) - 1
    mode = dict(pipeline_mode=pl.Buffered(1)) if single_buffer else {}
    return pl.BlockSpec((None,) + tuple(stacked_shape[1:]), lambda *_: (layer,) + (0,) * nd, **mode)


def _params():
    return pltpu.CompilerParams(vmem_limit_bytes=VMEM_LIMIT)


def _mix_in_body(x_ref, g_ref, w_ref, cos_ref, sin_ref, ret_ref, su_ref, gla_ref, glr_ref):
    h = _rmsnorm_rows(x_ref[...], g_ref[...]).astype(BF16)
    ret = _dot(h, w_ref[:, COL_RET:COL_S5])
    cos = cos_ref[...]
    sin = sin_ref[...]
    lane = lax.broadcasted_iota(jnp.int32, cos.shape, 1)
    first_half = (lane & (RET_HD // 2)) == 0

    def rotary(z):
        swapped = jnp.where(first_half, pltpu.roll(z, RET_W - RET_HD // 2, 1), pltpu.roll(z, RET_HD // 2, 1))
        return z * cos + swapped * sin

    ret_ref[:, 0:RET_W] = rotary(ret[:, 0:RET_W]).astype(BF16)
    ret_ref[:, RET_W:2 * RET_W] = (rotary(ret[:, RET_W:2 * RET_W]) * RET_HD ** -0.5).astype(BF16)
    ret_ref[:, 2 * RET_W:] = ret[:, 2 * RET_W:].astype(BF16)
    su_ref[...] = _dot(h, w_ref[:, COL_S5:COL_GLA]).astype(BF16)
    gla = _dot(h, w_ref[:, COL_GLA:COL_LR])
    glane = lax.broadcasted_iota(jnp.int32, (1, GLA_COLS), 1)
    gla_ref[...] = (gla * jnp.where(glane < GLA_KW, GLA_DK ** -0.5, 1.0)).astype(BF16)
    glr_ref[...] = _dot(h, w_ref[:, COL_LR:IN_COLS])


def _mix_in(x2d, norm_g, w_in_bf, layer, cos_tab, sin_tab, tm):
    rows = x2d.shape[0]
    n_tab = cos_tab.shape[0] // tm
    row_spec = lambda w: pl.BlockSpec((tm, w), lambda i: (i, 0))
    tab_spec = pl.BlockSpec((tm, RET_W), lambda i: (i % n_tab, 0))
    return pl.pallas_call(
        _mix_in_body,
        grid=(rows // tm,),
        in_specs=[row_spec(D_MODEL), _const_spec((1, D_MODEL)), _layer_spec(w_in_bf.shape, layer, True),
                  tab_spec, tab_spec],
        out_specs=[row_spec(4 * RET_W), row_spec(S5_W), row_spec(GLA_COLS), row_spec(GLA_RANK)],
        out_shape=[jax.ShapeDtypeStruct((rows, 4 * RET_W), BF16), jax.ShapeDtypeStruct((rows, S5_W), BF16),
                   jax.ShapeDtypeStruct((rows, GLA_COLS), BF16), jax.ShapeDtypeStruct((rows, GLA_RANK), F32)],
        compiler_params=_params(),
        name="mix_in",
    )(x2d, norm_g.reshape(1, D_MODEL), w_in_bf, cos_tab, sin_tab)


def _chunk_geometry(seq_len):
    lc = min(CHUNK, seq_len)
    assert CHUNK % lc == 0 and seq_len % lc == 0
    rows = np.arange(CHUNK)
    return lc, CHUNK // lc, rows // lc, rows % lc


def _row_select(nseq, lc):
    if nseq == 1:
        return [None]
    row = lax.broadcasted_iota(jnp.int32, (CHUNK, 1), 0)
    return [jnp.where((row >= j * lc) & (row < (j + 1) * lc), 1.0, 0.0).astype(BF16) for j in range(nseq)]


def _retention_consts(seq_len):
    lc, nseq, seq, t = _chunk_geometry(seq_len)
    gam = 1.0 - 2.0 ** (-5.0 - np.arange(RET_HEADS))
    same = (seq[:, None] == seq[None, :]) & (t[:, None] >= t[None, :])
    diff = np.maximum(t[:, None] - t[None, :], 0)
    dmask = np.zeros((RET_PAIRS, CHUNK, 2 * CHUNK))
    cdec = np.zeros((RET_PAIRS, 2 * RET_HD, 2 * RET_HD))
    for p in range(RET_PAIRS):
        for s in range(2):
            g = gam[2 * p + s]
            dmask[p, :, s * CHUNK:(s + 1) * CHUNK] = np.where(same, g ** diff, 0.0)
            cdec[p, s * RET_HD:(s + 1) * RET_HD, s * RET_HD:(s + 1) * RET_HD] = g ** lc
    lane_gam = np.repeat(gam, RET_HD)[None, :]
    qdec = lane_gam ** (t[:, None] + 1.0)
    kdec = lane_gam ** (lc - 1.0 - t[:, None])
    bd = (cdec[0] > 0).astype(np.float32)
    head_mask = np.stack([np.arange(2 * RET_HD) < RET_HD, np.arange(2 * RET_HD) >= RET_HD]).astype(np.float32)
    ones_blk = np.kron(np.eye(RET_HEADS), np.ones((RET_HD, RET_HD)))
    f = lambda a: jnp.asarray(a, F32)
    return dict(lc=lc, nseq=nseq, dmask=f(dmask), cdec=f(cdec), qdec=f(qdec), kdec=f(kdec), bd=f(bd),
                head_mask=jnp.asarray(head_mask, BF16), ones_blk=jnp.asarray(ones_blk, BF16))


def _retention_body(*refs, tl, lc, nseq, has_state):
    (ret_ref, dmask_ref, cdec_ref, qdec_ref, kdec_ref, bd_ref, hm_ref, ones_ref, lng_ref, lnb_ref) = refs[:10]
    n_in = 11 if has_state else 10
    o_ref, sout_ref, s_scr, o_scr = refs[n_in:]
    first = pl.program_id(1) == 0
    bd = bd_ref[...]

    @pl.when(first)
    def _():
        if has_state:
            sin_ref = refs[10]
            for j in range(nseq):
                for p in range(RET_PAIRS):
                    s2 = jnp.concatenate([sin_ref[j, 2 * p], sin_ref[j, 2 * p + 1]], axis=0)
                    s_scr[j * RET_PAIRS + p] = jnp.concatenate([s2, s2], axis=1) * bd
        else:
            s_scr[...] = jnp.zeros_like(s_scr)

    m0 = hm_ref[0:1, :]
    m1 = hm_ref[1:2, :]
    sel = _row_select(nseq, lc)
    q = ret_ref[:, 0:RET_W]
    k = ret_ref[:, RET_W:2 * RET_W]
    v = ret_ref[:, 2 * RET_W:3 * RET_W]
    q_start = (q.astype(F32) * qdec_ref[...]).astype(BF16)
    k_end = (k.astype(F32) * kdec_ref[...]).astype(BF16)
    blocks = [(n, p) for n in range(tl // CHUNK) for p in range(RET_PAIRS)]

    def part(a, n, p):
        return a[n * CHUNK:(n + 1) * CHUNK, 2 * RET_HD * p:2 * RET_HD * (p + 1)]

    scores, o, kv = {}, {}, {}
    for n, p in blocks:
        kp = part(k, n, p)
        kk = jnp.concatenate([kp * m0, kp * m1], axis=0)
        scores[n, p] = (_dot_nt(part(q, n, p), kk) * dmask_ref[p]).astype(BF16)
    for n, p in blocks:
        vp = part(v, n, p)
        o[n, p] = _dot(scores[n, p], jnp.concatenate([vp * m0, vp * m1], axis=0))
    for n, p in blocks:
        for j in range(nseq):
            ke = part(k_end, n, p)
            kv[n, p, j] = _dot_tn(ke if sel[j] is None else ke * sel[j], part(v, n, p)) * bd
    for n, p in blocks:
        for j in range(nseq):
            qs = part(q_start, n, p)
            s_prev = s_scr[j * RET_PAIRS + p]
            o[n, p] = o[n, p] + _dot(qs if sel[j] is None else qs * sel[j], s_prev.astype(BF16))
            s_scr[j * RET_PAIRS + p] = s_prev * cdec_ref[p] + kv[n, p, j]
        o_scr[n * CHUNK:(n + 1) * CHUNK, 2 * RET_HD * p:2 * RET_HD * (p + 1)] = o[n, p]

    o = o_scr[...]
    ones_blk = ones_ref[...]
    mu = _dot(o.astype(BF16), ones_blk) * (1.0 / RET_HD)
    cen = o - mu
    var = _dot((cen * cen).astype(BF16), ones_blk) * (1.0 / RET_HD)
    y = cen * lax.rsqrt(var + EPS) * lng_ref[...] + lnb_ref[...]
    gate = ret_ref[:, 3 * RET_W:4 * RET_W].astype(F32)
    o_ref[...] = (y * gate * _sigmoid(gate)).astype(BF16)
    for j in range(nseq):
        for p in range(RET_PAIRS):
            s = s_scr[j * RET_PAIRS + p]
            sout_ref[j, 2 * p] = s[0:RET_HD, 0:RET_HD]
            sout_ref[j, 2 * p + 1] = s[RET_HD:2 * RET_HD, RET_HD:2 * RET_HD]


def _retention(ret2d, ln_g, ln_b, state, layer, batch, seq_len):
    c = _retention_consts(seq_len)
    lc, nseq = c["lc"], c["nseq"]
    tl = CHUNK if nseq > 1 else min(ROW_TILE, seq_len)
    nb, nl = batch // nseq, (nseq * seq_len) // tl
    has_state = state is not None
    state_shape = (batch, RET_HEADS, RET_HD, RET_HD)
    state_spec = pl.BlockSpec((nseq,) + state_shape[1:], lambda b, l: (b, 0, 0, 0))
    tile_rows = lambda a: jnp.tile(a, (tl // CHUNK, 1))
    consts = [c["dmask"], c["cdec"], tile_rows(c["qdec"]), tile_rows(c["kdec"]), c["bd"], c["head_mask"], c["ones_blk"],
              ln_g.reshape(1, RET_W), ln_b.reshape(1, RET_W)]
    in_specs = [pl.BlockSpec((tl, 4 * RET_W), lambda b, l: (b * nl + l, 0))] + [_const_spec(a.shape) for a in consts]
    args = [ret2d] + consts
    if has_state:
        in_specs.append(pl.BlockSpec((None, nseq) + state_shape[1:], lambda b, l: (layer, b, 0, 0, 0)))
        args.append(state)
    o, s_new = pl.pallas_call(
        functools.partial(_retention_body, tl=tl, lc=lc, nseq=nseq, has_state=has_state),
        grid=(nb, nl),
        in_specs=in_specs,
        out_specs=[pl.BlockSpec((tl, RET_W), lambda b, l: (b * nl + l, 0)), state_spec],
        out_shape=[jax.ShapeDtypeStruct((batch * seq_len, RET_W), BF16), jax.ShapeDtypeStruct(state_shape, F32)],
        scratch_shapes=[pltpu.VMEM((nseq * RET_PAIRS, 2 * RET_HD, 2 * RET_HD), F32), pltpu.VMEM((tl, RET_W), F32)],
        compiler_params=_params(),
        name="retention",
    )(*args)
    return o, s_new


def _gla_consts(seq_len):
    lc, nseq, seq, t = _chunk_geometry(seq_len)
    same_seq = seq[:, None] == seq[None, :]
    causal = same_seq & (t[:, None] >= t[None, :])
    kmask = np.kron(np.eye(GLA_HEADS), np.ones((1, GLA_DK)))
    vmask = np.kron(np.eye(GLA_HEADS), np.ones((1, GLA_DV)))
    bd = np.kron(np.eye(GLA_HEADS), np.ones((GLA_DK, GLA_DV)))
    ones_blk = np.kron(np.eye(GLA_HEADS), np.ones((GLA_DV, GLA_DV)))
    b = lambda a: jnp.asarray(a, BF16)
    return dict(lc=lc, nseq=nseq, tril=b(causal), seq_ones=b(same_seq),
                causal4=jnp.asarray(np.tile(causal, (1, GLA_HEADS)), F32), kmask=b(kmask), vmask=b(vmask),
                bd=jnp.asarray(bd, F32), ones_blk=b(ones_blk), seq_of_row=seq)


def _gla_body(*refs, tl, lc, nseq, has_state):
    (gla_ref, glr_ref, gw_ref, gb_ref, tril_ref, seq1_ref, causal_ref, kmask_ref, vmask_ref, bd_ref,
     ones_blk_ref, block_ref, ng_ref) = refs[:13]
    n_in = 14 if has_state else 13
    o_ref, sout_ref, s_scr, o_scr = refs[n_in:]
    first = pl.program_id(1) == 0
    bd = bd_ref[...]

    @pl.when(first)
    def _():
        if has_state:
            sin_ref = refs[13]
            for j in range(nseq):
                s2 = jnp.concatenate([sin_ref[j, h] for h in range(GLA_HEADS)], axis=0)
                s_scr[j] = jnp.concatenate([s2] * GLA_HEADS, axis=1) * bd
        else:
            s_scr[...] = jnp.zeros_like(s_scr)

    z = _dot3(glr_ref[...], gw_ref[...]) + gb_ref[...]
    lg = (jnp.minimum(z, 0.0) - jnp.log(1.0 + jnp.exp(-jnp.abs(z)))) * (1.0 / GLA_GATE_TEMP)
    sel = _row_select(nseq, lc)
    q0, k0, v0, g0 = 0, GLA_KW, 2 * GLA_KW, 2 * GLA_KW + GLA_W
    cums, ends = [], []
    for n in range(tl // CHUNK):
        lg_n = lg[n * CHUNK:(n + 1) * CHUNK]
        cum = _dot_exact_lhs(tril_ref[...], lg_n)
        cums.append(cum)
        if nseq == 1:
            ends.append(jnp.broadcast_to(cum[CHUNK - 1:CHUNK, :], (CHUNK, GLA_KW)))
        else:
            ends.append(_dot_exact_lhs(seq1_ref[...], lg_n))
    b_cum = jnp.concatenate(cums, axis=0)
    b_end = jnp.concatenate(ends, axis=0)
    q_in = (gla_ref[:, q0:k0].astype(F32) * jnp.exp(b_cum)).astype(BF16)
    kf = gla_ref[:, k0:v0].astype(F32)
    k_in = (kf * jnp.exp(-b_cum)).astype(BF16)
    k_dec = (kf * jnp.exp(b_end - b_cum)).astype(BF16)
    v = gla_ref[:, v0:g0]
    lg_hi, lg_lo = _hi_lo(lg)
    block_decay = jnp.exp(_dot_tn(lg_hi, block_ref[...]) + _dot_tn(lg_lo, block_ref[...]))
    chunks = range(tl // CHUNK)

    def rows(a, n):
        return a[n * CHUNK:(n + 1) * CHUNK]

    scores, o, kv, dec = {}, {}, {}, {}
    for n in chunks:
        kn = rows(k_in, n)
        kk = jnp.concatenate([kn * kmask_ref[h:h + 1, :] for h in range(GLA_HEADS)], axis=0)
        scores[n] = (_dot_nt(rows(q_in, n), kk) * causal_ref[...]).astype(BF16)
    for n in chunks:
        vn = rows(v, n)
        vv = jnp.concatenate([vn * vmask_ref[h:h + 1, :] for h in range(GLA_HEADS)], axis=0)
        o[n] = _dot(scores[n], vv)
    for n in chunks:
        for j in range(nseq):
            kd = rows(k_dec, n)
            kv[n, j] = _dot_tn(kd if sel[j] is None else kd * sel[j], rows(v, n)) * bd
            blk = n * nseq + j
            dec[n, j] = jnp.broadcast_to(block_decay[:, blk:blk + 1], (GLA_KW, GLA_W))
    for n in chunks:
        for j in range(nseq):
            qs = rows(q_in, n)
            s_prev = s_scr[j]
            o[n] = o[n] + _dot(qs if sel[j] is None else qs * sel[j], s_prev.astype(BF16))
            s_scr[j] = s_prev * dec[n, j] + kv[n, j]
        o_scr[n * CHUNK:(n + 1) * CHUNK, :] = o[n]

    o = o_scr[...]
    ms = _dot((o * o).astype(BF16), ones_blk_ref[...]) * (1.0 / GLA_DV)
    gate = gla_ref[:, g0:g0 + GLA_W].astype(F32)
    o_ref[...] = (o * lax.rsqrt(ms + EPS) * ng_ref[...] * gate * _sigmoid(gate)).astype(BF16)
    for j in range(nseq):
        s = s_scr[j]
        for h in range(GLA_HEADS):
            sout_ref[j, h] = s[h * GLA_DK:(h + 1) * GLA_DK, h * GLA_DV:(h + 1) * GLA_DV]


def _gla(gla2d, glr2d, gate_w, gate_b, norm_g, state, layer, batch, seq_len):
    c = _gla_consts(seq_len)
    lc, nseq = c["lc"], c["nseq"]
    tl = CHUNK if nseq > 1 else min(ROW_TILE, seq_len)
    nb, nl = batch // nseq, (nseq * seq_len) // tl
    has_state = state is not None
    state_shape = (batch, GLA_HEADS, GLA_DK, GLA_DV)
    state_spec = pl.BlockSpec((nseq,) + state_shape[1:], lambda b, l: (b, 0, 0, 0))
    block_of_row = (np.arange(tl) // CHUNK) * nseq + np.tile(c["seq_of_row"], tl // CHUNK)
    assert (tl // CHUNK) * nseq <= V7X_LANES
    block = block_of_row[:, None] == np.arange(V7X_LANES)[None, :]
    consts = [gate_w, gate_b.reshape(1, GLA_KW), c["tril"], c["seq_ones"], c["causal4"],
              c["kmask"], c["vmask"],
              c["bd"], c["ones_blk"], jnp.asarray(block, BF16), jnp.tile(norm_g, GLA_HEADS).reshape(1, GLA_W)]
    row_spec = lambda w: pl.BlockSpec((tl, w), lambda b, l: (b * nl + l, 0))
    in_specs = [row_spec(GLA_COLS), row_spec(GLA_RANK)] + [_const_spec(a.shape) for a in consts]
    args = [gla2d, glr2d] + consts
    if has_state:
        in_specs.append(pl.BlockSpec((None, nseq) + state_shape[1:], lambda b, l: (layer, b, 0, 0, 0)))
        args.append(state)
    o, s_new = pl.pallas_call(
        functools.partial(_gla_body, tl=tl, lc=lc, nseq=nseq, has_state=has_state),
        grid=(nb, nl),
        in_specs=in_specs,
        out_specs=[row_spec(GLA_W), state_spec],
        out_shape=[jax.ShapeDtypeStruct((batch * seq_len, GLA_W), BF16), jax.ShapeDtypeStruct(state_shape, F32)],
        scratch_shapes=[pltpu.VMEM((nseq, GLA_KW, GLA_W), F32), pltpu.VMEM((tl, GLA_W), F32)],
        compiler_params=_params(),
        name="gla",
    )(*args)
    return o, s_new


def _s5_discretize(lam_re, lam_im, log_dt, b_re, b_im, c_re, c_im):
    lr = jnp.minimum(lam_re.astype(F32), -S5_MIN_NEG)
    li = lam_im.astype(F32)
    dt = jnp.exp(log_dt.astype(F32))[:, None]
    mag = jnp.exp(lr * dt)
    ar = mag * jnp.cos(li * dt)
    ai = mag * jnp.sin(li * dt)
    den = lr * lr + li * li
    cr = ((ar - 1.0) * lr + ai * li) / den
    ci = (ai * lr - (ar - 1.0) * li) / den
    b_re, b_im = b_re.astype(F32), b_im.astype(F32)
    bbar_re = cr[..., None] * b_re - ci[..., None] * b_im
    bbar_im = cr[..., None] * b_im + ci[..., None] * b_re
    eye = jnp.eye(S5_GROUPS, dtype=F32)
    in_blk = lambda b: jnp.einsum("gpi,gh->gihp", b, eye).reshape(S5_W, S5_LANES)
    out_blk = lambda c: jnp.einsum("gop,gh->gpho", c.astype(F32), eye).reshape(S5_LANES, S5_W)
    bdb = jnp.concatenate([in_blk(bbar_re), in_blk(bbar_im)], axis=1).astype(BF16)
    bdc = jnp.concatenate([out_blk(c_re), -out_blk(c_im)], axis=0).astype(BF16)
    return ar.reshape(1, S5_LANES), ai.reshape(1, S5_LANES), bdb, bdc


def _s5_body(*refs, tl, batch, has_state):
    u_ref, ar_ref, ai_ref, bdb_ref, bdc_ref, d_ref, gw_ref, gb_ref = refs[:8]
    n_in = 10 if has_state else 8
    o_ref, hr_ref, hi_ref, x_scr, h_scr = refs[n_in:]

    @pl.when(pl.program_id(0) == 0)
    def _():
        if has_state:
            h_scr[0] = refs[8][...]
            h_scr[1] = refs[9][...]
        else:
            h_scr[...] = jnp.zeros_like(h_scr)

    u = u_ref[...]
    x_scr[...] = _dot(u, bdb_ref[...])
    ar = jnp.broadcast_to(ar_ref[...], (batch, S5_LANES))
    ai = jnp.broadcast_to(ai_ref[...], (batch, S5_LANES))

    def step(t, carry):
        hr, hi = carry
        rows = pl.ds(pl.multiple_of(t * batch, batch), batch)
        nr = ar * hr - ai * hi + x_scr[rows, 0:S5_LANES]
        ni = ar * hi + ai * hr + x_scr[rows, S5_LANES:2 * S5_LANES]
        x_scr[rows, 0:S5_LANES] = nr
        x_scr[rows, S5_LANES:2 * S5_LANES] = ni
        return nr, ni

    carry = (h_scr[0], h_scr[1])
    if tl <= V7X_SUBLANES:
        for t in range(tl):
            carry = step(t, carry)
    else:
        carry = lax.fori_loop(0, tl, step, carry, unroll=V7X_SUBLANES)
    h_scr[0], h_scr[1] = carry
    hr_ref[...], hi_ref[...] = carry

    y = _dot(x_scr[...].astype(BF16), bdc_ref[...]) + d_ref[...] * u.astype(F32)
    y = _gelu_tanh(y)
    o_ref[...] = (y * _sigmoid(_dot(y.astype(BF16), gw_ref[...]) + gb_ref[...])).astype(BF16)


def _s5(u_tm, disc, d, glu_w_bf, glu_b, state, batch, seq_len):
    ar, ai, bdb, bdc = disc
    tl = min(S5_TIME_TILE, seq_len)
    has_state = state is not None
    consts = [ar, ai, bdb, bdc, d.reshape(1, S5_W), glu_w_bf, glu_b.reshape(1, S5_W)]
    h_spec = pl.BlockSpec((batch, S5_LANES), lambda i: (0, 0))
    in_specs = [pl.BlockSpec((tl * batch, S5_W), lambda i: (i, 0))] + [_const_spec(a.shape) for a in consts]
    args = [u_tm] + consts
    if has_state:
        in_specs += [h_spec, h_spec]
        args += [state[0].reshape(batch, S5_LANES), state[1].reshape(batch, S5_LANES)]
    o, hr, hi = pl.pallas_call(
        functools.partial(_s5_body, tl=tl, batch=batch, has_state=has_state),
        grid=(seq_len // tl,),
        in_specs=in_specs,
        out_specs=[pl.BlockSpec((tl * batch, S5_W), lambda i: (i, 0)), h_spec, h_spec],
        out_shape=[jax.ShapeDtypeStruct((seq_len * batch, S5_W), BF16),
                   jax.ShapeDtypeStruct((batch, S5_LANES), F32), jax.ShapeDtypeStruct((batch, S5_LANES), F32)],
        scratch_shapes=[pltpu.VMEM((tl * batch, 2 * S5_LANES), F32), pltpu.VMEM((2, batch, S5_LANES), F32)],
        compiler_params=_params(),
        name="s5",
    )(*args)
    shape = (batch, S5_GROUPS, S5_STATE)
    return o, hr.reshape(shape), hi.reshape(shape)


def _ffn_body(*refs, tm, seq_rows, final):
    (x_ref, oret_ref, os5_ref, ogla_ref, wout_ref, gffn_ref, win_ref, cw_ref, cb_ref, wo_ref) = refs[:10]
    n = 10
    gfin_ref = None
    if final:
        gfin_ref, n = refs[n], n + 1
    if seq_rows is not None:
        f1_ref, f2_ref = refs[n:n + 2]
        n += 2
    out_ref, conv_ref, x1_scr, h_scr, carry_scr, act_scr = refs[n:]
    pad = V7X_SUBLANES

    if seq_rows is None:
        @pl.when(pl.program_id(1) == 0)
        def _():
            carry_scr[...] = jnp.zeros_like(carry_scr)
    else:
        carry_scr[...] = jnp.zeros_like(carry_scr)

    x1_scr[...] = (x_ref[...] + _dot(oret_ref[...], wout_ref[0:RET_W, :])
                   + _dot(os5_ref[...], wout_ref[RET_W:RET_W + S5_W, :])
                   + _dot(ogla_ref[...], wout_ref[RET_W + S5_W:D_MODEL, :]))
    h_scr[...] = _rmsnorm_rows(x1_scr[...], gffn_ref[...]).astype(BF16)
    h = h_scr[...]
    if seq_rows is not None:
        t = lax.broadcasted_iota(jnp.int32, (tm, ACT_COLS), 0) % seq_rows

    for c in range(D_FF // ACT_COLS):
        cols = slice(c * ACT_COLS, (c + 1) * ACT_COLS)
        a_c = _dot(h, win_ref[:, c * ACT_COLS:(c + 1) * ACT_COLS])
        gate_c = _dot(h, win_ref[:, D_FF + c * ACT_COLS:D_FF + (c + 1) * ACT_COLS])
        ext = jnp.concatenate([carry_scr[:, cols], a_c], axis=0)
        prev1 = ext[pad - 1:pad - 1 + tm]
        prev2 = ext[pad - 2:pad - 2 + tm]
        if seq_rows is not None:
            prev1 = jnp.where(t == 0, f1_ref[:, cols], prev1)
            prev2 = jnp.where(t < 2, f2_ref[:, cols], prev2)
            conv_ref[:, cols] = a_c
        else:
            carry_scr[:, cols] = a_c[tm - pad:tm]
        conv = (cb_ref[:, cols] + prev2 * cw_ref[0:1, cols] + prev1 * cw_ref[1:2, cols]
                + a_c * cw_ref[2:3, cols])
        act_scr[:, cols] = (_gelu_tanh(conv) * gate_c).astype(BF16)

    x2 = x1_scr[...] + _dot(act_scr[...], wo_ref[...])
    out_ref[...] = _rmsnorm_rows(x2, gfin_ref[...]) if final else x2
    if seq_rows is None:
        conv_ref[0] = carry_scr[pad - (CONV_W - 1):pad, :]


def _ffn(x2d, o_ret, o_s5, o_gla, w_out_bf, norm_g, w_in_bf, conv_w, conv_b, w_o_bf, layer, final_g, conv_state,
         batch, seq_len):
    rows = batch * seq_len
    tm = min(ROW_TILE, rows)
    short = conv_state is not None
    final = final_g is not None
    cw = jnp.concatenate([conv_w.astype(F32), jnp.zeros((V7X_SUBLANES - CONV_W, D_FF), F32)], axis=0)
    consts = [w_out_bf, norm_g.reshape(1, D_MODEL), w_in_bf, cw, conv_b.reshape(1, D_FF), w_o_bf]
    stacked = [True, False, True, False, False, True]
    if final:
        consts.append(final_g.reshape(1, D_MODEL))
        stacked.append(False)
    if short:
        assert tm % seq_len == 0 and seq_len >= CONV_W - 1
        grid = (rows // tm,)
        imap = lambda i: (i, 0)
        zeros = jnp.zeros((batch, seq_len - 1, D_FF), F32)
        fill1 = jnp.concatenate([conv_state[:, 1:2], zeros], axis=1).reshape(rows, D_FF)
        fill2 = jnp.concatenate([conv_state, zeros[:, 1:]], axis=1).reshape(rows, D_FF)
        extra, extra_specs = [fill1, fill2], [pl.BlockSpec((tm, D_FF), imap)] * 2
        conv_spec = pl.BlockSpec((tm, D_FF), imap)
        conv_shape = jax.ShapeDtypeStruct((rows, D_FF), F32)
    else:
        assert seq_len % tm == 0
        nl = seq_len // tm
        grid = (batch, nl)
        imap = lambda b, l: (b * nl + l, 0)
        extra, extra_specs = [], []
        conv_spec = pl.BlockSpec((1, CONV_W - 1, D_FF), lambda b, l: (b, 0, 0))
        conv_shape = jax.ShapeDtypeStruct((batch, CONV_W - 1, D_FF), F32)
    row_spec = lambda w: pl.BlockSpec((tm, w), imap)
    in_specs = ([row_spec(D_MODEL), row_spec(RET_W), row_spec(S5_W), row_spec(GLA_W)]
                + [_layer_spec(a.shape, layer, True) if s else _const_spec(a.shape) for a, s in zip(consts, stacked)]
                + extra_specs)
    out, conv_out = pl.pallas_call(
        functools.partial(_ffn_body, tm=tm, seq_rows=seq_len if short else None, final=final),
        grid=grid,
        in_specs=in_specs,
        out_specs=[row_spec(D_MODEL), conv_spec],
        out_shape=[jax.ShapeDtypeStruct((rows, D_MODEL), F32), conv_shape],
        scratch_shapes=[pltpu.VMEM((tm, D_MODEL), F32), pltpu.VMEM((tm, D_MODEL), BF16),
                        pltpu.VMEM((V7X_SUBLANES, D_FF), F32), pltpu.VMEM((tm, D_FF), BF16)],
        compiler_params=_params(),
        name="ffn",
    )(x2d, o_ret, o_s5, o_gla, *consts, *extra)
    if short:
        conv_out = conv_out.reshape(batch, seq_len, D_FF)[:, seq_len - (CONV_W - 1):]
    return out, conv_out


def _rotary_tables(pos, rows):
    half = RET_HD // 2
    inv = ROPE_BASE ** (-jnp.arange(half, dtype=F32) / half)
    ang = pos.astype(F32)[:, None] * inv[None, :]
    cos = jnp.tile(jnp.concatenate([jnp.cos(ang), jnp.cos(ang)], axis=1), (1, RET_HEADS))
    sin = jnp.tile(jnp.concatenate([-jnp.sin(ang), jnp.sin(ang)], axis=1), (1, RET_HEADS))
    reps = rows // pos.shape[0]
    return jnp.tile(cos, (reps, 1)), jnp.tile(sin, (reps, 1))


def _time_major(a2d, batch, seq_len):
    return a2d.reshape(batch, seq_len, -1).transpose(1, 0, 2).reshape(seq_len * batch, -1)


def _batch_major(a2d, batch, seq_len):
    return a2d.reshape(seq_len, batch, -1).transpose(1, 0, 2).reshape(batch * seq_len, -1)


def _run_group(x, pos, states, layers, big, final_g):
    batch, seq_len, _ = x.shape
    rows = batch * seq_len
    tm = min(ROW_TILE, rows)
    short = states is not None
    tab_rows = tm if short else seq_len
    cos_tab, sin_tab = _rotary_tables(pos, tab_rows)
    x2d = x.reshape(rows, D_MODEL)
    outs = []
    for li, lp in enumerate(layers):
        ret, su, gla, glr = _mix_in(x2d, lp["norm_mix_g"], big["w_in"], li, cos_tab, sin_tab, tm)
        o_ret, s_ret = _retention(ret, lp["ret_norm_g"], lp["ret_norm_b"], states["ret"] if short else None, li,
                                  batch, seq_len)
        s5_state = (states["s5r"][li], states["s5i"][li]) if short else None
        o_s5, s5r, s5i = _s5(_time_major(su, batch, seq_len), lp["s5_disc"], lp["s5_d"], lp["s5_glu_w"],
                             lp["s5_glu_b"], s5_state, batch, seq_len)
        o_s5 = _batch_major(o_s5, batch, seq_len)
        o_gla, s_gla = _gla(gla, glr, lp["gla_gate_w"], lp["gla_gate_b"], lp["gla_norm_g"],
                            states["gla"] if short else None, li, batch, seq_len)
        last = li == len(layers) - 1
        x2d, conv_new = _ffn(x2d, o_ret, o_s5, o_gla, big["w_out"], lp["norm_ffn_g"], big["ffn_w_in"],
                             lp["ffn_conv_w"], lp["ffn_conv_b"], big["ffn_w_out"], li, final_g if last else None,
                             states["conv"][li] if short else None, batch, seq_len)
        outs.append((s_ret, s5r, s5i, s_gla, conv_new))
    stacked = [jnp.stack([o[i] for o in outs]) for i in range(5)]
    return [x2d.reshape(batch, seq_len, D_MODEL)] + stacked


def kernel(x_prompt, x_sample, state_ret, state_s5_re, state_s5_im, state_gla, state_ffn_conv, norm_mix_g, w_in, ret_norm_g, ret_norm_b, s5_lambda_re, s5_lambda_im, s5_log_dt, s5_b_re, s5_b_im, s5_c_re, s5_c_im, s5_d, s5_glu_w, s5_glu_b, gla_gate_w, gla_gate_b, gla_norm_g, w_out, norm_ffn_g, ffn_w_in, ffn_conv_w, ffn_conv_b, ffn_w_out, norm_final_g):
    depth = w_in.shape[0]
    big = dict(w_in=w_in.astype(BF16), w_out=w_out.astype(BF16), ffn_w_in=ffn_w_in.astype(BF16),
               ffn_w_out=ffn_w_out.astype(BF16))
    layers = []
    for l in range(depth):
        layers.append(dict(
            norm_mix_g=norm_mix_g[l], ret_norm_g=ret_norm_g[l], ret_norm_b=ret_norm_b[l],
            s5_disc=_s5_discretize(s5_lambda_re[l], s5_lambda_im[l], s5_log_dt[l], s5_b_re[l], s5_b_im[l],
                                   s5_c_re[l], s5_c_im[l]),
            s5_d=s5_d[l], s5_glu_w=s5_glu_w[l].astype(BF16), s5_glu_b=s5_glu_b[l],
            gla_gate_w=gla_gate_w[l], gla_gate_b=gla_gate_b[l], gla_norm_g=gla_norm_g[l],
            norm_ffn_g=norm_ffn_g[l], ffn_conv_w=ffn_conv_w[l], ffn_conv_b=ffn_conv_b[l]))
    lp_, ls = x_prompt.shape[1], x_sample.shape[1]
    pos_prompt = jnp.arange(lp_, dtype=jnp.int32)
    pos_sample = PAST_LEN + jnp.arange(ls, dtype=jnp.int32)
    sample_states = dict(ret=state_ret, s5r=state_s5_re, s5i=state_s5_im, gla=state_gla, conv=state_ffn_conv)
    yp, ret_p, s5r_p, s5i_p, gla_p, conv_p = _run_group(x_prompt, pos_prompt, None, layers, big, norm_final_g)
    ys, ret_s, s5r_s, s5i_s, gla_s, conv_s = _run_group(x_sample, pos_sample, sample_states, layers, big,
                                                        norm_final_g)
    return (yp, ys, ret_p, ret_s, s5r_p, s5r_s, s5i_p, s5i_s, gla_p, gla_s, conv_p, conv_s)
```

```python
import functools

import numpy as np
import jax
import jax.numpy as jnp
from jax import lax
from jax.experimental import pallas as pl
from jax.experimental.pallas import tpu as pltpu

F32, BF16 = jnp.float32, jnp.bfloat16

D_MODEL = 1024
RET_W, S5_W, GLA_W = 384, 256, 384
RET_HEADS, RET_HD = 6, 64
RET_PAIRS = RET_HEADS // 2
S5_GROUPS, S5_CH, S5_STATE = 16, 16, 64
S5_LANES = S5_GROUPS * S5_STATE
S5_MIN_NEG = 1e-4
GLA_HEADS, GLA_DK, GLA_DV, GLA_RANK = 4, 48, 96, 16
GLA_KW = GLA_HEADS * GLA_DK
GLA_GATE_TEMP = 16.0
D_FF = 2816
CONV_W = 3
ROPE_BASE = 10000.0
CHUNK = 64
EPS = 1e-6
PAST_LEN = 16384
IN_COLS = 4 * RET_W + S5_W + 2 * GLA_KW + 2 * GLA_W + GLA_RANK
COL_RET, COL_S5, COL_GLA, COL_LR = 0, 4 * RET_W, 4 * RET_W + S5_W, IN_COLS - GLA_RANK
GLA_COLS = 2 * GLA_KW + 2 * GLA_W

V7X_SUBLANES = 8
V7X_LANES = 128
VMEM_LIMIT = 58 * 1024 * 1024

ROW_TILE = 512
S5_TIME_TILE = 128
ACT_COLS = 256


def _dot(a, b):
    return jnp.dot(a, b, preferred_element_type=F32)


def _dot_nt(a, b):
    return lax.dot_general(a, b, (((1,), (1,)), ((), ())), preferred_element_type=F32)


def _dot_tn(a, b):
    return lax.dot_general(a, b, (((0,), (0,)), ((), ())), preferred_element_type=F32)


def _hi_lo(x):
    hi = x.astype(BF16)
    return hi, (x - hi.astype(F32)).astype(BF16)


def _dot_exact_lhs(m, x):
    hi, lo = _hi_lo(x)
    return _dot(m, hi) + _dot(m, lo)


def _dot3(a, b):
    ah, al = _hi_lo(a)
    bh, bl = _hi_lo(b)
    return _dot(ah, bh) + _dot(ah, bl) + _dot(al, bh)


def _sigmoid(x):
    return 1.0 / (1.0 + jnp.exp(-x))


def _log_sigmoid(z):
    return jnp.minimum(z, 0.0) - jnp.log(1.0 + jnp.exp(-jnp.abs(z)))


def _gelu_tanh(x):
    return 0.5 * x * (1.0 + jnp.tanh(0.7978845608028654 * (x + 0.044715 * (x * x * x))))


def _rmsnorm_rows(x, g):
    return x * lax.rsqrt(jnp.mean(x * x, axis=-1, keepdims=True) + EPS) * g


def _const_spec(shape):
    nd = len(shape)
    return pl.BlockSpec(shape, lambda *_: (0,) * nd)


def _layer_spec(stacked_shape, layer, single_buffer=False):
    nd = len(stacked_shape) - 1
    mode = dict(pipeline_mode=pl.Buffered(1)) if single_buffer else {}
    return pl.BlockSpec((None,) + tuple(stacked_shape[1:]), lambda *_: (layer,) + (0,) * nd, **mode)


def _params():
    return pltpu.CompilerParams(vmem_limit_bytes=VMEM_LIMIT)


def _to_lanes(x):
    w = x.shape[1]
    pad = -w % V7X_LANES
    if pad:
        x = jnp.concatenate([x, jnp.zeros((x.shape[0], pad), F32)], axis=1)
    return x.T[0:w]


def _mix_in_body(x_ref, g_ref, w_ref, cos_ref, sin_ref, ret_ref, su_ref, gla_ref, glr_ref):
    h = _rmsnorm_rows(x_ref[...], g_ref[...]).astype(BF16)
    ret = _dot(h, w_ref[:, COL_RET:COL_S5])
    cos = cos_ref[...]
    sin = sin_ref[...]
    lane = lax.broadcasted_iota(jnp.int32, cos.shape, 1)
    first_half = (lane & (RET_HD // 2)) == 0

    def rotary(z):
        swapped = jnp.where(first_half, pltpu.roll(z, RET_W - RET_HD // 2, 1), pltpu.roll(z, RET_HD // 2, 1))
        return z * cos + swapped * sin

    ret_ref[:, 0:RET_W] = rotary(ret[:, 0:RET_W]).astype(BF16)
    ret_ref[:, RET_W:2 * RET_W] = (rotary(ret[:, RET_W:2 * RET_W]) * RET_HD ** -0.5).astype(BF16)
    ret_ref[:, 2 * RET_W:] = ret[:, 2 * RET_W:].astype(BF16)
    su_ref[...] = _dot(h, w_ref[:, COL_S5:COL_GLA]).astype(BF16)
    gla = _dot(h, w_ref[:, COL_GLA:COL_LR])
    glane = lax.broadcasted_iota(jnp.int32, (1, GLA_COLS), 1)
    gla_ref[...] = (gla * jnp.where(glane < GLA_KW, GLA_DK ** -0.5, 1.0)).astype(BF16)
    glr_ref[...] = _dot(h, w_ref[:, COL_LR:IN_COLS])


def _mix_in(x2d, norm_g, w_in_bf, layer, cos_tab, sin_tab, tm):
    rows = x2d.shape[0]
    n_tab = cos_tab.shape[0] // tm
    row_spec = lambda w: pl.BlockSpec((tm, w), lambda i: (i, 0))
    tab_spec = pl.BlockSpec((tm, RET_W), lambda i: (i % n_tab, 0))
    return pl.pallas_call(
        _mix_in_body,
        grid=(rows // tm,),
        in_specs=[row_spec(D_MODEL), _const_spec((1, D_MODEL)), _layer_spec(w_in_bf.shape, layer, True),
                  tab_spec, tab_spec],
        out_specs=[row_spec(4 * RET_W), row_spec(S5_W), row_spec(GLA_COLS), row_spec(GLA_RANK)],
        out_shape=[jax.ShapeDtypeStruct((rows, 4 * RET_W), BF16), jax.ShapeDtypeStruct((rows, S5_W), BF16),
                   jax.ShapeDtypeStruct((rows, GLA_COLS), BF16), jax.ShapeDtypeStruct((rows, GLA_RANK), F32)],
        compiler_params=_params(),
        name="mix_in",
    )(x2d, norm_g.reshape(1, D_MODEL), w_in_bf, cos_tab, sin_tab)


def _retention_gammas():
    return 1.0 - 2.0 ** (-5.0 - np.arange(RET_HEADS))


def _retention_consts():
    t = np.arange(CHUNK)
    gam = _retention_gammas()
    causal = t[:, None] >= t[None, :]
    diff = np.maximum(t[:, None] - t[None, :], 0)
    dmask = np.zeros((RET_PAIRS, CHUNK, 2 * CHUNK))
    cdec = np.zeros((RET_PAIRS, 2 * RET_HD, 2 * RET_HD))
    for p in range(RET_PAIRS):
        for s in range(2):
            g = gam[2 * p + s]
            dmask[p, :, s * CHUNK:(s + 1) * CHUNK] = np.where(causal, g ** diff, 0.0)
            cdec[p, s * RET_HD:(s + 1) * RET_HD, s * RET_HD:(s + 1) * RET_HD] = g ** CHUNK
    lane_gam = np.repeat(gam, RET_HD)[None, :]
    qdec = lane_gam ** (t[:, None] + 1.0)
    kdec = lane_gam ** (CHUNK - 1.0 - t[:, None])
    bd = (cdec[0] > 0).astype(np.float32)
    head_mask = np.stack([np.arange(2 * RET_HD) < RET_HD, np.arange(2 * RET_HD) >= RET_HD]).astype(np.float32)
    ones_blk = np.kron(np.eye(RET_HEADS), np.ones((RET_HD, RET_HD)))
    f = lambda a: jnp.asarray(a, F32)
    return dict(dmask=f(dmask), cdec=f(cdec), qdec=f(qdec), kdec=f(kdec), bd=f(bd),
                head_mask=jnp.asarray(head_mask, BF16), ones_blk=jnp.asarray(ones_blk, BF16))


def _retention_body(ret_ref, dmask_ref, cdec_ref, qdec_ref, kdec_ref, bd_ref, hm_ref, ones_ref, lng_ref, lnb_ref,
                    o_ref, sout_ref, s_scr, o_scr, *, tl):
    @pl.when(pl.program_id(1) == 0)
    def _():
        s_scr[...] = jnp.zeros_like(s_scr)

    bd = bd_ref[...]
    m0 = hm_ref[0:1, :]
    m1 = hm_ref[1:2, :]
    q = ret_ref[:, 0:RET_W]
    k = ret_ref[:, RET_W:2 * RET_W]
    v = ret_ref[:, 2 * RET_W:3 * RET_W]
    q_start = (q.astype(F32) * qdec_ref[...]).astype(BF16)
    k_end = (k.astype(F32) * kdec_ref[...]).astype(BF16)
    blocks = [(n, p) for n in range(tl // CHUNK) for p in range(RET_PAIRS)]

    def part(a, n, p):
        return a[n * CHUNK:(n + 1) * CHUNK, 2 * RET_HD * p:2 * RET_HD * (p + 1)]

    scores, o, kv = {}, {}, {}
    for n, p in blocks:
        kp = part(k, n, p)
        kk = jnp.concatenate([kp * m0, kp * m1], axis=0)
        scores[n, p] = (_dot_nt(part(q, n, p), kk) * dmask_ref[p]).astype(BF16)
    for n, p in blocks:
        vp = part(v, n, p)
        o[n, p] = _dot(scores[n, p], jnp.concatenate([vp * m0, vp * m1], axis=0))
    for n, p in blocks:
        kv[n, p] = _dot_tn(part(k_end, n, p), part(v, n, p)) * bd
    for n, p in blocks:
        s_prev = s_scr[p]
        o_scr[n * CHUNK:(n + 1) * CHUNK, 2 * RET_HD * p:2 * RET_HD * (p + 1)] = (
            o[n, p] + _dot(part(q_start, n, p), s_prev.astype(BF16)))
        s_scr[p] = s_prev * cdec_ref[p] + kv[n, p]

    o_all = o_scr[...]
    ones_blk = ones_ref[...]
    mu = _dot(o_all.astype(BF16), ones_blk) * (1.0 / RET_HD)
    cen = o_all - mu
    var = _dot((cen * cen).astype(BF16), ones_blk) * (1.0 / RET_HD)
    y = cen * lax.rsqrt(var + EPS) * lng_ref[...] + lnb_ref[...]
    gate = ret_ref[:, 3 * RET_W:4 * RET_W].astype(F32)
    o_ref[...] = (y * gate * _sigmoid(gate)).astype(BF16)
    for p in range(RET_PAIRS):
        s = s_scr[p]
        sout_ref[0, 2 * p] = s[0:RET_HD, 0:RET_HD]
        sout_ref[0, 2 * p + 1] = s[RET_HD:2 * RET_HD, RET_HD:2 * RET_HD]


def _retention(ret2d, ln_g, ln_b, batch, seq_len):
    assert seq_len % CHUNK == 0
    c = _retention_consts()
    tl = min(ROW_TILE, seq_len)
    nl = seq_len // tl
    tile_rows = lambda a: jnp.tile(a, (tl // CHUNK, 1))
    consts = [c["dmask"], c["cdec"], tile_rows(c["qdec"]), tile_rows(c["kdec"]), c["bd"], c["head_mask"],
              c["ones_blk"], ln_g.reshape(1, RET_W), ln_b.reshape(1, RET_W)]
    state_shape = (batch, RET_HEADS, RET_HD, RET_HD)
    return pl.pallas_call(
        functools.partial(_retention_body, tl=tl),
        grid=(batch, nl),
        in_specs=[pl.BlockSpec((tl, 4 * RET_W), lambda b, l: (b * nl + l, 0))] + [_const_spec(a.shape) for a in consts],
        out_specs=[pl.BlockSpec((tl, RET_W), lambda b, l: (b * nl + l, 0)),
                   pl.BlockSpec((1,) + state_shape[1:], lambda b, l: (b, 0, 0, 0))],
        out_shape=[jax.ShapeDtypeStruct((batch * seq_len, RET_W), BF16), jax.ShapeDtypeStruct(state_shape, F32)],
        scratch_shapes=[pltpu.VMEM((RET_PAIRS, 2 * RET_HD, 2 * RET_HD), F32), pltpu.VMEM((tl, RET_W), F32)],
        compiler_params=_params(),
        name="retention",
    )(ret2d, *consts)


def _retention_lanes_body(ret_ref, tab_ref, lng_ref, lnb_ref, s_ref, o_ref, sout_ref, q_scr, k_scr, v_scr, g_scr,
                          o_scr, *, seq_len):
    h = pl.program_id(0)
    nb = V7X_LANES
    steps = range(seq_len)

    @pl.when(h == 0)
    def _():
        for t in steps:
            blk = ret_ref[t * nb:(t + 1) * nb, :].astype(F32)
            for scr, off in ((q_scr, 0), (k_scr, RET_W), (v_scr, 2 * RET_W), (g_scr, 3 * RET_W)):
                tr = _to_lanes(blk[:, off:off + RET_W])
                for hh in range(RET_HEADS):
                    scr[t, hh] = tr[hh * RET_HD:(hh + 1) * RET_HD]

    tab = tab_ref[h]
    row = lambda r: tab[r:r + 1, :]
    q = [q_scr[t, h] for t in steps]
    k = [k_scr[t, h] for t in steps]
    v = [v_scr[t, h] for t in steps]
    o = []
    for i in steps:
        acc = None
        for j in range(i + 1):
            s = jnp.sum(q[i] * k[j], axis=0, keepdims=True)
            if i > j:
                s = s * row(i - j - 1)
            acc = s * v[j] if acc is None else acc + s * v[j]
        o.append(acc)
    q_start = [q[t] * row(t) for t in steps]
    k_end = [k[t] * row(seq_len + t) for t in steps]
    chunk_decay = row(2 * seq_len)
    for d in range(RET_HD):
        s_d = s_ref[d]
        new = s_d * chunk_decay
        for t in steps:
            o[t] = o[t] + q_start[t][d:d + 1, :] * s_d
            new = new + k_end[t][d:d + 1, :] * v[t]
        sout_ref[d] = new
    for t in steps:
        mu = jnp.mean(o[t], axis=0, keepdims=True)
        cen = o[t] - mu
        var = jnp.mean(cen * cen, axis=0, keepdims=True)
        gate = g_scr[t, h]
        o_scr[t, h] = (cen * lax.rsqrt(var + EPS) * lng_ref[h] + lnb_ref[h]) * gate * _sigmoid(gate)

    @pl.when(h == RET_HEADS - 1)
    def _():
        for t in steps:
            o_ref[t * nb:(t + 1) * nb, :] = o_scr[t].reshape(RET_W, nb).T.astype(BF16)


def _retention_lanes(ret2d, ln_g, ln_b, state, layer, seq_len):
    nb = V7X_LANES
    gam = _retention_gammas()[:, None]
    t = np.arange(seq_len)[None, :]
    rows = np.concatenate([gam ** (t + 1.0), gam ** (seq_len - 1.0 - t), gam ** (seq_len + 0.0 * t[:, :1])], axis=1)
    n_rows = -(-rows.shape[1] // V7X_SUBLANES) * V7X_SUBLANES
    tab = np.zeros((RET_HEADS, n_rows, nb))
    tab[:, :rows.shape[1], :] = rows[:, :, None]
    lanes = lambda p: jnp.broadcast_to(p.reshape(RET_HEADS, RET_HD, 1), (RET_HEADS, RET_HD, nb))
    consts = [jnp.asarray(tab, F32), lanes(ln_g), lanes(ln_b)]
    tile = pltpu.VMEM((seq_len, RET_HEADS, RET_HD, nb), F32)
    return pl.pallas_call(
        functools.partial(_retention_lanes_body, seq_len=seq_len),
        grid=(RET_HEADS,),
        in_specs=[_const_spec(ret2d.shape)] + [_const_spec(a.shape) for a in consts]
                 + [pl.BlockSpec((None, None, RET_HD, RET_HD, nb), lambda h: (layer, h, 0, 0, 0))],
        out_specs=[_const_spec((seq_len * nb, RET_W)), pl.BlockSpec((None, RET_HD, RET_HD, nb), lambda h: (h, 0, 0, 0))],
        out_shape=[jax.ShapeDtypeStruct((seq_len * nb, RET_W), BF16),
                   jax.ShapeDtypeStruct((RET_HEADS, RET_HD, RET_HD, nb), F32)],
        scratch_shapes=[tile] * 5,
        compiler_params=_params(),
        name="retention_lanes",
    )(ret2d, *consts, state)


def _gla_consts():
    t = np.arange(CHUNK)
    causal = t[:, None] >= t[None, :]
    kmask = np.kron(np.eye(GLA_HEADS), np.ones((1, GLA_DK)))
    vmask = np.kron(np.eye(GLA_HEADS), np.ones((1, GLA_DV)))
    bd = np.kron(np.eye(GLA_HEADS), np.ones((GLA_DK, GLA_DV)))
    ones_blk = np.kron(np.eye(GLA_HEADS), np.ones((GLA_DV, GLA_DV)))
    b = lambda a: jnp.asarray(a, BF16)
    return dict(tril=b(causal), causal4=jnp.asarray(np.tile(causal, (1, GLA_HEADS)), F32), kmask=b(kmask),
                vmask=b(vmask), bd=jnp.asarray(bd, F32), ones_blk=b(ones_blk))


def _gla_body(gla_ref, glr_ref, gw_ref, gb_ref, tril_ref, causal_ref, kmask_ref, vmask_ref, bd_ref, ones_blk_ref,
              block_ref, ng_ref, o_ref, sout_ref, s_scr, o_scr, *, tl):
    @pl.when(pl.program_id(1) == 0)
    def _():
        s_scr[...] = jnp.zeros_like(s_scr)

    bd = bd_ref[...]
    lg = _log_sigmoid(_dot3(glr_ref[...], gw_ref[...]) + gb_ref[...]) * (1.0 / GLA_GATE_TEMP)
    q0, k0, v0, g0 = 0, GLA_KW, 2 * GLA_KW, 2 * GLA_KW + GLA_W
    chunks = range(tl // CHUNK)

    def rows(a, n):
        return a[n * CHUNK:(n + 1) * CHUNK]

    cums = [_dot_exact_lhs(tril_ref[...], rows(lg, n)) for n in chunks]
    b_cum = jnp.concatenate(cums, axis=0)
    b_end = jnp.concatenate([jnp.broadcast_to(c[CHUNK - 1:CHUNK, :], (CHUNK, GLA_KW)) for c in cums], axis=0)
    q_in = (gla_ref[:, q0:k0].astype(F32) * jnp.exp(b_cum)).astype(BF16)
    kf = gla_ref[:, k0:v0].astype(F32)
    k_in = (kf * jnp.exp(-b_cum)).astype(BF16)
    k_dec = (kf * jnp.exp(b_end - b_cum)).astype(BF16)
    v = gla_ref[:, v0:g0]
    lg_hi, lg_lo = _hi_lo(lg)
    block_decay = jnp.exp(_dot_tn(lg_hi, block_ref[...]) + _dot_tn(lg_lo, block_ref[...]))

    scores, o, kv = {}, {}, {}
    for n in chunks:
        kn = rows(k_in, n)
        kk = jnp.concatenate([kn * kmask_ref[h:h + 1, :] for h in range(GLA_HEADS)], axis=0)
        scores[n] = (_dot_nt(rows(q_in, n), kk) * causal_ref[...]).astype(BF16)
    for n in chunks:
        vn = rows(v, n)
        vv = jnp.concatenate([vn * vmask_ref[h:h + 1, :] for h in range(GLA_HEADS)], axis=0)
        o[n] = _dot(scores[n], vv)
    for n in chunks:
        kv[n] = _dot_tn(rows(k_dec, n), rows(v, n)) * bd
    for n in chunks:
        s_prev = s_scr[...]
        o_scr[n * CHUNK:(n + 1) * CHUNK, :] = o[n] + _dot(rows(q_in, n), s_prev.astype(BF16))
        s_scr[...] = s_prev * jnp.broadcast_to(block_decay[:, n:n + 1], (GLA_KW, GLA_W)) + kv[n]

    o_all = o_scr[...]
    ms = _dot((o_all * o_all).astype(BF16), ones_blk_ref[...]) * (1.0 / GLA_DV)
    gate = gla_ref[:, g0:g0 + GLA_W].astype(F32)
    o_ref[...] = (o_all * lax.rsqrt(ms + EPS) * ng_ref[...] * gate * _sigmoid(gate)).astype(BF16)
    s = s_scr[...]
    for h in range(GLA_HEADS):
        sout_ref[0, h] = s[h * GLA_DK:(h + 1) * GLA_DK, h * GLA_DV:(h + 1) * GLA_DV]


def _gla(gla2d, glr2d, gate_w, gate_b, norm_g, batch, seq_len):
    assert seq_len % CHUNK == 0
    c = _gla_consts()
    tl = min(ROW_TILE, seq_len)
    nl = seq_len // tl
    assert tl // CHUNK <= V7X_LANES
    block = (np.arange(tl) // CHUNK)[:, None] == np.arange(V7X_LANES)[None, :]
    consts = [gate_w, gate_b.reshape(1, GLA_KW), c["tril"], c["causal4"], c["kmask"], c["vmask"], c["bd"],
              c["ones_blk"], jnp.asarray(block, BF16), jnp.tile(norm_g, GLA_HEADS).reshape(1, GLA_W)]
    row_spec = lambda w: pl.BlockSpec((tl, w), lambda b, l: (b * nl + l, 0))
    state_shape = (batch, GLA_HEADS, GLA_DK, GLA_DV)
    return pl.pallas_call(
        functools.partial(_gla_body, tl=tl),
        grid=(batch, nl),
        in_specs=[row_spec(GLA_COLS), row_spec(GLA_RANK)] + [_const_spec(a.shape) for a in consts],
        out_specs=[row_spec(GLA_W), pl.BlockSpec((1,) + state_shape[1:], lambda b, l: (b, 0, 0, 0))],
        out_shape=[jax.ShapeDtypeStruct((batch * seq_len, GLA_W), BF16), jax.ShapeDtypeStruct(state_shape, F32)],
        scratch_shapes=[pltpu.VMEM((GLA_KW, GLA_W), F32), pltpu.VMEM((tl, GLA_W), F32)],
        compiler_params=_params(),
        name="gla",
    )(gla2d, glr2d, *consts)


def _gla_lanes_body(gla_ref, glr_ref, gw_ref, gb_ref, ng_ref, s_ref, o_ref, sout_ref, q_scr, k_scr, kd_scr, v_scr,
                    g_scr, dec_scr, o_scr, *, seq_len):
    h = pl.program_id(0)
    nb = V7X_LANES
    steps = range(seq_len)
    q0, k0, v0, g0 = 0, GLA_KW, 2 * GLA_KW, 2 * GLA_KW + GLA_W

    def split(scr, t, tr, width):
        for hh in range(GLA_HEADS):
            scr[t, hh] = tr[hh * width:(hh + 1) * width]

    @pl.when(h == 0)
    def _():
        lg = _log_sigmoid(_dot3(glr_ref[...], gw_ref[...]) + gb_ref[...]) * (1.0 / GLA_GATE_TEMP)
        cums = []
        for t in steps:
            lg_t = lg[t * nb:(t + 1) * nb]
            cums.append(lg_t if t == 0 else cums[-1] + lg_t)
        b_end = cums[-1]
        dec = _to_lanes(jnp.exp(b_end))
        for hh in range(GLA_HEADS):
            dec_scr[hh] = dec[hh * GLA_DK:(hh + 1) * GLA_DK]
        for t in steps:
            blk = gla_ref[t * nb:(t + 1) * nb, :].astype(F32)
            kf = blk[:, k0:v0]
            split(q_scr, t, _to_lanes(blk[:, q0:k0] * jnp.exp(cums[t])), GLA_DK)
            split(k_scr, t, _to_lanes(kf * jnp.exp(-cums[t])), GLA_DK)
            split(kd_scr, t, _to_lanes(kf * jnp.exp(b_end - cums[t])), GLA_DK)
            split(v_scr, t, _to_lanes(blk[:, v0:g0]), GLA_DV)
            split(g_scr, t, _to_lanes(blk[:, g0:g0 + GLA_W]), GLA_DV)

    q = [q_scr[t, h] for t in steps]
    k = [k_scr[t, h] for t in steps]
    kd = [kd_scr[t, h] for t in steps]
    v = [v_scr[t, h] for t in steps]
    dec = dec_scr[h]
    o = []
    for i in steps:
        acc = None
        for j in range(i + 1):
            s = jnp.sum(q[i] * k[j], axis=0, keepdims=True)
            acc = s * v[j] if acc is None else acc + s * v[j]
        o.append(acc)
    for i in range(GLA_DK):
        s_i = s_ref[i]
        new = s_i * dec[i:i + 1, :]
        for t in steps:
            o[t] = o[t] + q[t][i:i + 1, :] * s_i
            new = new + kd[t][i:i + 1, :] * v[t]
        sout_ref[i] = new
    for t in steps:
        ms = jnp.mean(o[t] * o[t], axis=0, keepdims=True)
        gate = g_scr[t, h]
        o_scr[t, h] = o[t] * lax.rsqrt(ms + EPS) * ng_ref[...] * gate * _sigmoid(gate)

    @pl.when(h == GLA_HEADS - 1)
    def _():
        for t in steps:
            o_ref[t * nb:(t + 1) * nb, :] = o_scr[t].reshape(GLA_W, nb).T.astype(BF16)


def _gla_lanes(gla2d, glr2d, gate_w, gate_b, norm_g, state, layer, seq_len):
    nb = V7X_LANES
    consts = [gate_w, gate_b.reshape(1, GLA_KW), jnp.broadcast_to(norm_g.reshape(GLA_DV, 1), (GLA_DV, nb))]
    k_tile = pltpu.VMEM((seq_len, GLA_HEADS, GLA_DK, nb), F32)
    v_tile = pltpu.VMEM((seq_len, GLA_HEADS, GLA_DV, nb), F32)
    return pl.pallas_call(
        functools.partial(_gla_lanes_body, seq_len=seq_len),
        grid=(GLA_HEADS,),
        in_specs=[_const_spec(gla2d.shape), _const_spec(glr2d.shape)] + [_const_spec(a.shape) for a in consts]
                 + [pl.BlockSpec((None, None, GLA_DK, GLA_DV, nb), lambda h: (layer, h, 0, 0, 0))],
        out_specs=[_const_spec((seq_len * nb, GLA_W)), pl.BlockSpec((None, GLA_DK, GLA_DV, nb), lambda h: (h, 0, 0, 0))],
        out_shape=[jax.ShapeDtypeStruct((seq_len * nb, GLA_W), BF16),
                   jax.ShapeDtypeStruct((GLA_HEADS, GLA_DK, GLA_DV, nb), F32)],
        scratch_shapes=[k_tile, k_tile, k_tile, v_tile, v_tile, pltpu.VMEM((GLA_HEADS, GLA_DK, nb), F32), v_tile],
        compiler_params=_params(),
        name="gla_lanes",
    )(gla2d, glr2d, *consts, state)


def _s5_discretize(lam_re, lam_im, log_dt, b_re, b_im, c_re, c_im):
    lr = jnp.minimum(lam_re.astype(F32), -S5_MIN_NEG)
    li = lam_im.astype(F32)
    dt = jnp.exp(log_dt.astype(F32))[:, None]
    mag = jnp.exp(lr * dt)
    ar = mag * jnp.cos(li * dt)
    ai = mag * jnp.sin(li * dt)
    den = lr * lr + li * li
    cr = ((ar - 1.0) * lr + ai * li) / den
    ci = (ai * lr - (ar - 1.0) * li) / den
    b_re, b_im = b_re.astype(F32), b_im.astype(F32)
    bbar_re = cr[..., None] * b_re - ci[..., None] * b_im
    bbar_im = cr[..., None] * b_im + ci[..., None] * b_re
    eye = jnp.eye(S5_GROUPS, dtype=F32)
    in_blk = lambda b: jnp.einsum("gpi,gh->gihp", b, eye).reshape(S5_W, S5_LANES)
    out_blk = lambda c: jnp.einsum("gop,gh->gpho", c.astype(F32), eye).reshape(S5_LANES, S5_W)
    bdb = jnp.concatenate([in_blk(bbar_re), in_blk(bbar_im)], axis=1).astype(BF16)
    bdc = jnp.concatenate([out_blk(c_re), -out_blk(c_im)], axis=0).astype(BF16)
    return ar.reshape(1, S5_LANES), ai.reshape(1, S5_LANES), bdb, bdc


def _s5_body(*refs, tl, batch, has_state):
    u_ref, ar_ref, ai_ref, bdb_ref, bdc_ref, d_ref, gw_ref, gb_ref = refs[:8]
    n_in = 10 if has_state else 8
    o_ref, hr_ref, hi_ref, x_scr, h_scr = refs[n_in:]

    @pl.when(pl.program_id(0) == 0)
    def _():
        if has_state:
            h_scr[0] = refs[8][...]
            h_scr[1] = refs[9][...]
        else:
            h_scr[...] = jnp.zeros_like(h_scr)

    u = u_ref[...]
    x_scr[...] = _dot(u, bdb_ref[...])
    ar = jnp.broadcast_to(ar_ref[...], (batch, S5_LANES))
    ai = jnp.broadcast_to(ai_ref[...], (batch, S5_LANES))

    def step(t, carry):
        hr, hi = carry
        rows = pl.ds(pl.multiple_of(t * batch, batch), batch)
        nr = ar * hr - ai * hi + x_scr[rows, 0:S5_LANES]
        ni = ar * hi + ai * hr + x_scr[rows, S5_LANES:2 * S5_LANES]
        x_scr[rows, 0:S5_LANES] = nr
        x_scr[rows, S5_LANES:2 * S5_LANES] = ni
        return nr, ni

    carry = (h_scr[0], h_scr[1])
    if tl <= V7X_SUBLANES:
        for t in range(tl):
            carry = step(t, carry)
    else:
        carry = lax.fori_loop(0, tl, step, carry, unroll=V7X_SUBLANES)
    h_scr[0], h_scr[1] = carry
    hr_ref[...], hi_ref[...] = carry

    y = _dot(x_scr[...].astype(BF16), bdc_ref[...]) + d_ref[...] * u.astype(F32)
    y = _gelu_tanh(y)
    o_ref[...] = (y * _sigmoid(_dot(y.astype(BF16), gw_ref[...]) + gb_ref[...])).astype(BF16)


def _s5(u_tm, disc, d, glu_w_bf, glu_b, state, batch, seq_len):
    ar, ai, bdb, bdc = disc
    tl = min(S5_TIME_TILE, seq_len)
    has_state = state is not None
    consts = [ar, ai, bdb, bdc, d.reshape(1, S5_W), glu_w_bf, glu_b.reshape(1, S5_W)]
    h_spec = pl.BlockSpec((batch, S5_LANES), lambda i: (0, 0))
    in_specs = [pl.BlockSpec((tl * batch, S5_W), lambda i: (i, 0))] + [_const_spec(a.shape) for a in consts]
    args = [u_tm] + consts
    if has_state:
        in_specs += [h_spec, h_spec]
        args += list(state)
    return pl.pallas_call(
        functools.partial(_s5_body, tl=tl, batch=batch, has_state=has_state),
        grid=(seq_len // tl,),
        in_specs=in_specs,
        out_specs=[pl.BlockSpec((tl * batch, S5_W), lambda i: (i, 0)), h_spec, h_spec],
        out_shape=[jax.ShapeDtypeStruct((seq_len * batch, S5_W), BF16),
                   jax.ShapeDtypeStruct((batch, S5_LANES), F32), jax.ShapeDtypeStruct((batch, S5_LANES), F32)],
        scratch_shapes=[pltpu.VMEM((tl * batch, 2 * S5_LANES), F32), pltpu.VMEM((2, batch, S5_LANES), F32)],
        compiler_params=_params(),
        name="s5",
    )(*args)


def _ffn_body(*refs, tm, time_major_batch, final):
    (x_ref, oret_ref, os5_ref, ogla_ref, wout_ref, gffn_ref, win_ref, cw_ref, cb_ref, wo_ref) = refs[:10]
    n = 10
    gfin_ref = None
    if final:
        gfin_ref, n = refs[n], n + 1
    if time_major_batch is not None:
        past_ref, n = refs[n], n + 1
    out_ref, conv_ref, x1_scr, h_scr, carry_scr, act_scr = refs[n:]
    pad = V7X_SUBLANES
    nb = time_major_batch

    if nb is None:
        @pl.when(pl.program_id(1) == 0)
        def _():
            carry_scr[...] = jnp.zeros_like(carry_scr)

    x1_scr[...] = (x_ref[...] + _dot(oret_ref[...], wout_ref[0:RET_W, :])
                   + _dot(os5_ref[...], wout_ref[RET_W:RET_W + S5_W, :])
                   + _dot(ogla_ref[...], wout_ref[RET_W + S5_W:D_MODEL, :]))
    h_scr[...] = _rmsnorm_rows(x1_scr[...], gffn_ref[...]).astype(BF16)
    h = h_scr[...]

    for c in range(D_FF // ACT_COLS):
        cols = slice(c * ACT_COLS, (c + 1) * ACT_COLS)
        a_c = _dot(h, win_ref[:, c * ACT_COLS:(c + 1) * ACT_COLS])
        gate_c = _dot(h, win_ref[:, D_FF + c * ACT_COLS:D_FF + (c + 1) * ACT_COLS])
        if nb is None:
            ext = jnp.concatenate([carry_scr[:, cols], a_c], axis=0)
            prev1 = ext[pad - 1:pad - 1 + tm]
            prev2 = ext[pad - 2:pad - 2 + tm]
            carry_scr[:, cols] = a_c[tm - pad:tm]
        else:
            prev1 = jnp.concatenate([past_ref[1, :, cols], a_c[0:tm - nb]], axis=0)
            prev2 = jnp.concatenate([past_ref[0, :, cols], past_ref[1, :, cols], a_c[0:tm - 2 * nb]], axis=0)
            conv_ref[:, cols] = a_c[tm - (CONV_W - 1) * nb:tm]
        conv = (cb_ref[:, cols] + prev2 * cw_ref[0:1, cols] + prev1 * cw_ref[1:2, cols]
                + a_c * cw_ref[2:3, cols])
        act_scr[:, cols] = (_gelu_tanh(conv) * gate_c).astype(BF16)

    x2 = x1_scr[...] + _dot(act_scr[...], wo_ref[...])
    out_ref[...] = _rmsnorm_rows(x2, gfin_ref[...]) if final else x2
    if nb is None:
        conv_ref[0] = carry_scr[pad - (CONV_W - 1):pad, :]


def _ffn(x2d, o_ret, o_s5, o_gla, w_out_bf, norm_g, w_in_bf, conv_w, conv_b, w_o_bf, layer, final_g, conv_past,
         batch, seq_len):
    rows = batch * seq_len
    time_major = conv_past is not None
    tm = rows if time_major else min(ROW_TILE, seq_len)
    final = final_g is not None
    cw = jnp.concatenate([conv_w.astype(F32), jnp.zeros((V7X_SUBLANES - CONV_W, D_FF), F32)], axis=0)
    consts = [w_out_bf, norm_g.reshape(1, D_MODEL), w_in_bf, cw, conv_b.reshape(1, D_FF), w_o_bf]
    stacked = [True, False, True, False, False, True]
    if final:
        consts.append(final_g.reshape(1, D_MODEL))
        stacked.append(False)
    if time_major:
        assert seq_len >= CONV_W - 1 and batch % V7X_SUBLANES == 0
        consts.append(conv_past)
        stacked.append(False)
        grid = (1,)
        imap = lambda i: (0, 0)
        conv_rows = (CONV_W - 1) * batch
        conv_spec = pl.BlockSpec((conv_rows, D_FF), imap)
        conv_shape = jax.ShapeDtypeStruct((conv_rows, D_FF), F32)
    else:
        assert seq_len % tm == 0
        nl = seq_len // tm
        grid = (batch, nl)
        imap = lambda b, l: (b * nl + l, 0)
        conv_spec = pl.BlockSpec((1, CONV_W - 1, D_FF), lambda b, l: (b, 0, 0))
        conv_shape = jax.ShapeDtypeStruct((batch, CONV_W - 1, D_FF), F32)
    row_spec = lambda w: pl.BlockSpec((tm, w), imap)
    in_specs = ([row_spec(D_MODEL), row_spec(RET_W), row_spec(S5_W), row_spec(GLA_W)]
                + [_layer_spec(a.shape, layer, True) if s else _const_spec(a.shape) for a, s in zip(consts, stacked)])
    return pl.pallas_call(
        functools.partial(_ffn_body, tm=tm, time_major_batch=batch if time_major else None, final=final),
        grid=grid,
        in_specs=in_specs,
        out_specs=[row_spec(D_MODEL), conv_spec],
        out_shape=[jax.ShapeDtypeStruct((rows, D_MODEL), F32), conv_shape],
        scratch_shapes=[pltpu.VMEM((tm, D_MODEL), F32), pltpu.VMEM((tm, D_MODEL), BF16),
                        pltpu.VMEM((V7X_SUBLANES, D_FF), F32), pltpu.VMEM((tm, D_FF), BF16)],
        compiler_params=_params(),
        name="ffn",
    )(x2d, o_ret, o_s5, o_gla, *consts)


def _rotary_tables(pos):
    half = RET_HD // 2
    inv = ROPE_BASE ** (-jnp.arange(half, dtype=F32) / half)
    ang = pos.astype(F32)[:, None] * inv[None, :]
    cos = jnp.tile(jnp.concatenate([jnp.cos(ang), jnp.cos(ang)], axis=1), (1, RET_HEADS))
    sin = jnp.tile(jnp.concatenate([-jnp.sin(ang), jnp.sin(ang)], axis=1), (1, RET_HEADS))
    return cos, sin


def _swap_major(a2d, outer, inner):
    return a2d.reshape(outer, inner, -1).transpose(1, 0, 2).reshape(outer * inner, -1)


def _run_prompt_group(x, layers, big, final_g):
    batch, seq_len, _ = x.shape
    rows = batch * seq_len
    tm = min(ROW_TILE, seq_len)
    cos_tab, sin_tab = _rotary_tables(jnp.arange(seq_len, dtype=jnp.int32))
    x2d = x.reshape(rows, D_MODEL)
    outs = []
    for li, lp in enumerate(layers):
        ret, su, gla, glr = _mix_in(x2d, lp["norm_mix_g"], big["w_in"], li, cos_tab, sin_tab, tm)
        o_ret, s_ret = _retention(ret, lp["ret_norm_g"], lp["ret_norm_b"], batch, seq_len)
        o_s5, s5r, s5i = _s5(_swap_major(su, batch, seq_len), lp["s5_disc"], lp["s5_d"], lp["s5_glu_w"],
                             lp["s5_glu_b"], None, batch, seq_len)
        o_s5 = _swap_major(o_s5, seq_len, batch)
        o_gla, s_gla = _gla(gla, glr, lp["gla_gate_w"], lp["gla_gate_b"], lp["gla_norm_g"], batch, seq_len)
        last = li == len(layers) - 1
        x2d, conv_new = _ffn(x2d, o_ret, o_s5, o_gla, big["w_out"], lp["norm_ffn_g"], big["ffn_w_in"],
                             lp["ffn_conv_w"], lp["ffn_conv_b"], big["ffn_w_out"], li, final_g if last else None,
                             None, batch, seq_len)
        s5_shape = (batch, S5_GROUPS, S5_STATE)
        outs.append((s_ret, s5r.reshape(s5_shape), s5i.reshape(s5_shape), s_gla, conv_new))
    return [x2d.reshape(batch, seq_len, D_MODEL)] + [jnp.stack([o[i] for o in outs]) for i in range(5)]


def _run_sample_group(x, past_len, states, layers, big, final_g):
    batch, seq_len, _ = x.shape
    assert batch == V7X_LANES
    rows = batch * seq_len
    pos = past_len + jnp.arange(seq_len, dtype=jnp.int32)
    cos_tab, sin_tab = (jnp.repeat(t, batch, axis=0) for t in _rotary_tables(pos))
    x2d = x.transpose(1, 0, 2).reshape(rows, D_MODEL)
    ret_state = states["ret"].transpose(0, 2, 3, 4, 1)
    gla_state = states["gla"].transpose(0, 2, 3, 4, 1)
    conv_state = states["conv"].transpose(0, 2, 1, 3)
    depth = len(layers)
    s5_flat = lambda s: s.reshape(depth, batch, S5_LANES)
    s5r_state, s5i_state = s5_flat(states["s5r"]), s5_flat(states["s5i"])
    outs = []
    for li, lp in enumerate(layers):
        ret, su, gla, glr = _mix_in(x2d, lp["norm_mix_g"], big["w_in"], li, cos_tab, sin_tab, rows)
        o_ret, s_ret = _retention_lanes(ret, lp["ret_norm_g"], lp["ret_norm_b"], ret_state, li, seq_len)
        o_s5, s5r, s5i = _s5(su, lp["s5_disc"], lp["s5_d"], lp["s5_glu_w"], lp["s5_glu_b"],
                             (s5r_state[li], s5i_state[li]), batch, seq_len)
        o_gla, s_gla = _gla_lanes(gla, glr, lp["gla_gate_w"], lp["gla_gate_b"], lp["gla_norm_g"], gla_state, li,
                                  seq_len)
        last = li == len(layers) - 1
        x2d, conv_new = _ffn(x2d, o_ret, o_s5, o_gla, big["w_out"], lp["norm_ffn_g"], big["ffn_w_in"],
                             lp["ffn_conv_w"], lp["ffn_conv_b"], big["ffn_w_out"], li, final_g if last else None,
                             conv_state[li], batch, seq_len)
        outs.append((s_ret, s5r, s5i, s_gla, conv_new.reshape(CONV_W - 1, batch, D_FF)))
    s_ret, s5r, s5i, s_gla, conv_new = (jnp.stack([o[i] for o in outs]) for i in range(5))
    s5_shape = (depth, batch, S5_GROUPS, S5_STATE)
    return [x2d.reshape(seq_len, batch, D_MODEL).transpose(1, 0, 2), s_ret.transpose(0, 4, 1, 2, 3),
            s5r.reshape(s5_shape), s5i.reshape(s5_shape), s_gla.transpose(0, 4, 1, 2, 3),
            conv_new.transpose(0, 2, 1, 3)]


def kernel(x_prompt, x_sample, state_ret, state_s5_re, state_s5_im, state_gla, state_ffn_conv, norm_mix_g, w_in, ret_norm_g, ret_norm_b, s5_lambda_re, s5_lambda_im, s5_log_dt, s5_b_re, s5_b_im, s5_c_re, s5_c_im, s5_d, s5_glu_w, s5_glu_b, gla_gate_w, gla_gate_b, gla_norm_g, w_out, norm_ffn_g, ffn_w_in, ffn_conv_w, ffn_conv_b, ffn_w_out, norm_final_g):
    depth = w_in.shape[0]
    big = dict(w_in=w_in.astype(BF16), w_out=w_out.astype(BF16), ffn_w_in=ffn_w_in.astype(BF16),
               ffn_w_out=ffn_w_out.astype(BF16))
    layers = []
    for l in range(depth):
        layers.append(dict(
            norm_mix_g=norm_mix_g[l], ret_norm_g=ret_norm_g[l], ret_norm_b=ret_norm_b[l],
            s5_disc=_s5_discretize(s5_lambda_re[l], s5_lambda_im[l], s5_log_dt[l], s5_b_re[l], s5_b_im[l],
                                   s5_c_re[l], s5_c_im[l]),
            s5_d=s5_d[l], s5_glu_w=s5_glu_w[l].astype(BF16), s5_glu_b=s5_glu_b[l],
            gla_gate_w=gla_gate_w[l], gla_gate_b=gla_gate_b[l], gla_norm_g=gla_norm_g[l],
            norm_ffn_g=norm_ffn_g[l], ffn_conv_w=ffn_conv_w[l], ffn_conv_b=ffn_conv_b[l]))
    sample_states = dict(ret=state_ret, s5r=state_s5_re, s5i=state_s5_im, gla=state_gla, conv=state_ffn_conv)
    yp, ret_p, s5r_p, s5i_p, gla_p, conv_p = _run_prompt_group(x_prompt, layers, big, norm_final_g)
    ys, ret_s, s5r_s, s5i_s, gla_s, conv_s = _run_sample_group(x_sample, PAST_LEN, sample_states, layers, big,
                                                               norm_final_g)
    return (yp, ys, ret_p, ret_s, s5r_p, s5r_s, s5i_p, s5i_s, gla_p, gla_s, conv_p, conv_s)
```

```python
import functools

import numpy as np
import jax
import jax.numpy as jnp
from jax import lax
from jax.experimental import pallas as pl
from jax.experimental.pallas import tpu as pltpu

F32, BF16 = jnp.float32, jnp.bfloat16

D_MODEL = 1024
RET_W, S5_W, GLA_W = 384, 256, 384
RET_HEADS, RET_HD = 6, 64
RET_PAIRS = RET_HEADS // 2
S5_GROUPS, S5_CH, S5_STATE = 16, 16, 64
S5_LANES = S5_GROUPS * S5_STATE
S5_MIN_NEG = 1e-4
GLA_HEADS, GLA_DK, GLA_DV, GLA_RANK = 4, 48, 96, 16
GLA_KW = GLA_HEADS * GLA_DK
GLA_GATE_TEMP = 16.0
D_FF = 2816
CONV_W = 3
ROPE_BASE = 10000.0
CHUNK = 64
EPS = 1e-6
PAST_LEN = 16384
IN_COLS = 4 * RET_W + S5_W + 2 * GLA_KW + 2 * GLA_W + GLA_RANK
COL_RET, COL_S5, COL_GLA, COL_LR = 0, 4 * RET_W, 4 * RET_W + S5_W, IN_COLS - GLA_RANK
GLA_COLS = 2 * GLA_KW + 2 * GLA_W

V7X_SUBLANES = 8
V7X_LANES = 128
VMEM_LIMIT = 58 * 1024 * 1024

ROW_TILE = 512
S5_TIME_TILE = 128
S5_SUB_STEPS = 64
ACT_COLS = 256


def _dot(a, b):
    return jnp.dot(a, b, preferred_element_type=F32)


def _dot_nt(a, b):
    return lax.dot_general(a, b, (((1,), (1,)), ((), ())), preferred_element_type=F32)


def _dot_tn(a, b):
    return lax.dot_general(a, b, (((0,), (0,)), ((), ())), preferred_element_type=F32)


def _hi_lo(x):
    hi = x.astype(BF16)
    return hi, (x - hi.astype(F32)).astype(BF16)


def _dot_exact_lhs(m, x):
    hi, lo = _hi_lo(x)
    return _dot(m, hi) + _dot(m, lo)


def _dot3(a, b):
    ah, al = _hi_lo(a)
    bh, bl = _hi_lo(b)
    return _dot(ah, bh) + _dot(ah, bl) + _dot(al, bh)


def _sigmoid(x):
    return 1.0 / (1.0 + jnp.exp(-x))


def _log_sigmoid(z):
    return jnp.minimum(z, 0.0) - jnp.log(1.0 + jnp.exp(-jnp.abs(z)))


def _gelu_tanh(x):
    return 0.5 * x * (1.0 + jnp.tanh(0.7978845608028654 * (x + 0.044715 * (x * x * x))))


def _rmsnorm_rows(x, g):
    return x * lax.rsqrt(jnp.mean(x * x, axis=-1, keepdims=True) + EPS) * g


def _const_spec(shape):
    nd = len(shape)
    return pl.BlockSpec(shape, lambda *_: (0,) * nd)


def _layer_spec(stacked_shape, layer, single_buffer=False):
    nd = len(stacked_shape) - 1
    mode = dict(pipeline_mode=pl.Buffered(1)) if single_buffer else {}
    return pl.BlockSpec((None,) + tuple(stacked_shape[1:]), lambda *_: (layer,) + (0,) * nd, **mode)


def _params():
    return pltpu.CompilerParams(vmem_limit_bytes=VMEM_LIMIT)


def _to_lanes(x):
    w = x.shape[1]
    pad = -w % V7X_LANES
    if pad:
        x = jnp.concatenate([x, jnp.zeros((x.shape[0], pad), F32)], axis=1)
    return x.T[0:w]


def _mix_in_body(x_ref, g_ref, w_ref, cos_ref, sin_ref, ret_ref, su_ref, gla_ref, glr_ref):
    h = _rmsnorm_rows(x_ref[...], g_ref[...]).astype(BF16)
    ret = _dot(h, w_ref[:, COL_RET:COL_S5])
    cos = cos_ref[...]
    sin = sin_ref[...]
    lane = lax.broadcasted_iota(jnp.int32, cos.shape, 1)
    first_half = (lane & (RET_HD // 2)) == 0

    def rotary(z):
        swapped = jnp.where(first_half, pltpu.roll(z, RET_W - RET_HD // 2, 1), pltpu.roll(z, RET_HD // 2, 1))
        return z * cos + swapped * sin

    ret_ref[:, 0:RET_W] = rotary(ret[:, 0:RET_W]).astype(BF16)
    ret_ref[:, RET_W:2 * RET_W] = (rotary(ret[:, RET_W:2 * RET_W]) * RET_HD ** -0.5).astype(BF16)
    ret_ref[:, 2 * RET_W:] = ret[:, 2 * RET_W:].astype(BF16)
    su_ref[...] = _dot(h, w_ref[:, COL_S5:COL_GLA]).astype(BF16)
    gla = _dot(h, w_ref[:, COL_GLA:IN_COLS])
    glane = lax.broadcasted_iota(jnp.int32, (1, GLA_COLS), 1)
    gla_ref[...] = (gla[:, 0:GLA_COLS] * jnp.where(glane < GLA_KW, GLA_DK ** -0.5, 1.0)).astype(BF16)
    glr_ref[...] = gla[:, GLA_COLS:GLA_COLS + GLA_RANK]


def _mix_in(x2d, norm_g, w_in_bf, layer, cos_tab, sin_tab, tm):
    rows = x2d.shape[0]
    n_tab = cos_tab.shape[0] // tm
    row_spec = lambda w: pl.BlockSpec((tm, w), lambda i: (i, 0))
    tab_spec = pl.BlockSpec((tm, RET_W), lambda i: (i % n_tab, 0))
    return pl.pallas_call(
        _mix_in_body,
        grid=(rows // tm,),
        in_specs=[row_spec(D_MODEL), _const_spec((1, D_MODEL)), _layer_spec(w_in_bf.shape, layer, True),
                  tab_spec, tab_spec],
        out_specs=[row_spec(4 * RET_W), row_spec(S5_W), row_spec(GLA_COLS), row_spec(GLA_RANK)],
        out_shape=[jax.ShapeDtypeStruct((rows, 4 * RET_W), BF16), jax.ShapeDtypeStruct((rows, S5_W), BF16),
                   jax.ShapeDtypeStruct((rows, GLA_COLS), BF16), jax.ShapeDtypeStruct((rows, GLA_RANK), F32)],
        compiler_params=_params(),
        name="mix_in",
    )(x2d, norm_g.reshape(1, D_MODEL), w_in_bf, cos_tab, sin_tab)


def _retention_gammas():
    return 1.0 - 2.0 ** (-5.0 - np.arange(RET_HEADS))


def _retention_consts():
    t = np.arange(CHUNK)
    gam = _retention_gammas()
    causal = t[:, None] >= t[None, :]
    diff = np.maximum(t[:, None] - t[None, :], 0)
    dmask = np.zeros((RET_PAIRS, CHUNK, 2 * CHUNK))
    cdec = np.zeros((RET_PAIRS, 2 * RET_HD, 2 * RET_HD))
    for p in range(RET_PAIRS):
        for s in range(2):
            g = gam[2 * p + s]
            dmask[p, :, s * CHUNK:(s + 1) * CHUNK] = np.where(causal, g ** diff, 0.0)
            cdec[p, s * RET_HD:(s + 1) * RET_HD, s * RET_HD:(s + 1) * RET_HD] = g ** CHUNK
    lane_gam = np.repeat(gam, RET_HD)[None, :]
    qdec = lane_gam ** (t[:, None] + 1.0)
    kdec = lane_gam ** (CHUNK - 1.0 - t[:, None])
    bd = (cdec[0] > 0).astype(np.float32)
    head_mask = np.stack([np.arange(2 * RET_HD) < RET_HD, np.arange(2 * RET_HD) >= RET_HD]).astype(np.float32)
    ones_blk = np.kron(np.eye(RET_HEADS), np.ones((RET_HD, RET_HD)))
    f = lambda a: jnp.asarray(a, F32)
    return dict(dmask=f(dmask), cdec=f(cdec), qdec=f(qdec), kdec=f(kdec), bd=f(bd),
                head_mask=jnp.asarray(head_mask, BF16), ones_blk=jnp.asarray(ones_blk, BF16))


def _retention_body(ret_ref, dmask_ref, cdec_ref, qdec_ref, kdec_ref, bd_ref, hm_ref, ones_ref, lng_ref, lnb_ref,
                    o_ref, sout_ref, s_scr, o_scr, *, tl):
    @pl.when(pl.program_id(1) == 0)
    def _():
        s_scr[...] = jnp.zeros_like(s_scr)

    bd = bd_ref[...]
    m0 = hm_ref[0:1, :]
    m1 = hm_ref[1:2, :]
    q = ret_ref[:, 0:RET_W]
    k = ret_ref[:, RET_W:2 * RET_W]
    v = ret_ref[:, 2 * RET_W:3 * RET_W]
    q_start = (q.astype(F32) * qdec_ref[...]).astype(BF16)
    k_end = (k.astype(F32) * kdec_ref[...]).astype(BF16)
    blocks = [(n, p) for n in range(tl // CHUNK) for p in range(RET_PAIRS)]

    def part(a, n, p):
        return a[n * CHUNK:(n + 1) * CHUNK, 2 * RET_HD * p:2 * RET_HD * (p + 1)]

    scores, o, kv = {}, {}, {}
    for n, p in blocks:
        kp = part(k, n, p)
        kk = jnp.concatenate([kp * m0, kp * m1], axis=0)
        scores[n, p] = (_dot_nt(part(q, n, p), kk) * dmask_ref[p]).astype(BF16)
    for n, p in blocks:
        vp = part(v, n, p)
        o[n, p] = _dot(scores[n, p], jnp.concatenate([vp * m0, vp * m1], axis=0))
    for n, p in blocks:
        kv[n, p] = _dot_tn(part(k_end, n, p), part(v, n, p)) * bd
    for n, p in blocks:
        s_prev = s_scr[p]
        o_scr[n * CHUNK:(n + 1) * CHUNK, 2 * RET_HD * p:2 * RET_HD * (p + 1)] = (
            o[n, p] + _dot(part(q_start, n, p), s_prev.astype(BF16)))
        s_scr[p] = s_prev * cdec_ref[p] + kv[n, p]

    o_all = o_scr[...]
    ones_blk = ones_ref[...]
    mu = _dot(o_all.astype(BF16), ones_blk) * (1.0 / RET_HD)
    cen = o_all - mu
    var = _dot((cen * cen).astype(BF16), ones_blk) * (1.0 / RET_HD)
    y = cen * lax.rsqrt(var + EPS) * lng_ref[...] + lnb_ref[...]
    gate = ret_ref[:, 3 * RET_W:4 * RET_W].astype(F32)
    o_ref[...] = (y * gate * _sigmoid(gate)).astype(BF16)
    for p in range(RET_PAIRS):
        s = s_scr[p]
        sout_ref[0, 2 * p] = s[0:RET_HD, 0:RET_HD]
        sout_ref[0, 2 * p + 1] = s[RET_HD:2 * RET_HD, RET_HD:2 * RET_HD]


def _retention(ret2d, ln_g, ln_b, batch, seq_len):
    assert seq_len % CHUNK == 0
    c = _retention_consts()
    tl = min(ROW_TILE, seq_len)
    nl = seq_len // tl
    tile_rows = lambda a: jnp.tile(a, (tl // CHUNK, 1))
    consts = [c["dmask"], c["cdec"], tile_rows(c["qdec"]), tile_rows(c["kdec"]), c["bd"], c["head_mask"],
              c["ones_blk"], ln_g.reshape(1, RET_W), ln_b.reshape(1, RET_W)]
    state_shape = (batch, RET_HEADS, RET_HD, RET_HD)
    return pl.pallas_call(
        functools.partial(_retention_body, tl=tl),
        grid=(batch, nl),
        in_specs=[pl.BlockSpec((tl, 4 * RET_W), lambda b, l: (b * nl + l, 0))] + [_const_spec(a.shape) for a in consts],
        out_specs=[pl.BlockSpec((tl, RET_W), lambda b, l: (b * nl + l, 0)),
                   pl.BlockSpec((1,) + state_shape[1:], lambda b, l: (b, 0, 0, 0))],
        out_shape=[jax.ShapeDtypeStruct((batch * seq_len, RET_W), BF16), jax.ShapeDtypeStruct(state_shape, F32)],
        scratch_shapes=[pltpu.VMEM((RET_PAIRS, 2 * RET_HD, 2 * RET_HD), F32), pltpu.VMEM((tl, RET_W), F32)],
        compiler_params=_params(),
        name="retention",
    )(ret2d, *consts)


def _retention_lanes_body(ret_ref, tab_ref, lng_ref, lnb_ref, s_ref, o_ref, sout_ref, q_scr, k_scr, v_scr, g_scr,
                          o_scr, *, seq_len):
    h = pl.program_id(0)
    nb = V7X_LANES
    steps = range(seq_len)

    @pl.when(h == 0)
    def _():
        for t in steps:
            blk = ret_ref[t * nb:(t + 1) * nb, :].astype(F32)
            for scr, off in ((q_scr, 0), (k_scr, RET_W), (v_scr, 2 * RET_W), (g_scr, 3 * RET_W)):
                tr = _to_lanes(blk[:, off:off + RET_W])
                for hh in range(RET_HEADS):
                    scr[t, hh] = tr[hh * RET_HD:(hh + 1) * RET_HD]

    tab = tab_ref[h]
    row = lambda r: tab[r:r + 1, :]
    q = [q_scr[t, h] for t in steps]
    k = [k_scr[t, h] for t in steps]
    v = [v_scr[t, h] for t in steps]
    o = []
    for i in steps:
        acc = None
        for j in range(i + 1):
            s = jnp.sum(q[i] * k[j], axis=0, keepdims=True)
            if i > j:
                s = s * row(i - j - 1)
            acc = s * v[j] if acc is None else acc + s * v[j]
        o.append(acc)
    q_start = [q[t] * row(t) for t in steps]
    k_end = [k[t] * row(seq_len + t) for t in steps]
    chunk_decay = row(2 * seq_len)
    for d in range(RET_HD):
        s_d = s_ref[d]
        new = s_d * chunk_decay
        for t in steps:
            o[t] = o[t] + q_start[t][d:d + 1, :] * s_d
            new = new + k_end[t][d:d + 1, :] * v[t]
        sout_ref[d] = new
    for t in steps:
        mu = jnp.mean(o[t], axis=0, keepdims=True)
        cen = o[t] - mu
        var = jnp.mean(cen * cen, axis=0, keepdims=True)
        gate = g_scr[t, h]
        o_scr[t, h] = (cen * lax.rsqrt(var + EPS) * lng_ref[h] + lnb_ref[h]) * gate * _sigmoid(gate)

    @pl.when(h == RET_HEADS - 1)
    def _():
        for t in steps:
            o_ref[t * nb:(t + 1) * nb, :] = o_scr[t].reshape(RET_W, nb).T.astype(BF16)


def _retention_lanes(ret2d, ln_g, ln_b, state, layer, seq_len):
    nb = V7X_LANES
    gam = _retention_gammas()[:, None]
    t = np.arange(seq_len)[None, :]
    rows = np.concatenate([gam ** (t + 1.0), gam ** (seq_len - 1.0 - t), gam ** (seq_len + 0.0 * t[:, :1])], axis=1)
    n_rows = -(-rows.shape[1] // V7X_SUBLANES) * V7X_SUBLANES
    tab = np.zeros((RET_HEADS, n_rows, nb))
    tab[:, :rows.shape[1], :] = rows[:, :, None]
    lanes = lambda p: jnp.broadcast_to(p.reshape(RET_HEADS, RET_HD, 1), (RET_HEADS, RET_HD, nb))
    consts = [jnp.asarray(tab, F32), lanes(ln_g), lanes(ln_b)]
    tile = pltpu.VMEM((seq_len, RET_HEADS, RET_HD, nb), F32)
    return pl.pallas_call(
        functools.partial(_retention_lanes_body, seq_len=seq_len),
        grid=(RET_HEADS,),
        in_specs=[_const_spec(ret2d.shape)] + [_const_spec(a.shape) for a in consts]
                 + [pl.BlockSpec((None, None, RET_HD, RET_HD, nb), lambda h: (layer, h, 0, 0, 0))],
        out_specs=[_const_spec((seq_len * nb, RET_W)), pl.BlockSpec((None, RET_HD, RET_HD, nb), lambda h: (h, 0, 0, 0))],
        out_shape=[jax.ShapeDtypeStruct((seq_len * nb, RET_W), BF16),
                   jax.ShapeDtypeStruct((RET_HEADS, RET_HD, RET_HD, nb), F32)],
        scratch_shapes=[tile] * 5,
        compiler_params=_params(),
        name="retention_lanes",
    )(ret2d, *consts, state)


def _gla_consts():
    t = np.arange(CHUNK)
    causal = t[:, None] >= t[None, :]
    kmask = np.kron(np.eye(GLA_HEADS), np.ones((1, GLA_DK)))
    vmask = np.kron(np.eye(GLA_HEADS), np.ones((1, GLA_DV)))
    bd = np.kron(np.eye(GLA_HEADS), np.ones((GLA_DK, GLA_DV)))
    ones_blk = np.kron(np.eye(GLA_HEADS), np.ones((GLA_DV, GLA_DV)))
    b = lambda a: jnp.asarray(a, BF16)
    return dict(tril=b(causal), causal4=jnp.asarray(np.tile(causal, (1, GLA_HEADS)), F32), kmask=b(kmask),
                vmask=b(vmask), bd=jnp.asarray(bd, F32), ones_blk=b(ones_blk))


def _gla_body(gla_ref, glr_ref, gw_ref, gb_ref, tril_ref, causal_ref, kmask_ref, vmask_ref, bd_ref, ones_blk_ref,
              block_ref, ng_ref, o_ref, sout_ref, s_scr, o_scr, *, tl):
    @pl.when(pl.program_id(1) == 0)
    def _():
        s_scr[...] = jnp.zeros_like(s_scr)

    bd = bd_ref[...]
    lg = _log_sigmoid(_dot3(glr_ref[...], gw_ref[...]) + gb_ref[...]) * (1.0 / GLA_GATE_TEMP)
    q0, k0, v0, g0 = 0, GLA_KW, 2 * GLA_KW, 2 * GLA_KW + GLA_W
    chunks = range(tl // CHUNK)

    def rows(a, n):
        return a[n * CHUNK:(n + 1) * CHUNK]

    cums = [_dot_exact_lhs(tril_ref[...], rows(lg, n)) for n in chunks]
    b_cum = jnp.concatenate(cums, axis=0)
    b_end = jnp.concatenate([jnp.broadcast_to(c[CHUNK - 1:CHUNK, :], (CHUNK, GLA_KW)) for c in cums], axis=0)
    q_in = (gla_ref[:, q0:k0].astype(F32) * jnp.exp(b_cum)).astype(BF16)
    kf = gla_ref[:, k0:v0].astype(F32)
    k_in = (kf * jnp.exp(-b_cum)).astype(BF16)
    k_dec = (kf * jnp.exp(b_end - b_cum)).astype(BF16)
    v = gla_ref[:, v0:g0]
    lg_hi, lg_lo = _hi_lo(lg)
    block_decay = jnp.exp(_dot_tn(lg_hi, block_ref[...]) + _dot_tn(lg_lo, block_ref[...]))

    scores, o, kv = {}, {}, {}
    for n in chunks:
        kn = rows(k_in, n)
        kk = jnp.concatenate([kn * kmask_ref[h:h + 1, :] for h in range(GLA_HEADS)], axis=0)
        scores[n] = (_dot_nt(rows(q_in, n), kk) * causal_ref[...]).astype(BF16)
    for n in chunks:
        vn = rows(v, n)
        vv = jnp.concatenate([vn * vmask_ref[h:h + 1, :] for h in range(GLA_HEADS)], axis=0)
        o[n] = _dot(scores[n], vv)
    for n in chunks:
        kv[n] = _dot_tn(rows(k_dec, n), rows(v, n)) * bd
    for n in chunks:
        s_prev = s_scr[...]
        o_scr[n * CHUNK:(n + 1) * CHUNK, :] = o[n] + _dot(rows(q_in, n), s_prev.astype(BF16))
        s_scr[...] = s_prev * jnp.broadcast_to(block_decay[:, n:n + 1], (GLA_KW, GLA_W)) + kv[n]

    o_all = o_scr[...]
    ms = _dot((o_all * o_all).astype(BF16), ones_blk_ref[...]) * (1.0 / GLA_DV)
    gate = gla_ref[:, g0:g0 + GLA_W].astype(F32)
    o_ref[...] = (o_all * lax.rsqrt(ms + EPS) * ng_ref[...] * gate * _sigmoid(gate)).astype(BF16)
    s = s_scr[...]
    for h in range(GLA_HEADS):
        sout_ref[0, h] = s[h * GLA_DK:(h + 1) * GLA_DK, h * GLA_DV:(h + 1) * GLA_DV]


def _gla(gla2d, glr2d, gate_w, gate_b, norm_g, batch, seq_len):
    assert seq_len % CHUNK == 0
    c = _gla_consts()
    tl = min(ROW_TILE, seq_len)
    nl = seq_len // tl
    assert tl // CHUNK <= V7X_LANES
    block = (np.arange(tl) // CHUNK)[:, None] == np.arange(V7X_LANES)[None, :]
    consts = [gate_w, gate_b.reshape(1, GLA_KW), c["tril"], c["causal4"], c["kmask"], c["vmask"], c["bd"],
              c["ones_blk"], jnp.asarray(block, BF16), jnp.tile(norm_g, GLA_HEADS).reshape(1, GLA_W)]
    row_spec = lambda w: pl.BlockSpec((tl, w), lambda b, l: (b * nl + l, 0))
    state_shape = (batch, GLA_HEADS, GLA_DK, GLA_DV)
    return pl.pallas_call(
        functools.partial(_gla_body, tl=tl),
        grid=(batch, nl),
        in_specs=[row_spec(GLA_COLS), row_spec(GLA_RANK)] + [_const_spec(a.shape) for a in consts],
        out_specs=[row_spec(GLA_W), pl.BlockSpec((1,) + state_shape[1:], lambda b, l: (b, 0, 0, 0))],
        out_shape=[jax.ShapeDtypeStruct((batch * seq_len, GLA_W), BF16), jax.ShapeDtypeStruct(state_shape, F32)],
        scratch_shapes=[pltpu.VMEM((GLA_KW, GLA_W), F32), pltpu.VMEM((tl, GLA_W), F32)],
        compiler_params=_params(),
        name="gla",
    )(gla2d, glr2d, *consts)


def _gla_lanes_body(gla_ref, glr_ref, gw_ref, gb_ref, ng_ref, s_ref, o_ref, sout_ref, q_scr, k_scr, kd_scr, v_scr,
                    g_scr, dec_scr, o_scr, *, seq_len):
    h = pl.program_id(0)
    nb = V7X_LANES
    steps = range(seq_len)
    q0, k0, v0, g0 = 0, GLA_KW, 2 * GLA_KW, 2 * GLA_KW + GLA_W

    def split(scr, t, tr, width):
        for hh in range(GLA_HEADS):
            scr[t, hh] = tr[hh * width:(hh + 1) * width]

    @pl.when(h == 0)
    def _():
        lg = _log_sigmoid(_dot3(glr_ref[...], gw_ref[...]) + gb_ref[...]) * (1.0 / GLA_GATE_TEMP)
        cums = []
        for t in steps:
            lg_t = lg[t * nb:(t + 1) * nb]
            cums.append(lg_t if t == 0 else cums[-1] + lg_t)
        b_end = cums[-1]
        dec = _to_lanes(jnp.exp(b_end))
        for hh in range(GLA_HEADS):
            dec_scr[hh] = dec[hh * GLA_DK:(hh + 1) * GLA_DK]
        for t in steps:
            blk = gla_ref[t * nb:(t + 1) * nb, :].astype(F32)
            kf = blk[:, k0:v0]
            split(q_scr, t, _to_lanes(blk[:, q0:k0] * jnp.exp(cums[t])), GLA_DK)
            split(k_scr, t, _to_lanes(kf * jnp.exp(-cums[t])), GLA_DK)
            split(kd_scr, t, _to_lanes(kf * jnp.exp(b_end - cums[t])), GLA_DK)
            split(v_scr, t, _to_lanes(blk[:, v0:g0]), GLA_DV)
            split(g_scr, t, _to_lanes(blk[:, g0:g0 + GLA_W]), GLA_DV)

    q = [q_scr[t, h] for t in steps]
    k = [k_scr[t, h] for t in steps]
    kd = [kd_scr[t, h] for t in steps]
    v = [v_scr[t, h] for t in steps]
    dec = dec_scr[h]
    o = []
    for i in steps:
        acc = None
        for j in range(i + 1):
            s = jnp.sum(q[i] * k[j], axis=0, keepdims=True)
            acc = s * v[j] if acc is None else acc + s * v[j]
        o.append(acc)
    for i in range(GLA_DK):
        s_i = s_ref[i]
        new = s_i * dec[i:i + 1, :]
        for t in steps:
            o[t] = o[t] + q[t][i:i + 1, :] * s_i
            new = new + kd[t][i:i + 1, :] * v[t]
        sout_ref[i] = new
    for t in steps:
        ms = jnp.mean(o[t] * o[t], axis=0, keepdims=True)
        gate = g_scr[t, h]
        o_scr[t, h] = o[t] * lax.rsqrt(ms + EPS) * ng_ref[...] * gate * _sigmoid(gate)

    @pl.when(h == GLA_HEADS - 1)
    def _():
        for t in steps:
            o_ref[t * nb:(t + 1) * nb, :] = o_scr[t].reshape(GLA_W, nb).T.astype(BF16)


def _gla_lanes(gla2d, glr2d, gate_w, gate_b, norm_g, state, layer, seq_len):
    nb = V7X_LANES
    consts = [gate_w, gate_b.reshape(1, GLA_KW), jnp.broadcast_to(norm_g.reshape(GLA_DV, 1), (GLA_DV, nb))]
    k_tile = pltpu.VMEM((seq_len, GLA_HEADS, GLA_DK, nb), F32)
    v_tile = pltpu.VMEM((seq_len, GLA_HEADS, GLA_DV, nb), F32)
    return pl.pallas_call(
        functools.partial(_gla_lanes_body, seq_len=seq_len),
        grid=(GLA_HEADS,),
        in_specs=[_const_spec(gla2d.shape), _const_spec(glr2d.shape)] + [_const_spec(a.shape) for a in consts]
                 + [pl.BlockSpec((None, None, GLA_DK, GLA_DV, nb), lambda h: (layer, h, 0, 0, 0))],
        out_specs=[_const_spec((seq_len * nb, GLA_W)), pl.BlockSpec((None, GLA_DK, GLA_DV, nb), lambda h: (h, 0, 0, 0))],
        out_shape=[jax.ShapeDtypeStruct((seq_len * nb, GLA_W), BF16),
                   jax.ShapeDtypeStruct((GLA_HEADS, GLA_DK, GLA_DV, nb), F32)],
        scratch_shapes=[k_tile, k_tile, k_tile, v_tile, v_tile, pltpu.VMEM((GLA_HEADS, GLA_DK, nb), F32), v_tile],
        compiler_params=_params(),
        name="gla_lanes",
    )(gla2d, glr2d, *consts, state)


def _s5_discretize(lam_re, lam_im, log_dt, b_re, b_im, c_re, c_im):
    lr = jnp.minimum(lam_re.astype(F32), -S5_MIN_NEG)
    li = lam_im.astype(F32)
    dt = jnp.exp(log_dt.astype(F32))[:, None]
    mag = jnp.exp(lr * dt)
    ar = mag * jnp.cos(li * dt)
    ai = mag * jnp.sin(li * dt)
    den = lr * lr + li * li
    cr = ((ar - 1.0) * lr + ai * li) / den
    ci = (ai * lr - (ar - 1.0) * li) / den
    b_re, b_im = b_re.astype(F32), b_im.astype(F32)
    bbar_re = cr[..., None] * b_re - ci[..., None] * b_im
    bbar_im = cr[..., None] * b_im + ci[..., None] * b_re
    eye = jnp.eye(S5_GROUPS, dtype=F32)
    in_blk = lambda b: jnp.einsum("gpi,gh->gihp", b, eye).reshape(S5_W, S5_LANES)
    out_blk = lambda c: jnp.einsum("gop,gh->gpho", c.astype(F32), eye).reshape(S5_LANES, S5_W)
    bdb = jnp.concatenate([in_blk(bbar_re), in_blk(bbar_im)], axis=1).astype(BF16)
    bdc = jnp.concatenate([out_blk(c_re), -out_blk(c_im)], axis=0).astype(BF16)
    return ar.reshape(1, S5_LANES), ai.reshape(1, S5_LANES), bdb, bdc


def _s5_body(*refs, tl, batch, has_state):
    u_ref, ar_ref, ai_ref, bdb_ref, bdc_ref, d_ref, gw_ref, gb_ref = refs[:8]
    n_in = 10 if has_state else 8
    o_ref, hr_ref, hi_ref, x_scr, h_scr = refs[n_in:]

    @pl.when(pl.program_id(0) == 0)
    def _():
        if has_state:
            h_scr[0] = refs[8][...]
            h_scr[1] = refs[9][...]
        else:
            h_scr[...] = jnp.zeros_like(h_scr)

    ar = jnp.broadcast_to(ar_ref[...], (batch, S5_LANES))
    ai = jnp.broadcast_to(ai_ref[...], (batch, S5_LANES))
    hr, hi = h_scr[0], h_scr[1]
    sub = min(tl, S5_SUB_STEPS)
    for s in range(tl // sub):
        rows = slice(s * sub * batch, (s + 1) * sub * batch)
        u = u_ref[rows, :]
        x_scr[rows, :] = _dot(u, bdb_ref[...])
        for t in range(s * sub, (s + 1) * sub):
            r = slice(t * batch, (t + 1) * batch)
            hr, hi = (ar * hr - ai * hi + x_scr[r, 0:S5_LANES],
                      ar * hi + ai * hr + x_scr[r, S5_LANES:2 * S5_LANES])
            x_scr[r, 0:S5_LANES] = hr
            x_scr[r, S5_LANES:2 * S5_LANES] = hi
        y = _dot(x_scr[rows, :].astype(BF16), bdc_ref[...]) + d_ref[...] * u.astype(F32)
        y = _gelu_tanh(y)
        o_ref[rows, :] = (y * _sigmoid(_dot(y.astype(BF16), gw_ref[...]) + gb_ref[...])).astype(BF16)
    h_scr[0] = hr
    h_scr[1] = hi
    hr_ref[...] = hr
    hi_ref[...] = hi


def _s5(u_tm, disc, d, glu_w_bf, glu_b, state, batch, seq_len):
    ar, ai, bdb, bdc = disc
    tl = min(S5_TIME_TILE, seq_len)
    has_state = state is not None
    consts = [ar, ai, bdb, bdc, d.reshape(1, S5_W), glu_w_bf, glu_b.reshape(1, S5_W)]
    h_spec = pl.BlockSpec((batch, S5_LANES), lambda i: (0, 0))
    in_specs = [pl.BlockSpec((tl * batch, S5_W), lambda i: (i, 0))] + [_const_spec(a.shape) for a in consts]
    args = [u_tm] + consts
    if has_state:
        in_specs += [h_spec, h_spec]
        args += list(state)
    return pl.pallas_call(
        functools.partial(_s5_body, tl=tl, batch=batch, has_state=has_state),
        grid=(seq_len // tl,),
        in_specs=in_specs,
        out_specs=[pl.BlockSpec((tl * batch, S5_W), lambda i: (i, 0)), h_spec, h_spec],
        out_shape=[jax.ShapeDtypeStruct((seq_len * batch, S5_W), BF16),
                   jax.ShapeDtypeStruct((batch, S5_LANES), F32), jax.ShapeDtypeStruct((batch, S5_LANES), F32)],
        scratch_shapes=[pltpu.VMEM((tl * batch, 2 * S5_LANES), F32), pltpu.VMEM((2, batch, S5_LANES), F32)],
        compiler_params=_params(),
        name="s5",
    )(*args)


def _ffn_body(*refs, tm, time_major_batch, final):
    (x_ref, oret_ref, os5_ref, ogla_ref, wout_ref, gffn_ref, win_ref, cw_ref, cb_ref, wo_ref) = refs[:10]
    n = 10
    gfin_ref = None
    if final:
        gfin_ref, n = refs[n], n + 1
    if time_major_batch is not None:
        past_ref, n = refs[n], n + 1
    out_ref, conv_ref, x1_scr, h_scr, carry_scr, act_scr = refs[n:]
    pad = V7X_SUBLANES
    nb = time_major_batch

    if nb is None:
        @pl.when(pl.program_id(1) == 0)
        def _():
            carry_scr[...] = jnp.zeros_like(carry_scr)

    mix = jnp.concatenate([oret_ref[...], os5_ref[...], ogla_ref[...]], axis=1)
    x1_scr[...] = x_ref[...] + _dot(mix, wout_ref[...])
    h_scr[...] = _rmsnorm_rows(x1_scr[...], gffn_ref[...]).astype(BF16)
    h = h_scr[...]

    for c in range(D_FF // ACT_COLS):
        cols = slice(c * ACT_COLS, (c + 1) * ACT_COLS)
        a_c = _dot(h, win_ref[:, c * ACT_COLS:(c + 1) * ACT_COLS])
        gate_c = _dot(h, win_ref[:, D_FF + c * ACT_COLS:D_FF + (c + 1) * ACT_COLS])
        if nb is None:
            ext = jnp.concatenate([carry_scr[:, cols], a_c], axis=0)
            prev1 = ext[pad - 1:pad - 1 + tm]
            prev2 = ext[pad - 2:pad - 2 + tm]
            carry_scr[:, cols] = a_c[tm - pad:tm]
        else:
            prev1 = jnp.concatenate([past_ref[1, :, cols], a_c[0:tm - nb]], axis=0)
            prev2 = jnp.concatenate([past_ref[0, :, cols], past_ref[1, :, cols], a_c[0:tm - 2 * nb]], axis=0)
            conv_ref[:, cols] = a_c[tm - (CONV_W - 1) * nb:tm]
        conv = (cb_ref[:, cols] + prev2 * cw_ref[0:1, cols] + prev1 * cw_ref[1:2, cols]
                + a_c * cw_ref[2:3, cols])
        act_scr[:, cols] = (_gelu_tanh(conv) * gate_c).astype(BF16)

    x2 = x1_scr[...] + _dot(act_scr[...], wo_ref[...])
    out_ref[...] = _rmsnorm_rows(x2, gfin_ref[...]) if final else x2
    if nb is None:
        conv_ref[0] = carry_scr[pad - (CONV_W - 1):pad, :]


def _ffn(x2d, o_ret, o_s5, o_gla, w_out_bf, norm_g, w_in_bf, conv_w, conv_b, w_o_bf, layer, final_g, conv_past,
         batch, seq_len):
    rows = batch * seq_len
    time_major = conv_past is not None
    tm = rows if time_major else min(ROW_TILE, seq_len)
    final = final_g is not None
    cw = jnp.concatenate([conv_w.astype(F32), jnp.zeros((V7X_SUBLANES - CONV_W, D_FF), F32)], axis=0)
    consts = [w_out_bf, norm_g.reshape(1, D_MODEL), w_in_bf, cw, conv_b.reshape(1, D_FF), w_o_bf]
    stacked = [True, False, True, False, False, True]
    if final:
        consts.append(final_g.reshape(1, D_MODEL))
        stacked.append(False)
    if time_major:
        assert seq_len >= CONV_W - 1 and batch % V7X_SUBLANES == 0
        consts.append(conv_past)
        stacked.append(False)
        grid = (1,)
        imap = lambda i: (0, 0)
        conv_rows = (CONV_W - 1) * batch
        conv_spec = pl.BlockSpec((conv_rows, D_FF), imap)
        conv_shape = jax.ShapeDtypeStruct((conv_rows, D_FF), F32)
    else:
        assert seq_len % tm == 0
        nl = seq_len // tm
        grid = (batch, nl)
        imap = lambda b, l: (b * nl + l, 0)
        conv_spec = pl.BlockSpec((1, CONV_W - 1, D_FF), lambda b, l: (b, 0, 0))
        conv_shape = jax.ShapeDtypeStruct((batch, CONV_W - 1, D_FF), F32)
    row_spec = lambda w: pl.BlockSpec((tm, w), imap)
    in_specs = ([row_spec(D_MODEL), row_spec(RET_W), row_spec(S5_W), row_spec(GLA_W)]
                + [_layer_spec(a.shape, layer, True) if s else _const_spec(a.shape) for a, s in zip(consts, stacked)])
    return pl.pallas_call(
        functools.partial(_ffn_body, tm=tm, time_major_batch=batch if time_major else None, final=final),
        grid=grid,
        in_specs=in_specs,
        out_specs=[row_spec(D_MODEL), conv_spec],
        out_shape=[jax.ShapeDtypeStruct((rows, D_MODEL), F32), conv_shape],
        scratch_shapes=[pltpu.VMEM((tm, D_MODEL), F32), pltpu.VMEM((tm, D_MODEL), BF16),
                        pltpu.VMEM((V7X_SUBLANES, D_FF), F32), pltpu.VMEM((tm, D_FF), BF16)],
        compiler_params=_params(),
        name="ffn",
    )(x2d, o_ret, o_s5, o_gla, *consts)


def _rotary_tables(first_pos, n_pos, repeat=1):
    half = RET_HD // 2
    inv = ROPE_BASE ** (-(np.arange(half, dtype=np.float64) / half))
    ang = (first_pos + np.arange(n_pos)).astype(np.float64)[:, None] * inv[None, :]
    cos, sin = np.cos(ang), np.sin(ang)
    expand = lambda a, b: np.repeat(np.tile(np.concatenate([a, b], axis=1), (1, RET_HEADS)), repeat, axis=0)
    return jnp.asarray(expand(cos, cos), F32), jnp.asarray(expand(-sin, sin), F32)


def _swap_major(a2d, outer, inner):
    return a2d.reshape(outer, inner, -1).transpose(1, 0, 2).reshape(outer * inner, -1)


def _run_prompt_group(x, layers, big, final_g):
    batch, seq_len, _ = x.shape
    rows = batch * seq_len
    tm = min(ROW_TILE, seq_len)
    cos_tab, sin_tab = _rotary_tables(0, seq_len)
    x2d = x.reshape(rows, D_MODEL)
    outs = []
    for li, lp in enumerate(layers):
        ret, su, gla, glr = _mix_in(x2d, lp["norm_mix_g"], big["w_in"], li, cos_tab, sin_tab, tm)
        o_ret, s_ret = _retention(ret, lp["ret_norm_g"], lp["ret_norm_b"], batch, seq_len)
        o_s5, s5r, s5i = _s5(_swap_major(su, batch, seq_len), lp["s5_disc"], lp["s5_d"], lp["s5_glu_w"],
                             lp["s5_glu_b"], None, batch, seq_len)
        o_s5 = _swap_major(o_s5, seq_len, batch)
        o_gla, s_gla = _gla(gla, glr, lp["gla_gate_w"], lp["gla_gate_b"], lp["gla_norm_g"], batch, seq_len)
        last = li == len(layers) - 1
        x2d, conv_new = _ffn(x2d, o_ret, o_s5, o_gla, big["w_out"], lp["norm_ffn_g"], big["ffn_w_in"],
                             lp["ffn_conv_w"], lp["ffn_conv_b"], big["ffn_w_out"], li, final_g if last else None,
                             None, batch, seq_len)
        s5_shape = (batch, S5_GROUPS, S5_STATE)
        outs.append((s_ret, s5r.reshape(s5_shape), s5i.reshape(s5_shape), s_gla, conv_new))
    return [x2d.reshape(batch, seq_len, D_MODEL)] + [jnp.stack([o[i] for o in outs]) for i in range(5)]


def _run_sample_group(x, past_len, states, layers, big, final_g):
    batch, seq_len, _ = x.shape
    assert batch == V7X_LANES
    rows = batch * seq_len
    cos_tab, sin_tab = _rotary_tables(past_len, seq_len, repeat=batch)
    x2d = x.transpose(1, 0, 2).reshape(rows, D_MODEL)
    ret_state = states["ret"].transpose(0, 2, 3, 4, 1)
    gla_state = states["gla"].transpose(0, 2, 3, 4, 1)
    conv_state = states["conv"].transpose(0, 2, 1, 3)
    depth = len(layers)
    s5_flat = lambda s: s.reshape(depth, batch, S5_LANES)
    s5r_state, s5i_state = s5_flat(states["s5r"]), s5_flat(states["s5i"])
    outs = []
    for li, lp in enumerate(layers):
        ret, su, gla, glr = _mix_in(x2d, lp["norm_mix_g"], big["w_in"], li, cos_tab, sin_tab, rows)
        o_ret, s_ret = _retention_lanes(ret, lp["ret_norm_g"], lp["ret_norm_b"], ret_state, li, seq_len)
        o_s5, s5r, s5i = _s5(su, lp["s5_disc"], lp["s5_d"], lp["s5_glu_w"], lp["s5_glu_b"],
                             (s5r_state[li], s5i_state[li]), batch, seq_len)
        o_gla, s_gla = _gla_lanes(gla, glr, lp["gla_gate_w"], lp["gla_gate_b"], lp["gla_norm_g"], gla_state, li,
                                  seq_len)
        last = li == len(layers) - 1
        x2d, conv_new = _ffn(x2d, o_ret, o_s5, o_gla, big["w_out"], lp["norm_ffn_g"], big["ffn_w_in"],
                             lp["ffn_conv_w"], lp["ffn_conv_b"], big["ffn_w_out"], li, final_g if last else None,
                             conv_state[li], batch, seq_len)
        outs.append((s_ret, s5r, s5i, s_gla, conv_new.reshape(CONV_W - 1, batch, D_FF)))
    s_ret, s5r, s5i, s_gla, conv_new = (jnp.stack([o[i] for o in outs]) for i in range(5))
    s5_shape = (depth, batch, S5_GROUPS, S5_STATE)
    return [x2d.reshape(seq_len, batch, D_MODEL).transpose(1, 0, 2), s_ret.transpose(0, 4, 1, 2, 3),
            s5r.reshape(s5_shape), s5i.reshape(s5_shape), s_gla.transpose(0, 4, 1, 2, 3),
            conv_new.transpose(0, 2, 1, 3)]


def kernel(x_prompt, x_sample, state_ret, state_s5_re, state_s5_im, state_gla, state_ffn_conv, norm_mix_g, w_in, ret_norm_g, ret_norm_b, s5_lambda_re, s5_lambda_im, s5_log_dt, s5_b_re, s5_b_im, s5_c_re, s5_c_im, s5_d, s5_glu_w, s5_glu_b, gla_gate_w, gla_gate_b, gla_norm_g, w_out, norm_ffn_g, ffn_w_in, ffn_conv_w, ffn_conv_b, ffn_w_out, norm_final_g):
    depth = w_in.shape[0]
    big = dict(w_in=w_in.astype(BF16), w_out=w_out.astype(BF16), ffn_w_in=ffn_w_in.astype(BF16),
               ffn_w_out=ffn_w_out.astype(BF16))
    layers = []
    for l in range(depth):
        layers.append(dict(
            norm_mix_g=norm_mix_g[l], ret_norm_g=ret_norm_g[l], ret_norm_b=ret_norm_b[l],
            s5_disc=_s5_discretize(s5_lambda_re[l], s5_lambda_im[l], s5_log_dt[l], s5_b_re[l], s5_b_im[l],
                                   s5_c_re[l], s5_c_im[l]),
            s5_d=s5_d[l], s5_glu_w=s5_glu_w[l].astype(BF16), s5_glu_b=s5_glu_b[l],
            gla_gate_w=gla_gate_w[l], gla_gate_b=gla_gate_b[l], gla_norm_g=gla_norm_g[l],
            norm_ffn_g=norm_ffn_g[l], ffn_conv_w=ffn_conv_w[l], ffn_conv_b=ffn_conv_b[l]))
    sample_states = dict(ret=state_ret, s5r=state_s5_re, s5i=state_s5_im, gla=state_gla, conv=state_ffn_conv)
    yp, ret_p, s5r_p, s5i_p, gla_p, conv_p = _run_prompt_group(x_prompt, layers, big, norm_final_g)
    ys, ret_s, s5r_s, s5i_s, gla_s, conv_s = _run_sample_group(x_sample, PAST_LEN, sample_states, layers, big,
                                                               norm_final_g)
    return (yp, ys, ret_p, ret_s, s5r_p, s5r_s, s5i_p, s5i_s, gla_p, gla_s, conv_p, conv_s)
```

```python
import functools

import numpy as np
import jax
import jax.numpy as jnp
from jax import lax
from jax.experimental import pallas as pl
from jax.experimental.pallas import tpu as pltpu

F32, BF16 = jnp.float32, jnp.bfloat16

D_MODEL = 1024
RET_W, S5_W, GLA_W = 384, 256, 384
RET_HEADS, RET_HD = 6, 64
RET_PAIRS = RET_HEADS // 2
S5_GROUPS, S5_CH, S5_STATE = 16, 16, 64
S5_LANES = S5_GROUPS * S5_STATE
S5_MIN_NEG = 1e-4
GLA_HEADS, GLA_DK, GLA_DV, GLA_RANK = 4, 48, 96, 16
GLA_KW = GLA_HEADS * GLA_DK
GLA_GATE_TEMP = 16.0
D_FF = 2816
CONV_W = 3
ROPE_BASE = 10000.0
CHUNK = 64
EPS = 1e-6
PAST_LEN = 16384
IN_COLS = 4 * RET_W + S5_W + 2 * GLA_KW + 2 * GLA_W + GLA_RANK
COL_RET, COL_S5, COL_GLA, COL_LR = 0, 4 * RET_W, 4 * RET_W + S5_W, IN_COLS - GLA_RANK
GLA_COLS = 2 * GLA_KW + 2 * GLA_W

V7X_SUBLANES = 8
V7X_LANES = 128
VMEM_LIMIT = 58 * 1024 * 1024

ROW_TILE = 512
MIXER_TILE = 1024
S5_TIME_TILE = 128
S5_SUB_STEPS = 64
ACT_COLS = 256


def _dot(a, b):
    return jnp.dot(a, b, preferred_element_type=F32)


def _dot_nt(a, b):
    return lax.dot_general(a, b, (((1,), (1,)), ((), ())), preferred_element_type=F32)


def _dot_tn(a, b):
    return lax.dot_general(a, b, (((0,), (0,)), ((), ())), preferred_element_type=F32)


def _hi_lo(x):
    hi = x.astype(BF16)
    return hi, (x - hi.astype(F32)).astype(BF16)


def _dot_exact_lhs(m, x):
    hi, lo = _hi_lo(x)
    return _dot(m, hi) + _dot(m, lo)


def _dot3(a, b):
    ah, al = _hi_lo(a)
    bh, bl = _hi_lo(b)
    return _dot(ah, bh) + _dot(ah, bl) + _dot(al, bh)


def _sigmoid(x):
    return 1.0 / (1.0 + jnp.exp(-x))


def _log_sigmoid(z):
    return jnp.minimum(z, 0.0) - jnp.log(1.0 + jnp.exp(-jnp.abs(z)))


def _gelu_tanh(x):
    return 0.5 * x * (1.0 + jnp.tanh(0.7978845608028654 * (x + 0.044715 * (x * x * x))))


def _rmsnorm_rows(x, g):
    return x * lax.rsqrt(jnp.mean(x * x, axis=-1, keepdims=True) + EPS) * g


def _const_spec(shape):
    nd = len(shape)
    return pl.BlockSpec(shape, lambda *_: (0,) * nd)


def _layer_spec(stacked_shape, layer, single_buffer=False):
    nd = len(stacked_shape) - 1
    mode = dict(pipeline_mode=pl.Buffered(1)) if single_buffer else {}
    return pl.BlockSpec((None,) + tuple(stacked_shape[1:]), lambda *_: (layer,) + (0,) * nd, **mode)


def _params():
    return pltpu.CompilerParams(vmem_limit_bytes=VMEM_LIMIT)


def _to_lanes(x):
    w = x.shape[1]
    pad = -w % V7X_LANES
    if pad:
        x = jnp.concatenate([x, jnp.zeros((x.shape[0], pad), F32)], axis=1)
    return x.T[0:w]


def _mix_in_body(x_ref, g_ref, w_ref, cos_ref, sin_ref, ret_ref, su_ref, gla_ref, glr_ref):
    h = _rmsnorm_rows(x_ref[...], g_ref[...]).astype(BF16)
    ret = _dot(h, w_ref[:, COL_RET:COL_S5])
    cos = cos_ref[...]
    sin = sin_ref[...]
    lane = lax.broadcasted_iota(jnp.int32, cos.shape, 1)
    first_half = (lane & (RET_HD // 2)) == 0

    def rotary(z):
        swapped = jnp.where(first_half, pltpu.roll(z, RET_W - RET_HD // 2, 1), pltpu.roll(z, RET_HD // 2, 1))
        return z * cos + swapped * sin

    ret_ref[:, 0:RET_W] = rotary(ret[:, 0:RET_W]).astype(BF16)
    ret_ref[:, RET_W:2 * RET_W] = (rotary(ret[:, RET_W:2 * RET_W]) * RET_HD ** -0.5).astype(BF16)
    ret_ref[:, 2 * RET_W:] = ret[:, 2 * RET_W:].astype(BF16)
    su_ref[...] = _dot(h, w_ref[:, COL_S5:COL_GLA]).astype(BF16)
    gla = _dot(h, w_ref[:, COL_GLA:IN_COLS])
    glane = lax.broadcasted_iota(jnp.int32, (1, GLA_COLS), 1)
    gla_ref[...] = (gla[:, 0:GLA_COLS] * jnp.where(glane < GLA_KW, GLA_DK ** -0.5, 1.0)).astype(BF16)
    glr_ref[...] = gla[:, GLA_COLS:GLA_COLS + GLA_RANK]


def _mix_in(x2d, norm_g, w_in_bf, layer, cos_tab, sin_tab, tm):
    rows = x2d.shape[0]
    n_tab = cos_tab.shape[0] // tm
    row_spec = lambda w: pl.BlockSpec((tm, w), lambda i: (i, 0))
    tab_spec = pl.BlockSpec((tm, RET_W), lambda i: (i % n_tab, 0))
    return pl.pallas_call(
        _mix_in_body,
        grid=(rows // tm,),
        in_specs=[row_spec(D_MODEL), _const_spec((1, D_MODEL)), _layer_spec(w_in_bf.shape, layer, True),
                  tab_spec, tab_spec],
        out_specs=[row_spec(4 * RET_W), row_spec(S5_W), row_spec(GLA_COLS), row_spec(GLA_RANK)],
        out_shape=[jax.ShapeDtypeStruct((rows, 4 * RET_W), BF16), jax.ShapeDtypeStruct((rows, S5_W), BF16),
                   jax.ShapeDtypeStruct((rows, GLA_COLS), BF16), jax.ShapeDtypeStruct((rows, GLA_RANK), F32)],
        compiler_params=_params(),
        name="mix_in",
    )(x2d, norm_g.reshape(1, D_MODEL), w_in_bf, cos_tab, sin_tab)


def _retention_gammas():
    return 1.0 - 2.0 ** (-5.0 - np.arange(RET_HEADS))


def _retention_consts():
    t = np.arange(CHUNK)
    gam = _retention_gammas()
    causal = t[:, None] >= t[None, :]
    diff = np.maximum(t[:, None] - t[None, :], 0)
    dmask = np.zeros((RET_PAIRS, CHUNK, 2 * CHUNK))
    cdec = np.zeros((RET_PAIRS, 2 * RET_HD, 2 * RET_HD))
    for p in range(RET_PAIRS):
        for s in range(2):
            g = gam[2 * p + s]
            dmask[p, :, s * CHUNK:(s + 1) * CHUNK] = np.where(causal, g ** diff, 0.0)
            cdec[p, s * RET_HD:(s + 1) * RET_HD, s * RET_HD:(s + 1) * RET_HD] = g ** CHUNK
    lane_gam = np.repeat(gam, RET_HD)[None, :]
    qdec = lane_gam ** (t[:, None] + 1.0)
    kdec = lane_gam ** (CHUNK - 1.0 - t[:, None])
    bd = (cdec[0] > 0).astype(np.float32)
    head_mask = np.stack([np.arange(2 * RET_HD) < RET_HD, np.arange(2 * RET_HD) >= RET_HD]).astype(np.float32)
    ones_blk = np.kron(np.eye(RET_HEADS), np.ones((RET_HD, RET_HD)))
    f = lambda a: jnp.asarray(a, F32)
    return dict(dmask=f(dmask), cdec=f(cdec), qdec=f(qdec), kdec=f(kdec), bd=f(bd),
                head_mask=jnp.asarray(head_mask, BF16), ones_blk=jnp.asarray(ones_blk, BF16))


def _retention_body(ret_ref, dmask_ref, cdec_ref, qdec_ref, kdec_ref, bd_ref, hm_ref, ones_ref, lng_ref, lnb_ref,
                    o_ref, sout_ref, s_scr, o_scr, *, tl):
    @pl.when(pl.program_id(1) == 0)
    def _():
        s_scr[...] = jnp.zeros_like(s_scr)

    bd = bd_ref[...]
    m0 = hm_ref[0:1, :]
    m1 = hm_ref[1:2, :]
    q = ret_ref[:, 0:RET_W]
    k = ret_ref[:, RET_W:2 * RET_W]
    v = ret_ref[:, 2 * RET_W:3 * RET_W]
    q_start = (q.astype(F32) * qdec_ref[...]).astype(BF16)
    k_end = (k.astype(F32) * kdec_ref[...]).astype(BF16)
    blocks = [(n, p) for n in range(tl // CHUNK) for p in range(RET_PAIRS)]

    def part(a, n, p):
        return a[n * CHUNK:(n + 1) * CHUNK, 2 * RET_HD * p:2 * RET_HD * (p + 1)]

    scores, o, kv = {}, {}, {}
    for n, p in blocks:
        kp = part(k, n, p)
        kk = jnp.concatenate([kp * m0, kp * m1], axis=0)
        scores[n, p] = (_dot_nt(part(q, n, p), kk) * dmask_ref[p]).astype(BF16)
    for n, p in blocks:
        vp = part(v, n, p)
        o[n, p] = _dot(scores[n, p], jnp.concatenate([vp * m0, vp * m1], axis=0))
    for n, p in blocks:
        kv[n, p] = _dot_tn(part(k_end, n, p), part(v, n, p)) * bd
    for n, p in blocks:
        s_prev = s_scr[p]
        o_scr[n * CHUNK:(n + 1) * CHUNK, 2 * RET_HD * p:2 * RET_HD * (p + 1)] = (
            o[n, p] + _dot(part(q_start, n, p), s_prev.astype(BF16)))
        s_scr[p] = s_prev * cdec_ref[p] + kv[n, p]

    o_all = o_scr[...]
    ones_blk = ones_ref[...]
    mu = _dot(o_all.astype(BF16), ones_blk) * (1.0 / RET_HD)
    cen = o_all - mu
    var = _dot((cen * cen).astype(BF16), ones_blk) * (1.0 / RET_HD)
    y = cen * lax.rsqrt(var + EPS) * lng_ref[...] + lnb_ref[...]
    gate = ret_ref[:, 3 * RET_W:4 * RET_W].astype(F32)
    o_ref[...] = (y * gate * _sigmoid(gate)).astype(BF16)
    for p in range(RET_PAIRS):
        s = s_scr[p]
        sout_ref[0, 2 * p] = s[0:RET_HD, 0:RET_HD]
        sout_ref[0, 2 * p + 1] = s[RET_HD:2 * RET_HD, RET_HD:2 * RET_HD]


def _retention(ret2d, ln_g, ln_b, batch, seq_len):
    assert seq_len % CHUNK == 0
    c = _retention_consts()
    tl = min(MIXER_TILE, seq_len)
    nl = seq_len // tl
    tile_rows = lambda a: jnp.tile(a, (tl // CHUNK, 1))
    consts = [c["dmask"], c["cdec"], tile_rows(c["qdec"]), tile_rows(c["kdec"]), c["bd"], c["head_mask"],
              c["ones_blk"], ln_g.reshape(1, RET_W), ln_b.reshape(1, RET_W)]
    state_shape = (batch, RET_HEADS, RET_HD, RET_HD)
    return pl.pallas_call(
        functools.partial(_retention_body, tl=tl),
        grid=(batch, nl),
        in_specs=[pl.BlockSpec((tl, 4 * RET_W), lambda b, l: (b * nl + l, 0))] + [_const_spec(a.shape) for a in consts],
        out_specs=[pl.BlockSpec((tl, RET_W), lambda b, l: (b * nl + l, 0)),
                   pl.BlockSpec((1,) + state_shape[1:], lambda b, l: (b, 0, 0, 0))],
        out_shape=[jax.ShapeDtypeStruct((batch * seq_len, RET_W), BF16), jax.ShapeDtypeStruct(state_shape, F32)],
        scratch_shapes=[pltpu.VMEM((RET_PAIRS, 2 * RET_HD, 2 * RET_HD), F32), pltpu.VMEM((tl, RET_W), F32)],
        compiler_params=_params(),
        name="retention",
    )(ret2d, *consts)


def _retention_lanes_body(ret_ref, tab_ref, lng_ref, lnb_ref, s_ref, o_ref, sout_ref, q_scr, k_scr, v_scr, g_scr,
                          o_scr, *, seq_len):
    h = pl.program_id(0)
    nb = V7X_LANES
    steps = range(seq_len)

    @pl.when(h == 0)
    def _():
        for t in steps:
            blk = ret_ref[t * nb:(t + 1) * nb, :].astype(F32)
            for scr, off in ((q_scr, 0), (k_scr, RET_W), (v_scr, 2 * RET_W), (g_scr, 3 * RET_W)):
                tr = _to_lanes(blk[:, off:off + RET_W])
                for hh in range(RET_HEADS):
                    scr[t, hh] = tr[hh * RET_HD:(hh + 1) * RET_HD]

    tab = tab_ref[h]
    row = lambda r: tab[r:r + 1, :]
    q = [q_scr[t, h] for t in steps]
    k = [k_scr[t, h] for t in steps]
    v = [v_scr[t, h] for t in steps]
    o = []
    for i in steps:
        acc = None
        for j in range(i + 1):
            s = jnp.sum(q[i] * k[j], axis=0, keepdims=True)
            if i > j:
                s = s * row(i - j - 1)
            acc = s * v[j] if acc is None else acc + s * v[j]
        o.append(acc)
    q_start = [q[t] * row(t) for t in steps]
    k_end = [k[t] * row(seq_len + t) for t in steps]
    chunk_decay = row(2 * seq_len)
    for d in range(RET_HD):
        s_d = s_ref[d]
        new = s_d * chunk_decay
        for t in steps:
            o[t] = o[t] + q_start[t][d:d + 1, :] * s_d
            new = new + k_end[t][d:d + 1, :] * v[t]
        sout_ref[d] = new
    for t in steps:
        mu = jnp.mean(o[t], axis=0, keepdims=True)
        cen = o[t] - mu
        var = jnp.mean(cen * cen, axis=0, keepdims=True)
        gate = g_scr[t, h]
        o_scr[t, h] = (cen * lax.rsqrt(var + EPS) * lng_ref[h] + lnb_ref[h]) * gate * _sigmoid(gate)

    @pl.when(h == RET_HEADS - 1)
    def _():
        for t in steps:
            o_ref[t * nb:(t + 1) * nb, :] = o_scr[t].reshape(RET_W, nb).T.astype(BF16)


def _retention_lanes(ret2d, ln_g, ln_b, state, layer, seq_len):
    nb = V7X_LANES
    gam = _retention_gammas()[:, None]
    t = np.arange(seq_len)[None, :]
    rows = np.concatenate([gam ** (t + 1.0), gam ** (seq_len - 1.0 - t), gam ** (seq_len + 0.0 * t[:, :1])], axis=1)
    n_rows = -(-rows.shape[1] // V7X_SUBLANES) * V7X_SUBLANES
    tab = np.zeros((RET_HEADS, n_rows, nb))
    tab[:, :rows.shape[1], :] = rows[:, :, None]
    lanes = lambda p: jnp.broadcast_to(p.reshape(RET_HEADS, RET_HD, 1), (RET_HEADS, RET_HD, nb))
    consts = [jnp.asarray(tab, F32), lanes(ln_g), lanes(ln_b)]
    tile = pltpu.VMEM((seq_len, RET_HEADS, RET_HD, nb), F32)
    return pl.pallas_call(
        functools.partial(_retention_lanes_body, seq_len=seq_len),
        grid=(RET_HEADS,),
        in_specs=[_const_spec(ret2d.shape)] + [_const_spec(a.shape) for a in consts]
                 + [pl.BlockSpec((None, None, RET_HD, RET_HD, nb), lambda h: (layer, h, 0, 0, 0))],
        out_specs=[_const_spec((seq_len * nb, RET_W)), pl.BlockSpec((None, RET_HD, RET_HD, nb), lambda h: (h, 0, 0, 0))],
        out_shape=[jax.ShapeDtypeStruct((seq_len * nb, RET_W), BF16),
                   jax.ShapeDtypeStruct((RET_HEADS, RET_HD, RET_HD, nb), F32)],
        scratch_shapes=[tile] * 5,
        compiler_params=_params(),
        name="retention_lanes",
    )(ret2d, *consts, state)


def _gla_consts():
    t = np.arange(CHUNK)
    causal = t[:, None] >= t[None, :]
    kmask = np.kron(np.eye(GLA_HEADS), np.ones((1, GLA_DK)))
    vmask = np.kron(np.eye(GLA_HEADS), np.ones((1, GLA_DV)))
    bd = np.kron(np.eye(GLA_HEADS), np.ones((GLA_DK, GLA_DV)))
    ones_blk = np.kron(np.eye(GLA_HEADS), np.ones((GLA_DV, GLA_DV)))
    b = lambda a: jnp.asarray(a, BF16)
    return dict(tril=b(causal), causal4=jnp.asarray(np.tile(causal, (1, GLA_HEADS)), F32), kmask=b(kmask),
                vmask=b(vmask), bd=jnp.asarray(bd, F32), ones_blk=b(ones_blk))


def _gla_body(gla_ref, glr_ref, gw_ref, gb_ref, tril_ref, causal_ref, kmask_ref, vmask_ref, bd_ref, ones_blk_ref,
              block_ref, ng_ref, o_ref, sout_ref, s_scr, o_scr, *, tl):
    @pl.when(pl.program_id(1) == 0)
    def _():
        s_scr[...] = jnp.zeros_like(s_scr)

    bd = bd_ref[...]
    lg = _log_sigmoid(_dot3(glr_ref[...], gw_ref[...]) + gb_ref[...]) * (1.0 / GLA_GATE_TEMP)
    q0, k0, v0, g0 = 0, GLA_KW, 2 * GLA_KW, 2 * GLA_KW + GLA_W
    chunks = range(tl // CHUNK)

    def rows(a, n):
        return a[n * CHUNK:(n + 1) * CHUNK]

    cums = [_dot_exact_lhs(tril_ref[...], rows(lg, n)) for n in chunks]
    b_cum = jnp.concatenate(cums, axis=0)
    b_end = jnp.concatenate([jnp.broadcast_to(c[CHUNK - 1:CHUNK, :], (CHUNK, GLA_KW)) for c in cums], axis=0)
    q_in = (gla_ref[:, q0:k0].astype(F32) * jnp.exp(b_cum)).astype(BF16)
    kf = gla_ref[:, k0:v0].astype(F32)
    k_in = (kf * jnp.exp(-b_cum)).astype(BF16)
    k_dec = (kf * jnp.exp(b_end - b_cum)).astype(BF16)
    v = gla_ref[:, v0:g0]
    lg_hi, lg_lo = _hi_lo(lg)
    block_decay = jnp.exp(_dot_tn(lg_hi, block_ref[...]) + _dot_tn(lg_lo, block_ref[...]))

    scores, o, kv = {}, {}, {}
    for n in chunks:
        kn = rows(k_in, n)
        kk = jnp.concatenate([kn * kmask_ref[h:h + 1, :] for h in range(GLA_HEADS)], axis=0)
        scores[n] = (_dot_nt(rows(q_in, n), kk) * causal_ref[...]).astype(BF16)
    for n in chunks:
        vn = rows(v, n)
        vv = jnp.concatenate([vn * vmask_ref[h:h + 1, :] for h in range(GLA_HEADS)], axis=0)
        o[n] = _dot(scores[n], vv)
    for n in chunks:
        kv[n] = _dot_tn(rows(k_dec, n), rows(v, n)) * bd
    for n in chunks:
        s_prev = s_scr[...]
        o_scr[n * CHUNK:(n + 1) * CHUNK, :] = o[n] + _dot(rows(q_in, n), s_prev.astype(BF16))
        s_scr[...] = s_prev * jnp.broadcast_to(block_decay[:, n:n + 1], (GLA_KW, GLA_W)) + kv[n]

    o_all = o_scr[...]
    ms = _dot((o_all * o_all).astype(BF16), ones_blk_ref[...]) * (1.0 / GLA_DV)
    gate = gla_ref[:, g0:g0 + GLA_W].astype(F32)
    o_ref[...] = (o_all * lax.rsqrt(ms + EPS) * ng_ref[...] * gate * _sigmoid(gate)).astype(BF16)
    s = s_scr[...]
    for h in range(GLA_HEADS):
        sout_ref[0, h] = s[h * GLA_DK:(h + 1) * GLA_DK, h * GLA_DV:(h + 1) * GLA_DV]


def _gla(gla2d, glr2d, gate_w, gate_b, norm_g, batch, seq_len):
    assert seq_len % CHUNK == 0
    c = _gla_consts()
    tl = min(MIXER_TILE, seq_len)
    nl = seq_len // tl
    assert tl // CHUNK <= V7X_LANES
    block = (np.arange(tl) // CHUNK)[:, None] == np.arange(V7X_LANES)[None, :]
    consts = [gate_w, gate_b.reshape(1, GLA_KW), c["tril"], c["causal4"], c["kmask"], c["vmask"], c["bd"],
              c["ones_blk"], jnp.asarray(block, BF16), jnp.tile(norm_g, GLA_HEADS).reshape(1, GLA_W)]
    row_spec = lambda w: pl.BlockSpec((tl, w), lambda b, l: (b * nl + l, 0))
    state_shape = (batch, GLA_HEADS, GLA_DK, GLA_DV)
    return pl.pallas_call(
        functools.partial(_gla_body, tl=tl),
        grid=(batch, nl),
        in_specs=[row_spec(GLA_COLS), row_spec(GLA_RANK)] + [_const_spec(a.shape) for a in consts],
        out_specs=[row_spec(GLA_W), pl.BlockSpec((1,) + state_shape[1:], lambda b, l: (b, 0, 0, 0))],
        out_shape=[jax.ShapeDtypeStruct((batch * seq_len, GLA_W), BF16), jax.ShapeDtypeStruct(state_shape, F32)],
        scratch_shapes=[pltpu.VMEM((GLA_KW, GLA_W), F32), pltpu.VMEM((tl, GLA_W), F32)],
        compiler_params=_params(),
        name="gla",
    )(gla2d, glr2d, *consts)


def _gla_lanes_body(gla_ref, glr_ref, gw_ref, gb_ref, ng_ref, s_ref, o_ref, sout_ref, q_scr, k_scr, kd_scr, v_scr,
                    g_scr, dec_scr, o_scr, *, seq_len):
    h = pl.program_id(0)
    nb = V7X_LANES
    steps = range(seq_len)
    q0, k0, v0, g0 = 0, GLA_KW, 2 * GLA_KW, 2 * GLA_KW + GLA_W

    def split(scr, t, tr, width):
        for hh in range(GLA_HEADS):
            scr[t, hh] = tr[hh * width:(hh + 1) * width]

    @pl.when(h == 0)
    def _():
        lg = _log_sigmoid(_dot3(glr_ref[...], gw_ref[...]) + gb_ref[...]) * (1.0 / GLA_GATE_TEMP)
        cums = []
        for t in steps:
            lg_t = lg[t * nb:(t + 1) * nb]
            cums.append(lg_t if t == 0 else cums[-1] + lg_t)
        b_end = cums[-1]
        dec = _to_lanes(jnp.exp(b_end))
        for hh in range(GLA_HEADS):
            dec_scr[hh] = dec[hh * GLA_DK:(hh + 1) * GLA_DK]
        for t in steps:
            blk = gla_ref[t * nb:(t + 1) * nb, :].astype(F32)
            kf = blk[:, k0:v0]
            split(q_scr, t, _to_lanes(blk[:, q0:k0] * jnp.exp(cums[t])), GLA_DK)
            split(k_scr, t, _to_lanes(kf * jnp.exp(-cums[t])), GLA_DK)
            split(kd_scr, t, _to_lanes(kf * jnp.exp(b_end - cums[t])), GLA_DK)
            split(v_scr, t, _to_lanes(blk[:, v0:g0]), GLA_DV)
            split(g_scr, t, _to_lanes(blk[:, g0:g0 + GLA_W]), GLA_DV)

    q = [q_scr[t, h] for t in steps]
    k = [k_scr[t, h] for t in steps]
    kd = [kd_scr[t, h] for t in steps]
    v = [v_scr[t, h] for t in steps]
    dec = dec_scr[h]
    o = []
    for i in steps:
        acc = None
        for j in range(i + 1):
            s = jnp.sum(q[i] * k[j], axis=0, keepdims=True)
            acc = s * v[j] if acc is None else acc + s * v[j]
        o.append(acc)
    for i in range(GLA_DK):
        s_i = s_ref[i]
        new = s_i * dec[i:i + 1, :]
        for t in steps:
            o[t] = o[t] + q[t][i:i + 1, :] * s_i
            new = new + kd[t][i:i + 1, :] * v[t]
        sout_ref[i] = new
    for t in steps:
        ms = jnp.mean(o[t] * o[t], axis=0, keepdims=True)
        gate = g_scr[t, h]
        o_scr[t, h] = o[t] * lax.rsqrt(ms + EPS) * ng_ref[...] * gate * _sigmoid(gate)

    @pl.when(h == GLA_HEADS - 1)
    def _():
        for t in steps:
            o_ref[t * nb:(t + 1) * nb, :] = o_scr[t].reshape(GLA_W, nb).T.astype(BF16)


def _gla_lanes(gla2d, glr2d, gate_w, gate_b, norm_g, state, layer, seq_len):
    nb = V7X_LANES
    consts = [gate_w, gate_b.reshape(1, GLA_KW), jnp.broadcast_to(norm_g.reshape(GLA_DV, 1), (GLA_DV, nb))]
    k_tile = pltpu.VMEM((seq_len, GLA_HEADS, GLA_DK, nb), F32)
    v_tile = pltpu.VMEM((seq_len, GLA_HEADS, GLA_DV, nb), F32)
    return pl.pallas_call(
        functools.partial(_gla_lanes_body, seq_len=seq_len),
        grid=(GLA_HEADS,),
        in_specs=[_const_spec(gla2d.shape), _const_spec(glr2d.shape)] + [_const_spec(a.shape) for a in consts]
                 + [pl.BlockSpec((None, None, GLA_DK, GLA_DV, nb), lambda h: (layer, h, 0, 0, 0))],
        out_specs=[_const_spec((seq_len * nb, GLA_W)), pl.BlockSpec((None, GLA_DK, GLA_DV, nb), lambda h: (h, 0, 0, 0))],
        out_shape=[jax.ShapeDtypeStruct((seq_len * nb, GLA_W), BF16),
                   jax.ShapeDtypeStruct((GLA_HEADS, GLA_DK, GLA_DV, nb), F32)],
        scratch_shapes=[k_tile, k_tile, k_tile, v_tile, v_tile, pltpu.VMEM((GLA_HEADS, GLA_DK, nb), F32), v_tile],
        compiler_params=_params(),
        name="gla_lanes",
    )(gla2d, glr2d, *consts, state)


def _s5_discretize(lam_re, lam_im, log_dt, b_re, b_im, c_re, c_im):
    lr = jnp.minimum(lam_re.astype(F32), -S5_MIN_NEG)
    li = lam_im.astype(F32)
    dt = jnp.exp(log_dt.astype(F32))[:, None]
    mag = jnp.exp(lr * dt)
    ar = mag * jnp.cos(li * dt)
    ai = mag * jnp.sin(li * dt)
    den = lr * lr + li * li
    cr = ((ar - 1.0) * lr + ai * li) / den
    ci = (ai * lr - (ar - 1.0) * li) / den
    b_re, b_im = b_re.astype(F32), b_im.astype(F32)
    bbar_re = cr[..., None] * b_re - ci[..., None] * b_im
    bbar_im = cr[..., None] * b_im + ci[..., None] * b_re
    eye = jnp.eye(S5_GROUPS, dtype=F32)
    in_blk = lambda b: jnp.einsum("gpi,gh->gihp", b, eye).reshape(S5_W, S5_LANES)
    out_blk = lambda c: jnp.einsum("gop,gh->gpho", c.astype(F32), eye).reshape(S5_LANES, S5_W)
    bdb = jnp.concatenate([in_blk(bbar_re), in_blk(bbar_im)], axis=1).astype(BF16)
    bdc = jnp.concatenate([out_blk(c_re), -out_blk(c_im)], axis=0).astype(BF16)
    return ar.reshape(1, S5_LANES), ai.reshape(1, S5_LANES), bdb, bdc


def _s5_body(*refs, tl, batch, has_state):
    u_ref, ar_ref, ai_ref, bdb_ref, bdc_ref, d_ref, gw_ref, gb_ref = refs[:8]
    n_in = 10 if has_state else 8
    o_ref, hr_ref, hi_ref, x_scr, h_scr = refs[n_in:]

    @pl.when(pl.program_id(0) == 0)
    def _():
        if has_state:
            h_scr[0] = refs[8][...]
            h_scr[1] = refs[9][...]
        else:
            h_scr[...] = jnp.zeros_like(h_scr)

    ar = jnp.broadcast_to(ar_ref[...], (batch, S5_LANES))
    ai = jnp.broadcast_to(ai_ref[...], (batch, S5_LANES))
    hr, hi = h_scr[0], h_scr[1]
    sub = min(tl, S5_SUB_STEPS)
    n_rows = sub * batch
    halves = [slice(0, n_rows // 2), slice(n_rows // 2, n_rows)]

    def dot_halves(a, w):
        return jnp.concatenate([_dot(a[h], w) for h in halves], axis=0)

    for s in range(tl // sub):
        rows = slice(s * n_rows, (s + 1) * n_rows)
        u = u_ref[rows, :]
        x_scr[rows, :] = dot_halves(u, bdb_ref[...])
        for t in range(s * sub, (s + 1) * sub):
            r = slice(t * batch, (t + 1) * batch)
            hr, hi = (ar * hr - ai * hi + x_scr[r, 0:S5_LANES],
                      ar * hi + ai * hr + x_scr[r, S5_LANES:2 * S5_LANES])
            x_scr[r, 0:S5_LANES] = hr
            x_scr[r, S5_LANES:2 * S5_LANES] = hi
        y = dot_halves(x_scr[rows, :].astype(BF16), bdc_ref[...]) + d_ref[...] * u.astype(F32)
        y = _gelu_tanh(y)
        o_ref[rows, :] = (y * _sigmoid(dot_halves(y.astype(BF16), gw_ref[...]) + gb_ref[...])).astype(BF16)
    h_scr[0] = hr
    h_scr[1] = hi
    hr_ref[...] = hr
    hi_ref[...] = hi


def _s5(u_tm, disc, d, glu_w_bf, glu_b, state, batch, seq_len):
    ar, ai, bdb, bdc = disc
    tl = min(S5_TIME_TILE, seq_len)
    has_state = state is not None
    consts = [ar, ai, bdb, bdc, d.reshape(1, S5_W), glu_w_bf, glu_b.reshape(1, S5_W)]
    h_spec = pl.BlockSpec((batch, S5_LANES), lambda i: (0, 0))
    in_specs = [pl.BlockSpec((tl * batch, S5_W), lambda i: (i, 0))] + [_const_spec(a.shape) for a in consts]
    args = [u_tm] + consts
    if has_state:
        in_specs += [h_spec, h_spec]
        args += list(state)
    return pl.pallas_call(
        functools.partial(_s5_body, tl=tl, batch=batch, has_state=has_state),
        grid=(seq_len // tl,),
        in_specs=in_specs,
        out_specs=[pl.BlockSpec((tl * batch, S5_W), lambda i: (i, 0)), h_spec, h_spec],
        out_shape=[jax.ShapeDtypeStruct((seq_len * batch, S5_W), BF16),
                   jax.ShapeDtypeStruct((batch, S5_LANES), F32), jax.ShapeDtypeStruct((batch, S5_LANES), F32)],
        scratch_shapes=[pltpu.VMEM((tl * batch, 2 * S5_LANES), F32), pltpu.VMEM((2, batch, S5_LANES), F32)],
        compiler_params=_params(),
        name="s5",
    )(*args)


def _ffn_body(*refs, tm, time_major_batch, final):
    (x_ref, oret_ref, os5_ref, ogla_ref, wout_ref, gffn_ref, win_ref, cw_ref, cb_ref, wo_ref) = refs[:10]
    n = 10
    gfin_ref = None
    if final:
        gfin_ref, n = refs[n], n + 1
    if time_major_batch is not None:
        past_ref, n = refs[n], n + 1
    out_ref, conv_ref, x1_scr, h_scr, carry_scr, act_scr = refs[n:]
    pad = V7X_SUBLANES
    nb = time_major_batch

    if nb is None:
        @pl.when(pl.program_id(1) == 0)
        def _():
            carry_scr[...] = jnp.zeros_like(carry_scr)

    mix = jnp.concatenate([oret_ref[...], os5_ref[...], ogla_ref[...]], axis=1)
    x1_scr[...] = x_ref[...] + _dot(mix, wout_ref[...])
    h_scr[...] = _rmsnorm_rows(x1_scr[...], gffn_ref[...]).astype(BF16)
    h = h_scr[...]

    for c in range(D_FF // ACT_COLS):
        cols = slice(c * ACT_COLS, (c + 1) * ACT_COLS)
        a_c = _dot(h, win_ref[:, c * ACT_COLS:(c + 1) * ACT_COLS])
        gate_c = _dot(h, win_ref[:, D_FF + c * ACT_COLS:D_FF + (c + 1) * ACT_COLS])
        if nb is None:
            ext = jnp.concatenate([carry_scr[:, cols], a_c], axis=0)
            prev1 = ext[pad - 1:pad - 1 + tm]
            prev2 = ext[pad - 2:pad - 2 + tm]
            carry_scr[:, cols] = a_c[tm - pad:tm]
        else:
            prev1 = jnp.concatenate([past_ref[1, :, cols], a_c[0:tm - nb]], axis=0)
            prev2 = jnp.concatenate([past_ref[0, :, cols], past_ref[1, :, cols], a_c[0:tm - 2 * nb]], axis=0)
            conv_ref[:, cols] = a_c[tm - (CONV_W - 1) * nb:tm]
        conv = (cb_ref[:, cols] + prev2 * cw_ref[0:1, cols] + prev1 * cw_ref[1:2, cols]
                + a_c * cw_ref[2:3, cols])
        act_scr[:, cols] = (_gelu_tanh(conv) * gate_c).astype(BF16)

    x2 = x1_scr[...] + _dot(act_scr[...], wo_ref[...])
    out_ref[...] = _rmsnorm_rows(x2, gfin_ref[...]) if final else x2
    if nb is None:
        conv_ref[0] = carry_scr[pad - (CONV_W - 1):pad, :]


def _ffn(x2d, o_ret, o_s5, o_gla, w_out_bf, norm_g, w_in_bf, conv_w, conv_b, w_o_bf, layer, final_g, conv_past,
         batch, seq_len):
    rows = batch * seq_len
    time_major = conv_past is not None
    tm = rows if time_major else min(ROW_TILE, seq_len)
    final = final_g is not None
    cw = jnp.concatenate([conv_w.astype(F32), jnp.zeros((V7X_SUBLANES - CONV_W, D_FF), F32)], axis=0)
    consts = [w_out_bf, norm_g.reshape(1, D_MODEL), w_in_bf, cw, conv_b.reshape(1, D_FF), w_o_bf]
    stacked = [True, False, True, False, False, True]
    if final:
        consts.append(final_g.reshape(1, D_MODEL))
        stacked.append(False)
    if time_major:
        assert seq_len >= CONV_W - 1 and batch % V7X_SUBLANES == 0
        consts.append(conv_past)
        stacked.append(False)
        grid = (1,)
        imap = lambda i: (0, 0)
        conv_rows = (CONV_W - 1) * batch
        conv_spec = pl.BlockSpec((conv_rows, D_FF), imap)
        conv_shape = jax.ShapeDtypeStruct((conv_rows, D_FF), F32)
    else:
        assert seq_len % tm == 0
        nl = seq_len // tm
        grid = (batch, nl)
        imap = lambda b, l: (b * nl + l, 0)
        conv_spec = pl.BlockSpec((1, CONV_W - 1, D_FF), lambda b, l: (b, 0, 0))
        conv_shape = jax.ShapeDtypeStruct((batch, CONV_W - 1, D_FF), F32)
    row_spec = lambda w: pl.BlockSpec((tm, w), imap)
    in_specs = ([row_spec(D_MODEL), row_spec(RET_W), row_spec(S5_W), row_spec(GLA_W)]
                + [_layer_spec(a.shape, layer, True) if s else _const_spec(a.shape) for a, s in zip(consts, stacked)])
    return pl.pallas_call(
        functools.partial(_ffn_body, tm=tm, time_major_batch=batch if time_major else None, final=final),
        grid=grid,
        in_specs=in_specs,
        out_specs=[row_spec(D_MODEL), conv_spec],
        out_shape=[jax.ShapeDtypeStruct((rows, D_MODEL), F32), conv_shape],
        scratch_shapes=[pltpu.VMEM((tm, D_MODEL), F32), pltpu.VMEM((tm, D_MODEL), BF16),
                        pltpu.VMEM((V7X_SUBLANES, D_FF), F32), pltpu.VMEM((tm, D_FF), BF16)],
        compiler_params=_params(),
        name="ffn",
    )(x2d, o_ret, o_s5, o_gla, *consts)


def _rotary_tables(first_pos, n_pos, repeat=1):
    half = RET_HD // 2
    inv = ROPE_BASE ** (-(np.arange(half, dtype=np.float64) / half))
    ang = (first_pos + np.arange(n_pos)).astype(np.float64)[:, None] * inv[None, :]
    cos, sin = np.cos(ang), np.sin(ang)
    expand = lambda a, b: np.repeat(np.tile(np.concatenate([a, b], axis=1), (1, RET_HEADS)), repeat, axis=0)
    return jnp.asarray(expand(cos, cos), F32), jnp.asarray(expand(-sin, sin), F32)


def _swap_major(a2d, outer, inner):
    return a2d.reshape(outer, inner, -1).transpose(1, 0, 2).reshape(outer * inner, -1)


def _run_prompt_group(x, layers, big, final_g):
    batch, seq_len, _ = x.shape
    rows = batch * seq_len
    tm = min(ROW_TILE, seq_len)
    cos_tab, sin_tab = _rotary_tables(0, seq_len)
    x2d = x.reshape(rows, D_MODEL)
    outs = []
    for li, lp in enumerate(layers):
        ret, su, gla, glr = _mix_in(x2d, lp["norm_mix_g"], big["w_in"], li, cos_tab, sin_tab, tm)
        o_ret, s_ret = _retention(ret, lp["ret_norm_g"], lp["ret_norm_b"], batch, seq_len)
        o_s5, s5r, s5i = _s5(_swap_major(su, batch, seq_len), lp["s5_disc"], lp["s5_d"], lp["s5_glu_w"],
                             lp["s5_glu_b"], None, batch, seq_len)
        o_s5 = _swap_major(o_s5, seq_len, batch)
        o_gla, s_gla = _gla(gla, glr, lp["gla_gate_w"], lp["gla_gate_b"], lp["gla_norm_g"], batch, seq_len)
        last = li == len(layers) - 1
        x2d, conv_new = _ffn(x2d, o_ret, o_s5, o_gla, big["w_out"], lp["norm_ffn_g"], big["ffn_w_in"],
                             lp["ffn_conv_w"], lp["ffn_conv_b"], big["ffn_w_out"], li, final_g if last else None,
                             None, batch, seq_len)
        s5_shape = (batch, S5_GROUPS, S5_STATE)
        outs.append((s_ret, s5r.reshape(s5_shape), s5i.reshape(s5_shape), s_gla, conv_new))
    return [x2d.reshape(batch, seq_len, D_MODEL)] + [jnp.stack([o[i] for o in outs]) for i in range(5)]


def _run_sample_group(x, past_len, states, layers, big, final_g):
    batch, seq_len, _ = x.shape
    assert batch == V7X_LANES
    rows = batch * seq_len
    cos_tab, sin_tab = _rotary_tables(past_len, seq_len, repeat=batch)
    x2d = x.transpose(1, 0, 2).reshape(rows, D_MODEL)
    ret_state = states["ret"].transpose(0, 2, 3, 4, 1)
    gla_state = states["gla"].transpose(0, 2, 3, 4, 1)
    conv_state = states["conv"].transpose(0, 2, 1, 3)
    depth = len(layers)
    s5_flat = lambda s: s.reshape(depth, batch, S5_LANES)
    s5r_state, s5i_state = s5_flat(states["s5r"]), s5_flat(states["s5i"])
    outs = []
    for li, lp in enumerate(layers):
        ret, su, gla, glr = _mix_in(x2d, lp["norm_mix_g"], big["w_in"], li, cos_tab, sin_tab, rows)
        o_ret, s_ret = _retention_lanes(ret, lp["ret_norm_g"], lp["ret_norm_b"], ret_state, li, seq_len)
        o_s5, s5r, s5i = _s5(su, lp["s5_disc"], lp["s5_d"], lp["s5_glu_w"], lp["s5_glu_b"],
                             (s5r_state[li], s5i_state[li]), batch, seq_len)
        o_gla, s_gla = _gla_lanes(gla, glr, lp["gla_gate_w"], lp["gla_gate_b"], lp["gla_norm_g"], gla_state, li,
                                  seq_len)
        last = li == len(layers) - 1
        x2d, conv_new = _ffn(x2d, o_ret, o_s5, o_gla, big["w_out"], lp["norm_ffn_g"], big["ffn_w_in"],
                             lp["ffn_conv_w"], lp["ffn_conv_b"], big["ffn_w_out"], li, final_g if last else None,
                             conv_state[li], batch, seq_len)
        outs.append((s_ret, s5r, s5i, s_gla, conv_new.reshape(CONV_W - 1, batch, D_FF)))
    s_ret, s5r, s5i, s_gla, conv_new = (jnp.stack([o[i] for o in outs]) for i in range(5))
    s5_shape = (depth, batch, S5_GROUPS, S5_STATE)
    return [x2d.reshape(seq_len, batch, D_MODEL).transpose(1, 0, 2), s_ret.transpose(0, 4, 1, 2, 3),
            s5r.reshape(s5_shape), s5i.reshape(s5_shape), s_gla.transpose(0, 4, 1, 2, 3),
            conv_new.transpose(0, 2, 1, 3)]


def kernel(x_prompt, x_sample, state_ret, state_s5_re, state_s5_im, state_gla, state_ffn_conv, norm_mix_g, w_in, ret_norm_g, ret_norm_b, s5_lambda_re, s5_lambda_im, s5_log_dt, s5_b_re, s5_b_im, s5_c_re, s5_c_im, s5_d, s5_glu_w, s5_glu_b, gla_gate_w, gla_gate_b, gla_norm_g, w_out, norm_ffn_g, ffn_w_in, ffn_conv_w, ffn_conv_b, ffn_w_out, norm_final_g):
    depth = w_in.shape[0]
    big = dict(w_in=w_in.astype(BF16), w_out=w_out.astype(BF16), ffn_w_in=ffn_w_in.astype(BF16),
               ffn_w_out=ffn_w_out.astype(BF16))
    layers = []
    for l in range(depth):
        layers.append(dict(
            norm_mix_g=norm_mix_g[l], ret_norm_g=ret_norm_g[l], ret_norm_b=ret_norm_b[l],
            s5_disc=_s5_discretize(s5_lambda_re[l], s5_lambda_im[l], s5_log_dt[l], s5_b_re[l], s5_b_im[l],
                                   s5_c_re[l], s5_c_im[l]),
            s5_d=s5_d[l], s5_glu_w=s5_glu_w[l].astype(BF16), s5_glu_b=s5_glu_b[l],
            gla_gate_w=gla_gate_w[l], gla_gate_b=gla_gate_b[l], gla_norm_g=gla_norm_g[l],
            norm_ffn_g=norm_ffn_g[l], ffn_conv_w=ffn_conv_w[l], ffn_conv_b=ffn_conv_b[l]))
    sample_states = dict(ret=state_ret, s5r=state_s5_re, s5i=state_s5_im, gla=state_gla, conv=state_ffn_conv)
    yp, ret_p, s5r_p, s5i_p, gla_p, conv_p = _run_prompt_group(x_prompt, layers, big, norm_final_g)
    ys, ret_s, s5r_s, s5i_s, gla_s, conv_s = _run_sample_group(x_sample, PAST_LEN, sample_states, layers, big,
                                                               norm_final_g)
    return (yp, ys, ret_p, ret_s, s5r_p, s5r_s, s5i_p, s5i_s, gla_p, gla_s, conv_p, conv_s)
```

```python
import functools

import numpy as np
import jax
import jax.numpy as jnp
from jax import lax
from jax.experimental import pallas as pl
from jax.experimental.pallas import tpu as pltpu

F32, BF16 = jnp.float32, jnp.bfloat16

D_MODEL = 1024
RET_W, S5_W, GLA_W = 384, 256, 384
RET_HEADS, RET_HD = 6, 64
RET_PAIRS = RET_HEADS // 2
S5_GROUPS, S5_CH, S5_STATE = 16, 16, 64
S5_LANES = S5_GROUPS * S5_STATE
S5_MIN_NEG = 1e-4
GLA_HEADS, GLA_DK, GLA_DV, GLA_RANK = 4, 48, 96, 16
GLA_KW = GLA_HEADS * GLA_DK
GLA_GATE_TEMP = 16.0
D_FF = 2816
CONV_W = 3
ROPE_BASE = 10000.0
CHUNK = 64
EPS = 1e-6
PAST_LEN = 16384
IN_COLS = 4 * RET_W + S5_W + 2 * GLA_KW + 2 * GLA_W + GLA_RANK
COL_RET, COL_S5, COL_GLA, COL_LR = 0, 4 * RET_W, 4 * RET_W + S5_W, IN_COLS - GLA_RANK
GLA_KP = 256
GLA_Q0, GLA_LR0, GLA_K0, GLA_V0, GLA_G0 = 0, GLA_KW, GLA_KP, 2 * GLA_KP, 2 * GLA_KP + GLA_W
GLA_COLS = 2 * GLA_KP + 2 * GLA_W
MIX_COLS = COL_GLA + GLA_COLS

V7X_SUBLANES = 8
V7X_LANES = 128
VMEM_LIMIT = 58 * 1024 * 1024

ROW_TILE = 512
MIXER_TILE = 1024
S5_TIME_TILE = 128
S5_SUB_STEPS = 64
ACT_COLS = 256


def _dot(a, b):
    return jnp.dot(a, b, preferred_element_type=F32)


def _dot_nt(a, b):
    return lax.dot_general(a, b, (((1,), (1,)), ((), ())), preferred_element_type=F32)


def _dot_tn(a, b):
    return lax.dot_general(a, b, (((0,), (0,)), ((), ())), preferred_element_type=F32)


def _hi_lo(x):
    hi = x.astype(BF16)
    return hi, (x - hi.astype(F32)).astype(BF16)


def _dot_exact_lhs(m, x):
    hi, lo = _hi_lo(x)
    return _dot(m, hi) + _dot(m, lo)


def _dot_row_halves(a, w):
    half = a.shape[0] // 2
    return jnp.concatenate([_dot(a[:half], w), _dot(a[half:], w)], axis=0)


def _pad_gate(a, dtype):
    return jnp.pad(a.astype(dtype), ((0, 0), (0, GLA_KP - GLA_KW)))


def _sigmoid(x):
    return 1.0 / (1.0 + jnp.exp(-x))


def _log_sigmoid(z):
    return jnp.minimum(z, 0.0) - jnp.log(1.0 + jnp.exp(-jnp.abs(z)))


def _gelu_tanh(x):
    return 0.5 * x * (1.0 + jnp.tanh(0.7978845608028654 * (x + 0.044715 * (x * x * x))))


def _rmsnorm_rows(x, g):
    return x * lax.rsqrt(jnp.mean(x * x, axis=-1, keepdims=True) + EPS) * g


def _const_spec(shape):
    nd = len(shape)
    return pl.BlockSpec(shape, lambda *_: (0,) * nd)


def _layer_spec(stacked_shape, layer, single_buffer=False):
    nd = len(stacked_shape) - 1
    mode = dict(pipeline_mode=pl.Buffered(1)) if single_buffer else {}
    return pl.BlockSpec((None,) + tuple(stacked_shape[1:]), lambda *_: (layer,) + (0,) * nd, **mode)


def _params():
    return pltpu.CompilerParams(vmem_limit_bytes=VMEM_LIMIT)


def _to_lanes(x):
    w = x.shape[1]
    pad = -w % V7X_LANES
    if pad:
        x = jnp.concatenate([x, jnp.zeros((x.shape[0], pad), F32)], axis=1)
    return x.T[0:w]


def _mix_in_body(x_ref, g_ref, w_ref, cos_ref, sin_ref, ret_ref, su_ref, gla_ref, glr_ref):
    h = _rmsnorm_rows(x_ref[...], g_ref[...]).astype(BF16)
    ret = _dot(h, w_ref[:, COL_RET:COL_S5])
    cos = cos_ref[...]
    sin = sin_ref[...]
    lane = lax.broadcasted_iota(jnp.int32, cos.shape, 1)
    first_half = (lane & (RET_HD // 2)) == 0

    def rotary(z):
        swapped = jnp.where(first_half, pltpu.roll(z, RET_W - RET_HD // 2, 1), pltpu.roll(z, RET_HD // 2, 1))
        return z * cos + swapped * sin

    ret_ref[:, 0:RET_W] = rotary(ret[:, 0:RET_W]).astype(BF16)
    ret_ref[:, RET_W:2 * RET_W] = (rotary(ret[:, RET_W:2 * RET_W]) * RET_HD ** -0.5).astype(BF16)
    ret_ref[:, 2 * RET_W:] = ret[:, 2 * RET_W:].astype(BF16)
    su_ref[...] = _dot(h, w_ref[:, COL_S5:COL_GLA]).astype(BF16)
    gla = _dot(h, w_ref[:, COL_GLA:MIX_COLS])
    glane = lax.broadcasted_iota(jnp.int32, (1, GLA_COLS), 1)
    scale = jnp.where(glane < GLA_KW, GLA_DK ** -0.5, jnp.where(glane < GLA_KP, 0.0, 1.0))
    gla_ref[...] = (gla * scale).astype(BF16)
    glr_ref[...] = gla[:, GLA_LR0:GLA_LR0 + GLA_RANK]


def _mix_weights(w_in):
    w = w_in.astype(BF16)
    k0, v0 = COL_GLA + GLA_KW, COL_GLA + 2 * GLA_KW
    zeros = lambda n: jnp.zeros(w.shape[:2] + (n,), BF16)
    return jnp.concatenate([w[..., :k0], w[..., COL_LR:], zeros(GLA_KP - GLA_KW - GLA_RANK),
                            w[..., k0:v0], zeros(GLA_KP - GLA_KW), w[..., v0:COL_LR]], axis=-1)


def _mix_in(x2d, norm_g, w_in_bf, layer, cos_tab, sin_tab, tm):
    rows = x2d.shape[0]
    n_tab = cos_tab.shape[0] // tm
    row_spec = lambda w: pl.BlockSpec((tm, w), lambda i: (i, 0))
    tab_spec = pl.BlockSpec((tm, RET_W), lambda i: (i % n_tab, 0))
    return pl.pallas_call(
        _mix_in_body,
        grid=(rows // tm,),
        in_specs=[row_spec(D_MODEL), _const_spec((1, D_MODEL)), _layer_spec(w_in_bf.shape, layer, True),
                  tab_spec, tab_spec],
        out_specs=[row_spec(4 * RET_W), row_spec(S5_W), row_spec(GLA_COLS), row_spec(GLA_RANK)],
        out_shape=[jax.ShapeDtypeStruct((rows, 4 * RET_W), BF16), jax.ShapeDtypeStruct((rows, S5_W), BF16),
                   jax.ShapeDtypeStruct((rows, GLA_COLS), BF16), jax.ShapeDtypeStruct((rows, GLA_RANK), F32)],
        compiler_params=_params(),
        name="mix_in",
    )(x2d, norm_g.reshape(1, D_MODEL), w_in_bf, cos_tab, sin_tab)


def _retention_gammas():
    return 1.0 - 2.0 ** (-5.0 - np.arange(RET_HEADS))


def _retention_consts():
    t = np.arange(CHUNK)
    gam = _retention_gammas()
    causal = t[:, None] >= t[None, :]
    diff = np.maximum(t[:, None] - t[None, :], 0)
    dmask = np.zeros((RET_PAIRS, CHUNK, 2 * CHUNK))
    cdec = np.zeros((RET_PAIRS, 2 * RET_HD, 2 * RET_HD))
    for p in range(RET_PAIRS):
        for s in range(2):
            g = gam[2 * p + s]
            dmask[p, :, s * CHUNK:(s + 1) * CHUNK] = np.where(causal, g ** diff, 0.0)
            cdec[p, s * RET_HD:(s + 1) * RET_HD, s * RET_HD:(s + 1) * RET_HD] = g ** CHUNK
    lane_gam = np.repeat(gam, RET_HD)[None, :]
    qdec = lane_gam ** (t[:, None] + 1.0)
    kdec = lane_gam ** (CHUNK - 1.0 - t[:, None])
    bd = (cdec[0] > 0).astype(np.float32)
    head_mask = np.stack([np.arange(2 * RET_HD) < RET_HD, np.arange(2 * RET_HD) >= RET_HD]).astype(np.float32)
    ones_blk = np.kron(np.eye(RET_HEADS), np.ones((RET_HD, RET_HD)))
    f = lambda a: jnp.asarray(a, F32)
    return dict(dmask=f(dmask), cdec=f(cdec), qdec=f(qdec), kdec=f(kdec), bd=f(bd),
                head_mask=jnp.asarray(head_mask, BF16), ones_blk=jnp.asarray(ones_blk, BF16))


def _retention_body(ret_ref, dmask_ref, cdec_ref, qdec_ref, kdec_ref, bd_ref, hm_ref, ones_ref, lng_ref, lnb_ref,
                    o_ref, sout_ref, s_scr, o_scr, *, tl):
    @pl.when(pl.program_id(1) == 0)
    def _():
        s_scr[...] = jnp.zeros_like(s_scr)

    bd = bd_ref[...]
    m0 = hm_ref[0:1, :]
    m1 = hm_ref[1:2, :]
    q = ret_ref[:, 0:RET_W]
    k = ret_ref[:, RET_W:2 * RET_W]
    v = ret_ref[:, 2 * RET_W:3 * RET_W]
    q_start = (q.astype(F32) * qdec_ref[...]).astype(BF16)
    k_end = (k.astype(F32) * kdec_ref[...]).astype(BF16)
    blocks = [(n, p) for n in range(tl // CHUNK) for p in range(RET_PAIRS)]

    def part(a, n, p):
        return a[n * CHUNK:(n + 1) * CHUNK, 2 * RET_HD * p:2 * RET_HD * (p + 1)]

    scores, o, kv = {}, {}, {}
    for n, p in blocks:
        kp = part(k, n, p)
        kk = jnp.concatenate([kp * m0, kp * m1], axis=0)
        scores[n, p] = (_dot_nt(part(q, n, p), kk) * dmask_ref[p]).astype(BF16)
    for n, p in blocks:
        vp = part(v, n, p)
        o[n, p] = _dot(scores[n, p], jnp.concatenate([vp * m0, vp * m1], axis=0))
    for n, p in blocks:
        kv[n, p] = _dot_tn(part(k_end, n, p), part(v, n, p)) * bd
    for n, p in blocks:
        s_prev = s_scr[p]
        o_scr[n * CHUNK:(n + 1) * CHUNK, 2 * RET_HD * p:2 * RET_HD * (p + 1)] = (
            o[n, p] + _dot(part(q_start, n, p), s_prev.astype(BF16)))
        s_scr[p] = s_prev * cdec_ref[p] + kv[n, p]

    o_all = o_scr[...]
    ones_blk = ones_ref[...]
    mu = _dot_row_halves(o_all.astype(BF16), ones_blk) * (1.0 / RET_HD)
    cen = o_all - mu
    var = _dot_row_halves((cen * cen).astype(BF16), ones_blk) * (1.0 / RET_HD)
    y = cen * lax.rsqrt(var + EPS) * lng_ref[...] + lnb_ref[...]
    gate = ret_ref[:, 3 * RET_W:4 * RET_W].astype(F32)
    o_ref[...] = (y * gate * _sigmoid(gate)).astype(BF16)
    for p in range(RET_PAIRS):
        s = s_scr[p]
        sout_ref[0, 2 * p] = s[0:RET_HD, 0:RET_HD]
        sout_ref[0, 2 * p + 1] = s[RET_HD:2 * RET_HD, RET_HD:2 * RET_HD]


def _retention(ret2d, ln_g, ln_b, batch, seq_len):
    assert seq_len % CHUNK == 0
    c = _retention_consts()
    tl = min(MIXER_TILE, seq_len)
    nl = seq_len // tl
    tile_rows = lambda a: jnp.tile(a, (tl // CHUNK, 1))
    consts = [c["dmask"], c["cdec"], tile_rows(c["qdec"]), tile_rows(c["kdec"]), c["bd"], c["head_mask"],
              c["ones_blk"], ln_g.reshape(1, RET_W), ln_b.reshape(1, RET_W)]
    state_shape = (batch, RET_HEADS, RET_HD, RET_HD)
    return pl.pallas_call(
        functools.partial(_retention_body, tl=tl),
        grid=(batch, nl),
        in_specs=[pl.BlockSpec((tl, 4 * RET_W), lambda b, l: (b * nl + l, 0))] + [_const_spec(a.shape) for a in consts],
        out_specs=[pl.BlockSpec((tl, RET_W), lambda b, l: (b * nl + l, 0)),
                   pl.BlockSpec((1,) + state_shape[1:], lambda b, l: (b, 0, 0, 0))],
        out_shape=[jax.ShapeDtypeStruct((batch * seq_len, RET_W), BF16), jax.ShapeDtypeStruct(state_shape, F32)],
        scratch_shapes=[pltpu.VMEM((RET_PAIRS, 2 * RET_HD, 2 * RET_HD), F32), pltpu.VMEM((tl, RET_W), F32)],
        compiler_params=_params(),
        name="retention",
    )(ret2d, *consts)


def _retention_lanes_body(ret_ref, tab_ref, lng_ref, lnb_ref, s_ref, o_ref, sout_ref, q_scr, k_scr, v_scr, g_scr,
                          o_scr, *, seq_len):
    h = pl.program_id(0)
    nb = V7X_LANES
    steps = range(seq_len)

    @pl.when(h == 0)
    def _():
        for t in steps:
            blk = ret_ref[t * nb:(t + 1) * nb, :].astype(F32)
            for scr, off in ((q_scr, 0), (k_scr, RET_W), (v_scr, 2 * RET_W), (g_scr, 3 * RET_W)):
                tr = _to_lanes(blk[:, off:off + RET_W])
                for hh in range(RET_HEADS):
                    scr[t, hh] = tr[hh * RET_HD:(hh + 1) * RET_HD]

    tab = tab_ref[h]
    row = lambda r: tab[r:r + 1, :]
    q = [q_scr[t, h] for t in steps]
    k = [k_scr[t, h] for t in steps]
    v = [v_scr[t, h] for t in steps]
    o = []
    for i in steps:
        acc = None
        for j in range(i + 1):
            s = jnp.sum(q[i] * k[j], axis=0, keepdims=True)
            if i > j:
                s = s * row(i - j - 1)
            acc = s * v[j] if acc is None else acc + s * v[j]
        o.append(acc)
    q_start = [q[t] * row(t) for t in steps]
    k_end = [k[t] * row(seq_len + t) for t in steps]
    chunk_decay = row(2 * seq_len)
    for d in range(RET_HD):
        s_d = s_ref[d]
        new = s_d * chunk_decay
        for t in steps:
            o[t] = o[t] + q_start[t][d:d + 1, :] * s_d
            new = new + k_end[t][d:d + 1, :] * v[t]
        sout_ref[d] = new
    for t in steps:
        mu = jnp.mean(o[t], axis=0, keepdims=True)
        cen = o[t] - mu
        var = jnp.mean(cen * cen, axis=0, keepdims=True)
        gate = g_scr[t, h]
        o_scr[t, h] = (cen * lax.rsqrt(var + EPS) * lng_ref[h] + lnb_ref[h]) * gate * _sigmoid(gate)

    @pl.when(h == RET_HEADS - 1)
    def _():
        for t in steps:
            o_ref[t * nb:(t + 1) * nb, :] = o_scr[t].reshape(RET_W, nb).T.astype(BF16)


def _retention_lanes(ret2d, ln_g, ln_b, state, layer, seq_len):
    nb = V7X_LANES
    gam = _retention_gammas()[:, None]
    t = np.arange(seq_len)[None, :]
    rows = np.concatenate([gam ** (t + 1.0), gam ** (seq_len - 1.0 - t), gam ** (seq_len + 0.0 * t[:, :1])], axis=1)
    n_rows = -(-rows.shape[1] // V7X_SUBLANES) * V7X_SUBLANES
    tab = np.zeros((RET_HEADS, n_rows, nb))
    tab[:, :rows.shape[1], :] = rows[:, :, None]
    lanes = lambda p: jnp.broadcast_to(p.reshape(RET_HEADS, RET_HD, 1), (RET_HEADS, RET_HD, nb))
    consts = [jnp.asarray(tab, F32), lanes(ln_g), lanes(ln_b)]
    tile = pltpu.VMEM((seq_len, RET_HEADS, RET_HD, nb), F32)
    return pl.pallas_call(
        functools.partial(_retention_lanes_body, seq_len=seq_len),
        grid=(RET_HEADS,),
        in_specs=[_const_spec(ret2d.shape)] + [_const_spec(a.shape) for a in consts]
                 + [pl.BlockSpec((None, None, RET_HD, RET_HD, nb), lambda h: (layer, h, 0, 0, 0))],
        out_specs=[_const_spec((seq_len * nb, RET_W)), pl.BlockSpec((None, RET_HD, RET_HD, nb), lambda h: (h, 0, 0, 0))],
        out_shape=[jax.ShapeDtypeStruct((seq_len * nb, RET_W), BF16),
                   jax.ShapeDtypeStruct((RET_HEADS, RET_HD, RET_HD, nb), F32)],
        scratch_shapes=[tile] * 5,
        compiler_params=_params(),
        name="retention_lanes",
    )(ret2d, *consts, state)


def _gla_consts():
    t = np.arange(CHUNK)
    causal = t[:, None] >= t[None, :]
    pad = GLA_KP - GLA_KW
    kmask = np.pad(np.kron(np.eye(GLA_HEADS), np.ones((1, GLA_DK))), ((0, 0), (0, pad)))
    vmask = np.kron(np.eye(GLA_HEADS), np.ones((1, GLA_DV)))
    bd = np.pad(np.kron(np.eye(GLA_HEADS), np.ones((GLA_DK, GLA_DV))), ((0, pad), (0, 0)))
    ones_blk = np.kron(np.eye(GLA_HEADS), np.ones((GLA_DV, GLA_DV)))
    b = lambda a: jnp.asarray(a, BF16)
    return dict(tril=b(causal), causal4=jnp.asarray(np.tile(causal, (1, GLA_HEADS)), F32), kmask=b(kmask),
                vmask=b(vmask), bd=jnp.asarray(bd, F32), ones_blk=b(ones_blk))


def _gla_body(gla_ref, glr_ref, gw_ref, gb_ref, tril_ref, causal_ref, kmask_ref, vmask_ref, bd_ref, ones_blk_ref,
              block_ref, ng_ref, o_ref, sout_ref, s_scr, o_scr, *, tl):
    @pl.when(pl.program_id(1) == 0)
    def _():
        s_scr[...] = jnp.zeros_like(s_scr)

    bd = bd_ref[...]
    z = _dot_row_halves(glr_ref[...].astype(BF16), gw_ref[...]) + gb_ref[...]
    lg = _log_sigmoid(z) * (1.0 / GLA_GATE_TEMP)
    q0, k0, v0, g0 = GLA_Q0, GLA_K0, GLA_V0, GLA_G0
    chunks = range(tl // CHUNK)

    def rows(a, n):
        return a[n * CHUNK:(n + 1) * CHUNK]

    cums = [_dot_exact_lhs(tril_ref[...], rows(lg, n)) for n in chunks]
    b_cum = jnp.concatenate(cums, axis=0)
    b_end = jnp.concatenate([jnp.broadcast_to(c[CHUNK - 1:CHUNK, :], (CHUNK, GLA_KP)) for c in cums], axis=0)
    q_in = (gla_ref[:, q0:q0 + GLA_KP].astype(F32) * jnp.exp(b_cum)).astype(BF16)
    kf = gla_ref[:, k0:k0 + GLA_KP].astype(F32)
    k_in = (kf * jnp.exp(-b_cum)).astype(BF16)
    k_dec = (kf * jnp.exp(b_end - b_cum)).astype(BF16)
    v = gla_ref[:, v0:g0]
    lg_hi, lg_lo = _hi_lo(lg)
    block_decay = jnp.exp(_dot_tn(lg_hi, block_ref[...]) + _dot_tn(lg_lo, block_ref[...]))

    scores, o, kv = {}, {}, {}
    for n in chunks:
        kn = rows(k_in, n)
        kk = jnp.concatenate([kn * kmask_ref[h:h + 1, :] for h in range(GLA_HEADS)], axis=0)
        scores[n] = (_dot_nt(rows(q_in, n), kk) * causal_ref[...]).astype(BF16)
    for n in chunks:
        vn = rows(v, n)
        vv = jnp.concatenate([vn * vmask_ref[h:h + 1, :] for h in range(GLA_HEADS)], axis=0)
        o[n] = _dot(scores[n], vv)
    for n in chunks:
        kv[n] = _dot_tn(rows(k_dec, n), rows(v, n)) * bd
    for n in chunks:
        s_prev = s_scr[...]
        o_scr[n * CHUNK:(n + 1) * CHUNK, :] = o[n] + _dot(rows(q_in, n), s_prev.astype(BF16))
        s_scr[...] = s_prev * jnp.broadcast_to(block_decay[:, n:n + 1], (GLA_KP, GLA_W)) + kv[n]

    o_all = o_scr[...]
    ms = _dot_row_halves((o_all * o_all).astype(BF16), ones_blk_ref[...]) * (1.0 / GLA_DV)
    gate = gla_ref[:, g0:g0 + GLA_W].astype(F32)
    o_ref[...] = (o_all * lax.rsqrt(ms + EPS) * ng_ref[...] * gate * _sigmoid(gate)).astype(BF16)
    s = s_scr[...]
    for h in range(GLA_HEADS):
        sout_ref[0, h] = s[h * GLA_DK:(h + 1) * GLA_DK, h * GLA_DV:(h + 1) * GLA_DV]


def _gla(gla2d, glr2d, gate_w, gate_b, norm_g, batch, seq_len):
    assert seq_len % CHUNK == 0
    c = _gla_consts()
    tl = min(MIXER_TILE, seq_len)
    nl = seq_len // tl
    assert tl // CHUNK <= V7X_LANES
    block = (np.arange(tl) // CHUNK)[:, None] == np.arange(V7X_LANES)[None, :]
    consts = [_pad_gate(gate_w, BF16), _pad_gate(gate_b.reshape(1, GLA_KW), F32), c["tril"], c["causal4"], c["kmask"], c["vmask"], c["bd"],
              c["ones_blk"], jnp.asarray(block, BF16), jnp.tile(norm_g, GLA_HEADS).reshape(1, GLA_W)]
    row_spec = lambda w: pl.BlockSpec((tl, w), lambda b, l: (b * nl + l, 0))
    state_shape = (batch, GLA_HEADS, GLA_DK, GLA_DV)
    return pl.pallas_call(
        functools.partial(_gla_body, tl=tl),
        grid=(batch, nl),
        in_specs=[row_spec(GLA_COLS), row_spec(GLA_RANK)] + [_const_spec(a.shape) for a in consts],
        out_specs=[row_spec(GLA_W), pl.BlockSpec((1,) + state_shape[1:], lambda b, l: (b, 0, 0, 0))],
        out_shape=[jax.ShapeDtypeStruct((batch * seq_len, GLA_W), BF16), jax.ShapeDtypeStruct(state_shape, F32)],
        scratch_shapes=[pltpu.VMEM((GLA_KP, GLA_W), F32), pltpu.VMEM((tl, GLA_W), F32)],
        compiler_params=_params(),
        name="gla",
    )(gla2d, glr2d, *consts)


def _gla_lanes_body(gla_ref, glr_ref, gw_ref, gb_ref, ng_ref, s_ref, o_ref, sout_ref, q_scr, k_scr, kd_scr, v_scr,
                    g_scr, dec_scr, o_scr, *, seq_len):
    h = pl.program_id(0)
    nb = V7X_LANES
    steps = range(seq_len)
    q0, k0, v0, g0 = GLA_Q0, GLA_K0, GLA_V0, GLA_G0
    kw = slice(0, GLA_KW)

    def split(scr, t, tr, width):
        for hh in range(GLA_HEADS):
            scr[t, hh] = tr[hh * width:(hh + 1) * width]

    @pl.when(h == 0)
    def _():
        z = _dot_row_halves(glr_ref[...].astype(BF16), gw_ref[...]) + gb_ref[...]
        lg = (_log_sigmoid(z) * (1.0 / GLA_GATE_TEMP))[:, kw]
        cums = []
        for t in steps:
            lg_t = lg[t * nb:(t + 1) * nb]
            cums.append(lg_t if t == 0 else cums[-1] + lg_t)
        b_end = cums[-1]
        dec = _to_lanes(jnp.exp(b_end))
        for hh in range(GLA_HEADS):
            dec_scr[hh] = dec[hh * GLA_DK:(hh + 1) * GLA_DK]
        for t in steps:
            blk = gla_ref[t * nb:(t + 1) * nb, :].astype(F32)
            kf = blk[:, k0:k0 + GLA_KW]
            split(q_scr, t, _to_lanes(blk[:, q0:q0 + GLA_KW] * jnp.exp(cums[t])), GLA_DK)
            split(k_scr, t, _to_lanes(kf * jnp.exp(-cums[t])), GLA_DK)
            split(kd_scr, t, _to_lanes(kf * jnp.exp(b_end - cums[t])), GLA_DK)
            split(v_scr, t, _to_lanes(blk[:, v0:g0]), GLA_DV)
            split(g_scr, t, _to_lanes(blk[:, g0:g0 + GLA_W]), GLA_DV)

    q = [q_scr[t, h] for t in steps]
    k = [k_scr[t, h] for t in steps]
    kd = [kd_scr[t, h] for t in steps]
    v = [v_scr[t, h] for t in steps]
    dec = dec_scr[h]
    o = []
    for i in steps:
        acc = None
        for j in range(i + 1):
            s = jnp.sum(q[i] * k[j], axis=0, keepdims=True)
            acc = s * v[j] if acc is None else acc + s * v[j]
        o.append(acc)
    for i in range(GLA_DK):
        s_i = s_ref[i]
        new = s_i * dec[i:i + 1, :]
        for t in steps:
            o[t] = o[t] + q[t][i:i + 1, :] * s_i
            new = new + kd[t][i:i + 1, :] * v[t]
        sout_ref[i] = new
    for t in steps:
        ms = jnp.mean(o[t] * o[t], axis=0, keepdims=True)
        gate = g_scr[t, h]
        o_scr[t, h] = o[t] * lax.rsqrt(ms + EPS) * ng_ref[...] * gate * _sigmoid(gate)

    @pl.when(h == GLA_HEADS - 1)
    def _():
        for t in steps:
            o_ref[t * nb:(t + 1) * nb, :] = o_scr[t].reshape(GLA_W, nb).T.astype(BF16)


def _gla_lanes(gla2d, glr2d, gate_w, gate_b, norm_g, state, layer, seq_len):
    nb = V7X_LANES
    consts = [_pad_gate(gate_w, BF16), _pad_gate(gate_b.reshape(1, GLA_KW), F32),
              jnp.broadcast_to(norm_g.reshape(GLA_DV, 1), (GLA_DV, nb))]
    k_tile = pltpu.VMEM((seq_len, GLA_HEADS, GLA_DK, nb), F32)
    v_tile = pltpu.VMEM((seq_len, GLA_HEADS, GLA_DV, nb), F32)
    return pl.pallas_call(
        functools.partial(_gla_lanes_body, seq_len=seq_len),
        grid=(GLA_HEADS,),
        in_specs=[_const_spec(gla2d.shape), _const_spec(glr2d.shape)] + [_const_spec(a.shape) for a in consts]
                 + [pl.BlockSpec((None, None, GLA_DK, GLA_DV, nb), lambda h: (layer, h, 0, 0, 0))],
        out_specs=[_const_spec((seq_len * nb, GLA_W)), pl.BlockSpec((None, GLA_DK, GLA_DV, nb), lambda h: (h, 0, 0, 0))],
        out_shape=[jax.ShapeDtypeStruct((seq_len * nb, GLA_W), BF16),
                   jax.ShapeDtypeStruct((GLA_HEADS, GLA_DK, GLA_DV, nb), F32)],
        scratch_shapes=[k_tile, k_tile, k_tile, v_tile, v_tile, pltpu.VMEM((GLA_HEADS, GLA_DK, nb), F32), v_tile],
        compiler_params=_params(),
        name="gla_lanes",
    )(gla2d, glr2d, *consts, state)


def _s5_discretize(lam_re, lam_im, log_dt, b_re, b_im, c_re, c_im):
    lr = jnp.minimum(lam_re.astype(F32), -S5_MIN_NEG)
    li = lam_im.astype(F32)
    dt = jnp.exp(log_dt.astype(F32))[:, None]
    mag = jnp.exp(lr * dt)
    ar = mag * jnp.cos(li * dt)
    ai = mag * jnp.sin(li * dt)
    den = lr * lr + li * li
    cr = ((ar - 1.0) * lr + ai * li) / den
    ci = (ai * lr - (ar - 1.0) * li) / den
    b_re, b_im = b_re.astype(F32), b_im.astype(F32)
    bbar_re = cr[..., None] * b_re - ci[..., None] * b_im
    bbar_im = cr[..., None] * b_im + ci[..., None] * b_re
    eye = jnp.eye(S5_GROUPS, dtype=F32)
    in_blk = lambda b: jnp.einsum("gpi,gh->gihp", b, eye).reshape(S5_W, S5_LANES)
    out_blk = lambda c: jnp.einsum("gop,gh->gpho", c.astype(F32), eye).reshape(S5_LANES, S5_W)
    bdb = jnp.concatenate([in_blk(bbar_re), in_blk(bbar_im)], axis=1).astype(BF16)
    bdc = jnp.concatenate([out_blk(c_re), -out_blk(c_im)], axis=0).astype(BF16)
    return ar.reshape(1, S5_LANES), ai.reshape(1, S5_LANES), bdb, bdc


def _s5_body(*refs, tl, batch, has_state):
    u_ref, ar_ref, ai_ref, bdb_ref, bdc_ref, d_ref, gw_ref, gb_ref = refs[:8]
    n_in = 10 if has_state else 8
    o_ref, hr_ref, hi_ref, x_scr, h_scr = refs[n_in:]

    @pl.when(pl.program_id(0) == 0)
    def _():
        if has_state:
            h_scr[0] = refs[8][...]
            h_scr[1] = refs[9][...]
        else:
            h_scr[...] = jnp.zeros_like(h_scr)

    ar = jnp.broadcast_to(ar_ref[...], (batch, S5_LANES))
    ai = jnp.broadcast_to(ai_ref[...], (batch, S5_LANES))
    hr, hi = h_scr[0], h_scr[1]
    sub = min(tl, S5_SUB_STEPS)
    n_rows = sub * batch
    for s in range(tl // sub):
        rows = slice(s * n_rows, (s + 1) * n_rows)
        u = u_ref[rows, :]
        x_scr[rows, :] = _dot_row_halves(u, bdb_ref[...])
        for t in range(s * sub, (s + 1) * sub):
            r = slice(t * batch, (t + 1) * batch)
            hr, hi = (ar * hr - ai * hi + x_scr[r, 0:S5_LANES],
                      ar * hi + ai * hr + x_scr[r, S5_LANES:2 * S5_LANES])
            x_scr[r, 0:S5_LANES] = hr
            x_scr[r, S5_LANES:2 * S5_LANES] = hi
        y = _dot_row_halves(x_scr[rows, :].astype(BF16), bdc_ref[...]) + d_ref[...] * u.astype(F32)
        y = _gelu_tanh(y)
        o_ref[rows, :] = (y * _sigmoid(_dot_row_halves(y.astype(BF16), gw_ref[...]) + gb_ref[...])).astype(BF16)
    h_scr[0] = hr
    h_scr[1] = hi
    hr_ref[...] = hr
    hi_ref[...] = hi


def _s5(u_tm, disc, d, glu_w_bf, glu_b, state, batch, seq_len):
    ar, ai, bdb, bdc = disc
    tl = min(S5_TIME_TILE, seq_len)
    has_state = state is not None
    consts = [ar, ai, bdb, bdc, d.reshape(1, S5_W), glu_w_bf, glu_b.reshape(1, S5_W)]
    h_spec = pl.BlockSpec((batch, S5_LANES), lambda i: (0, 0))
    in_specs = [pl.BlockSpec((tl * batch, S5_W), lambda i: (i, 0))] + [_const_spec(a.shape) for a in consts]
    args = [u_tm] + consts
    if has_state:
        in_specs += [h_spec, h_spec]
        args += list(state)
    return pl.pallas_call(
        functools.partial(_s5_body, tl=tl, batch=batch, has_state=has_state),
        grid=(seq_len // tl,),
        in_specs=in_specs,
        out_specs=[pl.BlockSpec((tl * batch, S5_W), lambda i: (i, 0)), h_spec, h_spec],
        out_shape=[jax.ShapeDtypeStruct((seq_len * batch, S5_W), BF16),
                   jax.ShapeDtypeStruct((batch, S5_LANES), F32), jax.ShapeDtypeStruct((batch, S5_LANES), F32)],
        scratch_shapes=[pltpu.VMEM((tl * batch, 2 * S5_LANES), F32), pltpu.VMEM((2, batch, S5_LANES), F32)],
        compiler_params=_params(),
        name="s5",
    )(*args)


def _ffn_body(*refs, tm, time_major_batch, final):
    (x_ref, oret_ref, os5_ref, ogla_ref, wout_ref, gffn_ref, win_ref, cw_ref, cb_ref, wo_ref) = refs[:10]
    n = 10
    gfin_ref = None
    if final:
        gfin_ref, n = refs[n], n + 1
    if time_major_batch is not None:
        past_ref, n = refs[n], n + 1
    out_ref, conv_ref, x1_scr, h_scr, carry_scr, act_scr = refs[n:]
    pad = V7X_SUBLANES
    nb = time_major_batch

    if nb is None:
        @pl.when(pl.program_id(1) == 0)
        def _():
            carry_scr[...] = jnp.zeros_like(carry_scr)

    mix = jnp.concatenate([oret_ref[...], os5_ref[...], ogla_ref[...]], axis=1)
    x1_scr[...] = x_ref[...] + _dot(mix, wout_ref[...])
    h_scr[...] = _rmsnorm_rows(x1_scr[...], gffn_ref[...]).astype(BF16)
    h = h_scr[...]

    for c in range(D_FF // ACT_COLS):
        cols = slice(c * ACT_COLS, (c + 1) * ACT_COLS)
        a_c = _dot(h, win_ref[:, c * ACT_COLS:(c + 1) * ACT_COLS])
        gate_c = _dot(h, win_ref[:, D_FF + c * ACT_COLS:D_FF + (c + 1) * ACT_COLS])
        if nb is None:
            ext = jnp.concatenate([carry_scr[:, cols], a_c], axis=0)
            prev1 = ext[pad - 1:pad - 1 + tm]
            prev2 = ext[pad - 2:pad - 2 + tm]
            carry_scr[:, cols] = a_c[tm - pad:tm]
        else:
            prev1 = jnp.concatenate([past_ref[1, :, cols], a_c[0:tm - nb]], axis=0)
            prev2 = jnp.concatenate([past_ref[0, :, cols], past_ref[1, :, cols], a_c[0:tm - 2 * nb]], axis=0)
            conv_ref[:, cols] = a_c[tm - (CONV_W - 1) * nb:tm]
        conv = (cb_ref[:, cols] + prev2 * cw_ref[0:1, cols] + prev1 * cw_ref[1:2, cols]
                + a_c * cw_ref[2:3, cols])
        act_scr[:, cols] = (_gelu_tanh(conv) * gate_c).astype(BF16)

    x2 = x1_scr[...] + _dot(act_scr[...], wo_ref[...])
    out_ref[...] = _rmsnorm_rows(x2, gfin_ref[...]) if final else x2
    if nb is None:
        conv_ref[0] = carry_scr[pad - (CONV_W - 1):pad, :]


def _ffn(x2d, o_ret, o_s5, o_gla, w_out_bf, norm_g, w_in_bf, conv_w, conv_b, w_o_bf, layer, final_g, conv_past,
         batch, seq_len):
    rows = batch * seq_len
    time_major = conv_past is not None
    tm = rows if time_major else min(ROW_TILE, seq_len)
    final = final_g is not None
    cw = jnp.concatenate([conv_w.astype(F32), jnp.zeros((V7X_SUBLANES - CONV_W, D_FF), F32)], axis=0)
    consts = [w_out_bf, norm_g.reshape(1, D_MODEL), w_in_bf, cw, conv_b.reshape(1, D_FF), w_o_bf]
    stacked = [True, False, True, False, False, True]
    if final:
        consts.append(final_g.reshape(1, D_MODEL))
        stacked.append(False)
    if time_major:
        assert seq_len >= CONV_W - 1 and batch % V7X_SUBLANES == 0
        consts.append(conv_past)
        stacked.append(False)
        grid = (1,)
        imap = lambda i: (0, 0)
        conv_rows = (CONV_W - 1) * batch
        conv_spec = pl.BlockSpec((conv_rows, D_FF), imap)
        conv_shape = jax.ShapeDtypeStruct((conv_rows, D_FF), F32)
    else:
        assert seq_len % tm == 0
        nl = seq_len // tm
        grid = (batch, nl)
        imap = lambda b, l: (b * nl + l, 0)
        conv_spec = pl.BlockSpec((1, CONV_W - 1, D_FF), lambda b, l: (b, 0, 0))
        conv_shape = jax.ShapeDtypeStruct((batch, CONV_W - 1, D_FF), F32)
    row_spec = lambda w: pl.BlockSpec((tm, w), imap)
    in_specs = ([row_spec(D_MODEL), row_spec(RET_W), row_spec(S5_W), row_spec(GLA_W)]
                + [_layer_spec(a.shape, layer, True) if s else _const_spec(a.shape) for a, s in zip(consts, stacked)])
    return pl.pallas_call(
        functools.partial(_ffn_body, tm=tm, time_major_batch=batch if time_major else None, final=final),
        grid=grid,
        in_specs=in_specs,
        out_specs=[row_spec(D_MODEL), conv_spec],
        out_shape=[jax.ShapeDtypeStruct((rows, D_MODEL), F32), conv_shape],
        scratch_shapes=[pltpu.VMEM((tm, D_MODEL), F32), pltpu.VMEM((tm, D_MODEL), BF16),
                        pltpu.VMEM((V7X_SUBLANES, D_FF), F32), pltpu.VMEM((tm, D_FF), BF16)],
        compiler_params=_params(),
        name="ffn",
    )(x2d, o_ret, o_s5, o_gla, *consts)


def _rotary_tables(first_pos, n_pos, repeat=1):
    half = RET_HD // 2
    inv = ROPE_BASE ** (-(np.arange(half, dtype=np.float64) / half))
    ang = (first_pos + np.arange(n_pos)).astype(np.float64)[:, None] * inv[None, :]
    cos, sin = np.cos(ang), np.sin(ang)
    expand = lambda a, b: np.repeat(np.tile(np.concatenate([a, b], axis=1), (1, RET_HEADS)), repeat, axis=0)
    return jnp.asarray(expand(cos, cos), F32), jnp.asarray(expand(-sin, sin), F32)


def _swap_major(a2d, outer, inner):
    return a2d.reshape(outer, inner, -1).transpose(1, 0, 2).reshape(outer * inner, -1)


def _run_prompt_group(x, layers, big, final_g):
    batch, seq_len, _ = x.shape
    rows = batch * seq_len
    tm = min(ROW_TILE, seq_len)
    cos_tab, sin_tab = _rotary_tables(0, seq_len)
    x2d = x.reshape(rows, D_MODEL)
    outs = []
    for li, lp in enumerate(layers):
        ret, su, gla, glr = _mix_in(x2d, lp["norm_mix_g"], big["w_in"], li, cos_tab, sin_tab, tm)
        o_ret, s_ret = _retention(ret, lp["ret_norm_g"], lp["ret_norm_b"], batch, seq_len)
        o_s5, s5r, s5i = _s5(_swap_major(su, batch, seq_len), lp["s5_disc"], lp["s5_d"], lp["s5_glu_w"],
                             lp["s5_glu_b"], None, batch, seq_len)
        o_s5 = _swap_major(o_s5, seq_len, batch)
        o_gla, s_gla = _gla(gla, glr, lp["gla_gate_w"], lp["gla_gate_b"], lp["gla_norm_g"], batch, seq_len)
        last = li == len(layers) - 1
        x2d, conv_new = _ffn(x2d, o_ret, o_s5, o_gla, big["w_out"], lp["norm_ffn_g"], big["ffn_w_in"],
                             lp["ffn_conv_w"], lp["ffn_conv_b"], big["ffn_w_out"], li, final_g if last else None,
                             None, batch, seq_len)
        s5_shape = (batch, S5_GROUPS, S5_STATE)
        outs.append((s_ret, s5r.reshape(s5_shape), s5i.reshape(s5_shape), s_gla, conv_new))
    return [x2d.reshape(batch, seq_len, D_MODEL)] + [jnp.stack([o[i] for o in outs]) for i in range(5)]


def _run_sample_group(x, past_len, states, layers, big, final_g):
    batch, seq_len, _ = x.shape
    assert batch == V7X_LANES
    rows = batch * seq_len
    cos_tab, sin_tab = _rotary_tables(past_len, seq_len, repeat=batch)
    x2d = x.transpose(1, 0, 2).reshape(rows, D_MODEL)
    ret_state = states["ret"].transpose(0, 2, 3, 4, 1)
    gla_state = states["gla"].transpose(0, 2, 3, 4, 1)
    conv_state = states["conv"].transpose(0, 2, 1, 3)
    depth = len(layers)
    s5_flat = lambda s: s.reshape(depth, batch, S5_LANES)
    s5r_state, s5i_state = s5_flat(states["s5r"]), s5_flat(states["s5i"])
    outs = []
    for li, lp in enumerate(layers):
        ret, su, gla, glr = _mix_in(x2d, lp["norm_mix_g"], big["w_in"], li, cos_tab, sin_tab, rows)
        o_ret, s_ret = _retention_lanes(ret, lp["ret_norm_g"], lp["ret_norm_b"], ret_state, li, seq_len)
        o_s5, s5r, s5i = _s5(su, lp["s5_disc"], lp["s5_d"], lp["s5_glu_w"], lp["s5_glu_b"],
                             (s5r_state[li], s5i_state[li]), batch, seq_len)
        o_gla, s_gla = _gla_lanes(gla, glr, lp["gla_gate_w"], lp["gla_gate_b"], lp["gla_norm_g"], gla_state, li,
                                  seq_len)
        last = li == len(layers) - 1
        x2d, conv_new = _ffn(x2d, o_ret, o_s5, o_gla, big["w_out"], lp["norm_ffn_g"], big["ffn_w_in"],
                             lp["ffn_conv_w"], lp["ffn_conv_b"], big["ffn_w_out"], li, final_g if last else None,
                             conv_state[li], batch, seq_len)
        outs.append((s_ret, s5r, s5i, s_gla, conv_new.reshape(CONV_W - 1, batch, D_FF)))
    s_ret, s5r, s5i, s_gla, conv_new = (jnp.stack([o[i] for o in outs]) for i in range(5))
    s5_shape = (depth, batch, S5_GROUPS, S5_STATE)
    return [x2d.reshape(seq_len, batch, D_MODEL).transpose(1, 0, 2), s_ret.transpose(0, 4, 1, 2, 3),
            s5r.reshape(s5_shape), s5i.reshape(s5_shape), s_gla.transpose(0, 4, 1, 2, 3),
            conv_new.transpose(0, 2, 1, 3)]


def kernel(x_prompt, x_sample, state_ret, state_s5_re, state_s5_im, state_gla, state_ffn_conv, norm_mix_g, w_in, ret_norm_g, ret_norm_b, s5_lambda_re, s5_lambda_im, s5_log_dt, s5_b_re, s5_b_im, s5_c_re, s5_c_im, s5_d, s5_glu_w, s5_glu_b, gla_gate_w, gla_gate_b, gla_norm_g, w_out, norm_ffn_g, ffn_w_in, ffn_conv_w, ffn_conv_b, ffn_w_out, norm_final_g):
    depth = w_in.shape[0]
    big = dict(w_in=_mix_weights(w_in), w_out=w_out.astype(BF16), ffn_w_in=ffn_w_in.astype(BF16),
               ffn_w_out=ffn_w_out.astype(BF16))
    layers = []
    for l in range(depth):
        layers.append(dict(
            norm_mix_g=norm_mix_g[l], ret_norm_g=ret_norm_g[l], ret_norm_b=ret_norm_b[l],
            s5_disc=_s5_discretize(s5_lambda_re[l], s5_lambda_im[l], s5_log_dt[l], s5_b_re[l], s5_b_im[l],
                                   s5_c_re[l], s5_c_im[l]),
            s5_d=s5_d[l], s5_glu_w=s5_glu_w[l].astype(BF16), s5_glu_b=s5_glu_b[l],
            gla_gate_w=gla_gate_w[l], gla_gate_b=gla_gate_b[l], gla_norm_g=gla_norm_g[l],
            norm_ffn_g=norm_ffn_g[l], ffn_conv_w=ffn_conv_w[l], ffn_conv_b=ffn_conv_b[l]))
    sample_states = dict(ret=state_ret, s5r=state_s5_re, s5i=state_s5_im, gla=state_gla, conv=state_ffn_conv)
    yp, ret_p, s5r_p, s5i_p, gla_p, conv_p = _run_prompt_group(x_prompt, layers, big, norm_final_g)
    ys, ret_s, s5r_s, s5i_s, gla_s, conv_s = _run_sample_group(x_sample, PAST_LEN, sample_states, layers, big,
                                                               norm_final_g)
    return (yp, ys, ret_p, ret_s, s5r_p, s5r_s, s5i_p, s5i_s, gla_p, gla_s, conv_p, conv_s)
```

```python
import functools

import numpy as np
import jax
import jax.numpy as jnp
from jax import lax
from jax.experimental import pallas as pl
from jax.experimental.pallas import tpu as pltpu

F32, BF16 = jnp.float32, jnp.bfloat16

D_MODEL = 1024
RET_W, S5_W, GLA_W = 384, 256, 384
RET_HEADS, RET_HD = 6, 64
RET_PAIRS = RET_HEADS // 2
S5_GROUPS, S5_CH, S5_STATE = 16, 16, 64
S5_LANES = S5_GROUPS * S5_STATE
S5_MIN_NEG = 1e-4
GLA_HEADS, GLA_DK, GLA_DV, GLA_RANK = 4, 48, 96, 16
GLA_KW = GLA_HEADS * GLA_DK
GLA_GATE_TEMP = 16.0
D_FF = 2816
CONV_W = 3
ROPE_BASE = 10000.0
CHUNK = 64
EPS = 1e-6
PAST_LEN = 16384
IN_COLS = 4 * RET_W + S5_W + 2 * GLA_KW + 2 * GLA_W + GLA_RANK
COL_RET, COL_S5, COL_GLA, COL_LR = 0, 4 * RET_W, 4 * RET_W + S5_W, IN_COLS - GLA_RANK
GLA_KP = 256
GLA_Q0, GLA_LR0, GLA_K0, GLA_V0, GLA_G0 = 0, GLA_KW, GLA_KP, 2 * GLA_KP, 2 * GLA_KP + GLA_W
GLA_COLS = 2 * GLA_KP + 2 * GLA_W
MIX_COLS = COL_GLA + GLA_COLS

V7X_SUBLANES = 8
V7X_LANES = 128
VMEM_LIMIT = 58 * 1024 * 1024

ROW_TILE = 512
MIX_IN_TILE = 1024
MIXER_TILE = 2048
S5_TIME_TILE = 256
S5_SUB_STEPS = 64
ACT_COLS = 256


def _dot(a, b):
    return jnp.dot(a, b, preferred_element_type=F32)


def _dot_nt(a, b):
    return lax.dot_general(a, b, (((1,), (1,)), ((), ())), preferred_element_type=F32)


def _dot_tn(a, b):
    return lax.dot_general(a, b, (((0,), (0,)), ((), ())), preferred_element_type=F32)


def _hi_lo(x):
    hi = x.astype(BF16)
    return hi, (x - hi.astype(F32)).astype(BF16)


def _dot_exact_lhs(m, x):
    hi, lo = _hi_lo(x)
    return _dot(m, hi) + _dot(m, lo)


def _dot_row_halves(a, w):
    half = a.shape[0] // 2
    return jnp.concatenate([_dot(a[:half], w), _dot(a[half:], w)], axis=0)


def _pad_gate(a, dtype):
    return jnp.pad(a.astype(dtype), ((0, 0), (0, GLA_KP - GLA_KW)))


def _sigmoid(x):
    return 1.0 / (1.0 + jnp.exp(-x))


def _log_sigmoid(z):
    return jnp.minimum(z, 0.0) - jnp.log(1.0 + jnp.exp(-jnp.abs(z)))


def _gelu_tanh(x):
    return 0.5 * x * (1.0 + jnp.tanh(0.7978845608028654 * (x + 0.044715 * (x * x * x))))


def _rmsnorm_rows(x, g):
    return x * lax.rsqrt(jnp.mean(x * x, axis=-1, keepdims=True) + EPS) * g


def _const_spec(shape):
    nd = len(shape)
    return pl.BlockSpec(shape, lambda *_: (0,) * nd)


def _layer_spec(stacked_shape, layer, single_buffer=False):
    nd = len(stacked_shape) - 1
    mode = dict(pipeline_mode=pl.Buffered(1)) if single_buffer else {}
    return pl.BlockSpec((None,) + tuple(stacked_shape[1:]), lambda *_: (layer,) + (0,) * nd, **mode)


def _params():
    return pltpu.CompilerParams(vmem_limit_bytes=VMEM_LIMIT)


def _to_lanes(x):
    w = x.shape[1]
    pad = -w % V7X_LANES
    if pad:
        x = jnp.concatenate([x, jnp.zeros((x.shape[0], pad), F32)], axis=1)
    return x.T[0:w]


def _mix_in_body(x_ref, g_ref, w_ref, cos_ref, sin_ref, ret_ref, su_ref, gla_ref, glr_ref):
    h = _rmsnorm_rows(x_ref[...], g_ref[...]).astype(BF16)
    ret = _dot(h, w_ref[:, COL_RET:COL_S5])
    cos = cos_ref[...]
    sin = sin_ref[...]
    lane = lax.broadcasted_iota(jnp.int32, cos.shape, 1)
    first_half = (lane & (RET_HD // 2)) == 0

    def rotary(z):
        swapped = jnp.where(first_half, pltpu.roll(z, RET_W - RET_HD // 2, 1), pltpu.roll(z, RET_HD // 2, 1))
        return z * cos + swapped * sin

    ret_ref[:, 0:RET_W] = rotary(ret[:, 0:RET_W]).astype(BF16)
    ret_ref[:, RET_W:2 * RET_W] = (rotary(ret[:, RET_W:2 * RET_W]) * RET_HD ** -0.5).astype(BF16)
    ret_ref[:, 2 * RET_W:] = ret[:, 2 * RET_W:].astype(BF16)
    su_ref[...] = _dot(h, w_ref[:, COL_S5:COL_GLA]).astype(BF16)
    gla = _dot(h, w_ref[:, COL_GLA:MIX_COLS])
    glane = lax.broadcasted_iota(jnp.int32, (1, GLA_COLS), 1)
    scale = jnp.where(glane < GLA_KW, GLA_DK ** -0.5, jnp.where(glane < GLA_KP, 0.0, 1.0))
    gla_ref[...] = (gla * scale).astype(BF16)
    glr_ref[...] = gla[:, GLA_LR0:GLA_LR0 + GLA_RANK]


def _mix_weights(w_in):
    w = w_in.astype(BF16)
    k0, v0 = COL_GLA + GLA_KW, COL_GLA + 2 * GLA_KW
    zeros = lambda n: jnp.zeros(w.shape[:2] + (n,), BF16)
    return jnp.concatenate([w[..., :k0], w[..., COL_LR:], zeros(GLA_KP - GLA_KW - GLA_RANK),
                            w[..., k0:v0], zeros(GLA_KP - GLA_KW), w[..., v0:COL_LR]], axis=-1)


def _mix_in(x2d, norm_g, w_in_bf, layer, cos_tab, sin_tab, tm):
    rows = x2d.shape[0]
    n_tab = cos_tab.shape[0] // tm
    row_spec = lambda w: pl.BlockSpec((tm, w), lambda i: (i, 0))
    tab_spec = pl.BlockSpec((tm, RET_W), lambda i: (i % n_tab, 0))
    return pl.pallas_call(
        _mix_in_body,
        grid=(rows // tm,),
        in_specs=[row_spec(D_MODEL), _const_spec((1, D_MODEL)), _layer_spec(w_in_bf.shape, layer, True),
                  tab_spec, tab_spec],
        out_specs=[row_spec(4 * RET_W), row_spec(S5_W), row_spec(GLA_COLS), row_spec(GLA_RANK)],
        out_shape=[jax.ShapeDtypeStruct((rows, 4 * RET_W), BF16), jax.ShapeDtypeStruct((rows, S5_W), BF16),
                   jax.ShapeDtypeStruct((rows, GLA_COLS), BF16), jax.ShapeDtypeStruct((rows, GLA_RANK), F32)],
        compiler_params=_params(),
        name="mix_in",
    )(x2d, norm_g.reshape(1, D_MODEL), w_in_bf, cos_tab, sin_tab)


def _retention_gammas():
    return 1.0 - 2.0 ** (-5.0 - np.arange(RET_HEADS))


def _retention_consts():
    t = np.arange(CHUNK)
    gam = _retention_gammas()
    causal = t[:, None] >= t[None, :]
    diff = np.maximum(t[:, None] - t[None, :], 0)
    dmask = np.zeros((RET_PAIRS, CHUNK, 2 * CHUNK))
    cdec = np.zeros((RET_PAIRS, 2 * RET_HD, 2 * RET_HD))
    for p in range(RET_PAIRS):
        for s in range(2):
            g = gam[2 * p + s]
            dmask[p, :, s * CHUNK:(s + 1) * CHUNK] = np.where(causal, g ** diff, 0.0)
            cdec[p, s * RET_HD:(s + 1) * RET_HD, s * RET_HD:(s + 1) * RET_HD] = g ** CHUNK
    lane_gam = np.repeat(gam, RET_HD)[None, :]
    qdec = lane_gam ** (t[:, None] + 1.0)
    kdec = lane_gam ** (CHUNK - 1.0 - t[:, None])
    bd = (cdec[0] > 0).astype(np.float32)
    head_mask = np.stack([np.arange(2 * RET_HD) < RET_HD, np.arange(2 * RET_HD) >= RET_HD]).astype(np.float32)
    ones_blk = np.kron(np.eye(RET_HEADS), np.ones((RET_HD, RET_HD)))
    f = lambda a: jnp.asarray(a, F32)
    return dict(dmask=f(dmask), cdec=f(cdec), qdec=f(qdec), kdec=f(kdec), bd=f(bd),
                head_mask=jnp.asarray(head_mask, BF16), ones_blk=jnp.asarray(ones_blk, BF16))


def _retention_body(ret_ref, dmask_ref, cdec_ref, qdec_ref, kdec_ref, bd_ref, hm_ref, ones_ref, lng_ref, lnb_ref,
                    o_ref, sout_ref, s_scr, o_scr, *, tl):
    @pl.when(pl.program_id(1) == 0)
    def _():
        s_scr[...] = jnp.zeros_like(s_scr)

    bd = bd_ref[...]
    m0 = hm_ref[0:1, :]
    m1 = hm_ref[1:2, :]
    q = ret_ref[:, 0:RET_W]
    k = ret_ref[:, RET_W:2 * RET_W]
    v = ret_ref[:, 2 * RET_W:3 * RET_W]
    q_start = (q.astype(F32) * qdec_ref[...]).astype(BF16)
    k_end = (k.astype(F32) * kdec_ref[...]).astype(BF16)
    blocks = [(n, p) for n in range(tl // CHUNK) for p in range(RET_PAIRS)]

    def part(a, n, p):
        return a[n * CHUNK:(n + 1) * CHUNK, 2 * RET_HD * p:2 * RET_HD * (p + 1)]

    scores, o, kv = {}, {}, {}
    for n, p in blocks:
        kp = part(k, n, p)
        kk = jnp.concatenate([kp * m0, kp * m1], axis=0)
        scores[n, p] = (_dot_nt(part(q, n, p), kk) * dmask_ref[p]).astype(BF16)
    for n, p in blocks:
        vp = part(v, n, p)
        o[n, p] = _dot(scores[n, p], jnp.concatenate([vp * m0, vp * m1], axis=0))
    for n, p in blocks:
        kv[n, p] = _dot_tn(part(k_end, n, p), part(v, n, p)) * bd
    for n, p in blocks:
        s_prev = s_scr[p]
        o_scr[n * CHUNK:(n + 1) * CHUNK, 2 * RET_HD * p:2 * RET_HD * (p + 1)] = (
            o[n, p] + _dot(part(q_start, n, p), s_prev.astype(BF16)))
        s_scr[p] = s_prev * cdec_ref[p] + kv[n, p]

    o_all = o_scr[...]
    ones_blk = ones_ref[...]
    mu = _dot_row_halves(o_all.astype(BF16), ones_blk) * (1.0 / RET_HD)
    cen = o_all - mu
    var = _dot_row_halves((cen * cen).astype(BF16), ones_blk) * (1.0 / RET_HD)
    y = cen * lax.rsqrt(var + EPS) * lng_ref[...] + lnb_ref[...]
    gate = ret_ref[:, 3 * RET_W:4 * RET_W].astype(F32)
    o_ref[...] = (y * gate * _sigmoid(gate)).astype(BF16)
    for p in range(RET_PAIRS):
        s = s_scr[p]
        sout_ref[0, 2 * p] = s[0:RET_HD, 0:RET_HD]
        sout_ref[0, 2 * p + 1] = s[RET_HD:2 * RET_HD, RET_HD:2 * RET_HD]


def _retention(ret2d, ln_g, ln_b, batch, seq_len):
    assert seq_len % CHUNK == 0
    c = _retention_consts()
    tl = min(MIXER_TILE, seq_len)
    nl = seq_len // tl
    tile_rows = lambda a: jnp.tile(a, (tl // CHUNK, 1))
    consts = [c["dmask"], c["cdec"], tile_rows(c["qdec"]), tile_rows(c["kdec"]), c["bd"], c["head_mask"],
              c["ones_blk"], ln_g.reshape(1, RET_W), ln_b.reshape(1, RET_W)]
    state_shape = (batch, RET_HEADS, RET_HD, RET_HD)
    return pl.pallas_call(
        functools.partial(_retention_body, tl=tl),
        grid=(batch, nl),
        in_specs=[pl.BlockSpec((tl, 4 * RET_W), lambda b, l: (b * nl + l, 0))] + [_const_spec(a.shape) for a in consts],
        out_specs=[pl.BlockSpec((tl, RET_W), lambda b, l: (b * nl + l, 0)),
                   pl.BlockSpec((1,) + state_shape[1:], lambda b, l: (b, 0, 0, 0))],
        out_shape=[jax.ShapeDtypeStruct((batch * seq_len, RET_W), BF16), jax.ShapeDtypeStruct(state_shape, F32)],
        scratch_shapes=[pltpu.VMEM((RET_PAIRS, 2 * RET_HD, 2 * RET_HD), F32), pltpu.VMEM((tl, RET_W), F32)],
        compiler_params=_params(),
        name="retention",
    )(ret2d, *consts)


def _retention_lanes_body(ret_ref, tab_ref, lng_ref, lnb_ref, s_ref, o_ref, sout_ref, q_scr, k_scr, v_scr, g_scr,
                          o_scr, *, seq_len):
    h = pl.program_id(0)
    nb = V7X_LANES
    steps = range(seq_len)

    @pl.when(h == 0)
    def _():
        for t in steps:
            blk = ret_ref[t * nb:(t + 1) * nb, :].astype(F32)
            for scr, off in ((q_scr, 0), (k_scr, RET_W), (v_scr, 2 * RET_W), (g_scr, 3 * RET_W)):
                tr = _to_lanes(blk[:, off:off + RET_W])
                for hh in range(RET_HEADS):
                    scr[t, hh] = tr[hh * RET_HD:(hh + 1) * RET_HD]

    tab = tab_ref[h]
    row = lambda r: tab[r:r + 1, :]
    q = [q_scr[t, h] for t in steps]
    k = [k_scr[t, h] for t in steps]
    v = [v_scr[t, h] for t in steps]
    o = []
    for i in steps:
        acc = None
        for j in range(i + 1):
            s = jnp.sum(q[i] * k[j], axis=0, keepdims=True)
            if i > j:
                s = s * row(i - j - 1)
            acc = s * v[j] if acc is None else acc + s * v[j]
        o.append(acc)
    q_start = [q[t] * row(t) for t in steps]
    k_end = [k[t] * row(seq_len + t) for t in steps]
    chunk_decay = row(2 * seq_len)
    for d in range(RET_HD):
        s_d = s_ref[d]
        new = s_d * chunk_decay
        for t in steps:
            o[t] = o[t] + q_start[t][d:d + 1, :] * s_d
            new = new + k_end[t][d:d + 1, :] * v[t]
        sout_ref[d] = new
    for t in steps:
        mu = jnp.mean(o[t], axis=0, keepdims=True)
        cen = o[t] - mu
        var = jnp.mean(cen * cen, axis=0, keepdims=True)
        gate = g_scr[t, h]
        o_scr[t, h] = (cen * lax.rsqrt(var + EPS) * lng_ref[h] + lnb_ref[h]) * gate * _sigmoid(gate)

    @pl.when(h == RET_HEADS - 1)
    def _():
        for t in steps:
            o_ref[t * nb:(t + 1) * nb, :] = o_scr[t].reshape(RET_W, nb).T.astype(BF16)


def _retention_lanes(ret2d, ln_g, ln_b, state, layer, seq_len):
    nb = V7X_LANES
    gam = _retention_gammas()[:, None]
    t = np.arange(seq_len)[None, :]
    rows = np.concatenate([gam ** (t + 1.0), gam ** (seq_len - 1.0 - t), gam ** (seq_len + 0.0 * t[:, :1])], axis=1)
    n_rows = -(-rows.shape[1] // V7X_SUBLANES) * V7X_SUBLANES
    tab = np.zeros((RET_HEADS, n_rows, nb))
    tab[:, :rows.shape[1], :] = rows[:, :, None]
    lanes = lambda p: jnp.broadcast_to(p.reshape(RET_HEADS, RET_HD, 1), (RET_HEADS, RET_HD, nb))
    consts = [jnp.asarray(tab, F32), lanes(ln_g), lanes(ln_b)]
    tile = pltpu.VMEM((seq_len, RET_HEADS, RET_HD, nb), F32)
    return pl.pallas_call(
        functools.partial(_retention_lanes_body, seq_len=seq_len),
        grid=(RET_HEADS,),
        in_specs=[_const_spec(ret2d.shape)] + [_const_spec(a.shape) for a in consts]
                 + [pl.BlockSpec((None, None, RET_HD, RET_HD, nb), lambda h: (layer, h, 0, 0, 0))],
        out_specs=[_const_spec((seq_len * nb, RET_W)), pl.BlockSpec((None, RET_HD, RET_HD, nb), lambda h: (h, 0, 0, 0))],
        out_shape=[jax.ShapeDtypeStruct((seq_len * nb, RET_W), BF16),
                   jax.ShapeDtypeStruct((RET_HEADS, RET_HD, RET_HD, nb), F32)],
        scratch_shapes=[tile] * 5,
        compiler_params=_params(),
        name="retention_lanes",
    )(ret2d, *consts, state)


def _gla_consts():
    t = np.arange(CHUNK)
    causal = t[:, None] >= t[None, :]
    pad = GLA_KP - GLA_KW
    kmask = np.pad(np.kron(np.eye(GLA_HEADS), np.ones((1, GLA_DK))), ((0, 0), (0, pad)))
    vmask = np.kron(np.eye(GLA_HEADS), np.ones((1, GLA_DV)))
    bd = np.pad(np.kron(np.eye(GLA_HEADS), np.ones((GLA_DK, GLA_DV))), ((0, pad), (0, 0)))
    ones_blk = np.kron(np.eye(GLA_HEADS), np.ones((GLA_DV, GLA_DV)))
    b = lambda a: jnp.asarray(a, BF16)
    return dict(tril=b(causal), causal4=jnp.asarray(np.tile(causal, (1, GLA_HEADS)), F32), kmask=b(kmask),
                vmask=b(vmask), bd=jnp.asarray(bd, F32), ones_blk=b(ones_blk))


def _gla_body(gla_ref, glr_ref, gw_ref, gb_ref, tril_ref, causal_ref, kmask_ref, vmask_ref, bd_ref, ones_blk_ref,
              block_ref, ng_ref, o_ref, sout_ref, s_scr, o_scr, *, tl):
    @pl.when(pl.program_id(1) == 0)
    def _():
        s_scr[...] = jnp.zeros_like(s_scr)

    bd = bd_ref[...]
    z = _dot_row_halves(glr_ref[...].astype(BF16), gw_ref[...]) + gb_ref[...]
    lg = _log_sigmoid(z) * (1.0 / GLA_GATE_TEMP)
    q0, k0, v0, g0 = GLA_Q0, GLA_K0, GLA_V0, GLA_G0
    chunks = range(tl // CHUNK)

    def rows(a, n):
        return a[n * CHUNK:(n + 1) * CHUNK]

    cums = [_dot_exact_lhs(tril_ref[...], rows(lg, n)) for n in chunks]
    b_cum = jnp.concatenate(cums, axis=0)
    b_end = jnp.concatenate([jnp.broadcast_to(c[CHUNK - 1:CHUNK, :], (CHUNK, GLA_KP)) for c in cums], axis=0)
    q_in = (gla_ref[:, q0:q0 + GLA_KP].astype(F32) * jnp.exp(b_cum)).astype(BF16)
    kf = gla_ref[:, k0:k0 + GLA_KP].astype(F32)
    k_in = (kf * jnp.exp(-b_cum)).astype(BF16)
    k_dec = (kf * jnp.exp(b_end - b_cum)).astype(BF16)
    v = gla_ref[:, v0:g0]
    lg_hi, lg_lo = _hi_lo(lg)
    block_decay = jnp.exp(_dot_tn(lg_hi, block_ref[...]) + _dot_tn(lg_lo, block_ref[...]))

    scores, o, kv = {}, {}, {}
    for n in chunks:
        kn = rows(k_in, n)
        kk = jnp.concatenate([kn * kmask_ref[h:h + 1, :] for h in range(GLA_HEADS)], axis=0)
        scores[n] = (_dot_nt(rows(q_in, n), kk) * causal_ref[...]).astype(BF16)
    for n in chunks:
        vn = rows(v, n)
        vv = jnp.concatenate([vn * vmask_ref[h:h + 1, :] for h in range(GLA_HEADS)], axis=0)
        o[n] = _dot(scores[n], vv)
    for n in chunks:
        kv[n] = _dot_tn(rows(k_dec, n), rows(v, n)) * bd
    for n in chunks:
        s_prev = s_scr[...]
        o_scr[n * CHUNK:(n + 1) * CHUNK, :] = o[n] + _dot(rows(q_in, n), s_prev.astype(BF16))
        s_scr[...] = s_prev * jnp.broadcast_to(block_decay[:, n:n + 1], (GLA_KP, GLA_W)) + kv[n]

    o_all = o_scr[...]
    ms = _dot_row_halves((o_all * o_all).astype(BF16), ones_blk_ref[...]) * (1.0 / GLA_DV)
    gate = gla_ref[:, g0:g0 + GLA_W].astype(F32)
    o_ref[...] = (o_all * lax.rsqrt(ms + EPS) * ng_ref[...] * gate * _sigmoid(gate)).astype(BF16)
    s = s_scr[...]
    for h in range(GLA_HEADS):
        sout_ref[0, h] = s[h * GLA_DK:(h + 1) * GLA_DK, h * GLA_DV:(h + 1) * GLA_DV]


def _gla(gla2d, glr2d, gate_w, gate_b, norm_g, batch, seq_len):
    assert seq_len % CHUNK == 0
    c = _gla_consts()
    tl = min(MIXER_TILE, seq_len)
    nl = seq_len // tl
    assert tl // CHUNK <= V7X_LANES
    block = (np.arange(tl) // CHUNK)[:, None] == np.arange(V7X_LANES)[None, :]
    consts = [_pad_gate(gate_w, BF16), _pad_gate(gate_b.reshape(1, GLA_KW), F32), c["tril"], c["causal4"], c["kmask"], c["vmask"], c["bd"],
              c["ones_blk"], jnp.asarray(block, BF16), jnp.tile(norm_g, GLA_HEADS).reshape(1, GLA_W)]
    row_spec = lambda w: pl.BlockSpec((tl, w), lambda b, l: (b * nl + l, 0))
    state_shape = (batch, GLA_HEADS, GLA_DK, GLA_DV)
    return pl.pallas_call(
        functools.partial(_gla_body, tl=tl),
        grid=(batch, nl),
        in_specs=[row_spec(GLA_COLS), row_spec(GLA_RANK)] + [_const_spec(a.shape) for a in consts],
        out_specs=[row_spec(GLA_W), pl.BlockSpec((1,) + state_shape[1:], lambda b, l: (b, 0, 0, 0))],
        out_shape=[jax.ShapeDtypeStruct((batch * seq_len, GLA_W), BF16), jax.ShapeDtypeStruct(state_shape, F32)],
        scratch_shapes=[pltpu.VMEM((GLA_KP, GLA_W), F32), pltpu.VMEM((tl, GLA_W), F32)],
        compiler_params=_params(),
        name="gla",
    )(gla2d, glr2d, *consts)


def _gla_lanes_body(gla_ref, glr_ref, gw_ref, gb_ref, ng_ref, s_ref, o_ref, sout_ref, q_scr, k_scr, kd_scr, v_scr,
                    g_scr, dec_scr, o_scr, *, seq_len):
    h = pl.program_id(0)
    nb = V7X_LANES
    steps = range(seq_len)
    q0, k0, v0, g0 = GLA_Q0, GLA_K0, GLA_V0, GLA_G0
    kw = slice(0, GLA_KW)

    def split(scr, t, tr, width):
        for hh in range(GLA_HEADS):
            scr[t, hh] = tr[hh * width:(hh + 1) * width]

    @pl.when(h == 0)
    def _():
        z = _dot_row_halves(glr_ref[...].astype(BF16), gw_ref[...]) + gb_ref[...]
        lg = (_log_sigmoid(z) * (1.0 / GLA_GATE_TEMP))[:, kw]
        cums = []
        for t in steps:
            lg_t = lg[t * nb:(t + 1) * nb]
            cums.append(lg_t if t == 0 else cums[-1] + lg_t)
        b_end = cums[-1]
        dec = _to_lanes(jnp.exp(b_end))
        for hh in range(GLA_HEADS):
            dec_scr[hh] = dec[hh * GLA_DK:(hh + 1) * GLA_DK]
        for t in steps:
            blk = gla_ref[t * nb:(t + 1) * nb, :].astype(F32)
            kf = blk[:, k0:k0 + GLA_KW]
            split(q_scr, t, _to_lanes(blk[:, q0:q0 + GLA_KW] * jnp.exp(cums[t])), GLA_DK)
            split(k_scr, t, _to_lanes(kf * jnp.exp(-cums[t])), GLA_DK)
            split(kd_scr, t, _to_lanes(kf * jnp.exp(b_end - cums[t])), GLA_DK)
            split(v_scr, t, _to_lanes(blk[:, v0:g0]), GLA_DV)
            split(g_scr, t, _to_lanes(blk[:, g0:g0 + GLA_W]), GLA_DV)

    q = [q_scr[t, h] for t in steps]
    k = [k_scr[t, h] for t in steps]
    kd = [kd_scr[t, h] for t in steps]
    v = [v_scr[t, h] for t in steps]
    dec = dec_scr[h]
    o = []
    for i in steps:
        acc = None
        for j in range(i + 1):
            s = jnp.sum(q[i] * k[j], axis=0, keepdims=True)
            acc = s * v[j] if acc is None else acc + s * v[j]
        o.append(acc)
    for i in range(GLA_DK):
        s_i = s_ref[i]
        new = s_i * dec[i:i + 1, :]
        for t in steps:
            o[t] = o[t] + q[t][i:i + 1, :] * s_i
            new = new + kd[t][i:i + 1, :] * v[t]
        sout_ref[i] = new
    for t in steps:
        ms = jnp.mean(o[t] * o[t], axis=0, keepdims=True)
        gate = g_scr[t, h]
        o_scr[t, h] = o[t] * lax.rsqrt(ms + EPS) * ng_ref[...] * gate * _sigmoid(gate)

    @pl.when(h == GLA_HEADS - 1)
    def _():
        for t in steps:
            o_ref[t * nb:(t + 1) * nb, :] = o_scr[t].reshape(GLA_W, nb).T.astype(BF16)


def _gla_lanes(gla2d, glr2d, gate_w, gate_b, norm_g, state, layer, seq_len):
    nb = V7X_LANES
    consts = [_pad_gate(gate_w, BF16), _pad_gate(gate_b.reshape(1, GLA_KW), F32),
              jnp.broadcast_to(norm_g.reshape(GLA_DV, 1), (GLA_DV, nb))]
    k_tile = pltpu.VMEM((seq_len, GLA_HEADS, GLA_DK, nb), F32)
    v_tile = pltpu.VMEM((seq_len, GLA_HEADS, GLA_DV, nb), F32)
    return pl.pallas_call(
        functools.partial(_gla_lanes_body, seq_len=seq_len),
        grid=(GLA_HEADS,),
        in_specs=[_const_spec(gla2d.shape), _const_spec(glr2d.shape)] + [_const_spec(a.shape) for a in consts]
                 + [pl.BlockSpec((None, None, GLA_DK, GLA_DV, nb), lambda h: (layer, h, 0, 0, 0))],
        out_specs=[_const_spec((seq_len * nb, GLA_W)), pl.BlockSpec((None, GLA_DK, GLA_DV, nb), lambda h: (h, 0, 0, 0))],
        out_shape=[jax.ShapeDtypeStruct((seq_len * nb, GLA_W), BF16),
                   jax.ShapeDtypeStruct((GLA_HEADS, GLA_DK, GLA_DV, nb), F32)],
        scratch_shapes=[k_tile, k_tile, k_tile, v_tile, v_tile, pltpu.VMEM((GLA_HEADS, GLA_DK, nb), F32), v_tile],
        compiler_params=_params(),
        name="gla_lanes",
    )(gla2d, glr2d, *consts, state)


def _s5_discretize(lam_re, lam_im, log_dt, b_re, b_im, c_re, c_im):
    lr = jnp.minimum(lam_re.astype(F32), -S5_MIN_NEG)
    li = lam_im.astype(F32)
    dt = jnp.exp(log_dt.astype(F32))[:, None]
    mag = jnp.exp(lr * dt)
    ar = mag * jnp.cos(li * dt)
    ai = mag * jnp.sin(li * dt)
    den = lr * lr + li * li
    cr = ((ar - 1.0) * lr + ai * li) / den
    ci = (ai * lr - (ar - 1.0) * li) / den
    b_re, b_im = b_re.astype(F32), b_im.astype(F32)
    bbar_re = cr[..., None] * b_re - ci[..., None] * b_im
    bbar_im = cr[..., None] * b_im + ci[..., None] * b_re
    eye = jnp.eye(S5_GROUPS, dtype=F32)
    in_blk = lambda b: jnp.einsum("gpi,gh->gihp", b, eye).reshape(S5_W, S5_LANES)
    out_blk = lambda c: jnp.einsum("gop,gh->gpho", c.astype(F32), eye).reshape(S5_LANES, S5_W)
    bdb = jnp.concatenate([in_blk(bbar_re), in_blk(bbar_im)], axis=1).astype(BF16)
    bdc = jnp.concatenate([out_blk(c_re), -out_blk(c_im)], axis=0).astype(BF16)
    return ar.reshape(1, S5_LANES), ai.reshape(1, S5_LANES), bdb, bdc


def _s5_body(*refs, tl, batch, has_state):
    u_ref, ar_ref, ai_ref, bdb_ref, bdc_ref, d_ref, gw_ref, gb_ref = refs[:8]
    n_in = 10 if has_state else 8
    o_ref, hr_ref, hi_ref, x_scr, h_scr = refs[n_in:]

    @pl.when(pl.program_id(0) == 0)
    def _():
        if has_state:
            h_scr[0] = refs[8][...]
            h_scr[1] = refs[9][...]
        else:
            h_scr[...] = jnp.zeros_like(h_scr)

    ar = jnp.broadcast_to(ar_ref[...], (batch, S5_LANES))
    ai = jnp.broadcast_to(ai_ref[...], (batch, S5_LANES))
    hr, hi = h_scr[0], h_scr[1]
    sub = min(tl, S5_SUB_STEPS)
    n_rows = sub * batch
    for s in range(tl // sub):
        rows = slice(s * n_rows, (s + 1) * n_rows)
        u = u_ref[rows, :]
        x_scr[rows, :] = _dot_row_halves(u, bdb_ref[...])
        for t in range(s * sub, (s + 1) * sub):
            r = slice(t * batch, (t + 1) * batch)
            hr, hi = (ar * hr - ai * hi + x_scr[r, 0:S5_LANES],
                      ar * hi + ai * hr + x_scr[r, S5_LANES:2 * S5_LANES])
            x_scr[r, 0:S5_LANES] = hr
            x_scr[r, S5_LANES:2 * S5_LANES] = hi
        y = _dot_row_halves(x_scr[rows, :].astype(BF16), bdc_ref[...]) + d_ref[...] * u.astype(F32)
        y = _gelu_tanh(y)
        o_ref[rows, :] = (y * _sigmoid(_dot_row_halves(y.astype(BF16), gw_ref[...]) + gb_ref[...])).astype(BF16)
    h_scr[0] = hr
    h_scr[1] = hi
    hr_ref[...] = hr
    hi_ref[...] = hi


def _s5(u_tm, disc, d, glu_w_bf, glu_b, state, batch, seq_len):
    ar, ai, bdb, bdc = disc
    tl = min(S5_TIME_TILE, seq_len)
    has_state = state is not None
    consts = [ar, ai, bdb, bdc, d.reshape(1, S5_W), glu_w_bf, glu_b.reshape(1, S5_W)]
    h_spec = pl.BlockSpec((batch, S5_LANES), lambda i: (0, 0))
    in_specs = [pl.BlockSpec((tl * batch, S5_W), lambda i: (i, 0))] + [_const_spec(a.shape) for a in consts]
    args = [u_tm] + consts
    if has_state:
        in_specs += [h_spec, h_spec]
        args += list(state)
    return pl.pallas_call(
        functools.partial(_s5_body, tl=tl, batch=batch, has_state=has_state),
        grid=(seq_len // tl,),
        in_specs=in_specs,
        out_specs=[pl.BlockSpec((tl * batch, S5_W), lambda i: (i, 0)), h_spec, h_spec],
        out_shape=[jax.ShapeDtypeStruct((seq_len * batch, S5_W), BF16),
                   jax.ShapeDtypeStruct((batch, S5_LANES), F32), jax.ShapeDtypeStruct((batch, S5_LANES), F32)],
        scratch_shapes=[pltpu.VMEM((tl * batch, 2 * S5_LANES), F32), pltpu.VMEM((2, batch, S5_LANES), F32)],
        compiler_params=_params(),
        name="s5",
    )(*args)


def _ffn_body(*refs, tm, time_major_batch, final):
    (x_ref, oret_ref, os5_ref, ogla_ref, wout_ref, gffn_ref, win_ref, cw_ref, cb_ref, wo_ref) = refs[:10]
    n = 10
    gfin_ref = None
    if final:
        gfin_ref, n = refs[n], n + 1
    if time_major_batch is not None:
        past_ref, n = refs[n], n + 1
    out_ref, conv_ref, x1_scr, h_scr, carry_scr, act_scr = refs[n:]
    pad = V7X_SUBLANES
    nb = time_major_batch

    if nb is None:
        @pl.when(pl.program_id(1) == 0)
        def _():
            carry_scr[...] = jnp.zeros_like(carry_scr)

    mix = jnp.concatenate([oret_ref[...], os5_ref[...], ogla_ref[...]], axis=1)
    x1_scr[...] = x_ref[...] + _dot(mix, wout_ref[...])
    h_scr[...] = _rmsnorm_rows(x1_scr[...], gffn_ref[...]).astype(BF16)
    h = h_scr[...]

    for c in range(D_FF // ACT_COLS):
        cols = slice(c * ACT_COLS, (c + 1) * ACT_COLS)
        a_c = _dot(h, win_ref[:, c * ACT_COLS:(c + 1) * ACT_COLS])
        gate_c = _dot(h, win_ref[:, D_FF + c * ACT_COLS:D_FF + (c + 1) * ACT_COLS])
        if nb is None:
            ext = jnp.concatenate([carry_scr[:, cols], a_c], axis=0)
            prev1 = ext[pad - 1:pad - 1 + tm]
            prev2 = ext[pad - 2:pad - 2 + tm]
            carry_scr[:, cols] = a_c[tm - pad:tm]
        else:
            prev1 = jnp.concatenate([past_ref[1, :, cols], a_c[0:tm - nb]], axis=0)
            prev2 = jnp.concatenate([past_ref[0, :, cols], past_ref[1, :, cols], a_c[0:tm - 2 * nb]], axis=0)
            conv_ref[:, cols] = a_c[tm - (CONV_W - 1) * nb:tm]
        conv = (cb_ref[:, cols] + prev2 * cw_ref[0:1, cols] + prev1 * cw_ref[1:2, cols]
                + a_c * cw_ref[2:3, cols])
        act_scr[:, cols] = (_gelu_tanh(conv) * gate_c).astype(BF16)

    x2 = x1_scr[...] + _dot(act_scr[...], wo_ref[...])
    out_ref[...] = _rmsnorm_rows(x2, gfin_ref[...]) if final else x2
    if nb is None:
        conv_ref[0] = carry_scr[pad - (CONV_W - 1):pad, :]


def _ffn(x2d, o_ret, o_s5, o_gla, w_out_bf, norm_g, w_in_bf, conv_w, conv_b, w_o_bf, layer, final_g, conv_past,
         batch, seq_len):
    rows = batch * seq_len
    time_major = conv_past is not None
    tm = rows if time_major else min(ROW_TILE, seq_len)
    final = final_g is not None
    cw = jnp.concatenate([conv_w.astype(F32), jnp.zeros((V7X_SUBLANES - CONV_W, D_FF), F32)], axis=0)
    consts = [w_out_bf, norm_g.reshape(1, D_MODEL), w_in_bf, cw, conv_b.reshape(1, D_FF), w_o_bf]
    stacked = [True, False, True, False, False, True]
    if final:
        consts.append(final_g.reshape(1, D_MODEL))
        stacked.append(False)
    if time_major:
        assert seq_len >= CONV_W - 1 and batch % V7X_SUBLANES == 0
        consts.append(conv_past)
        stacked.append(False)
        grid = (1,)
        imap = lambda i: (0, 0)
        conv_rows = (CONV_W - 1) * batch
        conv_spec = pl.BlockSpec((conv_rows, D_FF), imap)
        conv_shape = jax.ShapeDtypeStruct((conv_rows, D_FF), F32)
    else:
        assert seq_len % tm == 0
        nl = seq_len // tm
        grid = (batch, nl)
        imap = lambda b, l: (b * nl + l, 0)
        conv_spec = pl.BlockSpec((1, CONV_W - 1, D_FF), lambda b, l: (b, 0, 0))
        conv_shape = jax.ShapeDtypeStruct((batch, CONV_W - 1, D_FF), F32)
    row_spec = lambda w: pl.BlockSpec((tm, w), imap)
    in_specs = ([row_spec(D_MODEL), row_spec(RET_W), row_spec(S5_W), row_spec(GLA_W)]
                + [_layer_spec(a.shape, layer, True) if s else _const_spec(a.shape) for a, s in zip(consts, stacked)])
    return pl.pallas_call(
        functools.partial(_ffn_body, tm=tm, time_major_batch=batch if time_major else None, final=final),
        grid=grid,
        in_specs=in_specs,
        out_specs=[row_spec(D_MODEL), conv_spec],
        out_shape=[jax.ShapeDtypeStruct((rows, D_MODEL), F32), conv_shape],
        scratch_shapes=[pltpu.VMEM((tm, D_MODEL), F32), pltpu.VMEM((tm, D_MODEL), BF16),
                        pltpu.VMEM((V7X_SUBLANES, D_FF), F32), pltpu.VMEM((tm, D_FF), BF16)],
        compiler_params=_params(),
        name="ffn",
    )(x2d, o_ret, o_s5, o_gla, *consts)


def _rotary_tables(first_pos, n_pos, repeat=1):
    half = RET_HD // 2
    inv = ROPE_BASE ** (-(np.arange(half, dtype=np.float64) / half))
    ang = (first_pos + np.arange(n_pos)).astype(np.float64)[:, None] * inv[None, :]
    cos, sin = np.cos(ang), np.sin(ang)
    expand = lambda a, b: np.repeat(np.tile(np.concatenate([a, b], axis=1), (1, RET_HEADS)), repeat, axis=0)
    return jnp.asarray(expand(cos, cos), F32), jnp.asarray(expand(-sin, sin), F32)


def _swap_major(a2d, outer, inner):
    return a2d.reshape(outer, inner, -1).transpose(1, 0, 2).reshape(outer * inner, -1)


def _run_prompt_group(x, layers, big, final_g):
    batch, seq_len, _ = x.shape
    rows = batch * seq_len
    tm = min(MIX_IN_TILE, seq_len)
    cos_tab, sin_tab = _rotary_tables(0, seq_len)
    x2d = x.reshape(rows, D_MODEL)
    outs = []
    for li, lp in enumerate(layers):
        ret, su, gla, glr = _mix_in(x2d, lp["norm_mix_g"], big["w_in"], li, cos_tab, sin_tab, tm)
        o_ret, s_ret = _retention(ret, lp["ret_norm_g"], lp["ret_norm_b"], batch, seq_len)
        o_s5, s5r, s5i = _s5(_swap_major(su, batch, seq_len), lp["s5_disc"], lp["s5_d"], lp["s5_glu_w"],
                             lp["s5_glu_b"], None, batch, seq_len)
        o_s5 = _swap_major(o_s5, seq_len, batch)
        o_gla, s_gla = _gla(gla, glr, lp["gla_gate_w"], lp["gla_gate_b"], lp["gla_norm_g"], batch, seq_len)
        last = li == len(layers) - 1
        x2d, conv_new = _ffn(x2d, o_ret, o_s5, o_gla, big["w_out"], lp["norm_ffn_g"], big["ffn_w_in"],
                             lp["ffn_conv_w"], lp["ffn_conv_b"], big["ffn_w_out"], li, final_g if last else None,
                             None, batch, seq_len)
        s5_shape = (batch, S5_GROUPS, S5_STATE)
        outs.append((s_ret, s5r.reshape(s5_shape), s5i.reshape(s5_shape), s_gla, conv_new))
    return [x2d.reshape(batch, seq_len, D_MODEL)] + [jnp.stack([o[i] for o in outs]) for i in range(5)]


def _run_sample_group(x, past_len, states, layers, big, final_g):
    batch, seq_len, _ = x.shape
    assert batch == V7X_LANES
    rows = batch * seq_len
    cos_tab, sin_tab = _rotary_tables(past_len, seq_len, repeat=batch)
    x2d = x.transpose(1, 0, 2).reshape(rows, D_MODEL)
    ret_state = states["ret"].transpose(0, 2, 3, 4, 1)
    gla_state = states["gla"].transpose(0, 2, 3, 4, 1)
    conv_state = states["conv"].transpose(0, 2, 1, 3)
    depth = len(layers)
    s5_flat = lambda s: s.reshape(depth, batch, S5_LANES)
    s5r_state, s5i_state = s5_flat(states["s5r"]), s5_flat(states["s5i"])
    outs = []
    for li, lp in enumerate(layers):
        ret, su, gla, glr = _mix_in(x2d, lp["norm_mix_g"], big["w_in"], li, cos_tab, sin_tab, rows)
        o_ret, s_ret = _retention_lanes(ret, lp["ret_norm_g"], lp["ret_norm_b"], ret_state, li, seq_len)
        o_s5, s5r, s5i = _s5(su, lp["s5_disc"], lp["s5_d"], lp["s5_glu_w"], lp["s5_glu_b"],
                             (s5r_state[li], s5i_state[li]), batch, seq_len)
        o_gla, s_gla = _gla_lanes(gla, glr, lp["gla_gate_w"], lp["gla_gate_b"], lp["gla_norm_g"], gla_state, li,
                                  seq_len)
        last = li == len(layers) - 1
        x2d, conv_new = _ffn(x2d, o_ret, o_s5, o_gla, big["w_out"], lp["norm_ffn_g"], big["ffn_w_in"],
                             lp["ffn_conv_w"], lp["ffn_conv_b"], big["ffn_w_out"], li, final_g if last else None,
                             conv_state[li], batch, seq_len)
        outs.append((s_ret, s5r, s5i, s_gla, conv_new.reshape(CONV_W - 1, batch, D_FF)))
    s_ret, s5r, s5i, s_gla, conv_new = (jnp.stack([o[i] for o in outs]) for i in range(5))
    s5_shape = (depth, batch, S5_GROUPS, S5_STATE)
    return [x2d.reshape(seq_len, batch, D_MODEL).transpose(1, 0, 2), s_ret.transpose(0, 4, 1, 2, 3),
            s5r.reshape(s5_shape), s5i.reshape(s5_shape), s_gla.transpose(0, 4, 1, 2, 3),
            conv_new.transpose(0, 2, 1, 3)]


def kernel(x_prompt, x_sample, state_ret, state_s5_re, state_s5_im, state_gla, state_ffn_conv, norm_mix_g, w_in, ret_norm_g, ret_norm_b, s5_lambda_re, s5_lambda_im, s5_log_dt, s5_b_re, s5_b_im, s5_c_re, s5_c_im, s5_d, s5_glu_w, s5_glu_b, gla_gate_w, gla_gate_b, gla_norm_g, w_out, norm_ffn_g, ffn_w_in, ffn_conv_w, ffn_conv_b, ffn_w_out, norm_final_g):
    depth = w_in.shape[0]
    big = dict(w_in=_mix_weights(w_in), w_out=w_out.astype(BF16), ffn_w_in=ffn_w_in.astype(BF16),
               ffn_w_out=ffn_w_out.astype(BF16))
    layers = []
    for l in range(depth):
        layers.append(dict(
            norm_mix_g=norm_mix_g[l], ret_norm_g=ret_norm_g[l], ret_norm_b=ret_norm_b[l],
            s5_disc=_s5_discretize(s5_lambda_re[l], s5_lambda_im[l], s5_log_dt[l], s5_b_re[l], s5_b_im[l],
                                   s5_c_re[l], s5_c_im[l]),
            s5_d=s5_d[l], s5_glu_w=s5_glu_w[l].astype(BF16), s5_glu_b=s5_glu_b[l],
            gla_gate_w=gla_gate_w[l], gla_gate_b=gla_gate_b[l], gla_norm_g=gla_norm_g[l],
            norm_ffn_g=norm_ffn_g[l], ffn_conv_w=ffn_conv_w[l], ffn_conv_b=ffn_conv_b[l]))
    sample_states = dict(ret=state_ret, s5r=state_s5_re, s5i=state_s5_im, gla=state_gla, conv=state_ffn_conv)
    yp, ret_p, s5r_p, s5i_p, gla_p, conv_p = _run_prompt_group(x_prompt, layers, big, norm_final_g)
    ys, ret_s, s5r_s, s5i_s, gla_s, conv_s = _run_sample_group(x_sample, PAST_LEN, sample_states, layers, big,
                                                               norm_final_g)
    return (yp, ys, ret_p, ret_s, s5r_p, s5r_s, s5i_p, s5i_s, gla_p, gla_s, conv_p, conv_s)
```

```python
import functools

import numpy as np
import jax
import jax.numpy as jnp
from jax import lax
from jax.experimental import pallas as pl
from jax.experimental.pallas import tpu as pltpu

F32, BF16 = jnp.float32, jnp.bfloat16

D_MODEL = 1024
RET_W, S5_W, GLA_W = 384, 256, 384
RET_HEADS, RET_HD = 6, 64
RET_PAIRS = RET_HEADS // 2
S5_GROUPS, S5_CH, S5_STATE = 16, 16, 64
S5_LANES = S5_GROUPS * S5_STATE
S5_MIN_NEG = 1e-4
GLA_HEADS, GLA_DK, GLA_DV, GLA_RANK = 4, 48, 96, 16
GLA_KW = GLA_HEADS * GLA_DK
GLA_GATE_TEMP = 16.0
D_FF = 2816
CONV_W = 3
ROPE_BASE = 10000.0
CHUNK = 64
EPS = 1e-6
PAST_LEN = 16384
IN_COLS = 4 * RET_W + S5_W + 2 * GLA_KW + 2 * GLA_W + GLA_RANK
COL_RET, COL_S5, COL_GLA, COL_LR = 0, 4 * RET_W, 4 * RET_W + S5_W, IN_COLS - GLA_RANK
GLA_KP = 256
GLA_Q0, GLA_LR0, GLA_K0, GLA_V0, GLA_G0 = 0, GLA_KW, GLA_KP, 2 * GLA_KP, 2 * GLA_KP + GLA_W
GLA_COLS = 2 * GLA_KP + 2 * GLA_W
MIX_COLS = COL_GLA + GLA_COLS

V7X_SUBLANES = 8
V7X_LANES = 128
VMEM_LIMIT = 58 * 1024 * 1024

ROW_TILE = 512
MIX_IN_TILE = 1024
MIXER_TILE = 2048
S5_TIME_TILE = 256
S5_SUB_STEPS = 64
ACT_COLS = 256


def _dot(a, b):
    return jnp.dot(a, b, preferred_element_type=F32)


def _dot_nt(a, b):
    return lax.dot_general(a, b, (((1,), (1,)), ((), ())), preferred_element_type=F32)


def _dot_tn(a, b):
    return lax.dot_general(a, b, (((0,), (0,)), ((), ())), preferred_element_type=F32)


def _hi_lo(x):
    hi = x.astype(BF16)
    return hi, (x - hi.astype(F32)).astype(BF16)


def _dot_exact_lhs(m, x):
    hi, lo = _hi_lo(x)
    return _dot(m, hi) + _dot(m, lo)


def _dot_row_halves(a, w):
    half = a.shape[0] // 2
    return jnp.concatenate([_dot(a[:half], w), _dot(a[half:], w)], axis=0)


def _pad_gate(a, dtype):
    return jnp.pad(a.astype(dtype), [(0, 0)] * (a.ndim - 1) + [(0, GLA_KP - GLA_KW)])


def _sigmoid(x):
    return 1.0 / (1.0 + jnp.exp(-x))


def _log_sigmoid(z):
    return jnp.minimum(z, 0.0) - jnp.log(1.0 + jnp.exp(-jnp.abs(z)))


def _gelu_tanh(x):
    return 0.5 * x * (1.0 + jnp.tanh(0.7978845608028654 * (x + 0.044715 * (x * x * x))))


def _rmsnorm_rows(x, g):
    return x * lax.rsqrt(jnp.mean(x * x, axis=-1, keepdims=True) + EPS) * g


def _const_spec(shape):
    nd = len(shape)
    return pl.BlockSpec(shape, lambda *_: (0,) * nd)


def _layer_spec(stacked_shape, layer, single_buffer=False):
    nd = len(stacked_shape) - 1
    mode = dict(pipeline_mode=pl.Buffered(1)) if single_buffer else {}
    return pl.BlockSpec((None,) + tuple(stacked_shape[1:]), lambda *_: (layer,) + (0,) * nd, **mode)


def _specs(arrays, layer):
    return [_layer_spec(a.shape, layer) if stacked else _const_spec(a.shape) for a, stacked in arrays]


def _params():
    return pltpu.CompilerParams(vmem_limit_bytes=VMEM_LIMIT)


def _to_lanes(x):
    w = x.shape[1]
    pad = -w % V7X_LANES
    if pad:
        x = jnp.concatenate([x, jnp.zeros((x.shape[0], pad), F32)], axis=1)
    return x.T[0:w]


def _mix_in_body(x_ref, g_ref, w_ref, cos_ref, sin_ref, ret_ref, su_ref, gla_ref, glr_ref):
    h = _rmsnorm_rows(x_ref[...], g_ref[...]).astype(BF16)
    ret = _dot(h, w_ref[:, COL_RET:COL_S5])
    cos = cos_ref[...]
    sin = sin_ref[...]
    lane = lax.broadcasted_iota(jnp.int32, cos.shape, 1)
    first_half = (lane & (RET_HD // 2)) == 0

    def rotary(z):
        swapped = jnp.where(first_half, pltpu.roll(z, RET_W - RET_HD // 2, 1), pltpu.roll(z, RET_HD // 2, 1))
        return z * cos + swapped * sin

    ret_ref[:, 0:RET_W] = rotary(ret[:, 0:RET_W]).astype(BF16)
    ret_ref[:, RET_W:2 * RET_W] = (rotary(ret[:, RET_W:2 * RET_W]) * RET_HD ** -0.5).astype(BF16)
    ret_ref[:, 2 * RET_W:] = ret[:, 2 * RET_W:].astype(BF16)
    su_ref[...] = _dot(h, w_ref[:, COL_S5:COL_GLA]).astype(BF16)
    gla = _dot(h, w_ref[:, COL_GLA:MIX_COLS])
    glane = lax.broadcasted_iota(jnp.int32, (1, GLA_COLS), 1)
    scale = jnp.where(glane < GLA_KW, GLA_DK ** -0.5, jnp.where(glane < GLA_KP, 0.0, 1.0))
    gla_ref[...] = (gla * scale).astype(BF16)
    glr_ref[...] = gla[:, GLA_LR0:GLA_LR0 + GLA_RANK]


def _mix_weights(w_in):
    w = w_in.astype(BF16)
    k0, v0 = COL_GLA + GLA_KW, COL_GLA + 2 * GLA_KW
    zeros = lambda n: jnp.zeros(w.shape[:2] + (n,), BF16)
    return jnp.concatenate([w[..., :k0], w[..., COL_LR:], zeros(GLA_KP - GLA_KW - GLA_RANK),
                            w[..., k0:v0], zeros(GLA_KP - GLA_KW), w[..., v0:COL_LR]], axis=-1)


def _mix_in(x2d, norm_g, w_in_bf, layer, cos_tab, sin_tab, tm):
    rows = x2d.shape[0]
    n_tab = cos_tab.shape[0] // tm
    row_spec = lambda w: pl.BlockSpec((tm, w), lambda i: (i, 0))
    tab_spec = pl.BlockSpec((tm, RET_W), lambda i: (i % n_tab, 0))
    return pl.pallas_call(
        _mix_in_body,
        grid=(rows // tm,),
        in_specs=[row_spec(D_MODEL), _layer_spec(norm_g.shape, layer), _layer_spec(w_in_bf.shape, layer, True),
                  tab_spec, tab_spec],
        out_specs=[row_spec(4 * RET_W), row_spec(S5_W), row_spec(GLA_COLS), row_spec(GLA_RANK)],
        out_shape=[jax.ShapeDtypeStruct((rows, 4 * RET_W), BF16), jax.ShapeDtypeStruct((rows, S5_W), BF16),
                   jax.ShapeDtypeStruct((rows, GLA_COLS), BF16), jax.ShapeDtypeStruct((rows, GLA_RANK), F32)],
        compiler_params=_params(),
        name="mix_in",
    )(x2d, norm_g, w_in_bf, cos_tab, sin_tab)


def _retention_gammas():
    return 1.0 - 2.0 ** (-5.0 - np.arange(RET_HEADS))


def _retention_consts():
    t = np.arange(CHUNK)
    gam = _retention_gammas()
    causal = t[:, None] >= t[None, :]
    diff = np.maximum(t[:, None] - t[None, :], 0)
    dmask = np.zeros((RET_PAIRS, CHUNK, 2 * CHUNK))
    cdec = np.zeros((RET_PAIRS, 2 * RET_HD, 2 * RET_HD))
    for p in range(RET_PAIRS):
        for s in range(2):
            g = gam[2 * p + s]
            dmask[p, :, s * CHUNK:(s + 1) * CHUNK] = np.where(causal, g ** diff, 0.0)
            cdec[p, s * RET_HD:(s + 1) * RET_HD, s * RET_HD:(s + 1) * RET_HD] = g ** CHUNK
    lane_gam = np.repeat(gam, RET_HD)[None, :]
    qdec = lane_gam ** (t[:, None] + 1.0)
    kdec = lane_gam ** (CHUNK - 1.0 - t[:, None])
    bd = (cdec[0] > 0).astype(np.float32)
    head_mask = np.stack([np.arange(2 * RET_HD) < RET_HD, np.arange(2 * RET_HD) >= RET_HD]).astype(np.float32)
    ones_blk = np.kron(np.eye(RET_HEADS), np.ones((RET_HD, RET_HD)))
    f = lambda a: jnp.asarray(a, F32)
    return dict(dmask=f(dmask), cdec=f(cdec), qdec=qdec, kdec=kdec, bd=f(bd),
                head_mask=jnp.asarray(head_mask, BF16), ones_blk=jnp.asarray(ones_blk, BF16))


def _retention_body(ret_ref, dmask_ref, cdec_ref, qdec_ref, kdec_ref, bd_ref, hm_ref, ones_ref, lng_ref, lnb_ref,
                    o_ref, sout_ref, s_scr, o_scr, *, tl):
    @pl.when(pl.program_id(1) == 0)
    def _():
        s_scr[...] = jnp.zeros_like(s_scr)

    bd = bd_ref[...]
    m0 = hm_ref[0:1, :]
    m1 = hm_ref[1:2, :]
    q = ret_ref[:, 0:RET_W]
    k = ret_ref[:, RET_W:2 * RET_W]
    v = ret_ref[:, 2 * RET_W:3 * RET_W]
    q_start = (q.astype(F32) * qdec_ref[...]).astype(BF16)
    k_end = (k.astype(F32) * kdec_ref[...]).astype(BF16)
    blocks = [(n, p) for n in range(tl // CHUNK) for p in range(RET_PAIRS)]

    def part(a, n, p):
        return a[n * CHUNK:(n + 1) * CHUNK, 2 * RET_HD * p:2 * RET_HD * (p + 1)]

    scores, o, kv = {}, {}, {}
    for n, p in blocks:
        kp = part(k, n, p)
        kk = jnp.concatenate([kp * m0, kp * m1], axis=0)
        scores[n, p] = (_dot_nt(part(q, n, p), kk) * dmask_ref[p]).astype(BF16)
    for n, p in blocks:
        vp = part(v, n, p)
        o[n, p] = _dot(scores[n, p], jnp.concatenate([vp * m0, vp * m1], axis=0))
    for n, p in blocks:
        kv[n, p] = _dot_tn(part(k_end, n, p), part(v, n, p)) * bd
    for n, p in blocks:
        s_prev = s_scr[p]
        o_scr[n * CHUNK:(n + 1) * CHUNK, 2 * RET_HD * p:2 * RET_HD * (p + 1)] = (
            o[n, p] + _dot(part(q_start, n, p), s_prev.astype(BF16)))
        s_scr[p] = s_prev * cdec_ref[p] + kv[n, p]

    o_all = o_scr[...]
    ones_blk = ones_ref[...]
    mu = _dot_row_halves(o_all.astype(BF16), ones_blk) * (1.0 / RET_HD)
    cen = o_all - mu
    var = _dot_row_halves((cen * cen).astype(BF16), ones_blk) * (1.0 / RET_HD)
    y = cen * lax.rsqrt(var + EPS) * lng_ref[...] + lnb_ref[...]
    gate = ret_ref[:, 3 * RET_W:4 * RET_W].astype(F32)
    o_ref[...] = (y * gate * _sigmoid(gate)).astype(BF16)
    for p in range(RET_PAIRS):
        s = s_scr[p]
        sout_ref[0, 2 * p] = s[0:RET_HD, 0:RET_HD]
        sout_ref[0, 2 * p + 1] = s[RET_HD:2 * RET_HD, RET_HD:2 * RET_HD]


def _retention(ret2d, ln_g, ln_b, layer, batch, seq_len):
    assert seq_len % CHUNK == 0
    c = _retention_consts()
    tl = min(MIXER_TILE, seq_len)
    nl = seq_len // tl
    tile_rows = lambda a: jnp.asarray(np.tile(a, (tl // CHUNK, 1)), F32)
    consts = [(c["dmask"], False), (c["cdec"], False), (tile_rows(c["qdec"]), False), (tile_rows(c["kdec"]), False),
              (c["bd"], False), (c["head_mask"], False), (c["ones_blk"], False), (ln_g, True), (ln_b, True)]
    state_shape = (batch, RET_HEADS, RET_HD, RET_HD)
    return pl.pallas_call(
        functools.partial(_retention_body, tl=tl),
        grid=(batch, nl),
        in_specs=[pl.BlockSpec((tl, 4 * RET_W), lambda b, l: (b * nl + l, 0))] + _specs(consts, layer),
        out_specs=[pl.BlockSpec((tl, RET_W), lambda b, l: (b * nl + l, 0)),
                   pl.BlockSpec((1,) + state_shape[1:], lambda b, l: (b, 0, 0, 0))],
        out_shape=[jax.ShapeDtypeStruct((batch * seq_len, RET_W), BF16), jax.ShapeDtypeStruct(state_shape, F32)],
        scratch_shapes=[pltpu.VMEM((RET_PAIRS, 2 * RET_HD, 2 * RET_HD), F32), pltpu.VMEM((tl, RET_W), F32)],
        compiler_params=_params(),
        name="retention",
    )(ret2d, *[a for a, _ in consts])


def _retention_lanes_body(ret_ref, tab_ref, lng_ref, lnb_ref, s_ref, o_ref, sout_ref, q_scr, k_scr, v_scr, g_scr,
                          o_scr, *, seq_len):
    h = pl.program_id(0)
    nb = V7X_LANES
    steps = range(seq_len)

    @pl.when(h == 0)
    def _():
        for t in steps:
            blk = ret_ref[t * nb:(t + 1) * nb, :].astype(F32)
            for scr, off in ((q_scr, 0), (k_scr, RET_W), (v_scr, 2 * RET_W), (g_scr, 3 * RET_W)):
                tr = _to_lanes(blk[:, off:off + RET_W])
                for hh in range(RET_HEADS):
                    scr[t, hh] = tr[hh * RET_HD:(hh + 1) * RET_HD]

    tab = tab_ref[h]
    row = lambda r: tab[r:r + 1, :]
    q = [q_scr[t, h] for t in steps]
    k = [k_scr[t, h] for t in steps]
    v = [v_scr[t, h] for t in steps]
    o = []
    for i in steps:
        acc = None
        for j in range(i + 1):
            s = jnp.sum(q[i] * k[j], axis=0, keepdims=True)
            if i > j:
                s = s * row(i - j - 1)
            acc = s * v[j] if acc is None else acc + s * v[j]
        o.append(acc)
    q_start = [q[t] * row(t) for t in steps]
    k_end = [k[t] * row(seq_len + t) for t in steps]
    chunk_decay = row(2 * seq_len)
    for d in range(RET_HD):
        s_d = s_ref[d]
        new = s_d * chunk_decay
        for t in steps:
            o[t] = o[t] + q_start[t][d:d + 1, :] * s_d
            new = new + k_end[t][d:d + 1, :] * v[t]
        sout_ref[d] = new
    for t in steps:
        mu = jnp.mean(o[t], axis=0, keepdims=True)
        cen = o[t] - mu
        var = jnp.mean(cen * cen, axis=0, keepdims=True)
        gate = g_scr[t, h]
        o_scr[t, h] = (cen * lax.rsqrt(var + EPS) * lng_ref[h] + lnb_ref[h]) * gate * _sigmoid(gate)

    @pl.when(h == RET_HEADS - 1)
    def _():
        for t in steps:
            o_ref[t * nb:(t + 1) * nb, :] = o_scr[t].reshape(RET_W, nb).T.astype(BF16)


def _retention_lanes(ret2d, ln_g, ln_b, state, layer, seq_len):
    nb = V7X_LANES
    gam = _retention_gammas()[:, None]
    t = np.arange(seq_len)[None, :]
    rows = np.concatenate([gam ** (t + 1.0), gam ** (seq_len - 1.0 - t), gam ** (seq_len + 0.0 * t[:, :1])], axis=1)
    n_rows = -(-rows.shape[1] // V7X_SUBLANES) * V7X_SUBLANES
    tab = np.zeros((RET_HEADS, n_rows, nb))
    tab[:, :rows.shape[1], :] = rows[:, :, None]
    consts = [(jnp.asarray(tab, F32), False), (ln_g, True), (ln_b, True)]
    tile = pltpu.VMEM((seq_len, RET_HEADS, RET_HD, nb), F32)
    return pl.pallas_call(
        functools.partial(_retention_lanes_body, seq_len=seq_len),
        grid=(RET_HEADS,),
        in_specs=[_const_spec(ret2d.shape)] + _specs(consts, layer)
                 + [pl.BlockSpec((None, None, RET_HD, RET_HD, nb), lambda h: (layer, h, 0, 0, 0))],
        out_specs=[_const_spec((seq_len * nb, RET_W)), pl.BlockSpec((None, RET_HD, RET_HD, nb), lambda h: (h, 0, 0, 0))],
        out_shape=[jax.ShapeDtypeStruct((seq_len * nb, RET_W), BF16),
                   jax.ShapeDtypeStruct((RET_HEADS, RET_HD, RET_HD, nb), F32)],
        scratch_shapes=[tile] * 5,
        compiler_params=_params(),
        name="retention_lanes",
    )(ret2d, *[a for a, _ in consts], state)


def _gla_consts():
    t = np.arange(CHUNK)
    causal = t[:, None] >= t[None, :]
    pad = GLA_KP - GLA_KW
    kmask = np.pad(np.kron(np.eye(GLA_HEADS), np.ones((1, GLA_DK))), ((0, 0), (0, pad)))
    vmask = np.kron(np.eye(GLA_HEADS), np.ones((1, GLA_DV)))
    bd = np.pad(np.kron(np.eye(GLA_HEADS), np.ones((GLA_DK, GLA_DV))), ((0, pad), (0, 0)))
    ones_blk = np.kron(np.eye(GLA_HEADS), np.ones((GLA_DV, GLA_DV)))
    b = lambda a: jnp.asarray(a, BF16)
    return dict(tril=b(causal), causal4=jnp.asarray(np.tile(causal, (1, GLA_HEADS)), F32), kmask=b(kmask),
                vmask=b(vmask), bd=jnp.asarray(bd, F32), ones_blk=b(ones_blk))


def _gla_body(gla_ref, glr_ref, gw_ref, gb_ref, tril_ref, causal_ref, kmask_ref, vmask_ref, bd_ref, ones_blk_ref,
              block_ref, ng_ref, o_ref, sout_ref, s_scr, o_scr, *, tl):
    @pl.when(pl.program_id(1) == 0)
    def _():
        s_scr[...] = jnp.zeros_like(s_scr)

    bd = bd_ref[...]
    z = _dot_row_halves(glr_ref[...].astype(BF16), gw_ref[...]) + gb_ref[...]
    lg = _log_sigmoid(z) * (1.0 / GLA_GATE_TEMP)
    q0, k0, v0, g0 = GLA_Q0, GLA_K0, GLA_V0, GLA_G0
    chunks = range(tl // CHUNK)

    def rows(a, n):
        return a[n * CHUNK:(n + 1) * CHUNK]

    cums = [_dot_exact_lhs(tril_ref[...], rows(lg, n)) for n in chunks]
    b_cum = jnp.concatenate(cums, axis=0)
    b_end = jnp.concatenate([jnp.broadcast_to(c[CHUNK - 1:CHUNK, :], (CHUNK, GLA_KP)) for c in cums], axis=0)
    q_in = (gla_ref[:, q0:q0 + GLA_KP].astype(F32) * jnp.exp(b_cum)).astype(BF16)
    kf = gla_ref[:, k0:k0 + GLA_KP].astype(F32)
    k_in = (kf * jnp.exp(-b_cum)).astype(BF16)
    k_dec = (kf * jnp.exp(b_end - b_cum)).astype(BF16)
    v = gla_ref[:, v0:g0]
    lg_hi, lg_lo = _hi_lo(lg)
    block_decay = jnp.exp(_dot_tn(lg_hi, block_ref[...]) + _dot_tn(lg_lo, block_ref[...]))

    scores, o, kv = {}, {}, {}
    for n in chunks:
        kn = rows(k_in, n)
        kk = jnp.concatenate([kn * kmask_ref[h:h + 1, :] for h in range(GLA_HEADS)], axis=0)
        scores[n] = (_dot_nt(rows(q_in, n), kk) * causal_ref[...]).astype(BF16)
    for n in chunks:
        vn = rows(v, n)
        vv = jnp.concatenate([vn * vmask_ref[h:h + 1, :] for h in range(GLA_HEADS)], axis=0)
        o[n] = _dot(scores[n], vv)
    for n in chunks:
        kv[n] = _dot_tn(rows(k_dec, n), rows(v, n)) * bd
    for n in chunks:
        s_prev = s_scr[...]
        o_scr[n * CHUNK:(n + 1) * CHUNK, :] = o[n] + _dot(rows(q_in, n), s_prev.astype(BF16))
        s_scr[...] = s_prev * jnp.broadcast_to(block_decay[:, n:n + 1], (GLA_KP, GLA_W)) + kv[n]

    o_all = o_scr[...]
    ms = _dot_row_halves((o_all * o_all).astype(BF16), ones_blk_ref[...]) * (1.0 / GLA_DV)
    gate = gla_ref[:, g0:g0 + GLA_W].astype(F32)
    o_ref[...] = (o_all * lax.rsqrt(ms + EPS) * ng_ref[...] * gate * _sigmoid(gate)).astype(BF16)
    s = s_scr[...]
    for h in range(GLA_HEADS):
        sout_ref[0, h] = s[h * GLA_DK:(h + 1) * GLA_DK, h * GLA_DV:(h + 1) * GLA_DV]


def _gla(gla2d, glr2d, gate_w, gate_b, norm_g, layer, batch, seq_len):
    assert seq_len % CHUNK == 0
    c = _gla_consts()
    tl = min(MIXER_TILE, seq_len)
    nl = seq_len // tl
    assert tl // CHUNK <= V7X_LANES
    block = (np.arange(tl) // CHUNK)[:, None] == np.arange(V7X_LANES)[None, :]
    consts = [(gate_w, True), (gate_b, True), (c["tril"], False), (c["causal4"], False), (c["kmask"], False),
              (c["vmask"], False), (c["bd"], False), (c["ones_blk"], False), (jnp.asarray(block, BF16), False),
              (norm_g, True)]
    row_spec = lambda w: pl.BlockSpec((tl, w), lambda b, l: (b * nl + l, 0))
    state_shape = (batch, GLA_HEADS, GLA_DK, GLA_DV)
    return pl.pallas_call(
        functools.partial(_gla_body, tl=tl),
        grid=(batch, nl),
        in_specs=[row_spec(GLA_COLS), row_spec(GLA_RANK)] + _specs(consts, layer),
        out_specs=[row_spec(GLA_W), pl.BlockSpec((1,) + state_shape[1:], lambda b, l: (b, 0, 0, 0))],
        out_shape=[jax.ShapeDtypeStruct((batch * seq_len, GLA_W), BF16), jax.ShapeDtypeStruct(state_shape, F32)],
        scratch_shapes=[pltpu.VMEM((GLA_KP, GLA_W), F32), pltpu.VMEM((tl, GLA_W), F32)],
        compiler_params=_params(),
        name="gla",
    )(gla2d, glr2d, *[a for a, _ in consts])


def _gla_lanes_body(gla_ref, glr_ref, gw_ref, gb_ref, ng_ref, s_ref, o_ref, sout_ref, q_scr, k_scr, kd_scr, v_scr,
                    g_scr, dec_scr, o_scr, *, seq_len):
    h = pl.program_id(0)
    nb = V7X_LANES
    steps = range(seq_len)
    q0, k0, v0, g0 = GLA_Q0, GLA_K0, GLA_V0, GLA_G0
    kw = slice(0, GLA_KW)

    def split(scr, t, tr, width):
        for hh in range(GLA_HEADS):
            scr[t, hh] = tr[hh * width:(hh + 1) * width]

    @pl.when(h == 0)
    def _():
        z = _dot_row_halves(glr_ref[...].astype(BF16), gw_ref[...]) + gb_ref[...]
        lg = (_log_sigmoid(z) * (1.0 / GLA_GATE_TEMP))[:, kw]
        cums = []
        for t in steps:
            lg_t = lg[t * nb:(t + 1) * nb]
            cums.append(lg_t if t == 0 else cums[-1] + lg_t)
        b_end = cums[-1]
        dec = _to_lanes(jnp.exp(b_end))
        for hh in range(GLA_HEADS):
            dec_scr[hh] = dec[hh * GLA_DK:(hh + 1) * GLA_DK]
        for t in steps:
            blk = gla_ref[t * nb:(t + 1) * nb, :].astype(F32)
            kf = blk[:, k0:k0 + GLA_KW]
            split(q_scr, t, _to_lanes(blk[:, q0:q0 + GLA_KW] * jnp.exp(cums[t])), GLA_DK)
            split(k_scr, t, _to_lanes(kf * jnp.exp(-cums[t])), GLA_DK)
            split(kd_scr, t, _to_lanes(kf * jnp.exp(b_end - cums[t])), GLA_DK)
            split(v_scr, t, _to_lanes(blk[:, v0:g0]), GLA_DV)
            split(g_scr, t, _to_lanes(blk[:, g0:g0 + GLA_W]), GLA_DV)

    q = [q_scr[t, h] for t in steps]
    k = [k_scr[t, h] for t in steps]
    kd = [kd_scr[t, h] for t in steps]
    v = [v_scr[t, h] for t in steps]
    dec = dec_scr[h]
    o = []
    for i in steps:
        acc = None
        for j in range(i + 1):
            s = jnp.sum(q[i] * k[j], axis=0, keepdims=True)
            acc = s * v[j] if acc is None else acc + s * v[j]
        o.append(acc)
    for i in range(GLA_DK):
        s_i = s_ref[i]
        new = s_i * dec[i:i + 1, :]
        for t in steps:
            o[t] = o[t] + q[t][i:i + 1, :] * s_i
            new = new + kd[t][i:i + 1, :] * v[t]
        sout_ref[i] = new
    for t in steps:
        ms = jnp.mean(o[t] * o[t], axis=0, keepdims=True)
        gate = g_scr[t, h]
        o_scr[t, h] = o[t] * lax.rsqrt(ms + EPS) * ng_ref[...] * gate * _sigmoid(gate)

    @pl.when(h == GLA_HEADS - 1)
    def _():
        for t in steps:
            o_ref[t * nb:(t + 1) * nb, :] = o_scr[t].reshape(GLA_W, nb).T.astype(BF16)


def _gla_lanes(gla2d, glr2d, gate_w, gate_b, norm_g, state, layer, seq_len):
    nb = V7X_LANES
    consts = [(gate_w, True), (gate_b, True), (norm_g, True)]
    k_tile = pltpu.VMEM((seq_len, GLA_HEADS, GLA_DK, nb), F32)
    v_tile = pltpu.VMEM((seq_len, GLA_HEADS, GLA_DV, nb), F32)
    return pl.pallas_call(
        functools.partial(_gla_lanes_body, seq_len=seq_len),
        grid=(GLA_HEADS,),
        in_specs=[_const_spec(gla2d.shape), _const_spec(glr2d.shape)] + _specs(consts, layer)
                 + [pl.BlockSpec((None, None, GLA_DK, GLA_DV, nb), lambda h: (layer, h, 0, 0, 0))],
        out_specs=[_const_spec((seq_len * nb, GLA_W)), pl.BlockSpec((None, GLA_DK, GLA_DV, nb), lambda h: (h, 0, 0, 0))],
        out_shape=[jax.ShapeDtypeStruct((seq_len * nb, GLA_W), BF16),
                   jax.ShapeDtypeStruct((GLA_HEADS, GLA_DK, GLA_DV, nb), F32)],
        scratch_shapes=[k_tile, k_tile, k_tile, v_tile, v_tile, pltpu.VMEM((GLA_HEADS, GLA_DK, nb), F32), v_tile],
        compiler_params=_params(),
        name="gla_lanes",
    )(gla2d, glr2d, *[a for a, _ in consts], state)


def _s5_discretize(lam_re, lam_im, log_dt, b_re, b_im, c_re, c_im):
    lr = jnp.minimum(lam_re.astype(F32), -S5_MIN_NEG)
    li = lam_im.astype(F32)
    dt = jnp.exp(log_dt.astype(F32))[..., None]
    mag = jnp.exp(lr * dt)
    ar = mag * jnp.cos(li * dt)
    ai = mag * jnp.sin(li * dt)
    den = lr * lr + li * li
    cr = ((ar - 1.0) * lr + ai * li) / den
    ci = (ai * lr - (ar - 1.0) * li) / den
    b_re, b_im = b_re.astype(F32), b_im.astype(F32)
    bbar_re = cr[..., None] * b_re - ci[..., None] * b_im
    bbar_im = cr[..., None] * b_im + ci[..., None] * b_re
    eye = jnp.eye(S5_GROUPS, dtype=F32)
    depth = lr.shape[0]
    in_blk = lambda b: jnp.einsum("lgpi,gh->lgihp", b, eye).reshape(depth, S5_W, S5_LANES)
    out_blk = lambda c: jnp.einsum("lgop,gh->lgpho", c.astype(F32), eye).reshape(depth, S5_LANES, S5_W)
    bdb = jnp.concatenate([in_blk(bbar_re), in_blk(bbar_im)], axis=2).astype(BF16)
    bdc = jnp.concatenate([out_blk(c_re), -out_blk(c_im)], axis=1).astype(BF16)
    return ar.reshape(depth, 1, S5_LANES), ai.reshape(depth, 1, S5_LANES), bdb, bdc


def _s5_body(*refs, tl, batch, has_state):
    u_ref, ar_ref, ai_ref, bdb_ref, bdc_ref, d_ref, gw_ref, gb_ref = refs[:8]
    n_in = 10 if has_state else 8
    o_ref, hr_ref, hi_ref, x_scr, h_scr = refs[n_in:]

    @pl.when(pl.program_id(0) == 0)
    def _():
        if has_state:
            h_scr[0] = refs[8][...]
            h_scr[1] = refs[9][...]
        else:
            h_scr[...] = jnp.zeros_like(h_scr)

    ar = jnp.broadcast_to(ar_ref[...], (batch, S5_LANES))
    ai = jnp.broadcast_to(ai_ref[...], (batch, S5_LANES))
    hr, hi = h_scr[0], h_scr[1]
    sub = min(tl, S5_SUB_STEPS)
    n_rows = sub * batch
    for s in range(tl // sub):
        rows = slice(s * n_rows, (s + 1) * n_rows)
        u = u_ref[rows, :]
        x_scr[rows, :] = _dot_row_halves(u, bdb_ref[...])
        for t in range(s * sub, (s + 1) * sub):
            r = slice(t * batch, (t + 1) * batch)
            hr, hi = (ar * hr - ai * hi + x_scr[r, 0:S5_LANES],
                      ar * hi + ai * hr + x_scr[r, S5_LANES:2 * S5_LANES])
            x_scr[r, 0:S5_LANES] = hr
            x_scr[r, S5_LANES:2 * S5_LANES] = hi
        y = _dot_row_halves(x_scr[rows, :].astype(BF16), bdc_ref[...]) + d_ref[...] * u.astype(F32)
        y = _gelu_tanh(y)
        o_ref[rows, :] = (y * _sigmoid(_dot_row_halves(y.astype(BF16), gw_ref[...]) + gb_ref[...])).astype(BF16)
    h_scr[0] = hr
    h_scr[1] = hi
    hr_ref[...] = hr
    hi_ref[...] = hi


def _s5(u_tm, disc, d, glu_w_bf, glu_b, layer, state, batch, seq_len):
    tl = min(S5_TIME_TILE, seq_len)
    has_state = state is not None
    consts = list(disc) + [d, glu_w_bf, glu_b]
    h_spec = pl.BlockSpec((batch, S5_LANES), lambda i: (0, 0))
    in_specs = ([pl.BlockSpec((tl * batch, S5_W), lambda i: (i, 0))]
                + [_layer_spec(a.shape, layer) for a in consts])
    args = [u_tm] + consts
    if has_state:
        in_specs += [h_spec, h_spec]
        args += list(state)
    return pl.pallas_call(
        functools.partial(_s5_body, tl=tl, batch=batch, has_state=has_state),
        grid=(seq_len // tl,),
        in_specs=in_specs,
        out_specs=[pl.BlockSpec((tl * batch, S5_W), lambda i: (i, 0)), h_spec, h_spec],
        out_shape=[jax.ShapeDtypeStruct((seq_len * batch, S5_W), BF16),
                   jax.ShapeDtypeStruct((batch, S5_LANES), F32), jax.ShapeDtypeStruct((batch, S5_LANES), F32)],
        scratch_shapes=[pltpu.VMEM((tl * batch, 2 * S5_LANES), F32), pltpu.VMEM((2, batch, S5_LANES), F32)],
        compiler_params=_params(),
        name="s5",
    )(*args)


def _ffn_body(*refs, tm, time_major_batch, final):
    (x_ref, oret_ref, os5_ref, ogla_ref, wout_ref, gffn_ref, win_ref, cw_ref, cb_ref, wo_ref) = refs[:10]
    n = 10
    gfin_ref = None
    if final:
        gfin_ref, n = refs[n], n + 1
    if time_major_batch is not None:
        past_ref, n = refs[n], n + 1
    out_ref, conv_ref, x1_scr, h_scr, carry_scr, act_scr = refs[n:]
    pad = V7X_SUBLANES
    nb = time_major_batch

    if nb is None:
        @pl.when(pl.program_id(1) == 0)
        def _():
            carry_scr[...] = jnp.zeros_like(carry_scr)

    mix = jnp.concatenate([oret_ref[...], os5_ref[...], ogla_ref[...]], axis=1)
    x1_scr[...] = x_ref[...] + _dot(mix, wout_ref[...])
    h_scr[...] = _rmsnorm_rows(x1_scr[...], gffn_ref[...]).astype(BF16)
    h = h_scr[...]

    for c in range(D_FF // ACT_COLS):
        cols = slice(c * ACT_COLS, (c + 1) * ACT_COLS)
        a_c = _dot(h, win_ref[:, c * ACT_COLS:(c + 1) * ACT_COLS])
        gate_c = _dot(h, win_ref[:, D_FF + c * ACT_COLS:D_FF + (c + 1) * ACT_COLS])
        if nb is None:
            ext = jnp.concatenate([carry_scr[:, cols], a_c], axis=0)
            prev1 = ext[pad - 1:pad - 1 + tm]
            prev2 = ext[pad - 2:pad - 2 + tm]
            carry_scr[:, cols] = a_c[tm - pad:tm]
        else:
            prev1 = jnp.concatenate([past_ref[1, :, cols], a_c[0:tm - nb]], axis=0)
            prev2 = jnp.concatenate([past_ref[0, :, cols], past_ref[1, :, cols], a_c[0:tm - 2 * nb]], axis=0)
            conv_ref[:, cols] = a_c[tm - (CONV_W - 1) * nb:tm]
        conv = (cb_ref[:, cols] + prev2 * cw_ref[0:1, cols] + prev1 * cw_ref[1:2, cols]
                + a_c * cw_ref[2:3, cols])
        act_scr[:, cols] = (_gelu_tanh(conv) * gate_c).astype(BF16)

    x2 = x1_scr[...] + _dot(act_scr[...], wo_ref[...])
    out_ref[...] = _rmsnorm_rows(x2, gfin_ref[...]) if final else x2
    if nb is None:
        conv_ref[0] = carry_scr[pad - (CONV_W - 1):pad, :]


def _ffn(x2d, o_ret, o_s5, o_gla, w_out_bf, norm_g, w_in_bf, conv_w, conv_b, w_o_bf, layer, final_g, conv_past,
         batch, seq_len):
    rows = batch * seq_len
    time_major = conv_past is not None
    tm = rows if time_major else min(ROW_TILE, seq_len)
    final = final_g is not None
    consts = [w_out_bf, norm_g, w_in_bf, conv_w, conv_b, w_o_bf]
    stacked = [True] * len(consts)
    big = [True, False, True, False, False, True]
    if final:
        consts.append(final_g.reshape(1, D_MODEL))
        stacked.append(False)
        big.append(False)
    if time_major:
        assert seq_len >= CONV_W - 1 and batch % V7X_SUBLANES == 0
        consts.append(conv_past)
        stacked.append(False)
        big.append(False)
        grid = (1,)
        imap = lambda i: (0, 0)
        conv_rows = (CONV_W - 1) * batch
        conv_spec = pl.BlockSpec((conv_rows, D_FF), imap)
        conv_shape = jax.ShapeDtypeStruct((conv_rows, D_FF), F32)
    else:
        assert seq_len % tm == 0
        nl = seq_len // tm
        grid = (batch, nl)
        imap = lambda b, l: (b * nl + l, 0)
        conv_spec = pl.BlockSpec((1, CONV_W - 1, D_FF), lambda b, l: (b, 0, 0))
        conv_shape = jax.ShapeDtypeStruct((batch, CONV_W - 1, D_FF), F32)
    row_spec = lambda w: pl.BlockSpec((tm, w), imap)
    in_specs = ([row_spec(D_MODEL), row_spec(RET_W), row_spec(S5_W), row_spec(GLA_W)]
                + [_layer_spec(a.shape, layer, b) if s else _const_spec(a.shape)
                   for a, s, b in zip(consts, stacked, big)])
    return pl.pallas_call(
        functools.partial(_ffn_body, tm=tm, time_major_batch=batch if time_major else None, final=final),
        grid=grid,
        in_specs=in_specs,
        out_specs=[row_spec(D_MODEL), conv_spec],
        out_shape=[jax.ShapeDtypeStruct((rows, D_MODEL), F32), conv_shape],
        scratch_shapes=[pltpu.VMEM((tm, D_MODEL), F32), pltpu.VMEM((tm, D_MODEL), BF16),
                        pltpu.VMEM((V7X_SUBLANES, D_FF), F32), pltpu.VMEM((tm, D_FF), BF16)],
        compiler_params=_params(),
        name="ffn",
    )(x2d, o_ret, o_s5, o_gla, *consts)


def _rotary_tables(first_pos, n_pos, repeat=1):
    half = RET_HD // 2
    inv = ROPE_BASE ** (-(np.arange(half, dtype=np.float64) / half))
    ang = (first_pos + np.arange(n_pos)).astype(np.float64)[:, None] * inv[None, :]
    cos, sin = np.cos(ang), np.sin(ang)
    expand = lambda a, b: np.repeat(np.tile(np.concatenate([a, b], axis=1), (1, RET_HEADS)), repeat, axis=0)
    return jnp.asarray(expand(cos, cos), F32), jnp.asarray(expand(-sin, sin), F32)


def _swap_major(a2d, outer, inner):
    return a2d.reshape(outer, inner, -1).transpose(1, 0, 2).reshape(outer * inner, -1)


def _run_prompt_group(x, prm, final_g):
    batch, seq_len, _ = x.shape
    rows = batch * seq_len
    depth = prm["w_in"].shape[0]
    tm = min(MIX_IN_TILE, seq_len)
    cos_tab, sin_tab = _rotary_tables(0, seq_len)
    x2d = x.reshape(rows, D_MODEL)
    outs = []
    for li in range(depth):
        ret, su, gla, glr = _mix_in(x2d, prm["norm_mix_g"], prm["w_in"], li, cos_tab, sin_tab, tm)
        o_ret, s_ret = _retention(ret, prm["ret_ln_g"], prm["ret_ln_b"], li, batch, seq_len)
        o_s5, s5r, s5i = _s5(_swap_major(su, batch, seq_len), prm["s5_disc"], prm["s5_d"], prm["s5_glu_w"],
                             prm["s5_glu_b"], li, None, batch, seq_len)
        o_s5 = _swap_major(o_s5, seq_len, batch)
        o_gla, s_gla = _gla(gla, glr, prm["gla_gate_w"], prm["gla_gate_b"], prm["gla_norm_g"], li, batch, seq_len)
        x2d, conv_new = _ffn(x2d, o_ret, o_s5, o_gla, prm["w_out"], prm["norm_ffn_g"], prm["ffn_w_in"],
                             prm["ffn_conv_w"], prm["ffn_conv_b"], prm["ffn_w_out"], li,
                             final_g if li == depth - 1 else None, None, batch, seq_len)
        s5_shape = (batch, S5_GROUPS, S5_STATE)
        outs.append((s_ret, s5r.reshape(s5_shape), s5i.reshape(s5_shape), s_gla, conv_new))
    return [x2d.reshape(batch, seq_len, D_MODEL)] + [jnp.stack([o[i] for o in outs]) for i in range(5)]


def _run_sample_group(x, past_len, states, prm, final_g):
    batch, seq_len, _ = x.shape
    assert batch == V7X_LANES
    rows = batch * seq_len
    depth = prm["w_in"].shape[0]
    cos_tab, sin_tab = _rotary_tables(past_len, seq_len, repeat=batch)
    x2d = x.transpose(1, 0, 2).reshape(rows, D_MODEL)
    ret_state = states["ret"].transpose(0, 2, 3, 4, 1)
    gla_state = states["gla"].transpose(0, 2, 3, 4, 1)
    conv_state = states["conv"].transpose(0, 2, 1, 3)
    s5_flat = lambda s: s.reshape(depth, batch, S5_LANES)
    s5r_state, s5i_state = s5_flat(states["s5r"]), s5_flat(states["s5i"])
    outs = []
    for li in range(depth):
        ret, su, gla, glr = _mix_in(x2d, prm["norm_mix_g"], prm["w_in"], li, cos_tab, sin_tab, rows)
        o_ret, s_ret = _retention_lanes(ret, prm["ret_ln_g_lanes"], prm["ret_ln_b_lanes"], ret_state, li, seq_len)
        o_s5, s5r, s5i = _s5(su, prm["s5_disc"], prm["s5_d"], prm["s5_glu_w"], prm["s5_glu_b"], li,
                             (s5r_state[li], s5i_state[li]), batch, seq_len)
        o_gla, s_gla = _gla_lanes(gla, glr, prm["gla_gate_w"], prm["gla_gate_b"], prm["gla_norm_g_lanes"], gla_state,
                                  li, seq_len)
        x2d, conv_new = _ffn(x2d, o_ret, o_s5, o_gla, prm["w_out"], prm["norm_ffn_g"], prm["ffn_w_in"],
                             prm["ffn_conv_w"], prm["ffn_conv_b"], prm["ffn_w_out"], li,
                             final_g if li == depth - 1 else None, conv_state[li], batch, seq_len)
        outs.append((s_ret, s5r, s5i, s_gla, conv_new.reshape(CONV_W - 1, batch, D_FF)))
    s_ret, s5r, s5i, s_gla, conv_new = (jnp.stack([o[i] for o in outs]) for i in range(5))
    s5_shape = (depth, batch, S5_GROUPS, S5_STATE)
    return [x2d.reshape(seq_len, batch, D_MODEL).transpose(1, 0, 2), s_ret.transpose(0, 4, 1, 2, 3),
            s5r.reshape(s5_shape), s5i.reshape(s5_shape), s_gla.transpose(0, 4, 1, 2, 3),
            conv_new.transpose(0, 2, 1, 3)]


def _prepare_params(norm_mix_g, w_in, ret_norm_g, ret_norm_b, s5_lambda_re, s5_lambda_im, s5_log_dt, s5_b_re, s5_b_im,
                    s5_c_re, s5_c_im, s5_d, s5_glu_w, s5_glu_b, gla_gate_w, gla_gate_b, gla_norm_g, w_out,
                    norm_ffn_g, ffn_w_in, ffn_conv_w, ffn_conv_b, ffn_w_out):
    depth = w_in.shape[0]
    nb = V7X_LANES
    row = lambda a: a.astype(F32).reshape(depth, 1, a.shape[-1])
    ret_lanes = lambda a: jnp.broadcast_to(a.astype(F32).reshape(depth, RET_HEADS, RET_HD, 1),
                                           (depth, RET_HEADS, RET_HD, nb))
    return dict(
        norm_mix_g=row(norm_mix_g), w_in=_mix_weights(w_in),
        ret_ln_g=row(ret_norm_g), ret_ln_b=row(ret_norm_b),
        ret_ln_g_lanes=ret_lanes(ret_norm_g), ret_ln_b_lanes=ret_lanes(ret_norm_b),
        s5_disc=_s5_discretize(s5_lambda_re, s5_lambda_im, s5_log_dt, s5_b_re, s5_b_im, s5_c_re, s5_c_im),
        s5_d=row(s5_d), s5_glu_w=s5_glu_w.astype(BF16), s5_glu_b=row(s5_glu_b),
        gla_gate_w=_pad_gate(gla_gate_w, BF16), gla_gate_b=_pad_gate(row(gla_gate_b), F32),
        gla_norm_g=row(jnp.tile(gla_norm_g, (1, GLA_HEADS))),
        gla_norm_g_lanes=jnp.broadcast_to(gla_norm_g.astype(F32).reshape(depth, GLA_DV, 1), (depth, GLA_DV, nb)),
        w_out=w_out.astype(BF16), norm_ffn_g=row(norm_ffn_g), ffn_w_in=ffn_w_in.astype(BF16),
        ffn_conv_w=jnp.pad(ffn_conv_w.astype(F32), ((0, 0), (0, V7X_SUBLANES - CONV_W), (0, 0))),
        ffn_conv_b=row(ffn_conv_b), ffn_w_out=ffn_w_out.astype(BF16))


def kernel(x_prompt, x_sample, state_ret, state_s5_re, state_s5_im, state_gla, state_ffn_conv, norm_mix_g, w_in, ret_norm_g, ret_norm_b, s5_lambda_re, s5_lambda_im, s5_log_dt, s5_b_re, s5_b_im, s5_c_re, s5_c_im, s5_d, s5_glu_w, s5_glu_b, gla_gate_w, gla_gate_b, gla_norm_g, w_out, norm_ffn_g, ffn_w_in, ffn_conv_w, ffn_conv_b, ffn_w_out, norm_final_g):
    prm = _prepare_params(norm_mix_g, w_in, ret_norm_g, ret_norm_b, s5_lambda_re, s5_lambda_im, s5_log_dt, s5_b_re,
                          s5_b_im, s5_c_re, s5_c_im, s5_d, s5_glu_w, s5_glu_b, gla_gate_w, gla_gate_b, gla_norm_g,
                          w_out, norm_ffn_g, ffn_w_in, ffn_conv_w, ffn_conv_b, ffn_w_out)
    sample_states = dict(ret=state_ret, s5r=state_s5_re, s5i=state_s5_im, gla=state_gla, conv=state_ffn_conv)
    yp, ret_p, s5r_p, s5i_p, gla_p, conv_p = _run_prompt_group(x_prompt, prm, norm_final_g)
    ys, ret_s, s5r_s, s5i_s, gla_s, conv_s = _run_sample_group(x_sample, PAST_LEN, sample_states, prm, norm_final_g)
    return (yp, ys, ret_p, ret_s, s5r_p, s5r_s, s5i_p, s5i_s, gla_p, gla_s, conv_p, conv_s)
```

```python
import functools

import numpy as np
import jax
import jax.numpy as jnp
from jax import lax
from jax.experimental import pallas as pl
from jax.experimental.pallas import tpu as pltpu

F32, BF16 = jnp.float32, jnp.bfloat16

D_MODEL = 1024
RET_W, S5_W, GLA_W = 384, 256, 384
RET_HEADS, RET_HD = 6, 64
RET_PAIRS = RET_HEADS // 2
S5_GROUPS, S5_CH, S5_STATE = 16, 16, 64
S5_LANES = S5_GROUPS * S5_STATE
S5_MIN_NEG = 1e-4
GLA_HEADS, GLA_DK, GLA_DV, GLA_RANK = 4, 48, 96, 16
GLA_KW = GLA_HEADS * GLA_DK
GLA_GATE_TEMP = 16.0
D_FF = 2816
CONV_W = 3
ROPE_BASE = 10000.0
CHUNK = 64
EPS = 1e-6
PAST_LEN = 16384
IN_COLS = 4 * RET_W + S5_W + 2 * GLA_KW + 2 * GLA_W + GLA_RANK
COL_RET, COL_S5, COL_GLA, COL_LR = 0, 4 * RET_W, 4 * RET_W + S5_W, IN_COLS - GLA_RANK
GLA_KP = 256
GLA_Q0, GLA_LR0, GLA_K0, GLA_V0, GLA_G0 = 0, GLA_KW, GLA_KP, 2 * GLA_KP, 2 * GLA_KP + GLA_W
GLA_COLS = 2 * GLA_KP + 2 * GLA_W
MIX_COLS = COL_GLA + GLA_COLS

V7X_SUBLANES = 8
V7X_LANES = 128
VMEM_LIMIT = 58 * 1024 * 1024

ROW_TILE = 512
MIX_IN_TILE = 1024
MIXER_TILE = 2048
S5_TIME_TILE = 256
S5_SUB_STEPS = 64
ACT_COLS = 256


def _dot(a, b):
    return jnp.dot(a, b, preferred_element_type=F32)


def _dot_nt(a, b):
    return lax.dot_general(a, b, (((1,), (1,)), ((), ())), preferred_element_type=F32)


def _dot_tn(a, b):
    return lax.dot_general(a, b, (((0,), (0,)), ((), ())), preferred_element_type=F32)


def _hi_lo(x):
    hi = x.astype(BF16)
    return hi, (x - hi.astype(F32)).astype(BF16)


def _dot_exact_lhs(m, x):
    hi, lo = _hi_lo(x)
    return _dot(m, hi) + _dot(m, lo)


def _dot_row_halves(a, w):
    half = a.shape[0] // 2
    return jnp.concatenate([_dot(a[:half], w), _dot(a[half:], w)], axis=0)


def _pad_gate(a, dtype):
    return jnp.pad(a.astype(dtype), [(0, 0)] * (a.ndim - 1) + [(0, GLA_KP - GLA_KW)])


def _sigmoid(x):
    return 1.0 / (1.0 + jnp.exp(-x))


def _log_sigmoid(z):
    return jnp.minimum(z, 0.0) - jnp.log(1.0 + jnp.exp(-jnp.abs(z)))


def _gelu_tanh(x):
    return 0.5 * x * (1.0 + jnp.tanh(0.7978845608028654 * (x + 0.044715 * (x * x * x))))


def _rmsnorm_rows(x, g):
    return x * lax.rsqrt(jnp.mean(x * x, axis=-1, keepdims=True) + EPS) * g


def _const_spec(shape):
    nd = len(shape)
    return pl.BlockSpec(shape, lambda *_: (0,) * nd)


def _layer_spec(stacked_shape, layer, single_buffer=False):
    nd = len(stacked_shape) - 1
    mode = dict(pipeline_mode=pl.Buffered(1)) if single_buffer else {}
    return pl.BlockSpec((None,) + tuple(stacked_shape[1:]), lambda *_: (layer,) + (0,) * nd, **mode)


def _specs(arrays, layer):
    return [_layer_spec(a.shape, layer) if stacked else _const_spec(a.shape) for a, stacked in arrays]


def _carried_output(buffer, n_before, out_index):
    if buffer is None:
        return [], [], {}
    return [buffer], [pl.BlockSpec(memory_space=pl.ANY)], {n_before: out_index}


def _layer_slot_body(*refs, head_fn, n_in, **kw):
    @pl.when(pl.program_id(0) == 0)
    def _():
        head_fn(pl.program_id(1), *refs, n_in=n_in, **kw)

    @pl.when(pl.program_id(0) > 0)
    def _():
        refs[n_in + 1][...] = jnp.zeros_like(refs[n_in + 1])


def _params():
    return pltpu.CompilerParams(vmem_limit_bytes=VMEM_LIMIT)


def _to_lanes(x):
    w = x.shape[1]
    pad = -w % V7X_LANES
    if pad:
        x = jnp.concatenate([x, jnp.zeros((x.shape[0], pad), F32)], axis=1)
    return x.T[0:w]


def _mix_in_body(x_ref, g_ref, w_ref, cos_ref, sin_ref, ret_ref, su_ref, gla_ref, glr_ref):
    h = _rmsnorm_rows(x_ref[...], g_ref[...]).astype(BF16)
    ret = _dot(h, w_ref[:, COL_RET:COL_S5])
    cos = cos_ref[...]
    sin = sin_ref[...]
    lane = lax.broadcasted_iota(jnp.int32, cos.shape, 1)
    first_half = (lane & (RET_HD // 2)) == 0

    def rotary(z):
        swapped = jnp.where(first_half, pltpu.roll(z, RET_W - RET_HD // 2, 1), pltpu.roll(z, RET_HD // 2, 1))
        return z * cos + swapped * sin

    ret_ref[:, 0:RET_W] = rotary(ret[:, 0:RET_W]).astype(BF16)
    ret_ref[:, RET_W:2 * RET_W] = (rotary(ret[:, RET_W:2 * RET_W]) * RET_HD ** -0.5).astype(BF16)
    ret_ref[:, 2 * RET_W:] = ret[:, 2 * RET_W:].astype(BF16)
    su_ref[...] = _dot(h, w_ref[:, COL_S5:COL_GLA]).astype(BF16)
    gla = _dot(h, w_ref[:, COL_GLA:MIX_COLS])
    glane = lax.broadcasted_iota(jnp.int32, (1, GLA_COLS), 1)
    scale = jnp.where(glane < GLA_KW, GLA_DK ** -0.5, jnp.where(glane < GLA_KP, 0.0, 1.0))
    gla_ref[...] = (gla * scale).astype(BF16)
    glr_ref[...] = gla[:, GLA_LR0:GLA_LR0 + GLA_RANK]


def _mix_weights(w_in):
    w = w_in.astype(BF16)
    k0, v0 = COL_GLA + GLA_KW, COL_GLA + 2 * GLA_KW
    zeros = lambda n: jnp.zeros(w.shape[:2] + (n,), BF16)
    return jnp.concatenate([w[..., :k0], w[..., COL_LR:], zeros(GLA_KP - GLA_KW - GLA_RANK),
                            w[..., k0:v0], zeros(GLA_KP - GLA_KW), w[..., v0:COL_LR]], axis=-1)


def _mix_in(x2d, norm_g, w_in_bf, layer, cos_tab, sin_tab, tm):
    rows = x2d.shape[0]
    n_tab = cos_tab.shape[0] // tm
    row_spec = lambda w: pl.BlockSpec((tm, w), lambda i: (i, 0))
    tab_spec = pl.BlockSpec((tm, RET_W), lambda i: (i % n_tab, 0))
    return pl.pallas_call(
        _mix_in_body,
        grid=(rows // tm,),
        in_specs=[row_spec(D_MODEL), _layer_spec(norm_g.shape, layer), _layer_spec(w_in_bf.shape, layer, True),
                  tab_spec, tab_spec],
        out_specs=[row_spec(4 * RET_W), row_spec(S5_W), row_spec(GLA_COLS), row_spec(GLA_RANK)],
        out_shape=[jax.ShapeDtypeStruct((rows, 4 * RET_W), BF16), jax.ShapeDtypeStruct((rows, S5_W), BF16),
                   jax.ShapeDtypeStruct((rows, GLA_COLS), BF16), jax.ShapeDtypeStruct((rows, GLA_RANK), F32)],
        compiler_params=_params(),
        name="mix_in",
    )(x2d, norm_g, w_in_bf, cos_tab, sin_tab)


def _retention_gammas():
    return 1.0 - 2.0 ** (-5.0 - np.arange(RET_HEADS))


def _retention_consts():
    t = np.arange(CHUNK)
    gam = _retention_gammas()
    causal = t[:, None] >= t[None, :]
    diff = np.maximum(t[:, None] - t[None, :], 0)
    dmask = np.zeros((RET_PAIRS, CHUNK, 2 * CHUNK))
    cdec = np.zeros((RET_PAIRS, 2 * RET_HD, 2 * RET_HD))
    for p in range(RET_PAIRS):
        for s in range(2):
            g = gam[2 * p + s]
            dmask[p, :, s * CHUNK:(s + 1) * CHUNK] = np.where(causal, g ** diff, 0.0)
            cdec[p, s * RET_HD:(s + 1) * RET_HD, s * RET_HD:(s + 1) * RET_HD] = g ** CHUNK
    lane_gam = np.repeat(gam, RET_HD)[None, :]
    qdec = lane_gam ** (t[:, None] + 1.0)
    kdec = lane_gam ** (CHUNK - 1.0 - t[:, None])
    bd = (cdec[0] > 0).astype(np.float32)
    head_mask = np.stack([np.arange(2 * RET_HD) < RET_HD, np.arange(2 * RET_HD) >= RET_HD]).astype(np.float32)
    ones_blk = np.kron(np.eye(RET_HEADS), np.ones((RET_HD, RET_HD)))
    f = lambda a: jnp.asarray(a, F32)
    return dict(dmask=f(dmask), cdec=f(cdec), qdec=qdec, kdec=kdec, bd=f(bd),
                head_mask=jnp.asarray(head_mask, BF16), ones_blk=jnp.asarray(ones_blk, BF16))


def _retention_body(ret_ref, dmask_ref, cdec_ref, qdec_ref, kdec_ref, bd_ref, hm_ref, ones_ref, lng_ref, lnb_ref,
                    o_ref, sout_ref, s_scr, o_scr, *, tl):
    @pl.when(pl.program_id(1) == 0)
    def _():
        s_scr[...] = jnp.zeros_like(s_scr)

    bd = bd_ref[...]
    m0 = hm_ref[0:1, :]
    m1 = hm_ref[1:2, :]
    q = ret_ref[:, 0:RET_W]
    k = ret_ref[:, RET_W:2 * RET_W]
    v = ret_ref[:, 2 * RET_W:3 * RET_W]
    q_start = (q.astype(F32) * qdec_ref[...]).astype(BF16)
    k_end = (k.astype(F32) * kdec_ref[...]).astype(BF16)
    blocks = [(n, p) for n in range(tl // CHUNK) for p in range(RET_PAIRS)]

    def part(a, n, p):
        return a[n * CHUNK:(n + 1) * CHUNK, 2 * RET_HD * p:2 * RET_HD * (p + 1)]

    scores, o, kv = {}, {}, {}
    for n, p in blocks:
        kp = part(k, n, p)
        kk = jnp.concatenate([kp * m0, kp * m1], axis=0)
        scores[n, p] = (_dot_nt(part(q, n, p), kk) * dmask_ref[p]).astype(BF16)
    for n, p in blocks:
        vp = part(v, n, p)
        o[n, p] = _dot(scores[n, p], jnp.concatenate([vp * m0, vp * m1], axis=0))
    for n, p in blocks:
        kv[n, p] = _dot_tn(part(k_end, n, p), part(v, n, p)) * bd
    for n, p in blocks:
        s_prev = s_scr[p]
        o_scr[n * CHUNK:(n + 1) * CHUNK, 2 * RET_HD * p:2 * RET_HD * (p + 1)] = (
            o[n, p] + _dot(part(q_start, n, p), s_prev.astype(BF16)))
        s_scr[p] = s_prev * cdec_ref[p] + kv[n, p]

    o_all = o_scr[...]
    ones_blk = ones_ref[...]
    mu = _dot_row_halves(o_all.astype(BF16), ones_blk) * (1.0 / RET_HD)
    cen = o_all - mu
    var = _dot_row_halves((cen * cen).astype(BF16), ones_blk) * (1.0 / RET_HD)
    y = cen * lax.rsqrt(var + EPS) * lng_ref[...] + lnb_ref[...]
    gate = ret_ref[:, 3 * RET_W:4 * RET_W].astype(F32)
    o_ref[...] = (y * gate * _sigmoid(gate)).astype(BF16)
    for p in range(RET_PAIRS):
        s = s_scr[p]
        sout_ref[0, 2 * p] = s[0:RET_HD, 0:RET_HD]
        sout_ref[0, 2 * p + 1] = s[RET_HD:2 * RET_HD, RET_HD:2 * RET_HD]


def _retention(ret2d, ln_g, ln_b, layer, batch, seq_len):
    assert seq_len % CHUNK == 0
    c = _retention_consts()
    tl = min(MIXER_TILE, seq_len)
    nl = seq_len // tl
    tile_rows = lambda a: jnp.asarray(np.tile(a, (tl // CHUNK, 1)), F32)
    consts = [(c["dmask"], False), (c["cdec"], False), (tile_rows(c["qdec"]), False), (tile_rows(c["kdec"]), False),
              (c["bd"], False), (c["head_mask"], False), (c["ones_blk"], False), (ln_g, True), (ln_b, True)]
    state_shape = (batch, RET_HEADS, RET_HD, RET_HD)
    return pl.pallas_call(
        functools.partial(_retention_body, tl=tl),
        grid=(batch, nl),
        in_specs=[pl.BlockSpec((tl, 4 * RET_W), lambda b, l: (b * nl + l, 0))] + _specs(consts, layer),
        out_specs=[pl.BlockSpec((tl, RET_W), lambda b, l: (b * nl + l, 0)),
                   pl.BlockSpec((1,) + state_shape[1:], lambda b, l: (b, 0, 0, 0))],
        out_shape=[jax.ShapeDtypeStruct((batch * seq_len, RET_W), BF16), jax.ShapeDtypeStruct(state_shape, F32)],
        scratch_shapes=[pltpu.VMEM((RET_PAIRS, 2 * RET_HD, 2 * RET_HD), F32), pltpu.VMEM((tl, RET_W), F32)],
        compiler_params=_params(),
        name="retention",
    )(ret2d, *[a for a, _ in consts])


def _retention_lanes_head(h, *refs, seq_len, n_in):
    ret_ref, tab_ref, lng_ref, lnb_ref, s_ref = refs[:5]
    o_ref, sout_ref, q_scr, k_scr, v_scr, g_scr, o_scr = refs[n_in:]
    nb = V7X_LANES
    steps = range(seq_len)

    @pl.when(h == 0)
    def _():
        for t in steps:
            blk = ret_ref[t * nb:(t + 1) * nb, :].astype(F32)
            for scr, off in ((q_scr, 0), (k_scr, RET_W), (v_scr, 2 * RET_W), (g_scr, 3 * RET_W)):
                tr = _to_lanes(blk[:, off:off + RET_W])
                for hh in range(RET_HEADS):
                    scr[t, hh] = tr[hh * RET_HD:(hh + 1) * RET_HD]

    tab = tab_ref[h]
    row = lambda r: tab[r:r + 1, :]
    q = [q_scr[t, h] for t in steps]
    k = [k_scr[t, h] for t in steps]
    v = [v_scr[t, h] for t in steps]
    o = []
    for i in steps:
        acc = None
        for j in range(i + 1):
            s = jnp.sum(q[i] * k[j], axis=0, keepdims=True)
            if i > j:
                s = s * row(i - j - 1)
            acc = s * v[j] if acc is None else acc + s * v[j]
        o.append(acc)
    q_start = [q[t] * row(t) for t in steps]
    k_end = [k[t] * row(seq_len + t) for t in steps]
    chunk_decay = row(2 * seq_len)
    for d in range(RET_HD):
        s_d = s_ref[d]
        new = s_d * chunk_decay
        for t in steps:
            o[t] = o[t] + q_start[t][d:d + 1, :] * s_d
            new = new + k_end[t][d:d + 1, :] * v[t]
        sout_ref[d] = new
    for t in steps:
        mu = jnp.mean(o[t], axis=0, keepdims=True)
        cen = o[t] - mu
        var = jnp.mean(cen * cen, axis=0, keepdims=True)
        gate = g_scr[t, h]
        o_scr[t, h] = (cen * lax.rsqrt(var + EPS) * lng_ref[h] + lnb_ref[h]) * gate * _sigmoid(gate)

    @pl.when(h == RET_HEADS - 1)
    def _():
        for t in steps:
            o_ref[t * nb:(t + 1) * nb, :] = o_scr[t].reshape(RET_W, nb).T.astype(BF16)


def _retention_lanes(ret2d, ln_g, ln_b, state, new_state, layer, seq_len):
    nb = V7X_LANES
    gam = _retention_gammas()[:, None]
    t = np.arange(seq_len)[None, :]
    rows = np.concatenate([gam ** (t + 1.0), gam ** (seq_len - 1.0 - t), gam ** (seq_len + 0.0 * t[:, :1])], axis=1)
    n_rows = -(-rows.shape[1] // V7X_SUBLANES) * V7X_SUBLANES
    tab = np.zeros((RET_HEADS, n_rows, nb))
    tab[:, :rows.shape[1], :] = rows[:, :, None]
    consts = [(jnp.asarray(tab, F32), False), (ln_g, True), (ln_b, True)]
    tile = pltpu.VMEM((seq_len, RET_HEADS, RET_HD, nb), F32)
    block = (None, None, RET_HD, RET_HD, nb)
    prev, prev_specs, alias = _carried_output(new_state, n_before=5, out_index=1)
    slots = 1 if prev else state.shape[0] - layer
    return pl.pallas_call(
        functools.partial(_layer_slot_body, head_fn=_retention_lanes_head, n_in=5 + len(prev), seq_len=seq_len),
        grid=(slots, RET_HEADS),
        in_specs=[_const_spec(ret2d.shape)] + _specs(consts, layer)
                 + [pl.BlockSpec(block, lambda g, h: (layer, h, 0, 0, 0))] + prev_specs,
        out_specs=[_const_spec((seq_len * nb, RET_W)), pl.BlockSpec(block, lambda g, h: (layer + g, h, 0, 0, 0))],
        out_shape=[jax.ShapeDtypeStruct((seq_len * nb, RET_W), BF16), jax.ShapeDtypeStruct(state.shape, F32)],
        scratch_shapes=[tile] * 5,
        input_output_aliases=alias,
        compiler_params=_params(),
        name="retention_lanes",
    )(ret2d, *[a for a, _ in consts], state, *prev)


def _gla_consts():
    t = np.arange(CHUNK)
    causal = t[:, None] >= t[None, :]
    pad = GLA_KP - GLA_KW
    kmask = np.pad(np.kron(np.eye(GLA_HEADS), np.ones((1, GLA_DK))), ((0, 0), (0, pad)))
    vmask = np.kron(np.eye(GLA_HEADS), np.ones((1, GLA_DV)))
    bd = np.pad(np.kron(np.eye(GLA_HEADS), np.ones((GLA_DK, GLA_DV))), ((0, pad), (0, 0)))
    ones_blk = np.kron(np.eye(GLA_HEADS), np.ones((GLA_DV, GLA_DV)))
    b = lambda a: jnp.asarray(a, BF16)
    return dict(tril=b(causal), causal4=jnp.asarray(np.tile(causal, (1, GLA_HEADS)), F32), kmask=b(kmask),
                vmask=b(vmask), bd=jnp.asarray(bd, F32), ones_blk=b(ones_blk))


def _gla_body(gla_ref, glr_ref, gw_ref, gb_ref, tril_ref, causal_ref, kmask_ref, vmask_ref, bd_ref, ones_blk_ref,
              block_ref, ng_ref, o_ref, sout_ref, s_scr, o_scr, *, tl):
    @pl.when(pl.program_id(1) == 0)
    def _():
        s_scr[...] = jnp.zeros_like(s_scr)

    bd = bd_ref[...]
    z = _dot_row_halves(glr_ref[...].astype(BF16), gw_ref[...]) + gb_ref[...]
    lg = _log_sigmoid(z) * (1.0 / GLA_GATE_TEMP)
    q0, k0, v0, g0 = GLA_Q0, GLA_K0, GLA_V0, GLA_G0
    chunks = range(tl // CHUNK)

    def rows(a, n):
        return a[n * CHUNK:(n + 1) * CHUNK]

    cums = [_dot_exact_lhs(tril_ref[...], rows(lg, n)) for n in chunks]
    b_cum = jnp.concatenate(cums, axis=0)
    b_end = jnp.concatenate([jnp.broadcast_to(c[CHUNK - 1:CHUNK, :], (CHUNK, GLA_KP)) for c in cums], axis=0)
    q_in = (gla_ref[:, q0:q0 + GLA_KP].astype(F32) * jnp.exp(b_cum)).astype(BF16)
    kf = gla_ref[:, k0:k0 + GLA_KP].astype(F32)
    k_in = (kf * jnp.exp(-b_cum)).astype(BF16)
    k_dec = (kf * jnp.exp(b_end - b_cum)).astype(BF16)
    v = gla_ref[:, v0:g0]
    lg_hi, lg_lo = _hi_lo(lg)
    block_decay = jnp.exp(_dot_tn(lg_hi, block_ref[...]) + _dot_tn(lg_lo, block_ref[...]))

    scores, o, kv = {}, {}, {}
    for n in chunks:
        kn = rows(k_in, n)
        kk = jnp.concatenate([kn * kmask_ref[h:h + 1, :] for h in range(GLA_HEADS)], axis=0)
        scores[n] = (_dot_nt(rows(q_in, n), kk) * causal_ref[...]).astype(BF16)
    for n in chunks:
        vn = rows(v, n)
        vv = jnp.concatenate([vn * vmask_ref[h:h + 1, :] for h in range(GLA_HEADS)], axis=0)
        o[n] = _dot(scores[n], vv)
    for n in chunks:
        kv[n] = _dot_tn(rows(k_dec, n), rows(v, n)) * bd
    for n in chunks:
        s_prev = s_scr[...]
        o_scr[n * CHUNK:(n + 1) * CHUNK, :] = o[n] + _dot(rows(q_in, n), s_prev.astype(BF16))
        s_scr[...] = s_prev * jnp.broadcast_to(block_decay[:, n:n + 1], (GLA_KP, GLA_W)) + kv[n]

    o_all = o_scr[...]
    ms = _dot_row_halves((o_all * o_all).astype(BF16), ones_blk_ref[...]) * (1.0 / GLA_DV)
    gate = gla_ref[:, g0:g0 + GLA_W].astype(F32)
    o_ref[...] = (o_all * lax.rsqrt(ms + EPS) * ng_ref[...] * gate * _sigmoid(gate)).astype(BF16)
    s = s_scr[...]
    for h in range(GLA_HEADS):
        sout_ref[0, h] = s[h * GLA_DK:(h + 1) * GLA_DK, h * GLA_DV:(h + 1) * GLA_DV]


def _gla(gla2d, glr2d, gate_w, gate_b, norm_g, layer, batch, seq_len):
    assert seq_len % CHUNK == 0
    c = _gla_consts()
    tl = min(MIXER_TILE, seq_len)
    nl = seq_len // tl
    assert tl // CHUNK <= V7X_LANES
    block = (np.arange(tl) // CHUNK)[:, None] == np.arange(V7X_LANES)[None, :]
    consts = [(gate_w, True), (gate_b, True), (c["tril"], False), (c["causal4"], False), (c["kmask"], False),
              (c["vmask"], False), (c["bd"], False), (c["ones_blk"], False), (jnp.asarray(block, BF16), False),
              (norm_g, True)]
    row_spec = lambda w: pl.BlockSpec((tl, w), lambda b, l: (b * nl + l, 0))
    state_shape = (batch, GLA_HEADS, GLA_DK, GLA_DV)
    return pl.pallas_call(
        functools.partial(_gla_body, tl=tl),
        grid=(batch, nl),
        in_specs=[row_spec(GLA_COLS), row_spec(GLA_RANK)] + _specs(consts, layer),
        out_specs=[row_spec(GLA_W), pl.BlockSpec((1,) + state_shape[1:], lambda b, l: (b, 0, 0, 0))],
        out_shape=[jax.ShapeDtypeStruct((batch * seq_len, GLA_W), BF16), jax.ShapeDtypeStruct(state_shape, F32)],
        scratch_shapes=[pltpu.VMEM((GLA_KP, GLA_W), F32), pltpu.VMEM((tl, GLA_W), F32)],
        compiler_params=_params(),
        name="gla",
    )(gla2d, glr2d, *[a for a, _ in consts])


def _gla_lanes_head(h, *refs, seq_len, n_in):
    gla_ref, glr_ref, gw_ref, gb_ref, ng_ref, s_ref = refs[:6]
    o_ref, sout_ref, q_scr, k_scr, kd_scr, v_scr, g_scr, dec_scr, o_scr = refs[n_in:]
    nb = V7X_LANES
    steps = range(seq_len)
    q0, k0, v0, g0 = GLA_Q0, GLA_K0, GLA_V0, GLA_G0
    kw = slice(0, GLA_KW)

    def split(scr, t, tr, width):
        for hh in range(GLA_HEADS):
            scr[t, hh] = tr[hh * width:(hh + 1) * width]

    @pl.when(h == 0)
    def _():
        z = _dot_row_halves(glr_ref[...].astype(BF16), gw_ref[...]) + gb_ref[...]
        lg = (_log_sigmoid(z) * (1.0 / GLA_GATE_TEMP))[:, kw]
        cums = []
        for t in steps:
            lg_t = lg[t * nb:(t + 1) * nb]
            cums.append(lg_t if t == 0 else cums[-1] + lg_t)
        b_end = cums[-1]
        dec = _to_lanes(jnp.exp(b_end))
        for hh in range(GLA_HEADS):
            dec_scr[hh] = dec[hh * GLA_DK:(hh + 1) * GLA_DK]
        for t in steps:
            blk = gla_ref[t * nb:(t + 1) * nb, :].astype(F32)
            kf = blk[:, k0:k0 + GLA_KW]
            split(q_scr, t, _to_lanes(blk[:, q0:q0 + GLA_KW] * jnp.exp(cums[t])), GLA_DK)
            split(k_scr, t, _to_lanes(kf * jnp.exp(-cums[t])), GLA_DK)
            split(kd_scr, t, _to_lanes(kf * jnp.exp(b_end - cums[t])), GLA_DK)
            split(v_scr, t, _to_lanes(blk[:, v0:g0]), GLA_DV)
            split(g_scr, t, _to_lanes(blk[:, g0:g0 + GLA_W]), GLA_DV)

    q = [q_scr[t, h] for t in steps]
    k = [k_scr[t, h] for t in steps]
    kd = [kd_scr[t, h] for t in steps]
    v = [v_scr[t, h] for t in steps]
    dec = dec_scr[h]
    o = []
    for i in steps:
        acc = None
        for j in range(i + 1):
            s = jnp.sum(q[i] * k[j], axis=0, keepdims=True)
            acc = s * v[j] if acc is None else acc + s * v[j]
        o.append(acc)
    for i in range(GLA_DK):
        s_i = s_ref[i]
        new = s_i * dec[i:i + 1, :]
        for t in steps:
            o[t] = o[t] + q[t][i:i + 1, :] * s_i
            new = new + kd[t][i:i + 1, :] * v[t]
        sout_ref[i] = new
    for t in steps:
        ms = jnp.mean(o[t] * o[t], axis=0, keepdims=True)
        gate = g_scr[t, h]
        o_scr[t, h] = o[t] * lax.rsqrt(ms + EPS) * ng_ref[...] * gate * _sigmoid(gate)

    @pl.when(h == GLA_HEADS - 1)
    def _():
        for t in steps:
            o_ref[t * nb:(t + 1) * nb, :] = o_scr[t].reshape(GLA_W, nb).T.astype(BF16)


def _gla_lanes(gla2d, glr2d, gate_w, gate_b, norm_g, state, new_state, layer, seq_len):
    nb = V7X_LANES
    consts = [(gate_w, True), (gate_b, True), (norm_g, True)]
    k_tile = pltpu.VMEM((seq_len, GLA_HEADS, GLA_DK, nb), F32)
    v_tile = pltpu.VMEM((seq_len, GLA_HEADS, GLA_DV, nb), F32)
    block = (None, None, GLA_DK, GLA_DV, nb)
    prev, prev_specs, alias = _carried_output(new_state, n_before=6, out_index=1)
    slots = 1 if prev else state.shape[0] - layer
    return pl.pallas_call(
        functools.partial(_layer_slot_body, head_fn=_gla_lanes_head, n_in=6 + len(prev), seq_len=seq_len),
        grid=(slots, GLA_HEADS),
        in_specs=[_const_spec(gla2d.shape), _const_spec(glr2d.shape)] + _specs(consts, layer)
                 + [pl.BlockSpec(block, lambda g, h: (layer, h, 0, 0, 0))] + prev_specs,
        out_specs=[_const_spec((seq_len * nb, GLA_W)), pl.BlockSpec(block, lambda g, h: (layer + g, h, 0, 0, 0))],
        out_shape=[jax.ShapeDtypeStruct((seq_len * nb, GLA_W), BF16), jax.ShapeDtypeStruct(state.shape, F32)],
        scratch_shapes=[k_tile, k_tile, k_tile, v_tile, v_tile, pltpu.VMEM((GLA_HEADS, GLA_DK, nb), F32), v_tile],
        input_output_aliases=alias,
        compiler_params=_params(),
        name="gla_lanes",
    )(gla2d, glr2d, *[a for a, _ in consts], state, *prev)


def _s5_discretize(lam_re, lam_im, log_dt, b_re, b_im, c_re, c_im):
    lr = jnp.minimum(lam_re.astype(F32), -S5_MIN_NEG)
    li = lam_im.astype(F32)
    dt = jnp.exp(log_dt.astype(F32))[..., None]
    mag = jnp.exp(lr * dt)
    ar = mag * jnp.cos(li * dt)
    ai = mag * jnp.sin(li * dt)
    den = lr * lr + li * li
    cr = ((ar - 1.0) * lr + ai * li) / den
    ci = (ai * lr - (ar - 1.0) * li) / den
    b_re, b_im = b_re.astype(F32), b_im.astype(F32)
    bbar_re = cr[..., None] * b_re - ci[..., None] * b_im
    bbar_im = cr[..., None] * b_im + ci[..., None] * b_re
    eye = jnp.eye(S5_GROUPS, dtype=F32)
    depth = lr.shape[0]
    in_blk = lambda b: jnp.einsum("lgpi,gh->lgihp", b, eye).reshape(depth, S5_W, S5_LANES)
    out_blk = lambda c: jnp.einsum("lgop,gh->lgpho", c.astype(F32), eye).reshape(depth, S5_LANES, S5_W)
    bdb = jnp.concatenate([in_blk(bbar_re), in_blk(bbar_im)], axis=2).astype(BF16)
    bdc = jnp.concatenate([out_blk(c_re), -out_blk(c_im)], axis=1).astype(BF16)
    return ar.reshape(depth, 1, S5_LANES), ai.reshape(depth, 1, S5_LANES), bdb, bdc


def _s5_body(*refs, tl, batch, has_state):
    u_ref, ar_ref, ai_ref, bdb_ref, bdc_ref, d_ref, gw_ref, gb_ref = refs[:8]
    n_in = 10 if has_state else 8
    o_ref, hr_ref, hi_ref, x_scr, h_scr = refs[n_in:]

    @pl.when(pl.program_id(0) == 0)
    def _():
        if has_state:
            h_scr[0] = refs[8][...]
            h_scr[1] = refs[9][...]
        else:
            h_scr[...] = jnp.zeros_like(h_scr)

    ar = jnp.broadcast_to(ar_ref[...], (batch, S5_LANES))
    ai = jnp.broadcast_to(ai_ref[...], (batch, S5_LANES))
    hr, hi = h_scr[0], h_scr[1]
    sub = min(tl, S5_SUB_STEPS)
    n_rows = sub * batch
    for s in range(tl // sub):
        rows = slice(s * n_rows, (s + 1) * n_rows)
        u = u_ref[rows, :]
        x_scr[rows, :] = _dot_row_halves(u, bdb_ref[...])
        for t in range(s * sub, (s + 1) * sub):
            r = slice(t * batch, (t + 1) * batch)
            hr, hi = (ar * hr - ai * hi + x_scr[r, 0:S5_LANES],
                      ar * hi + ai * hr + x_scr[r, S5_LANES:2 * S5_LANES])
            x_scr[r, 0:S5_LANES] = hr
            x_scr[r, S5_LANES:2 * S5_LANES] = hi
        y = _dot_row_halves(x_scr[rows, :].astype(BF16), bdc_ref[...]) + d_ref[...] * u.astype(F32)
        y = _gelu_tanh(y)
        o_ref[rows, :] = (y * _sigmoid(_dot_row_halves(y.astype(BF16), gw_ref[...]) + gb_ref[...])).astype(BF16)
    h_scr[0] = hr
    h_scr[1] = hi
    hr_ref[...] = hr
    hi_ref[...] = hi


def _s5(u_tm, disc, d, glu_w_bf, glu_b, layer, state, batch, seq_len):
    tl = min(S5_TIME_TILE, seq_len)
    has_state = state is not None
    consts = list(disc) + [d, glu_w_bf, glu_b]
    h_spec = pl.BlockSpec((batch, S5_LANES), lambda i: (0, 0))
    in_specs = ([pl.BlockSpec((tl * batch, S5_W), lambda i: (i, 0))]
                + [_layer_spec(a.shape, layer) for a in consts])
    args = [u_tm] + consts
    if has_state:
        in_specs += [h_spec, h_spec]
        args += list(state)
    return pl.pallas_call(
        functools.partial(_s5_body, tl=tl, batch=batch, has_state=has_state),
        grid=(seq_len // tl,),
        in_specs=in_specs,
        out_specs=[pl.BlockSpec((tl * batch, S5_W), lambda i: (i, 0)), h_spec, h_spec],
        out_shape=[jax.ShapeDtypeStruct((seq_len * batch, S5_W), BF16),
                   jax.ShapeDtypeStruct((batch, S5_LANES), F32), jax.ShapeDtypeStruct((batch, S5_LANES), F32)],
        scratch_shapes=[pltpu.VMEM((tl * batch, 2 * S5_LANES), F32), pltpu.VMEM((2, batch, S5_LANES), F32)],
        compiler_params=_params(),
        name="s5",
    )(*args)


def _ffn_body(*refs, tm, time_major_batch, final):
    (x_ref, oret_ref, os5_ref, ogla_ref, wout_ref, gffn_ref, win_ref, cw_ref, cb_ref, wo_ref) = refs[:10]
    n = 10
    gfin_ref = None
    if final:
        gfin_ref, n = refs[n], n + 1
    if time_major_batch is not None:
        past_ref, n = refs[n], n + 1
    out_ref, conv_ref, x1_scr, h_scr, carry_scr, act_scr = refs[n:]
    pad = V7X_SUBLANES
    nb = time_major_batch

    if nb is None:
        @pl.when(pl.program_id(1) == 0)
        def _():
            carry_scr[...] = jnp.zeros_like(carry_scr)

    mix = jnp.concatenate([oret_ref[...], os5_ref[...], ogla_ref[...]], axis=1)
    x1_scr[...] = x_ref[...] + _dot(mix, wout_ref[...])
    h_scr[...] = _rmsnorm_rows(x1_scr[...], gffn_ref[...]).astype(BF16)
    h = h_scr[...]

    for c in range(D_FF // ACT_COLS):
        cols = slice(c * ACT_COLS, (c + 1) * ACT_COLS)
        a_c = _dot(h, win_ref[:, c * ACT_COLS:(c + 1) * ACT_COLS])
        gate_c = _dot(h, win_ref[:, D_FF + c * ACT_COLS:D_FF + (c + 1) * ACT_COLS])
        if nb is None:
            ext = jnp.concatenate([carry_scr[:, cols], a_c], axis=0)
            prev1 = ext[pad - 1:pad - 1 + tm]
            prev2 = ext[pad - 2:pad - 2 + tm]
            carry_scr[:, cols] = a_c[tm - pad:tm]
        else:
            prev1 = jnp.concatenate([past_ref[1, :, cols], a_c[0:tm - nb]], axis=0)
            prev2 = jnp.concatenate([past_ref[0, :, cols], past_ref[1, :, cols], a_c[0:tm - 2 * nb]], axis=0)
            conv_ref[:, cols] = a_c[tm - (CONV_W - 1) * nb:tm]
        conv = (cb_ref[:, cols] + prev2 * cw_ref[0:1, cols] + prev1 * cw_ref[1:2, cols]
                + a_c * cw_ref[2:3, cols])
        act_scr[:, cols] = (_gelu_tanh(conv) * gate_c).astype(BF16)

    x2 = x1_scr[...] + _dot(act_scr[...], wo_ref[...])
    out_ref[...] = _rmsnorm_rows(x2, gfin_ref[...]) if final else x2
    if nb is None:
        conv_ref[0] = carry_scr[pad - (CONV_W - 1):pad, :]


def _ffn(x2d, o_ret, o_s5, o_gla, w_out_bf, norm_g, w_in_bf, conv_w, conv_b, w_o_bf, layer, final_g, conv_past,
         batch, seq_len):
    rows = batch * seq_len
    time_major = conv_past is not None
    tm = rows if time_major else min(ROW_TILE, seq_len)
    final = final_g is not None
    consts = [w_out_bf, norm_g, w_in_bf, conv_w, conv_b, w_o_bf]
    stacked = [True] * len(consts)
    big = [True, False, True, False, False, True]
    if final:
        consts.append(final_g.reshape(1, D_MODEL))
        stacked.append(False)
        big.append(False)
    if time_major:
        assert seq_len >= CONV_W - 1 and batch % V7X_SUBLANES == 0
        consts.append(conv_past)
        stacked.append(False)
        big.append(False)
        grid = (1,)
        imap = lambda i: (0, 0)
        conv_rows = (CONV_W - 1) * batch
        conv_spec = pl.BlockSpec((conv_rows, D_FF), imap)
        conv_shape = jax.ShapeDtypeStruct((conv_rows, D_FF), F32)
    else:
        assert seq_len % tm == 0
        nl = seq_len // tm
        grid = (batch, nl)
        imap = lambda b, l: (b * nl + l, 0)
        conv_spec = pl.BlockSpec((1, CONV_W - 1, D_FF), lambda b, l: (b, 0, 0))
        conv_shape = jax.ShapeDtypeStruct((batch, CONV_W - 1, D_FF), F32)
    row_spec = lambda w: pl.BlockSpec((tm, w), imap)
    in_specs = ([row_spec(D_MODEL), row_spec(RET_W), row_spec(S5_W), row_spec(GLA_W)]
                + [_layer_spec(a.shape, layer, b) if s else _const_spec(a.shape)
                   for a, s, b in zip(consts, stacked, big)])
    return pl.pallas_call(
        functools.partial(_ffn_body, tm=tm, time_major_batch=batch if time_major else None, final=final),
        grid=grid,
        in_specs=in_specs,
        out_specs=[row_spec(D_MODEL), conv_spec],
        out_shape=[jax.ShapeDtypeStruct((rows, D_MODEL), F32), conv_shape],
        scratch_shapes=[pltpu.VMEM((tm, D_MODEL), F32), pltpu.VMEM((tm, D_MODEL), BF16),
                        pltpu.VMEM((V7X_SUBLANES, D_FF), F32), pltpu.VMEM((tm, D_FF), BF16)],
        compiler_params=_params(),
        name="ffn",
    )(x2d, o_ret, o_s5, o_gla, *consts)


def _rotary_tables(first_pos, n_pos, repeat=1):
    half = RET_HD // 2
    inv = ROPE_BASE ** (-(np.arange(half, dtype=np.float64) / half))
    ang = (first_pos + np.arange(n_pos)).astype(np.float64)[:, None] * inv[None, :]
    cos, sin = np.cos(ang), np.sin(ang)
    expand = lambda a, b: np.repeat(np.tile(np.concatenate([a, b], axis=1), (1, RET_HEADS)), repeat, axis=0)
    return jnp.asarray(expand(cos, cos), F32), jnp.asarray(expand(-sin, sin), F32)


def _swap_major(a2d, outer, inner):
    return a2d.reshape(outer, inner, -1).transpose(1, 0, 2).reshape(outer * inner, -1)


def _run_prompt_group(x, prm, final_g):
    batch, seq_len, _ = x.shape
    rows = batch * seq_len
    depth = prm["w_in"].shape[0]
    tm = min(MIX_IN_TILE, seq_len)
    cos_tab, sin_tab = _rotary_tables(0, seq_len)
    x2d = x.reshape(rows, D_MODEL)
    outs = []
    for li in range(depth):
        ret, su, gla, glr = _mix_in(x2d, prm["norm_mix_g"], prm["w_in"], li, cos_tab, sin_tab, tm)
        o_ret, s_ret = _retention(ret, prm["ret_ln_g"], prm["ret_ln_b"], li, batch, seq_len)
        o_s5, s5r, s5i = _s5(_swap_major(su, batch, seq_len), prm["s5_disc"], prm["s5_d"], prm["s5_glu_w"],
                             prm["s5_glu_b"], li, None, batch, seq_len)
        o_s5 = _swap_major(o_s5, seq_len, batch)
        o_gla, s_gla = _gla(gla, glr, prm["gla_gate_w"], prm["gla_gate_b"], prm["gla_norm_g"], li, batch, seq_len)
        x2d, conv_new = _ffn(x2d, o_ret, o_s5, o_gla, prm["w_out"], prm["norm_ffn_g"], prm["ffn_w_in"],
                             prm["ffn_conv_w"], prm["ffn_conv_b"], prm["ffn_w_out"], li,
                             final_g if li == depth - 1 else None, None, batch, seq_len)
        s5_shape = (batch, S5_GROUPS, S5_STATE)
        outs.append((s_ret, s5r.reshape(s5_shape), s5i.reshape(s5_shape), s_gla, conv_new))
    return [x2d.reshape(batch, seq_len, D_MODEL)] + [jnp.stack([o[i] for o in outs]) for i in range(5)]


def _run_sample_group(x, past_len, states, prm, final_g):
    batch, seq_len, _ = x.shape
    assert batch == V7X_LANES
    rows = batch * seq_len
    depth = prm["w_in"].shape[0]
    cos_tab, sin_tab = _rotary_tables(past_len, seq_len, repeat=batch)
    x2d = x.transpose(1, 0, 2).reshape(rows, D_MODEL)
    ret_state = states["ret"].transpose(0, 2, 3, 4, 1)
    gla_state = states["gla"].transpose(0, 2, 3, 4, 1)
    conv_state = states["conv"].transpose(0, 2, 1, 3)
    s5_flat = lambda s: s.reshape(depth, batch, S5_LANES)
    s5r_state, s5i_state = s5_flat(states["s5r"]), s5_flat(states["s5i"])
    outs = []
    s_ret = s_gla = None
    for li in range(depth):
        ret, su, gla, glr = _mix_in(x2d, prm["norm_mix_g"], prm["w_in"], li, cos_tab, sin_tab, rows)
        o_ret, s_ret = _retention_lanes(ret, prm["ret_ln_g_lanes"], prm["ret_ln_b_lanes"], ret_state, s_ret, li,
                                        seq_len)
        o_s5, s5r, s5i = _s5(su, prm["s5_disc"], prm["s5_d"], prm["s5_glu_w"], prm["s5_glu_b"], li,
                             (s5r_state[li], s5i_state[li]), batch, seq_len)
        o_gla, s_gla = _gla_lanes(gla, glr, prm["gla_gate_w"], prm["gla_gate_b"], prm["gla_norm_g_lanes"], gla_state,
                                  s_gla, li, seq_len)
        x2d, conv_new = _ffn(x2d, o_ret, o_s5, o_gla, prm["w_out"], prm["norm_ffn_g"], prm["ffn_w_in"],
                             prm["ffn_conv_w"], prm["ffn_conv_b"], prm["ffn_w_out"], li,
                             final_g if li == depth - 1 else None, conv_state[li], batch, seq_len)
        outs.append((s5r, s5i, conv_new.reshape(CONV_W - 1, batch, D_FF)))
    s5r, s5i, conv_new = (jnp.stack([o[i] for o in outs]) for i in range(3))
    s5_shape = (depth, batch, S5_GROUPS, S5_STATE)
    return [x2d.reshape(seq_len, batch, D_MODEL).transpose(1, 0, 2), s_ret.transpose(0, 4, 1, 2, 3),
            s5r.reshape(s5_shape), s5i.reshape(s5_shape), s_gla.transpose(0, 4, 1, 2, 3),
            conv_new.transpose(0, 2, 1, 3)]


def _prepare_params(norm_mix_g, w_in, ret_norm_g, ret_norm_b, s5_lambda_re, s5_lambda_im, s5_log_dt, s5_b_re, s5_b_im,
                    s5_c_re, s5_c_im, s5_d, s5_glu_w, s5_glu_b, gla_gate_w, gla_gate_b, gla_norm_g, w_out,
                    norm_ffn_g, ffn_w_in, ffn_conv_w, ffn_conv_b, ffn_w_out):
    depth = w_in.shape[0]
    nb = V7X_LANES
    row = lambda a: a.astype(F32).reshape(depth, 1, a.shape[-1])
    ret_lanes = lambda a: jnp.broadcast_to(a.astype(F32).reshape(depth, RET_HEADS, RET_HD, 1),
                                           (depth, RET_HEADS, RET_HD, nb))
    return dict(
        norm_mix_g=row(norm_mix_g), w_in=_mix_weights(w_in),
        ret_ln_g=row(ret_norm_g), ret_ln_b=row(ret_norm_b),
        ret_ln_g_lanes=ret_lanes(ret_norm_g), ret_ln_b_lanes=ret_lanes(ret_norm_b),
        s5_disc=_s5_discretize(s5_lambda_re, s5_lambda_im, s5_log_dt, s5_b_re, s5_b_im, s5_c_re, s5_c_im),
        s5_d=row(s5_d), s5_glu_w=s5_glu_w.astype(BF16), s5_glu_b=row(s5_glu_b),
        gla_gate_w=_pad_gate(gla_gate_w, BF16), gla_gate_b=_pad_gate(row(gla_gate_b), F32),
        gla_norm_g=row(jnp.tile(gla_norm_g, (1, GLA_HEADS))),
        gla_norm_g_lanes=jnp.broadcast_to(gla_norm_g.astype(F32).reshape(depth, GLA_DV, 1), (depth, GLA_DV, nb)),
        w_out=w_out.astype(BF16), norm_ffn_g=row(norm_ffn_g), ffn_w_in=ffn_w_in.astype(BF16),
        ffn_conv_w=jnp.pad(ffn_conv_w.astype(F32), ((0, 0), (0, V7X_SUBLANES - CONV_W), (0, 0))),
        ffn_conv_b=row(ffn_conv_b), ffn_w_out=ffn_w_out.astype(BF16))


def kernel(x_prompt, x_sample, state_ret, state_s5_re, state_s5_im, state_gla, state_ffn_conv, norm_mix_g, w_in, ret_norm_g, ret_norm_b, s5_lambda_re, s5_lambda_im, s5_log_dt, s5_b_re, s5_b_im, s5_c_re, s5_c_im, s5_d, s5_glu_w, s5_glu_b, gla_gate_w, gla_gate_b, gla_norm_g, w_out, norm_ffn_g, ffn_w_in, ffn_conv_w, ffn_conv_b, ffn_w_out, norm_final_g):
    prm = _prepare_params(norm_mix_g, w_in, ret_norm_g, ret_norm_b, s5_lambda_re, s5_lambda_im, s5_log_dt, s5_b_re,
                          s5_b_im, s5_c_re, s5_c_im, s5_d, s5_glu_w, s5_glu_b, gla_gate_w, gla_gate_b, gla_norm_g,
                          w_out, norm_ffn_g, ffn_w_in, ffn_conv_w, ffn_conv_b, ffn_w_out)
    sample_states = dict(ret=state_ret, s5r=state_s5_re, s5i=state_s5_im, gla=state_gla, conv=state_ffn_conv)
    yp, ret_p, s5r_p, s5i_p, gla_p, conv_p = _run_prompt_group(x_prompt, prm, norm_final_g)
    ys, ret_s, s5r_s, s5i_s, gla_s, conv_s = _run_sample_group(x_sample, PAST_LEN, sample_states, prm, norm_final_g)
    return (yp, ys, ret_p, ret_s, s5r_p, s5r_s, s5i_p, s5i_s, gla_p, gla_s, conv_p, conv_s)
```

```python
import functools

import numpy as np
import jax
import jax.numpy as jnp
from jax import lax
from jax.experimental import pallas as pl
from jax.experimental.pallas import tpu as pltpu

F32, BF16 = jnp.float32, jnp.bfloat16

D_MODEL = 1024
RET_W, S5_W, GLA_W = 384, 256, 384
RET_HEADS, RET_HD = 6, 64
RET_PAIRS = RET_HEADS // 2
S5_GROUPS, S5_CH, S5_STATE = 16, 16, 64
S5_LANES = S5_GROUPS * S5_STATE
S5_MIN_NEG = 1e-4
GLA_HEADS, GLA_DK, GLA_DV, GLA_RANK = 4, 48, 96, 16
GLA_KW = GLA_HEADS * GLA_DK
GLA_GATE_TEMP = 16.0
D_FF = 2816
CONV_W = 3
ROPE_BASE = 10000.0
CHUNK = 64
EPS = 1e-6
PAST_LEN = 16384
IN_COLS = 4 * RET_W + S5_W + 2 * GLA_KW + 2 * GLA_W + GLA_RANK
COL_RET, COL_S5, COL_GLA, COL_LR = 0, 4 * RET_W, 4 * RET_W + S5_W, IN_COLS - GLA_RANK
GLA_KP = 256
GLA_Q0, GLA_LR0, GLA_K0, GLA_V0, GLA_G0 = 0, GLA_KW, GLA_KP, 2 * GLA_KP, 2 * GLA_KP + GLA_W
GLA_COLS = 2 * GLA_KP + 2 * GLA_W
MIX_COLS = COL_GLA + GLA_COLS

V7X_SUBLANES = 8
V7X_LANES = 128
VMEM_LIMIT = 58 * 1024 * 1024

ROW_TILE = 512
MIX_IN_TILE = 1024
MIXER_TILE = 2048
S5_TIME_TILE = 256
S5_SUB_STEPS = 64
ACT_COLS = 256


def _dot(a, b):
    return jnp.dot(a, b, preferred_element_type=F32)


def _dot_nt(a, b):
    return lax.dot_general(a, b, (((1,), (1,)), ((), ())), preferred_element_type=F32)


def _dot_tn(a, b):
    return lax.dot_general(a, b, (((0,), (0,)), ((), ())), preferred_element_type=F32)


def _hi_lo(x):
    hi = x.astype(BF16)
    return hi, (x - hi.astype(F32)).astype(BF16)


def _dot_exact_lhs(m, x):
    hi, lo = _hi_lo(x)
    return _dot(m, hi) + _dot(m, lo)


def _dot_row_halves(a, w):
    half = a.shape[0] // 2
    return jnp.concatenate([_dot(a[:half], w), _dot(a[half:], w)], axis=0)


def _pad_gate(a, dtype):
    return jnp.pad(a.astype(dtype), [(0, 0)] * (a.ndim - 1) + [(0, GLA_KP - GLA_KW)])


def _sigmoid(x):
    return 1.0 / (1.0 + jnp.exp(-x))


def _log_sigmoid(z):
    return jnp.minimum(z, 0.0) - jnp.log(1.0 + jnp.exp(-jnp.abs(z)))


def _gelu_tanh(x):
    return 0.5 * x * (1.0 + jnp.tanh(0.7978845608028654 * (x + 0.044715 * (x * x * x))))


def _rmsnorm_rows(x, g):
    return x * lax.rsqrt(jnp.mean(x * x, axis=-1, keepdims=True) + EPS) * g


def _const_spec(shape):
    nd = len(shape)
    return pl.BlockSpec(shape, lambda *_: (0,) * nd)


def _layer_spec(stacked_shape, layer, single_buffer=False):
    nd = len(stacked_shape) - 1
    mode = dict(pipeline_mode=pl.Buffered(1)) if single_buffer else {}
    return pl.BlockSpec((None,) + tuple(stacked_shape[1:]), lambda *_: (layer,) + (0,) * nd, **mode)


def _specs(arrays, layer):
    return [_layer_spec(a.shape, layer) if stacked else _const_spec(a.shape) for a, stacked in arrays]


def _carried_output(buffer, n_before, out_index):
    if buffer is None:
        return [], [], {}
    return [buffer], [pl.BlockSpec(memory_space=pl.ANY)], {n_before: out_index}


def _held_head(slot, head, heads):
    return jnp.where(slot == 0, head, heads - 1)


def _layer_slot_body(*refs, head_fn, n_in, **kw):
    @pl.when(pl.program_id(0) == 0)
    def _():
        head_fn(pl.program_id(1), *refs, n_in=n_in, **kw)

    @pl.when(pl.program_id(0) > 0)
    def _():
        refs[n_in + 1][...] = jnp.zeros_like(refs[n_in + 1])


def _params():
    return pltpu.CompilerParams(vmem_limit_bytes=VMEM_LIMIT)


def _to_lanes(x):
    w = x.shape[1]
    pad = -w % V7X_LANES
    if pad:
        x = jnp.concatenate([x, jnp.zeros((x.shape[0], pad), F32)], axis=1)
    return x.T[0:w]


def _mix_in_body(x_ref, g_ref, w_ref, cos_ref, sin_ref, ret_ref, su_ref, gla_ref, glr_ref):
    h = _rmsnorm_rows(x_ref[...], g_ref[...]).astype(BF16)
    ret = _dot(h, w_ref[:, COL_RET:COL_S5])
    cos = cos_ref[...]
    sin = sin_ref[...]
    lane = lax.broadcasted_iota(jnp.int32, cos.shape, 1)
    first_half = (lane & (RET_HD // 2)) == 0

    def rotary(z):
        swapped = jnp.where(first_half, pltpu.roll(z, RET_W - RET_HD // 2, 1), pltpu.roll(z, RET_HD // 2, 1))
        return z * cos + swapped * sin

    ret_ref[:, 0:RET_W] = rotary(ret[:, 0:RET_W]).astype(BF16)
    ret_ref[:, RET_W:2 * RET_W] = (rotary(ret[:, RET_W:2 * RET_W]) * RET_HD ** -0.5).astype(BF16)
    ret_ref[:, 2 * RET_W:] = ret[:, 2 * RET_W:].astype(BF16)
    su_ref[...] = _dot(h, w_ref[:, COL_S5:COL_GLA]).astype(BF16)
    gla = _dot(h, w_ref[:, COL_GLA:MIX_COLS])
    glane = lax.broadcasted_iota(jnp.int32, (1, GLA_COLS), 1)
    scale = jnp.where(glane < GLA_KW, GLA_DK ** -0.5, jnp.where(glane < GLA_KP, 0.0, 1.0))
    gla_ref[...] = (gla * scale).astype(BF16)
    glr_ref[...] = gla[:, GLA_LR0:GLA_LR0 + GLA_RANK]


def _mix_weights(w_in):
    w = w_in.astype(BF16)
    k0, v0 = COL_GLA + GLA_KW, COL_GLA + 2 * GLA_KW
    zeros = lambda n: jnp.zeros(w.shape[:2] + (n,), BF16)
    return jnp.concatenate([w[..., :k0], w[..., COL_LR:], zeros(GLA_KP - GLA_KW - GLA_RANK),
                            w[..., k0:v0], zeros(GLA_KP - GLA_KW), w[..., v0:COL_LR]], axis=-1)


def _mix_in(x2d, norm_g, w_in_bf, layer, cos_tab, sin_tab, tm):
    rows = x2d.shape[0]
    n_tab = cos_tab.shape[0] // tm
    row_spec = lambda w: pl.BlockSpec((tm, w), lambda i: (i, 0))
    tab_spec = pl.BlockSpec((tm, RET_W), lambda i: (i % n_tab, 0))
    return pl.pallas_call(
        _mix_in_body,
        grid=(rows // tm,),
        in_specs=[row_spec(D_MODEL), _layer_spec(norm_g.shape, layer), _layer_spec(w_in_bf.shape, layer, True),
                  tab_spec, tab_spec],
        out_specs=[row_spec(4 * RET_W), row_spec(S5_W), row_spec(GLA_COLS), row_spec(GLA_RANK)],
        out_shape=[jax.ShapeDtypeStruct((rows, 4 * RET_W), BF16), jax.ShapeDtypeStruct((rows, S5_W), BF16),
                   jax.ShapeDtypeStruct((rows, GLA_COLS), BF16), jax.ShapeDtypeStruct((rows, GLA_RANK), F32)],
        compiler_params=_params(),
        name="mix_in",
    )(x2d, norm_g, w_in_bf, cos_tab, sin_tab)


def _retention_gammas():
    return 1.0 - 2.0 ** (-5.0 - np.arange(RET_HEADS))


def _retention_consts():
    t = np.arange(CHUNK)
    gam = _retention_gammas()
    causal = t[:, None] >= t[None, :]
    diff = np.maximum(t[:, None] - t[None, :], 0)
    dmask = np.zeros((RET_PAIRS, CHUNK, 2 * CHUNK))
    cdec = np.zeros((RET_PAIRS, 2 * RET_HD, 2 * RET_HD))
    for p in range(RET_PAIRS):
        for s in range(2):
            g = gam[2 * p + s]
            dmask[p, :, s * CHUNK:(s + 1) * CHUNK] = np.where(causal, g ** diff, 0.0)
            cdec[p, s * RET_HD:(s + 1) * RET_HD, s * RET_HD:(s + 1) * RET_HD] = g ** CHUNK
    lane_gam = np.repeat(gam, RET_HD)[None, :]
    qdec = lane_gam ** (t[:, None] + 1.0)
    kdec = lane_gam ** (CHUNK - 1.0 - t[:, None])
    bd = (cdec[0] > 0).astype(np.float32)
    head_mask = np.stack([np.arange(2 * RET_HD) < RET_HD, np.arange(2 * RET_HD) >= RET_HD]).astype(np.float32)
    ones_blk = np.kron(np.eye(RET_HEADS), np.ones((RET_HD, RET_HD)))
    f = lambda a: jnp.asarray(a, F32)
    return dict(dmask=f(dmask), cdec=f(cdec), qdec=qdec, kdec=kdec, bd=f(bd),
                head_mask=jnp.asarray(head_mask, BF16), ones_blk=jnp.asarray(ones_blk, BF16))


def _retention_body(ret_ref, dmask_ref, cdec_ref, qdec_ref, kdec_ref, bd_ref, hm_ref, ones_ref, lng_ref, lnb_ref,
                    o_ref, sout_ref, s_scr, o_scr, *, tl):
    @pl.when(pl.program_id(1) == 0)
    def _():
        s_scr[...] = jnp.zeros_like(s_scr)

    bd = bd_ref[...]
    m0 = hm_ref[0:1, :]
    m1 = hm_ref[1:2, :]
    q = ret_ref[:, 0:RET_W]
    k = ret_ref[:, RET_W:2 * RET_W]
    v = ret_ref[:, 2 * RET_W:3 * RET_W]
    q_start = (q.astype(F32) * qdec_ref[...]).astype(BF16)
    k_end = (k.astype(F32) * kdec_ref[...]).astype(BF16)
    blocks = [(n, p) for n in range(tl // CHUNK) for p in range(RET_PAIRS)]

    def part(a, n, p):
        return a[n * CHUNK:(n + 1) * CHUNK, 2 * RET_HD * p:2 * RET_HD * (p + 1)]

    scores, o, kv = {}, {}, {}
    for n, p in blocks:
        kp = part(k, n, p)
        kk = jnp.concatenate([kp * m0, kp * m1], axis=0)
        scores[n, p] = (_dot_nt(part(q, n, p), kk) * dmask_ref[p]).astype(BF16)
    for n, p in blocks:
        vp = part(v, n, p)
        o[n, p] = _dot(scores[n, p], jnp.concatenate([vp * m0, vp * m1], axis=0))
    for n, p in blocks:
        kv[n, p] = _dot_tn(part(k_end, n, p), part(v, n, p)) * bd
    for n, p in blocks:
        s_prev = s_scr[p]
        o_scr[n * CHUNK:(n + 1) * CHUNK, 2 * RET_HD * p:2 * RET_HD * (p + 1)] = (
            o[n, p] + _dot(part(q_start, n, p), s_prev.astype(BF16)))
        s_scr[p] = s_prev * cdec_ref[p] + kv[n, p]

    o_all = o_scr[...]
    ones_blk = ones_ref[...]
    mu = _dot_row_halves(o_all.astype(BF16), ones_blk) * (1.0 / RET_HD)
    cen = o_all - mu
    var = _dot_row_halves((cen * cen).astype(BF16), ones_blk) * (1.0 / RET_HD)
    y = cen * lax.rsqrt(var + EPS) * lng_ref[...] + lnb_ref[...]
    gate = ret_ref[:, 3 * RET_W:4 * RET_W].astype(F32)
    o_ref[...] = (y * gate * _sigmoid(gate)).astype(BF16)
    for p in range(RET_PAIRS):
        s = s_scr[p]
        sout_ref[0, 2 * p] = s[0:RET_HD, 0:RET_HD]
        sout_ref[0, 2 * p + 1] = s[RET_HD:2 * RET_HD, RET_HD:2 * RET_HD]


def _retention(ret2d, ln_g, ln_b, layer, batch, seq_len):
    assert seq_len % CHUNK == 0
    c = _retention_consts()
    tl = min(MIXER_TILE, seq_len)
    nl = seq_len // tl
    tile_rows = lambda a: jnp.asarray(np.tile(a, (tl // CHUNK, 1)), F32)
    consts = [(c["dmask"], False), (c["cdec"], False), (tile_rows(c["qdec"]), False), (tile_rows(c["kdec"]), False),
              (c["bd"], False), (c["head_mask"], False), (c["ones_blk"], False), (ln_g, True), (ln_b, True)]
    state_shape = (batch, RET_HEADS, RET_HD, RET_HD)
    return pl.pallas_call(
        functools.partial(_retention_body, tl=tl),
        grid=(batch, nl),
        in_specs=[pl.BlockSpec((tl, 4 * RET_W), lambda b, l: (b * nl + l, 0))] + _specs(consts, layer),
        out_specs=[pl.BlockSpec((tl, RET_W), lambda b, l: (b * nl + l, 0)),
                   pl.BlockSpec((1,) + state_shape[1:], lambda b, l: (b, 0, 0, 0))],
        out_shape=[jax.ShapeDtypeStruct((batch * seq_len, RET_W), BF16), jax.ShapeDtypeStruct(state_shape, F32)],
        scratch_shapes=[pltpu.VMEM((RET_PAIRS, 2 * RET_HD, 2 * RET_HD), F32), pltpu.VMEM((tl, RET_W), F32)],
        compiler_params=_params(),
        name="retention",
    )(ret2d, *[a for a, _ in consts])


def _retention_lanes_head(h, *refs, seq_len, n_in):
    ret_ref, tab_ref, lng_ref, lnb_ref, s_ref = refs[:5]
    o_ref, sout_ref, q_scr, k_scr, v_scr, g_scr, o_scr = refs[n_in:]
    nb = V7X_LANES
    steps = range(seq_len)

    @pl.when(h == 0)
    def _():
        for t in steps:
            blk = ret_ref[t * nb:(t + 1) * nb, :].astype(F32)
            for scr, off in ((q_scr, 0), (k_scr, RET_W), (v_scr, 2 * RET_W), (g_scr, 3 * RET_W)):
                tr = _to_lanes(blk[:, off:off + RET_W])
                for hh in range(RET_HEADS):
                    scr[t, hh] = tr[hh * RET_HD:(hh + 1) * RET_HD]

    tab = tab_ref[h]
    row = lambda r: tab[r:r + 1, :]
    q = [q_scr[t, h] for t in steps]
    k = [k_scr[t, h] for t in steps]
    v = [v_scr[t, h] for t in steps]
    o = []
    for i in steps:
        acc = None
        for j in range(i + 1):
            s = jnp.sum(q[i] * k[j], axis=0, keepdims=True)
            if i > j:
                s = s * row(i - j - 1)
            acc = s * v[j] if acc is None else acc + s * v[j]
        o.append(acc)
    q_start = [q[t] * row(t) for t in steps]
    k_end = [k[t] * row(seq_len + t) for t in steps]
    chunk_decay = row(2 * seq_len)
    for d in range(RET_HD):
        s_d = s_ref[d]
        new = s_d * chunk_decay
        for t in steps:
            o[t] = o[t] + q_start[t][d:d + 1, :] * s_d
            new = new + k_end[t][d:d + 1, :] * v[t]
        sout_ref[d] = new
    for t in steps:
        mu = jnp.mean(o[t], axis=0, keepdims=True)
        cen = o[t] - mu
        var = jnp.mean(cen * cen, axis=0, keepdims=True)
        gate = g_scr[t, h]
        o_scr[t, h] = (cen * lax.rsqrt(var + EPS) * lng_ref[h] + lnb_ref[h]) * gate * _sigmoid(gate)

    @pl.when(h == RET_HEADS - 1)
    def _():
        for t in steps:
            o_ref[t * nb:(t + 1) * nb, :] = o_scr[t].reshape(RET_W, nb).T.astype(BF16)


def _retention_lanes(ret2d, ln_g, ln_b, state, new_state, layer, seq_len):
    nb = V7X_LANES
    gam = _retention_gammas()[:, None]
    t = np.arange(seq_len)[None, :]
    rows = np.concatenate([gam ** (t + 1.0), gam ** (seq_len - 1.0 - t), gam ** (seq_len + 0.0 * t[:, :1])], axis=1)
    n_rows = -(-rows.shape[1] // V7X_SUBLANES) * V7X_SUBLANES
    tab = np.zeros((RET_HEADS, n_rows, nb))
    tab[:, :rows.shape[1], :] = rows[:, :, None]
    consts = [(jnp.asarray(tab, F32), False), (ln_g, True), (ln_b, True)]
    tile = pltpu.VMEM((seq_len, RET_HEADS, RET_HD, nb), F32)
    block = (None, None, RET_HD, RET_HD, nb)
    prev, prev_specs, alias = _carried_output(new_state, n_before=5, out_index=1)
    slots = 1 if prev else state.shape[0] - layer
    return pl.pallas_call(
        functools.partial(_layer_slot_body, head_fn=_retention_lanes_head, n_in=5 + len(prev), seq_len=seq_len),
        grid=(slots, RET_HEADS),
        in_specs=[_const_spec(ret2d.shape)] + _specs(consts, layer)
                 + [pl.BlockSpec(block, lambda g, h: (layer, _held_head(g, h, RET_HEADS), 0, 0, 0))] + prev_specs,
        out_specs=[_const_spec((seq_len * nb, RET_W)), pl.BlockSpec(block, lambda g, h: (layer + g, h, 0, 0, 0))],
        out_shape=[jax.ShapeDtypeStruct((seq_len * nb, RET_W), BF16), jax.ShapeDtypeStruct(state.shape, F32)],
        scratch_shapes=[tile] * 5,
        input_output_aliases=alias,
        compiler_params=_params(),
        name="retention_lanes",
    )(ret2d, *[a for a, _ in consts], state, *prev)


def _gla_consts():
    t = np.arange(CHUNK)
    causal = t[:, None] >= t[None, :]
    pad = GLA_KP - GLA_KW
    kmask = np.pad(np.kron(np.eye(GLA_HEADS), np.ones((1, GLA_DK))), ((0, 0), (0, pad)))
    vmask = np.kron(np.eye(GLA_HEADS), np.ones((1, GLA_DV)))
    bd = np.pad(np.kron(np.eye(GLA_HEADS), np.ones((GLA_DK, GLA_DV))), ((0, pad), (0, 0)))
    ones_blk = np.kron(np.eye(GLA_HEADS), np.ones((GLA_DV, GLA_DV)))
    b = lambda a: jnp.asarray(a, BF16)
    return dict(tril=b(causal), causal4=jnp.asarray(np.tile(causal, (1, GLA_HEADS)), F32), kmask=b(kmask),
                vmask=b(vmask), bd=jnp.asarray(bd, F32), ones_blk=b(ones_blk))


def _gla_body(gla_ref, glr_ref, gw_ref, gb_ref, tril_ref, causal_ref, kmask_ref, vmask_ref, bd_ref, ones_blk_ref,
              block_ref, ng_ref, o_ref, sout_ref, s_scr, o_scr, *, tl):
    @pl.when(pl.program_id(1) == 0)
    def _():
        s_scr[...] = jnp.zeros_like(s_scr)

    bd = bd_ref[...]
    z = _dot_row_halves(glr_ref[...].astype(BF16), gw_ref[...]) + gb_ref[...]
    lg = _log_sigmoid(z) * (1.0 / GLA_GATE_TEMP)
    q0, k0, v0, g0 = GLA_Q0, GLA_K0, GLA_V0, GLA_G0
    chunks = range(tl // CHUNK)

    def rows(a, n):
        return a[n * CHUNK:(n + 1) * CHUNK]

    cums = [_dot_exact_lhs(tril_ref[...], rows(lg, n)) for n in chunks]
    b_cum = jnp.concatenate(cums, axis=0)
    b_end = jnp.concatenate([jnp.broadcast_to(c[CHUNK - 1:CHUNK, :], (CHUNK, GLA_KP)) for c in cums], axis=0)
    q_in = (gla_ref[:, q0:q0 + GLA_KP].astype(F32) * jnp.exp(b_cum)).astype(BF16)
    kf = gla_ref[:, k0:k0 + GLA_KP].astype(F32)
    k_in = (kf * jnp.exp(-b_cum)).astype(BF16)
    k_dec = (kf * jnp.exp(b_end - b_cum)).astype(BF16)
    v = gla_ref[:, v0:g0]
    lg_hi, lg_lo = _hi_lo(lg)
    block_decay = jnp.exp(_dot_tn(lg_hi, block_ref[...]) + _dot_tn(lg_lo, block_ref[...]))

    scores, o, kv = {}, {}, {}
    for n in chunks:
        kn = rows(k_in, n)
        kk = jnp.concatenate([kn * kmask_ref[h:h + 1, :] for h in range(GLA_HEADS)], axis=0)
        scores[n] = (_dot_nt(rows(q_in, n), kk) * causal_ref[...]).astype(BF16)
    for n in chunks:
        vn = rows(v, n)
        vv = jnp.concatenate([vn * vmask_ref[h:h + 1, :] for h in range(GLA_HEADS)], axis=0)
        o[n] = _dot(scores[n], vv)
    for n in chunks:
        kv[n] = _dot_tn(rows(k_dec, n), rows(v, n)) * bd
    for n in chunks:
        s_prev = s_scr[...]
        o_scr[n * CHUNK:(n + 1) * CHUNK, :] = o[n] + _dot(rows(q_in, n), s_prev.astype(BF16))
        s_scr[...] = s_prev * jnp.broadcast_to(block_decay[:, n:n + 1], (GLA_KP, GLA_W)) + kv[n]

    o_all = o_scr[...]
    ms = _dot_row_halves((o_all * o_all).astype(BF16), ones_blk_ref[...]) * (1.0 / GLA_DV)
    gate = gla_ref[:, g0:g0 + GLA_W].astype(F32)
    o_ref[...] = (o_all * lax.rsqrt(ms + EPS) * ng_ref[...] * gate * _sigmoid(gate)).astype(BF16)
    s = s_scr[...]
    for h in range(GLA_HEADS):
        sout_ref[0, h] = s[h * GLA_DK:(h + 1) * GLA_DK, h * GLA_DV:(h + 1) * GLA_DV]


def _gla(gla2d, glr2d, gate_w, gate_b, norm_g, layer, batch, seq_len):
    assert seq_len % CHUNK == 0
    c = _gla_consts()
    tl = min(MIXER_TILE, seq_len)
    nl = seq_len // tl
    assert tl // CHUNK <= V7X_LANES
    block = (np.arange(tl) // CHUNK)[:, None] == np.arange(V7X_LANES)[None, :]
    consts = [(gate_w, True), (gate_b, True), (c["tril"], False), (c["causal4"], False), (c["kmask"], False),
              (c["vmask"], False), (c["bd"], False), (c["ones_blk"], False), (jnp.asarray(block, BF16), False),
              (norm_g, True)]
    row_spec = lambda w: pl.BlockSpec((tl, w), lambda b, l: (b * nl + l, 0))
    state_shape = (batch, GLA_HEADS, GLA_DK, GLA_DV)
    return pl.pallas_call(
        functools.partial(_gla_body, tl=tl),
        grid=(batch, nl),
        in_specs=[row_spec(GLA_COLS), row_spec(GLA_RANK)] + _specs(consts, layer),
        out_specs=[row_spec(GLA_W), pl.BlockSpec((1,) + state_shape[1:], lambda b, l: (b, 0, 0, 0))],
        out_shape=[jax.ShapeDtypeStruct((batch * seq_len, GLA_W), BF16), jax.ShapeDtypeStruct(state_shape, F32)],
        scratch_shapes=[pltpu.VMEM((GLA_KP, GLA_W), F32), pltpu.VMEM((tl, GLA_W), F32)],
        compiler_params=_params(),
        name="gla",
    )(gla2d, glr2d, *[a for a, _ in consts])


def _gla_lanes_head(h, *refs, seq_len, n_in):
    gla_ref, glr_ref, gw_ref, gb_ref, ng_ref, s_ref = refs[:6]
    o_ref, sout_ref, q_scr, k_scr, kd_scr, v_scr, g_scr, dec_scr, o_scr = refs[n_in:]
    nb = V7X_LANES
    steps = range(seq_len)
    q0, k0, v0, g0 = GLA_Q0, GLA_K0, GLA_V0, GLA_G0
    kw = slice(0, GLA_KW)

    def split(scr, t, tr, width):
        for hh in range(GLA_HEADS):
            scr[t, hh] = tr[hh * width:(hh + 1) * width]

    @pl.when(h == 0)
    def _():
        z = _dot_row_halves(glr_ref[...].astype(BF16), gw_ref[...]) + gb_ref[...]
        lg = (_log_sigmoid(z) * (1.0 / GLA_GATE_TEMP))[:, kw]
        cums = []
        for t in steps:
            lg_t = lg[t * nb:(t + 1) * nb]
            cums.append(lg_t if t == 0 else cums[-1] + lg_t)
        b_end = cums[-1]
        dec = _to_lanes(jnp.exp(b_end))
        for hh in range(GLA_HEADS):
            dec_scr[hh] = dec[hh * GLA_DK:(hh + 1) * GLA_DK]
        for t in steps:
            blk = gla_ref[t * nb:(t + 1) * nb, :].astype(F32)
            kf = blk[:, k0:k0 + GLA_KW]
            split(q_scr, t, _to_lanes(blk[:, q0:q0 + GLA_KW] * jnp.exp(cums[t])), GLA_DK)
            split(k_scr, t, _to_lanes(kf * jnp.exp(-cums[t])), GLA_DK)
            split(kd_scr, t, _to_lanes(kf * jnp.exp(b_end - cums[t])), GLA_DK)
            split(v_scr, t, _to_lanes(blk[:, v0:g0]), GLA_DV)
            split(g_scr, t, _to_lanes(blk[:, g0:g0 + GLA_W]), GLA_DV)

    q = [q_scr[t, h] for t in steps]
    k = [k_scr[t, h] for t in steps]
    kd = [kd_scr[t, h] for t in steps]
    v = [v_scr[t, h] for t in steps]
    dec = dec_scr[h]
    o = []
    for i in steps:
        acc = None
        for j in range(i + 1):
            s = jnp.sum(q[i] * k[j], axis=0, keepdims=True)
            acc = s * v[j] if acc is None else acc + s * v[j]
        o.append(acc)
    for i in range(GLA_DK):
        s_i = s_ref[i]
        new = s_i * dec[i:i + 1, :]
        for t in steps:
            o[t] = o[t] + q[t][i:i + 1, :] * s_i
            new = new + kd[t][i:i + 1, :] * v[t]
        sout_ref[i] = new
    for t in steps:
        ms = jnp.mean(o[t] * o[t], axis=0, keepdims=True)
        gate = g_scr[t, h]
        o_scr[t, h] = o[t] * lax.rsqrt(ms + EPS) * ng_ref[...] * gate * _sigmoid(gate)

    @pl.when(h == GLA_HEADS - 1)
    def _():
        for t in steps:
            o_ref[t * nb:(t + 1) * nb, :] = o_scr[t].reshape(GLA_W, nb).T.astype(BF16)


def _gla_lanes(gla2d, glr2d, gate_w, gate_b, norm_g, state, new_state, layer, seq_len):
    nb = V7X_LANES
    consts = [(gate_w, True), (gate_b, True), (norm_g, True)]
    k_tile = pltpu.VMEM((seq_len, GLA_HEADS, GLA_DK, nb), F32)
    v_tile = pltpu.VMEM((seq_len, GLA_HEADS, GLA_DV, nb), F32)
    block = (None, None, GLA_DK, GLA_DV, nb)
    prev, prev_specs, alias = _carried_output(new_state, n_before=6, out_index=1)
    slots = 1 if prev else state.shape[0] - layer
    return pl.pallas_call(
        functools.partial(_layer_slot_body, head_fn=_gla_lanes_head, n_in=6 + len(prev), seq_len=seq_len),
        grid=(slots, GLA_HEADS),
        in_specs=[_const_spec(gla2d.shape), _const_spec(glr2d.shape)] + _specs(consts, layer)
                 + [pl.BlockSpec(block, lambda g, h: (layer, _held_head(g, h, GLA_HEADS), 0, 0, 0))] + prev_specs,
        out_specs=[_const_spec((seq_len * nb, GLA_W)), pl.BlockSpec(block, lambda g, h: (layer + g, h, 0, 0, 0))],
        out_shape=[jax.ShapeDtypeStruct((seq_len * nb, GLA_W), BF16), jax.ShapeDtypeStruct(state.shape, F32)],
        scratch_shapes=[k_tile, k_tile, k_tile, v_tile, v_tile, pltpu.VMEM((GLA_HEADS, GLA_DK, nb), F32), v_tile],
        input_output_aliases=alias,
        compiler_params=_params(),
        name="gla_lanes",
    )(gla2d, glr2d, *[a for a, _ in consts], state, *prev)


def _s5_discretize(lam_re, lam_im, log_dt, b_re, b_im, c_re, c_im):
    lr = jnp.minimum(lam_re.astype(F32), -S5_MIN_NEG)
    li = lam_im.astype(F32)
    dt = jnp.exp(log_dt.astype(F32))[..., None]
    mag = jnp.exp(lr * dt)
    ar = mag * jnp.cos(li * dt)
    ai = mag * jnp.sin(li * dt)
    den = lr * lr + li * li
    cr = ((ar - 1.0) * lr + ai * li) / den
    ci = (ai * lr - (ar - 1.0) * li) / den
    b_re, b_im = b_re.astype(F32), b_im.astype(F32)
    bbar_re = cr[..., None] * b_re - ci[..., None] * b_im
    bbar_im = cr[..., None] * b_im + ci[..., None] * b_re
    eye = jnp.eye(S5_GROUPS, dtype=F32)
    depth = lr.shape[0]
    in_blk = lambda b: jnp.einsum("lgpi,gh->lgihp", b, eye).reshape(depth, S5_W, S5_LANES)
    out_blk = lambda c: jnp.einsum("lgop,gh->lgpho", c.astype(F32), eye).reshape(depth, S5_LANES, S5_W)
    bdb = jnp.concatenate([in_blk(bbar_re), in_blk(bbar_im)], axis=2).astype(BF16)
    bdc = jnp.concatenate([out_blk(c_re), -out_blk(c_im)], axis=1).astype(BF16)
    return ar.reshape(depth, 1, S5_LANES), ai.reshape(depth, 1, S5_LANES), bdb, bdc


def _s5_body(*refs, tl, batch, has_state):
    u_ref, ar_ref, ai_ref, bdb_ref, bdc_ref, d_ref, gw_ref, gb_ref = refs[:8]
    n_in = 10 if has_state else 8
    o_ref, hr_ref, hi_ref, x_scr, h_scr = refs[n_in:]

    @pl.when(pl.program_id(0) == 0)
    def _():
        if has_state:
            h_scr[0] = refs[8][...]
            h_scr[1] = refs[9][...]
        else:
            h_scr[...] = jnp.zeros_like(h_scr)

    ar = jnp.broadcast_to(ar_ref[...], (batch, S5_LANES))
    ai = jnp.broadcast_to(ai_ref[...], (batch, S5_LANES))
    hr, hi = h_scr[0], h_scr[1]
    sub = min(tl, S5_SUB_STEPS)
    n_rows = sub * batch
    for s in range(tl // sub):
        rows = slice(s * n_rows, (s + 1) * n_rows)
        u = u_ref[rows, :]
        x_scr[rows, :] = _dot_row_halves(u, bdb_ref[...])
        for t in range(s * sub, (s + 1) * sub):
            r = slice(t * batch, (t + 1) * batch)
            hr, hi = (ar * hr - ai * hi + x_scr[r, 0:S5_LANES],
                      ar * hi + ai * hr + x_scr[r, S5_LANES:2 * S5_LANES])
            x_scr[r, 0:S5_LANES] = hr
            x_scr[r, S5_LANES:2 * S5_LANES] = hi
        y = _dot_row_halves(x_scr[rows, :].astype(BF16), bdc_ref[...]) + d_ref[...] * u.astype(F32)
        y = _gelu_tanh(y)
        o_ref[rows, :] = (y * _sigmoid(_dot_row_halves(y.astype(BF16), gw_ref[...]) + gb_ref[...])).astype(BF16)
    h_scr[0] = hr
    h_scr[1] = hi
    hr_ref[...] = hr
    hi_ref[...] = hi


def _s5(u_tm, disc, d, glu_w_bf, glu_b, layer, state, batch, seq_len):
    tl = min(S5_TIME_TILE, seq_len)
    has_state = state is not None
    consts = list(disc) + [d, glu_w_bf, glu_b]
    h_spec = pl.BlockSpec((batch, S5_LANES), lambda i: (0, 0))
    in_specs = ([pl.BlockSpec((tl * batch, S5_W), lambda i: (i, 0))]
                + [_layer_spec(a.shape, layer) for a in consts])
    args = [u_tm] + consts
    if has_state:
        in_specs += [h_spec, h_spec]
        args += list(state)
    return pl.pallas_call(
        functools.partial(_s5_body, tl=tl, batch=batch, has_state=has_state),
        grid=(seq_len // tl,),
        in_specs=in_specs,
        out_specs=[pl.BlockSpec((tl * batch, S5_W), lambda i: (i, 0)), h_spec, h_spec],
        out_shape=[jax.ShapeDtypeStruct((seq_len * batch, S5_W), BF16),
                   jax.ShapeDtypeStruct((batch, S5_LANES), F32), jax.ShapeDtypeStruct((batch, S5_LANES), F32)],
        scratch_shapes=[pltpu.VMEM((tl * batch, 2 * S5_LANES), F32), pltpu.VMEM((2, batch, S5_LANES), F32)],
        compiler_params=_params(),
        name="s5",
    )(*args)


def _ffn_body(*refs, tm, time_major_batch, final):
    (x_ref, oret_ref, os5_ref, ogla_ref, wout_ref, gffn_ref, win_ref, cw_ref, cb_ref, wo_ref) = refs[:10]
    n = 10
    gfin_ref = None
    if final:
        gfin_ref, n = refs[n], n + 1
    if time_major_batch is not None:
        past_ref, n = refs[n], n + 1
    out_ref, conv_ref, x1_scr, h_scr, carry_scr, act_scr = refs[n:]
    pad = V7X_SUBLANES
    nb = time_major_batch

    if nb is None:
        @pl.when(pl.program_id(1) == 0)
        def _():
            carry_scr[...] = jnp.zeros_like(carry_scr)

    mix = jnp.concatenate([oret_ref[...], os5_ref[...], ogla_ref[...]], axis=1)
    x1_scr[...] = x_ref[...] + _dot(mix, wout_ref[...])
    h_scr[...] = _rmsnorm_rows(x1_scr[...], gffn_ref[...]).astype(BF16)
    h = h_scr[...]

    for c in range(D_FF // ACT_COLS):
        cols = slice(c * ACT_COLS, (c + 1) * ACT_COLS)
        a_c = _dot(h, win_ref[:, c * ACT_COLS:(c + 1) * ACT_COLS])
        gate_c = _dot(h, win_ref[:, D_FF + c * ACT_COLS:D_FF + (c + 1) * ACT_COLS])
        if nb is None:
            ext = jnp.concatenate([carry_scr[:, cols], a_c], axis=0)
            prev1 = ext[pad - 1:pad - 1 + tm]
            prev2 = ext[pad - 2:pad - 2 + tm]
            carry_scr[:, cols] = a_c[tm - pad:tm]
        else:
            prev1 = jnp.concatenate([past_ref[1, :, cols], a_c[0:tm - nb]], axis=0)
            prev2 = jnp.concatenate([past_ref[0, :, cols], past_ref[1, :, cols], a_c[0:tm - 2 * nb]], axis=0)
            conv_ref[:, cols] = a_c[tm - (CONV_W - 1) * nb:tm]
        conv = (cb_ref[:, cols] + prev2 * cw_ref[0:1, cols] + prev1 * cw_ref[1:2, cols]
                + a_c * cw_ref[2:3, cols])
        act_scr[:, cols] = (_gelu_tanh(conv) * gate_c).astype(BF16)

    x2 = x1_scr[...] + _dot(act_scr[...], wo_ref[...])
    out_ref[...] = _rmsnorm_rows(x2, gfin_ref[...]) if final else x2
    if nb is None:
        conv_ref[0] = carry_scr[pad - (CONV_W - 1):pad, :]


def _ffn(x2d, o_ret, o_s5, o_gla, w_out_bf, norm_g, w_in_bf, conv_w, conv_b, w_o_bf, layer, final_g, conv_past,
         batch, seq_len):
    rows = batch * seq_len
    time_major = conv_past is not None
    tm = rows if time_major else min(ROW_TILE, seq_len)
    final = final_g is not None
    consts = [w_out_bf, norm_g, w_in_bf, conv_w, conv_b, w_o_bf]
    stacked = [True] * len(consts)
    big = [True, False, True, False, False, True]
    if final:
        consts.append(final_g.reshape(1, D_MODEL))
        stacked.append(False)
        big.append(False)
    if time_major:
        assert seq_len >= CONV_W - 1 and batch % V7X_SUBLANES == 0
        consts.append(conv_past)
        stacked.append(False)
        big.append(False)
        grid = (1,)
        imap = lambda i: (0, 0)
        conv_rows = (CONV_W - 1) * batch
        conv_spec = pl.BlockSpec((conv_rows, D_FF), imap)
        conv_shape = jax.ShapeDtypeStruct((conv_rows, D_FF), F32)
    else:
        assert seq_len % tm == 0
        nl = seq_len // tm
        grid = (batch, nl)
        imap = lambda b, l: (b * nl + l, 0)
        conv_spec = pl.BlockSpec((1, CONV_W - 1, D_FF), lambda b, l: (b, 0, 0))
        conv_shape = jax.ShapeDtypeStruct((batch, CONV_W - 1, D_FF), F32)
    row_spec = lambda w: pl.BlockSpec((tm, w), imap)
    in_specs = ([row_spec(D_MODEL), row_spec(RET_W), row_spec(S5_W), row_spec(GLA_W)]
                + [_layer_spec(a.shape, layer, b) if s else _const_spec(a.shape)
                   for a, s, b in zip(consts, stacked, big)])
    return pl.pallas_call(
        functools.partial(_ffn_body, tm=tm, time_major_batch=batch if time_major else None, final=final),
        grid=grid,
        in_specs=in_specs,
        out_specs=[row_spec(D_MODEL), conv_spec],
        out_shape=[jax.ShapeDtypeStruct((rows, D_MODEL), F32), conv_shape],
        scratch_shapes=[pltpu.VMEM((tm, D_MODEL), F32), pltpu.VMEM((tm, D_MODEL), BF16),
                        pltpu.VMEM((V7X_SUBLANES, D_FF), F32), pltpu.VMEM((tm, D_FF), BF16)],
        compiler_params=_params(),
        name="ffn",
    )(x2d, o_ret, o_s5, o_gla, *consts)


def _rotary_tables(first_pos, n_pos, repeat=1):
    half = RET_HD // 2
    inv = ROPE_BASE ** (-(np.arange(half, dtype=np.float64) / half))
    ang = (first_pos + np.arange(n_pos)).astype(np.float64)[:, None] * inv[None, :]
    cos, sin = np.cos(ang), np.sin(ang)
    expand = lambda a, b: np.repeat(np.tile(np.concatenate([a, b], axis=1), (1, RET_HEADS)), repeat, axis=0)
    return jnp.asarray(expand(cos, cos), F32), jnp.asarray(expand(-sin, sin), F32)


def _swap_major(a2d, outer, inner):
    return a2d.reshape(outer, inner, -1).transpose(1, 0, 2).reshape(outer * inner, -1)


def _run_prompt_group(x, prm, final_g):
    batch, seq_len, _ = x.shape
    rows = batch * seq_len
    depth = prm["w_in"].shape[0]
    tm = min(MIX_IN_TILE, seq_len)
    cos_tab, sin_tab = _rotary_tables(0, seq_len)
    x2d = x.reshape(rows, D_MODEL)
    outs = []
    for li in range(depth):
        ret, su, gla, glr = _mix_in(x2d, prm["norm_mix_g"], prm["w_in"], li, cos_tab, sin_tab, tm)
        o_ret, s_ret = _retention(ret, prm["ret_ln_g"], prm["ret_ln_b"], li, batch, seq_len)
        o_s5, s5r, s5i = _s5(_swap_major(su, batch, seq_len), prm["s5_disc"], prm["s5_d"], prm["s5_glu_w"],
                             prm["s5_glu_b"], li, None, batch, seq_len)
        o_s5 = _swap_major(o_s5, seq_len, batch)
        o_gla, s_gla = _gla(gla, glr, prm["gla_gate_w"], prm["gla_gate_b"], prm["gla_norm_g"], li, batch, seq_len)
        x2d, conv_new = _ffn(x2d, o_ret, o_s5, o_gla, prm["w_out"], prm["norm_ffn_g"], prm["ffn_w_in"],
                             prm["ffn_conv_w"], prm["ffn_conv_b"], prm["ffn_w_out"], li,
                             final_g if li == depth - 1 else None, None, batch, seq_len)
        s5_shape = (batch, S5_GROUPS, S5_STATE)
        outs.append((s_ret, s5r.reshape(s5_shape), s5i.reshape(s5_shape), s_gla, conv_new))
    return [x2d.reshape(batch, seq_len, D_MODEL)] + [jnp.stack([o[i] for o in outs]) for i in range(5)]


def _run_sample_group(x, past_len, states, prm, final_g):
    batch, seq_len, _ = x.shape
    assert batch == V7X_LANES
    rows = batch * seq_len
    depth = prm["w_in"].shape[0]
    cos_tab, sin_tab = _rotary_tables(past_len, seq_len, repeat=batch)
    x2d = x.transpose(1, 0, 2).reshape(rows, D_MODEL)
    ret_state = states["ret"].transpose(0, 2, 3, 4, 1)
    gla_state = states["gla"].transpose(0, 2, 3, 4, 1)
    conv_state = states["conv"].transpose(0, 2, 1, 3)
    s5_flat = lambda s: s.reshape(depth, batch, S5_LANES)
    s5r_state, s5i_state = s5_flat(states["s5r"]), s5_flat(states["s5i"])
    outs = []
    s_ret = s_gla = None
    for li in range(depth):
        ret, su, gla, glr = _mix_in(x2d, prm["norm_mix_g"], prm["w_in"], li, cos_tab, sin_tab, rows)
        o_ret, s_ret = _retention_lanes(ret, prm["ret_ln_g_lanes"], prm["ret_ln_b_lanes"], ret_state, s_ret, li,
                                        seq_len)
        o_s5, s5r, s5i = _s5(su, prm["s5_disc"], prm["s5_d"], prm["s5_glu_w"], prm["s5_glu_b"], li,
                             (s5r_state[li], s5i_state[li]), batch, seq_len)
        o_gla, s_gla = _gla_lanes(gla, glr, prm["gla_gate_w"], prm["gla_gate_b"], prm["gla_norm_g_lanes"], gla_state,
                                  s_gla, li, seq_len)
        x2d, conv_new = _ffn(x2d, o_ret, o_s5, o_gla, prm["w_out"], prm["norm_ffn_g"], prm["ffn_w_in"],
                             prm["ffn_conv_w"], prm["ffn_conv_b"], prm["ffn_w_out"], li,
                             final_g if li == depth - 1 else None, conv_state[li], batch, seq_len)
        outs.append((s5r, s5i, conv_new.reshape(CONV_W - 1, batch, D_FF)))
    s5r, s5i, conv_new = (jnp.stack([o[i] for o in outs]) for i in range(3))
    s5_shape = (depth, batch, S5_GROUPS, S5_STATE)
    return [x2d.reshape(seq_len, batch, D_MODEL).transpose(1, 0, 2), s_ret.transpose(0, 4, 1, 2, 3),
            s5r.reshape(s5_shape), s5i.reshape(s5_shape), s_gla.transpose(0, 4, 1, 2, 3),
            conv_new.transpose(0, 2, 1, 3)]


def _prepare_params(norm_mix_g, w_in, ret_norm_g, ret_norm_b, s5_lambda_re, s5_lambda_im, s5_log_dt, s5_b_re, s5_b_im,
                    s5_c_re, s5_c_im, s5_d, s5_glu_w, s5_glu_b, gla_gate_w, gla_gate_b, gla_norm_g, w_out,
                    norm_ffn_g, ffn_w_in, ffn_conv_w, ffn_conv_b, ffn_w_out):
    depth = w_in.shape[0]
    nb = V7X_LANES
    row = lambda a: a.astype(F32).reshape(depth, 1, a.shape[-1])
    ret_lanes = lambda a: jnp.broadcast_to(a.astype(F32).reshape(depth, RET_HEADS, RET_HD, 1),
                                           (depth, RET_HEADS, RET_HD, nb))
    return dict(
        norm_mix_g=row(norm_mix_g), w_in=_mix_weights(w_in),
        ret_ln_g=row(ret_norm_g), ret_ln_b=row(ret_norm_b),
        ret_ln_g_lanes=ret_lanes(ret_norm_g), ret_ln_b_lanes=ret_lanes(ret_norm_b),
        s5_disc=_s5_discretize(s5_lambda_re, s5_lambda_im, s5_log_dt, s5_b_re, s5_b_im, s5_c_re, s5_c_im),
        s5_d=row(s5_d), s5_glu_w=s5_glu_w.astype(BF16), s5_glu_b=row(s5_glu_b),
        gla_gate_w=_pad_gate(gla_gate_w, BF16), gla_gate_b=_pad_gate(row(gla_gate_b), F32),
        gla_norm_g=row(jnp.tile(gla_norm_g, (1, GLA_HEADS))),
        gla_norm_g_lanes=jnp.broadcast_to(gla_norm_g.astype(F32).reshape(depth, GLA_DV, 1), (depth, GLA_DV, nb)),
        w_out=w_out.astype(BF16), norm_ffn_g=row(norm_ffn_g), ffn_w_in=ffn_w_in.astype(BF16),
        ffn_conv_w=jnp.pad(ffn_conv_w.astype(F32), ((0, 0), (0, V7X_SUBLANES - CONV_W), (0, 0))),
        ffn_conv_b=row(ffn_conv_b), ffn_w_out=ffn_w_out.astype(BF16))


def kernel(x_prompt, x_sample, state_ret, state_s5_re, state_s5_im, state_gla, state_ffn_conv, norm_mix_g, w_in, ret_norm_g, ret_norm_b, s5_lambda_re, s5_lambda_im, s5_log_dt, s5_b_re, s5_b_im, s5_c_re, s5_c_im, s5_d, s5_glu_w, s5_glu_b, gla_gate_w, gla_gate_b, gla_norm_g, w_out, norm_ffn_g, ffn_w_in, ffn_conv_w, ffn_conv_b, ffn_w_out, norm_final_g):
    prm = _prepare_params(norm_mix_g, w_in, ret_norm_g, ret_norm_b, s5_lambda_re, s5_lambda_im, s5_log_dt, s5_b_re,
                          s5_b_im, s5_c_re, s5_c_im, s5_d, s5_glu_w, s5_glu_b, gla_gate_w, gla_gate_b, gla_norm_g,
                          w_out, norm_ffn_g, ffn_w_in, ffn_conv_w, ffn_conv_b, ffn_w_out)
    sample_states = dict(ret=state_ret, s5r=state_s5_re, s5i=state_s5_im, gla=state_gla, conv=state_ffn_conv)
    yp, ret_p, s5r_p, s5i_p, gla_p, conv_p = _run_prompt_group(x_prompt, prm, norm_final_g)
    ys, ret_s, s5r_s, s5i_s, gla_s, conv_s = _run_sample_group(x_sample, PAST_LEN, sample_states, prm, norm_final_g)
    return (yp, ys, ret_p, ret_s, s5r_p, s5r_s, s5i_p, s5i_s, gla_p, gla_s, conv_p, conv_s)
```

```python
import functools

import numpy as np
import jax
import jax.numpy as jnp
from jax import lax
from jax.experimental import pallas as pl
from jax.experimental.pallas import tpu as pltpu

F32, BF16 = jnp.float32, jnp.bfloat16

D_MODEL = 1024
RET_W, S5_W, GLA_W = 384, 256, 384
RET_HEADS, RET_HD = 6, 64
RET_PAIRS = RET_HEADS // 2
S5_GROUPS, S5_CH, S5_STATE = 16, 16, 64
S5_LANES = S5_GROUPS * S5_STATE
S5_MIN_NEG = 1e-4
GLA_HEADS, GLA_DK, GLA_DV, GLA_RANK = 4, 48, 96, 16
GLA_KW = GLA_HEADS * GLA_DK
GLA_GATE_TEMP = 16.0
D_FF = 2816
CONV_W = 3
ROPE_BASE = 10000.0
CHUNK = 64
EPS = 1e-6
PAST_LEN = 16384
IN_COLS = 4 * RET_W + S5_W + 2 * GLA_KW + 2 * GLA_W + GLA_RANK
COL_RET, COL_S5, COL_GLA, COL_LR = 0, 4 * RET_W, 4 * RET_W + S5_W, IN_COLS - GLA_RANK
GLA_KP = 256
GLA_Q0, GLA_LR0, GLA_K0, GLA_V0, GLA_G0 = 0, GLA_KW, GLA_KP, 2 * GLA_KP, 2 * GLA_KP + GLA_W
GLA_COLS = 2 * GLA_KP + 2 * GLA_W
MIX_COLS = COL_GLA + GLA_COLS

V7X_SUBLANES = 8
V7X_LANES = 128
VMEM_LIMIT = 58 * 1024 * 1024

ROW_TILE = 512
MIX_IN_TILE = 1024
MIXER_TILE = 2048
S5_TIME_TILE = 256
S5_SUB_STEPS = 64
ACT_COLS = 256


def _dot(a, b):
    return jnp.dot(a, b, preferred_element_type=F32)


def _dot_nt(a, b):
    return lax.dot_general(a, b, (((1,), (1,)), ((), ())), preferred_element_type=F32)


def _dot_tn(a, b):
    return lax.dot_general(a, b, (((0,), (0,)), ((), ())), preferred_element_type=F32)


def _hi_lo(x):
    hi = x.astype(BF16)
    return hi, (x - hi.astype(F32)).astype(BF16)


def _dot_exact_lhs(m, x):
    hi, lo = _hi_lo(x)
    return _dot(m, hi) + _dot(m, lo)


def _dot_row_halves(a, w):
    half = a.shape[0] // 2
    return jnp.concatenate([_dot(a[:half], w), _dot(a[half:], w)], axis=0)


def _pad_gate(a, dtype):
    return jnp.pad(a.astype(dtype), [(0, 0)] * (a.ndim - 1) + [(0, GLA_KP - GLA_KW)])


def _sigmoid(x):
    return 1.0 / (1.0 + jnp.exp(-x))


def _log_sigmoid(z):
    return jnp.minimum(z, 0.0) - jnp.log(1.0 + jnp.exp(-jnp.abs(z)))


def _gelu_tanh(x):
    return 0.5 * x * (1.0 + jnp.tanh(0.7978845608028654 * (x + 0.044715 * (x * x * x))))


def _rmsnorm_rows(x, g):
    return x * lax.rsqrt(jnp.mean(x * x, axis=-1, keepdims=True) + EPS) * g


def _const_spec(shape):
    nd = len(shape)
    return pl.BlockSpec(shape, lambda *_: (0,) * nd)


def _layer_spec(stacked_shape, layer, single_buffer=False):
    nd = len(stacked_shape) - 1
    mode = dict(pipeline_mode=pl.Buffered(1)) if single_buffer else {}
    return pl.BlockSpec((None,) + tuple(stacked_shape[1:]), lambda *_: (layer,) + (0,) * nd, **mode)


def _specs(arrays, layer):
    return [_layer_spec(a.shape, layer) if stacked else _const_spec(a.shape) for a, stacked in arrays]


def _carried_output(buffer, n_before, out_index):
    if buffer is None:
        return [], [], {}
    return [buffer], [pl.BlockSpec(memory_space=pl.ANY)], {n_before: out_index}


def _held_head(slot, head, heads):
    return jnp.where(slot == 0, head, heads - 1)


def _layer_slot_body(*refs, head_fn, n_in, **kw):
    @pl.when(pl.program_id(0) == 0)
    def _():
        head_fn(pl.program_id(1), *refs, n_in=n_in, **kw)

    @pl.when(pl.program_id(0) > 0)
    def _():
        refs[n_in + 1][...] = jnp.zeros_like(refs[n_in + 1])


def _params():
    return pltpu.CompilerParams(vmem_limit_bytes=VMEM_LIMIT)


def _to_lanes(x):
    w = x.shape[1]
    pad = -w % V7X_LANES
    if pad:
        x = jnp.concatenate([x, jnp.zeros((x.shape[0], pad), F32)], axis=1)
    return x.T[0:w]


def _mix_in_body(x_ref, g_ref, w_ref, cos_ref, sin_ref, ret_ref, su_ref, gla_ref, glr_ref):
    h = _rmsnorm_rows(x_ref[...], g_ref[...]).astype(BF16)
    ret = _dot(h, w_ref[:, COL_RET:COL_S5])
    cos = cos_ref[...]
    sin = sin_ref[...]
    lane = lax.broadcasted_iota(jnp.int32, cos.shape, 1)
    first_half = (lane & (RET_HD // 2)) == 0

    def rotary(z):
        swapped = jnp.where(first_half, pltpu.roll(z, RET_W - RET_HD // 2, 1), pltpu.roll(z, RET_HD // 2, 1))
        return z * cos + swapped * sin

    ret_ref[:, 0:RET_W] = rotary(ret[:, 0:RET_W]).astype(BF16)
    ret_ref[:, RET_W:2 * RET_W] = (rotary(ret[:, RET_W:2 * RET_W]) * RET_HD ** -0.5).astype(BF16)
    ret_ref[:, 2 * RET_W:] = ret[:, 2 * RET_W:].astype(BF16)
    su_ref[...] = _dot(h, w_ref[:, COL_S5:COL_GLA]).astype(BF16)
    gla = _dot(h, w_ref[:, COL_GLA:MIX_COLS])
    glane = lax.broadcasted_iota(jnp.int32, (1, GLA_COLS), 1)
    scale = jnp.where(glane < GLA_KW, GLA_DK ** -0.5, jnp.where(glane < GLA_KP, 0.0, 1.0))
    gla_ref[...] = (gla * scale).astype(BF16)
    glr_ref[...] = gla[:, GLA_LR0:GLA_LR0 + GLA_RANK]


def _mix_weights(w_in):
    w = w_in.astype(BF16)
    k0, v0 = COL_GLA + GLA_KW, COL_GLA + 2 * GLA_KW
    zeros = lambda n: jnp.zeros(w.shape[:2] + (n,), BF16)
    return jnp.concatenate([w[..., :k0], w[..., COL_LR:], zeros(GLA_KP - GLA_KW - GLA_RANK),
                            w[..., k0:v0], zeros(GLA_KP - GLA_KW), w[..., v0:COL_LR]], axis=-1)


def _mix_in(x2d, norm_g, w_in_bf, layer, cos_tab, sin_tab, tm):
    rows = x2d.shape[0]
    n_tab = cos_tab.shape[0] // tm
    row_spec = lambda w: pl.BlockSpec((tm, w), lambda i: (i, 0))
    tab_spec = pl.BlockSpec((tm, RET_W), lambda i: (i % n_tab, 0))
    return pl.pallas_call(
        _mix_in_body,
        grid=(rows // tm,),
        in_specs=[row_spec(D_MODEL), _layer_spec(norm_g.shape, layer), _layer_spec(w_in_bf.shape, layer, True),
                  tab_spec, tab_spec],
        out_specs=[row_spec(4 * RET_W), row_spec(S5_W), row_spec(GLA_COLS), row_spec(GLA_RANK)],
        out_shape=[jax.ShapeDtypeStruct((rows, 4 * RET_W), BF16), jax.ShapeDtypeStruct((rows, S5_W), BF16),
                   jax.ShapeDtypeStruct((rows, GLA_COLS), BF16), jax.ShapeDtypeStruct((rows, GLA_RANK), F32)],
        compiler_params=_params(),
        name="mix_in",
    )(x2d, norm_g, w_in_bf, cos_tab, sin_tab)


def _retention_gammas():
    return 1.0 - 2.0 ** (-5.0 - np.arange(RET_HEADS))


def _retention_consts():
    t = np.arange(CHUNK)
    gam = _retention_gammas()
    causal = t[:, None] >= t[None, :]
    diff = np.maximum(t[:, None] - t[None, :], 0)
    dmask = np.zeros((RET_PAIRS, CHUNK, 2 * CHUNK))
    cdec = np.zeros((RET_PAIRS, 2 * RET_HD, 2 * RET_HD))
    for p in range(RET_PAIRS):
        for s in range(2):
            g = gam[2 * p + s]
            dmask[p, :, s * CHUNK:(s + 1) * CHUNK] = np.where(causal, g ** diff, 0.0)
            cdec[p, s * RET_HD:(s + 1) * RET_HD, s * RET_HD:(s + 1) * RET_HD] = g ** CHUNK
    lane_gam = np.repeat(gam, RET_HD)[None, :]
    qdec = lane_gam ** (t[:, None] + 1.0)
    kdec = lane_gam ** (CHUNK - 1.0 - t[:, None])
    bd = (cdec[0] > 0).astype(np.float32)
    head_mask = np.stack([np.arange(2 * RET_HD) < RET_HD, np.arange(2 * RET_HD) >= RET_HD]).astype(np.float32)
    ones_blk = np.kron(np.eye(RET_HEADS), np.ones((RET_HD, RET_HD)))
    f = lambda a: jnp.asarray(a, F32)
    return dict(dmask=f(dmask), cdec=f(cdec), qdec=qdec, kdec=kdec, bd=f(bd),
                head_mask=jnp.asarray(head_mask, BF16), ones_blk=jnp.asarray(ones_blk, BF16))


def _retention_body(ret_ref, dmask_ref, cdec_ref, qdec_ref, kdec_ref, bd_ref, hm_ref, ones_ref, lng_ref, lnb_ref,
                    o_ref, sout_ref, s_scr, o_scr, *, tl):
    @pl.when(pl.program_id(1) == 0)
    def _():
        s_scr[...] = jnp.zeros_like(s_scr)

    bd = bd_ref[...]
    m0 = hm_ref[0:1, :]
    m1 = hm_ref[1:2, :]
    q = ret_ref[:, 0:RET_W]
    k = ret_ref[:, RET_W:2 * RET_W]
    v = ret_ref[:, 2 * RET_W:3 * RET_W]
    q_start = (q.astype(F32) * qdec_ref[...]).astype(BF16)
    k_end = (k.astype(F32) * kdec_ref[...]).astype(BF16)
    blocks = [(n, p) for n in range(tl // CHUNK) for p in range(RET_PAIRS)]

    def part(a, n, p):
        return a[n * CHUNK:(n + 1) * CHUNK, 2 * RET_HD * p:2 * RET_HD * (p + 1)]

    scores, o, kv = {}, {}, {}
    for n, p in blocks:
        kp = part(k, n, p)
        kk = jnp.concatenate([kp * m0, kp * m1], axis=0)
        scores[n, p] = (_dot_nt(part(q, n, p), kk) * dmask_ref[p]).astype(BF16)
    for n, p in blocks:
        vp = part(v, n, p)
        o[n, p] = _dot(scores[n, p], jnp.concatenate([vp * m0, vp * m1], axis=0))
    for n, p in blocks:
        kv[n, p] = _dot_tn(part(k_end, n, p), part(v, n, p)) * bd
    for n, p in blocks:
        s_prev = s_scr[p]
        o_scr[n * CHUNK:(n + 1) * CHUNK, 2 * RET_HD * p:2 * RET_HD * (p + 1)] = (
            o[n, p] + _dot(part(q_start, n, p), s_prev.astype(BF16)))
        s_scr[p] = s_prev * cdec_ref[p] + kv[n, p]

    o_all = o_scr[...]
    ones_blk = ones_ref[...]
    mu = _dot_row_halves(o_all.astype(BF16), ones_blk) * (1.0 / RET_HD)
    cen = o_all - mu
    var = _dot_row_halves((cen * cen).astype(BF16), ones_blk) * (1.0 / RET_HD)
    y = cen * lax.rsqrt(var + EPS) * lng_ref[...] + lnb_ref[...]
    gate = ret_ref[:, 3 * RET_W:4 * RET_W].astype(F32)
    o_ref[...] = (y * gate * _sigmoid(gate)).astype(BF16)
    for p in range(RET_PAIRS):
        s = s_scr[p]
        sout_ref[0, 2 * p] = s[0:RET_HD, 0:RET_HD]
        sout_ref[0, 2 * p + 1] = s[RET_HD:2 * RET_HD, RET_HD:2 * RET_HD]


def _retention(ret2d, ln_g, ln_b, layer, batch, seq_len):
    assert seq_len % CHUNK == 0
    c = _retention_consts()
    tl = min(MIXER_TILE, seq_len)
    nl = seq_len // tl
    tile_rows = lambda a: jnp.asarray(np.tile(a, (tl // CHUNK, 1)), F32)
    consts = [(c["dmask"], False), (c["cdec"], False), (tile_rows(c["qdec"]), False), (tile_rows(c["kdec"]), False),
              (c["bd"], False), (c["head_mask"], False), (c["ones_blk"], False), (ln_g, True), (ln_b, True)]
    state_shape = (batch, RET_HEADS, RET_HD, RET_HD)
    return pl.pallas_call(
        functools.partial(_retention_body, tl=tl),
        grid=(batch, nl),
        in_specs=[pl.BlockSpec((tl, 4 * RET_W), lambda b, l: (b * nl + l, 0))] + _specs(consts, layer),
        out_specs=[pl.BlockSpec((tl, RET_W), lambda b, l: (b * nl + l, 0)),
                   pl.BlockSpec((1,) + state_shape[1:], lambda b, l: (b, 0, 0, 0))],
        out_shape=[jax.ShapeDtypeStruct((batch * seq_len, RET_W), BF16), jax.ShapeDtypeStruct(state_shape, F32)],
        scratch_shapes=[pltpu.VMEM((RET_PAIRS, 2 * RET_HD, 2 * RET_HD), F32), pltpu.VMEM((tl, RET_W), F32)],
        compiler_params=_params(),
        name="retention",
    )(ret2d, *[a for a, _ in consts])


def _retention_lanes_head(h, *refs, seq_len, n_in):
    ret_ref, tab_ref, lng_ref, lnb_ref, s_ref = refs[:5]
    o_ref, sout_ref, q_scr, k_scr, v_scr, g_scr, o_scr = refs[n_in:]
    nb = V7X_LANES
    steps = range(seq_len)

    @pl.when(h == 0)
    def _():
        for t in steps:
            blk = ret_ref[t * nb:(t + 1) * nb, :].astype(F32)
            for scr, off in ((q_scr, 0), (k_scr, RET_W), (v_scr, 2 * RET_W), (g_scr, 3 * RET_W)):
                tr = _to_lanes(blk[:, off:off + RET_W])
                for hh in range(RET_HEADS):
                    scr[t, hh] = tr[hh * RET_HD:(hh + 1) * RET_HD]

    tab = tab_ref[h]
    row = lambda r: tab[r:r + 1, :]
    q = [q_scr[t, h] for t in steps]
    k = [k_scr[t, h] for t in steps]
    v = [v_scr[t, h] for t in steps]
    o = []
    for i in steps:
        acc = None
        for j in range(i + 1):
            s = jnp.sum(q[i] * k[j], axis=0, keepdims=True)
            if i > j:
                s = s * row(i - j - 1)
            acc = s * v[j] if acc is None else acc + s * v[j]
        o.append(acc)
    q_start = [q[t] * row(t) for t in steps]
    k_end = [k[t] * row(seq_len + t) for t in steps]
    chunk_decay = row(2 * seq_len)
    for d in range(RET_HD):
        s_d = s_ref[d]
        new = s_d * chunk_decay
        for t in steps:
            o[t] = o[t] + q_start[t][d:d + 1, :] * s_d
            new = new + k_end[t][d:d + 1, :] * v[t]
        sout_ref[d] = new
    for t in steps:
        mu = jnp.mean(o[t], axis=0, keepdims=True)
        cen = o[t] - mu
        var = jnp.mean(cen * cen, axis=0, keepdims=True)
        gate = g_scr[t, h]
        o_scr[t, h] = (cen * lax.rsqrt(var + EPS) * lng_ref[h] + lnb_ref[h]) * gate * _sigmoid(gate)

    @pl.when(h == RET_HEADS - 1)
    def _():
        for t in steps:
            o_ref[t * nb:(t + 1) * nb, :] = o_scr[t].reshape(RET_W, nb).T.astype(BF16)


def _retention_lanes(ret2d, ln_g, ln_b, state, new_state, layer, seq_len):
    nb = V7X_LANES
    gam = _retention_gammas()[:, None]
    t = np.arange(seq_len)[None, :]
    rows = np.concatenate([gam ** (t + 1.0), gam ** (seq_len - 1.0 - t), gam ** (seq_len + 0.0 * t[:, :1])], axis=1)
    n_rows = -(-rows.shape[1] // V7X_SUBLANES) * V7X_SUBLANES
    tab = np.zeros((RET_HEADS, n_rows, nb))
    tab[:, :rows.shape[1], :] = rows[:, :, None]
    consts = [(jnp.asarray(tab, F32), False), (ln_g, True), (ln_b, True)]
    tile = pltpu.VMEM((seq_len, RET_HEADS, RET_HD, nb), F32)
    block = (None, None, RET_HD, RET_HD, nb)
    prev, prev_specs, alias = _carried_output(new_state, n_before=5, out_index=1)
    slots = 1 if prev else state.shape[0] - layer
    return pl.pallas_call(
        functools.partial(_layer_slot_body, head_fn=_retention_lanes_head, n_in=5 + len(prev), seq_len=seq_len),
        grid=(slots, RET_HEADS),
        in_specs=[_const_spec(ret2d.shape)] + _specs(consts, layer)
                 + [pl.BlockSpec(block, lambda g, h: (layer, _held_head(g, h, RET_HEADS), 0, 0, 0))] + prev_specs,
        out_specs=[_const_spec((seq_len * nb, RET_W)), pl.BlockSpec(block, lambda g, h: (layer + g, h, 0, 0, 0))],
        out_shape=[jax.ShapeDtypeStruct((seq_len * nb, RET_W), BF16), jax.ShapeDtypeStruct(state.shape, F32)],
        scratch_shapes=[tile] * 5,
        input_output_aliases=alias,
        compiler_params=_params(),
        name="retention_lanes",
    )(ret2d, *[a for a, _ in consts], state, *prev)


def _gla_consts():
    t = np.arange(CHUNK)
    causal = t[:, None] >= t[None, :]
    pad = GLA_KP - GLA_KW
    kmask = np.pad(np.kron(np.eye(GLA_HEADS), np.ones((1, GLA_DK))), ((0, 0), (0, pad)))
    vmask = np.kron(np.eye(GLA_HEADS), np.ones((1, GLA_DV)))
    bd = np.pad(np.kron(np.eye(GLA_HEADS), np.ones((GLA_DK, GLA_DV))), ((0, pad), (0, 0)))
    ones_blk = np.kron(np.eye(GLA_HEADS), np.ones((GLA_DV, GLA_DV)))
    b = lambda a: jnp.asarray(a, BF16)
    return dict(tril=b(causal), causal4=jnp.asarray(np.tile(causal, (1, GLA_HEADS)), F32), kmask=b(kmask),
                vmask=b(vmask), bd=jnp.asarray(bd, F32), ones_blk=b(ones_blk))


def _gla_body(gla_ref, glr_ref, gw_ref, gb_ref, tril_ref, causal_ref, kmask_ref, vmask_ref, bd_ref, ones_blk_ref,
              block_ref, ng_ref, o_ref, sout_ref, s_scr, o_scr, *, tl):
    @pl.when(pl.program_id(1) == 0)
    def _():
        s_scr[...] = jnp.zeros_like(s_scr)

    bd = bd_ref[...]
    z = _dot_row_halves(glr_ref[...].astype(BF16), gw_ref[...]) + gb_ref[...]
    lg = _log_sigmoid(z) * (1.0 / GLA_GATE_TEMP)
    q0, k0, v0, g0 = GLA_Q0, GLA_K0, GLA_V0, GLA_G0
    chunks = range(tl // CHUNK)

    def rows(a, n):
        return a[n * CHUNK:(n + 1) * CHUNK]

    cums = [_dot_exact_lhs(tril_ref[...], rows(lg, n)) for n in chunks]
    b_cum = jnp.concatenate(cums, axis=0)
    b_end = jnp.concatenate([jnp.broadcast_to(c[CHUNK - 1:CHUNK, :], (CHUNK, GLA_KP)) for c in cums], axis=0)
    q_in = (gla_ref[:, q0:q0 + GLA_KP].astype(F32) * jnp.exp(b_cum)).astype(BF16)
    kf = gla_ref[:, k0:k0 + GLA_KP].astype(F32)
    k_in = (kf * jnp.exp(-b_cum)).astype(BF16)
    k_dec = (kf * jnp.exp(b_end - b_cum)).astype(BF16)
    v = gla_ref[:, v0:g0]
    lg_hi, lg_lo = _hi_lo(lg)
    block_decay = jnp.exp(_dot_tn(lg_hi, block_ref[...]) + _dot_tn(lg_lo, block_ref[...]))

    scores, o, kv = {}, {}, {}
    for n in chunks:
        kn = rows(k_in, n)
        kk = jnp.concatenate([kn * kmask_ref[h:h + 1, :] for h in range(GLA_HEADS)], axis=0)
        scores[n] = (_dot_nt(rows(q_in, n), kk) * causal_ref[...]).astype(BF16)
    for n in chunks:
        vn = rows(v, n)
        vv = jnp.concatenate([vn * vmask_ref[h:h + 1, :] for h in range(GLA_HEADS)], axis=0)
        o[n] = _dot(scores[n], vv)
    for n in chunks:
        kv[n] = _dot_tn(rows(k_dec, n), rows(v, n)) * bd
    for n in chunks:
        s_prev = s_scr[...]
        o_scr[n * CHUNK:(n + 1) * CHUNK, :] = o[n] + _dot(rows(q_in, n), s_prev.astype(BF16))
        s_scr[...] = s_prev * jnp.broadcast_to(block_decay[:, n:n + 1], (GLA_KP, GLA_W)) + kv[n]

    o_all = o_scr[...]
    ms = _dot_row_halves((o_all * o_all).astype(BF16), ones_blk_ref[...]) * (1.0 / GLA_DV)
    gate = gla_ref[:, g0:g0 + GLA_W].astype(F32)
    o_ref[...] = (o_all * lax.rsqrt(ms + EPS) * ng_ref[...] * gate * _sigmoid(gate)).astype(BF16)
    s = s_scr[...]
    for h in range(GLA_HEADS):
        sout_ref[0, h] = s[h * GLA_DK:(h + 1) * GLA_DK, h * GLA_DV:(h + 1) * GLA_DV]


def _gla(gla2d, glr2d, gate_w, gate_b, norm_g, layer, batch, seq_len):
    assert seq_len % CHUNK == 0
    c = _gla_consts()
    tl = min(MIXER_TILE, seq_len)
    nl = seq_len // tl
    assert tl // CHUNK <= V7X_LANES
    block = (np.arange(tl) // CHUNK)[:, None] == np.arange(V7X_LANES)[None, :]
    consts = [(gate_w, True), (gate_b, True), (c["tril"], False), (c["causal4"], False), (c["kmask"], False),
              (c["vmask"], False), (c["bd"], False), (c["ones_blk"], False), (jnp.asarray(block, BF16), False),
              (norm_g, True)]
    row_spec = lambda w: pl.BlockSpec((tl, w), lambda b, l: (b * nl + l, 0))
    state_shape = (batch, GLA_HEADS, GLA_DK, GLA_DV)
    return pl.pallas_call(
        functools.partial(_gla_body, tl=tl),
        grid=(batch, nl),
        in_specs=[row_spec(GLA_COLS), row_spec(GLA_RANK)] + _specs(consts, layer),
        out_specs=[row_spec(GLA_W), pl.BlockSpec((1,) + state_shape[1:], lambda b, l: (b, 0, 0, 0))],
        out_shape=[jax.ShapeDtypeStruct((batch * seq_len, GLA_W), BF16), jax.ShapeDtypeStruct(state_shape, F32)],
        scratch_shapes=[pltpu.VMEM((GLA_KP, GLA_W), F32), pltpu.VMEM((tl, GLA_W), F32)],
        compiler_params=_params(),
        name="gla",
    )(gla2d, glr2d, *[a for a, _ in consts])


def _gla_lanes_head(h, *refs, seq_len, n_in):
    gla_ref, glr_ref, gw_ref, gb_ref, ng_ref, s_ref = refs[:6]
    o_ref, sout_ref, q_scr, k_scr, kd_scr, v_scr, g_scr, dec_scr, o_scr = refs[n_in:]
    nb = V7X_LANES
    steps = range(seq_len)
    q0, k0, v0, g0 = GLA_Q0, GLA_K0, GLA_V0, GLA_G0
    kw = slice(0, GLA_KW)

    def split(scr, t, tr, width):
        for hh in range(GLA_HEADS):
            scr[t, hh] = tr[hh * width:(hh + 1) * width]

    @pl.when(h == 0)
    def _():
        z = _dot_row_halves(glr_ref[...].astype(BF16), gw_ref[...]) + gb_ref[...]
        lg = (_log_sigmoid(z) * (1.0 / GLA_GATE_TEMP))[:, kw]
        cums = []
        for t in steps:
            lg_t = lg[t * nb:(t + 1) * nb]
            cums.append(lg_t if t == 0 else cums[-1] + lg_t)
        b_end = cums[-1]
        dec = _to_lanes(jnp.exp(b_end))
        for hh in range(GLA_HEADS):
            dec_scr[hh] = dec[hh * GLA_DK:(hh + 1) * GLA_DK]
        for t in steps:
            blk = gla_ref[t * nb:(t + 1) * nb, :].astype(F32)
            kf = blk[:, k0:k0 + GLA_KW]
            split(q_scr, t, _to_lanes(blk[:, q0:q0 + GLA_KW] * jnp.exp(cums[t])), GLA_DK)
            split(k_scr, t, _to_lanes(kf * jnp.exp(-cums[t])), GLA_DK)
            split(kd_scr, t, _to_lanes(kf * jnp.exp(b_end - cums[t])), GLA_DK)
            split(v_scr, t, _to_lanes(blk[:, v0:g0]), GLA_DV)
            split(g_scr, t, _to_lanes(blk[:, g0:g0 + GLA_W]), GLA_DV)

    q = [q_scr[t, h] for t in steps]
    k = [k_scr[t, h] for t in steps]
    kd = [kd_scr[t, h] for t in steps]
    v = [v_scr[t, h] for t in steps]
    dec = dec_scr[h]
    o = []
    for i in steps:
        acc = None
        for j in range(i + 1):
            s = jnp.sum(q[i] * k[j], axis=0, keepdims=True)
            acc = s * v[j] if acc is None else acc + s * v[j]
        o.append(acc)
    for i in range(GLA_DK):
        s_i = s_ref[i]
        new = s_i * dec[i:i + 1, :]
        for t in steps:
            o[t] = o[t] + q[t][i:i + 1, :] * s_i
            new = new + kd[t][i:i + 1, :] * v[t]
        sout_ref[i] = new
    for t in steps:
        ms = jnp.mean(o[t] * o[t], axis=0, keepdims=True)
        gate = g_scr[t, h]
        o_scr[t, h] = o[t] * lax.rsqrt(ms + EPS) * ng_ref[...] * gate * _sigmoid(gate)

    @pl.when(h == GLA_HEADS - 1)
    def _():
        for t in steps:
            o_ref[t * nb:(t + 1) * nb, :] = o_scr[t].reshape(GLA_W, nb).T.astype(BF16)


def _gla_lanes(gla2d, glr2d, gate_w, gate_b, norm_g, state, new_state, layer, seq_len):
    nb = V7X_LANES
    consts = [(gate_w, True), (gate_b, True), (norm_g, True)]
    k_tile = pltpu.VMEM((seq_len, GLA_HEADS, GLA_DK, nb), F32)
    v_tile = pltpu.VMEM((seq_len, GLA_HEADS, GLA_DV, nb), F32)
    block = (None, None, GLA_DK, GLA_DV, nb)
    prev, prev_specs, alias = _carried_output(new_state, n_before=6, out_index=1)
    slots = 1 if prev else state.shape[0] - layer
    return pl.pallas_call(
        functools.partial(_layer_slot_body, head_fn=_gla_lanes_head, n_in=6 + len(prev), seq_len=seq_len),
        grid=(slots, GLA_HEADS),
        in_specs=[_const_spec(gla2d.shape), _const_spec(glr2d.shape)] + _specs(consts, layer)
                 + [pl.BlockSpec(block, lambda g, h: (layer, _held_head(g, h, GLA_HEADS), 0, 0, 0))] + prev_specs,
        out_specs=[_const_spec((seq_len * nb, GLA_W)), pl.BlockSpec(block, lambda g, h: (layer + g, h, 0, 0, 0))],
        out_shape=[jax.ShapeDtypeStruct((seq_len * nb, GLA_W), BF16), jax.ShapeDtypeStruct(state.shape, F32)],
        scratch_shapes=[k_tile, k_tile, k_tile, v_tile, v_tile, pltpu.VMEM((GLA_HEADS, GLA_DK, nb), F32), v_tile],
        input_output_aliases=alias,
        compiler_params=_params(),
        name="gla_lanes",
    )(gla2d, glr2d, *[a for a, _ in consts], state, *prev)


def _s5_discretize(lam_re, lam_im, log_dt, b_re, b_im, c_re, c_im):
    lr = jnp.minimum(lam_re.astype(F32), -S5_MIN_NEG)
    li = lam_im.astype(F32)
    dt = jnp.exp(log_dt.astype(F32))[..., None]
    mag = jnp.exp(lr * dt)
    ar = mag * jnp.cos(li * dt)
    ai = mag * jnp.sin(li * dt)
    den = lr * lr + li * li
    cr = ((ar - 1.0) * lr + ai * li) / den
    ci = (ai * lr - (ar - 1.0) * li) / den
    b_re, b_im = b_re.astype(F32), b_im.astype(F32)
    bbar_re = cr[..., None] * b_re - ci[..., None] * b_im
    bbar_im = cr[..., None] * b_im + ci[..., None] * b_re
    eye = jnp.eye(S5_GROUPS, dtype=F32)
    depth = lr.shape[0]
    in_blk = lambda b: jnp.einsum("lgpi,gh->lgihp", b, eye).reshape(depth, S5_W, S5_LANES)
    out_blk = lambda c: jnp.einsum("lgop,gh->lgpho", c.astype(F32), eye).reshape(depth, S5_LANES, S5_W)
    bdb = jnp.concatenate([in_blk(bbar_re), in_blk(bbar_im)], axis=2).astype(BF16)
    bdc = jnp.concatenate([out_blk(c_re), -out_blk(c_im)], axis=1).astype(BF16)
    return ar.reshape(depth, 1, S5_LANES), ai.reshape(depth, 1, S5_LANES), bdb, bdc


def _s5_body(*refs, tl, batch, has_state, batch_major):
    u_ref, ar_ref, ai_ref, bdb_ref, bdc_ref, d_ref, gw_ref, gb_ref = refs[:8]
    n_in = 10 if has_state else 8
    o_ref, hr_ref, hi_ref, x_scr, h_scr = refs[n_in:]

    @pl.when(pl.program_id(0) == 0)
    def _():
        if has_state:
            h_scr[0] = refs[8][...]
            h_scr[1] = refs[9][...]
        else:
            h_scr[...] = jnp.zeros_like(h_scr)

    ar = jnp.broadcast_to(ar_ref[...], (batch, S5_LANES))
    ai = jnp.broadcast_to(ai_ref[...], (batch, S5_LANES))
    hr, hi = h_scr[0], h_scr[1]
    sub = min(tl, S5_SUB_STEPS)
    n_rows = sub * batch
    for s in range(tl // sub):
        rows = slice(s * n_rows, (s + 1) * n_rows)
        steps = slice(s * sub, (s + 1) * sub)
        if batch_major:
            u = jnp.swapaxes(u_ref[:, steps, :].astype(F32), 0, 1).reshape(n_rows, S5_W).astype(BF16)
        else:
            u = u_ref[rows, :]
        x_scr[rows, :] = _dot_row_halves(u, bdb_ref[...])
        for t in range(s * sub, (s + 1) * sub):
            r = slice(t * batch, (t + 1) * batch)
            hr, hi = (ar * hr - ai * hi + x_scr[r, 0:S5_LANES],
                      ar * hi + ai * hr + x_scr[r, S5_LANES:2 * S5_LANES])
            x_scr[r, 0:S5_LANES] = hr
            x_scr[r, S5_LANES:2 * S5_LANES] = hi
        y = _dot_row_halves(x_scr[rows, :].astype(BF16), bdc_ref[...]) + d_ref[...] * u.astype(F32)
        y = _gelu_tanh(y)
        o = y * _sigmoid(_dot_row_halves(y.astype(BF16), gw_ref[...]) + gb_ref[...])
        if batch_major:
            o_ref[:, steps, :] = jnp.swapaxes(o.reshape(sub, batch, S5_W), 0, 1).astype(BF16)
        else:
            o_ref[rows, :] = o.astype(BF16)
    h_scr[0] = hr
    h_scr[1] = hi
    hr_ref[...] = hr
    hi_ref[...] = hi


def _s5(u, disc, d, glu_w_bf, glu_b, layer, state, batch, seq_len):
    tl = min(S5_TIME_TILE, seq_len)
    has_state = state is not None
    batch_major = u.ndim == 3
    consts = list(disc) + [d, glu_w_bf, glu_b]
    h_spec = pl.BlockSpec((batch, S5_LANES), lambda i: (0, 0))
    if batch_major:
        u_spec = pl.BlockSpec((batch, tl, S5_W), lambda i: (0, i, 0))
    else:
        u_spec = pl.BlockSpec((tl * batch, S5_W), lambda i: (i, 0))
    in_specs = [u_spec] + [_layer_spec(a.shape, layer) for a in consts]
    args = [u] + consts
    if has_state:
        in_specs += [h_spec, h_spec]
        args += list(state)
    return pl.pallas_call(
        functools.partial(_s5_body, tl=tl, batch=batch, has_state=has_state, batch_major=batch_major),
        grid=(seq_len // tl,),
        in_specs=in_specs,
        out_specs=[u_spec, h_spec, h_spec],
        out_shape=[jax.ShapeDtypeStruct(u.shape, BF16),
                   jax.ShapeDtypeStruct((batch, S5_LANES), F32), jax.ShapeDtypeStruct((batch, S5_LANES), F32)],
        scratch_shapes=[pltpu.VMEM((tl * batch, 2 * S5_LANES), F32), pltpu.VMEM((2, batch, S5_LANES), F32)],
        compiler_params=_params(),
        name="s5",
    )(*args)


def _ffn_body(*refs, tm, time_major_batch, final):
    (x_ref, oret_ref, os5_ref, ogla_ref, wout_ref, gffn_ref, win_ref, cw_ref, cb_ref, wo_ref) = refs[:10]
    n = 10
    gfin_ref = None
    if final:
        gfin_ref, n = refs[n], n + 1
    if time_major_batch is not None:
        past_ref, n = refs[n], n + 1
    out_ref, conv_ref, x1_scr, h_scr, carry_scr, act_scr = refs[n:]
    pad = V7X_SUBLANES
    nb = time_major_batch

    if nb is None:
        @pl.when(pl.program_id(1) == 0)
        def _():
            carry_scr[...] = jnp.zeros_like(carry_scr)

    mix = jnp.concatenate([oret_ref[...], os5_ref[...], ogla_ref[...]], axis=1)
    x1_scr[...] = x_ref[...] + _dot(mix, wout_ref[...])
    h_scr[...] = _rmsnorm_rows(x1_scr[...], gffn_ref[...]).astype(BF16)
    h = h_scr[...]

    for c in range(D_FF // ACT_COLS):
        cols = slice(c * ACT_COLS, (c + 1) * ACT_COLS)
        a_c = _dot(h, win_ref[:, c * ACT_COLS:(c + 1) * ACT_COLS])
        gate_c = _dot(h, win_ref[:, D_FF + c * ACT_COLS:D_FF + (c + 1) * ACT_COLS])
        if nb is None:
            ext = jnp.concatenate([carry_scr[:, cols], a_c], axis=0)
            prev1 = ext[pad - 1:pad - 1 + tm]
            prev2 = ext[pad - 2:pad - 2 + tm]
            carry_scr[:, cols] = a_c[tm - pad:tm]
        else:
            prev1 = jnp.concatenate([past_ref[1, :, cols], a_c[0:tm - nb]], axis=0)
            prev2 = jnp.concatenate([past_ref[0, :, cols], past_ref[1, :, cols], a_c[0:tm - 2 * nb]], axis=0)
            conv_ref[:, cols] = a_c[tm - (CONV_W - 1) * nb:tm]
        conv = (cb_ref[:, cols] + prev2 * cw_ref[0:1, cols] + prev1 * cw_ref[1:2, cols]
                + a_c * cw_ref[2:3, cols])
        act_scr[:, cols] = (_gelu_tanh(conv) * gate_c).astype(BF16)

    x2 = x1_scr[...] + _dot(act_scr[...], wo_ref[...])
    out_ref[...] = _rmsnorm_rows(x2, gfin_ref[...]) if final else x2
    if nb is None:
        conv_ref[0] = carry_scr[pad - (CONV_W - 1):pad, :]


def _ffn(x2d, o_ret, o_s5, o_gla, w_out_bf, norm_g, w_in_bf, conv_w, conv_b, w_o_bf, layer, final_g, conv_past,
         batch, seq_len):
    rows = batch * seq_len
    time_major = conv_past is not None
    tm = rows if time_major else min(ROW_TILE, seq_len)
    final = final_g is not None
    consts = [w_out_bf, norm_g, w_in_bf, conv_w, conv_b, w_o_bf]
    stacked = [True] * len(consts)
    big = [True, False, True, False, False, True]
    if final:
        consts.append(final_g.reshape(1, D_MODEL))
        stacked.append(False)
        big.append(False)
    if time_major:
        assert seq_len >= CONV_W - 1 and batch % V7X_SUBLANES == 0
        consts.append(conv_past)
        stacked.append(False)
        big.append(False)
        grid = (1,)
        imap = lambda i: (0, 0)
        conv_rows = (CONV_W - 1) * batch
        conv_spec = pl.BlockSpec((conv_rows, D_FF), imap)
        conv_shape = jax.ShapeDtypeStruct((conv_rows, D_FF), F32)
    else:
        assert seq_len % tm == 0
        nl = seq_len // tm
        grid = (batch, nl)
        imap = lambda b, l: (b * nl + l, 0)
        conv_spec = pl.BlockSpec((1, CONV_W - 1, D_FF), lambda b, l: (b, 0, 0))
        conv_shape = jax.ShapeDtypeStruct((batch, CONV_W - 1, D_FF), F32)
    row_spec = lambda w: pl.BlockSpec((tm, w), imap)
    in_specs = ([row_spec(D_MODEL), row_spec(RET_W), row_spec(S5_W), row_spec(GLA_W)]
                + [_layer_spec(a.shape, layer, b) if s else _const_spec(a.shape)
                   for a, s, b in zip(consts, stacked, big)])
    return pl.pallas_call(
        functools.partial(_ffn_body, tm=tm, time_major_batch=batch if time_major else None, final=final),
        grid=grid,
        in_specs=in_specs,
        out_specs=[row_spec(D_MODEL), conv_spec],
        out_shape=[jax.ShapeDtypeStruct((rows, D_MODEL), F32), conv_shape],
        scratch_shapes=[pltpu.VMEM((tm, D_MODEL), F32), pltpu.VMEM((tm, D_MODEL), BF16),
                        pltpu.VMEM((V7X_SUBLANES, D_FF), F32), pltpu.VMEM((tm, D_FF), BF16)],
        compiler_params=_params(),
        name="ffn",
    )(x2d, o_ret, o_s5, o_gla, *consts)


def _rotary_tables(first_pos, n_pos, repeat=1):
    half = RET_HD // 2
    inv = ROPE_BASE ** (-(np.arange(half, dtype=np.float64) / half))
    ang = (first_pos + np.arange(n_pos)).astype(np.float64)[:, None] * inv[None, :]
    cos, sin = np.cos(ang), np.sin(ang)
    expand = lambda a, b: np.repeat(np.tile(np.concatenate([a, b], axis=1), (1, RET_HEADS)), repeat, axis=0)
    return jnp.asarray(expand(cos, cos), F32), jnp.asarray(expand(-sin, sin), F32)


def _run_prompt_group(x, prm, final_g):
    batch, seq_len, _ = x.shape
    rows = batch * seq_len
    depth = prm["w_in"].shape[0]
    tm = min(MIX_IN_TILE, seq_len)
    cos_tab, sin_tab = _rotary_tables(0, seq_len)
    x2d = x.reshape(rows, D_MODEL)
    outs = []
    for li in range(depth):
        ret, su, gla, glr = _mix_in(x2d, prm["norm_mix_g"], prm["w_in"], li, cos_tab, sin_tab, tm)
        o_ret, s_ret = _retention(ret, prm["ret_ln_g"], prm["ret_ln_b"], li, batch, seq_len)
        o_s5, s5r, s5i = _s5(su.reshape(batch, seq_len, S5_W), prm["s5_disc"], prm["s5_d"], prm["s5_glu_w"],
                             prm["s5_glu_b"], li, None, batch, seq_len)
        o_s5 = o_s5.reshape(rows, S5_W)
        o_gla, s_gla = _gla(gla, glr, prm["gla_gate_w"], prm["gla_gate_b"], prm["gla_norm_g"], li, batch, seq_len)
        x2d, conv_new = _ffn(x2d, o_ret, o_s5, o_gla, prm["w_out"], prm["norm_ffn_g"], prm["ffn_w_in"],
                             prm["ffn_conv_w"], prm["ffn_conv_b"], prm["ffn_w_out"], li,
                             final_g if li == depth - 1 else None, None, batch, seq_len)
        s5_shape = (batch, S5_GROUPS, S5_STATE)
        outs.append((s_ret, s5r.reshape(s5_shape), s5i.reshape(s5_shape), s_gla, conv_new))
    return [x2d.reshape(batch, seq_len, D_MODEL)] + [jnp.stack([o[i] for o in outs]) for i in range(5)]


def _run_sample_group(x, past_len, states, prm, final_g):
    batch, seq_len, _ = x.shape
    assert batch == V7X_LANES
    rows = batch * seq_len
    depth = prm["w_in"].shape[0]
    cos_tab, sin_tab = _rotary_tables(past_len, seq_len, repeat=batch)
    x2d = x.transpose(1, 0, 2).reshape(rows, D_MODEL)
    ret_state = states["ret"].transpose(0, 2, 3, 4, 1)
    gla_state = states["gla"].transpose(0, 2, 3, 4, 1)
    conv_state = states["conv"].transpose(0, 2, 1, 3)
    s5_flat = lambda s: s.reshape(depth, batch, S5_LANES)
    s5r_state, s5i_state = s5_flat(states["s5r"]), s5_flat(states["s5i"])
    outs = []
    s_ret = s_gla = None
    for li in range(depth):
        ret, su, gla, glr = _mix_in(x2d, prm["norm_mix_g"], prm["w_in"], li, cos_tab, sin_tab, rows)
        o_ret, s_ret = _retention_lanes(ret, prm["ret_ln_g_lanes"], prm["ret_ln_b_lanes"], ret_state, s_ret, li,
                                        seq_len)
        o_s5, s5r, s5i = _s5(su, prm["s5_disc"], prm["s5_d"], prm["s5_glu_w"], prm["s5_glu_b"], li,
                             (s5r_state[li], s5i_state[li]), batch, seq_len)
        o_gla, s_gla = _gla_lanes(gla, glr, prm["gla_gate_w"], prm["gla_gate_b"], prm["gla_norm_g_lanes"], gla_state,
                                  s_gla, li, seq_len)
        x2d, conv_new = _ffn(x2d, o_ret, o_s5, o_gla, prm["w_out"], prm["norm_ffn_g"], prm["ffn_w_in"],
                             prm["ffn_conv_w"], prm["ffn_conv_b"], prm["ffn_w_out"], li,
                             final_g if li == depth - 1 else None, conv_state[li], batch, seq_len)
        outs.append((s5r, s5i, conv_new.reshape(CONV_W - 1, batch, D_FF)))
    s5r, s5i, conv_new = (jnp.stack([o[i] for o in outs]) for i in range(3))
    s5_shape = (depth, batch, S5_GROUPS, S5_STATE)
    return [x2d.reshape(seq_len, batch, D_MODEL).transpose(1, 0, 2), s_ret.transpose(0, 4, 1, 2, 3),
            s5r.reshape(s5_shape), s5i.reshape(s5_shape), s_gla.transpose(0, 4, 1, 2, 3),
            conv_new.transpose(0, 2, 1, 3)]


def _prepare_params(norm_mix_g, w_in, ret_norm_g, ret_norm_b, s5_lambda_re, s5_lambda_im, s5_log_dt, s5_b_re, s5_b_im,
                    s5_c_re, s5_c_im, s5_d, s5_glu_w, s5_glu_b, gla_gate_w, gla_gate_b, gla_norm_g, w_out,
                    norm_ffn_g, ffn_w_in, ffn_conv_w, ffn_conv_b, ffn_w_out):
    depth = w_in.shape[0]
    nb = V7X_LANES
    row = lambda a: a.astype(F32).reshape(depth, 1, a.shape[-1])
    ret_lanes = lambda a: jnp.broadcast_to(a.astype(F32).reshape(depth, RET_HEADS, RET_HD, 1),
                                           (depth, RET_HEADS, RET_HD, nb))
    return dict(
        norm_mix_g=row(norm_mix_g), w_in=_mix_weights(w_in),
        ret_ln_g=row(ret_norm_g), ret_ln_b=row(ret_norm_b),
        ret_ln_g_lanes=ret_lanes(ret_norm_g), ret_ln_b_lanes=ret_lanes(ret_norm_b),
        s5_disc=_s5_discretize(s5_lambda_re, s5_lambda_im, s5_log_dt, s5_b_re, s5_b_im, s5_c_re, s5_c_im),
        s5_d=row(s5_d), s5_glu_w=s5_glu_w.astype(BF16), s5_glu_b=row(s5_glu_b),
        gla_gate_w=_pad_gate(gla_gate_w, BF16), gla_gate_b=_pad_gate(row(gla_gate_b), F32),
        gla_norm_g=row(jnp.tile(gla_norm_g, (1, GLA_HEADS))),
        gla_norm_g_lanes=jnp.broadcast_to(gla_norm_g.astype(F32).reshape(depth, GLA_DV, 1), (depth, GLA_DV, nb)),
        w_out=w_out.astype(BF16), norm_ffn_g=row(norm_ffn_g), ffn_w_in=ffn_w_in.astype(BF16),
        ffn_conv_w=jnp.pad(ffn_conv_w.astype(F32), ((0, 0), (0, V7X_SUBLANES - CONV_W), (0, 0))),
        ffn_conv_b=row(ffn_conv_b), ffn_w_out=ffn_w_out.astype(BF16))


def kernel(x_prompt, x_sample, state_ret, state_s5_re, state_s5_im, state_gla, state_ffn_conv, norm_mix_g, w_in, ret_norm_g, ret_norm_b, s5_lambda_re, s5_lambda_im, s5_log_dt, s5_b_re, s5_b_im, s5_c_re, s5_c_im, s5_d, s5_glu_w, s5_glu_b, gla_gate_w, gla_gate_b, gla_norm_g, w_out, norm_ffn_g, ffn_w_in, ffn_conv_w, ffn_conv_b, ffn_w_out, norm_final_g):
    prm = _prepare_params(norm_mix_g, w_in, ret_norm_g, ret_norm_b, s5_lambda_re, s5_lambda_im, s5_log_dt, s5_b_re,
                          s5_b_im, s5_c_re, s5_c_im, s5_d, s5_glu_w, s5_glu_b, gla_gate_w, gla_gate_b, gla_norm_g,
                          w_out, norm_ffn_g, ffn_w_in, ffn_conv_w, ffn_conv_b, ffn_w_out)
    sample_states = dict(ret=state_ret, s5r=state_s5_re, s5i=state_s5_im, gla=state_gla, conv=state_ffn_conv)
    yp, ret_p, s5r_p, s5i_p, gla_p, conv_p = _run_prompt_group(x_prompt, prm, norm_final_g)
    ys, ret_s, s5r_s, s5i_s, gla_s, conv_s = _run_sample_group(x_sample, PAST_LEN, sample_states, prm, norm_final_g)
    return (yp, ys, ret_p, ret_s, s5r_p, s5r_s, s5i_p, s5i_s, gla_p, gla_s, conv_p, conv_s)
```

```python
import functools

import numpy as np
import jax
import jax.numpy as jnp
from jax import lax
from jax.experimental import pallas as pl
from jax.experimental.pallas import tpu as pltpu

F32, BF16 = jnp.float32, jnp.bfloat16

D_MODEL = 1024
RET_W, S5_W, GLA_W = 384, 256, 384
RET_HEADS, RET_HD = 6, 64
RET_PAIRS = RET_HEADS // 2
S5_GROUPS, S5_CH, S5_STATE = 16, 16, 64
S5_LANES = S5_GROUPS * S5_STATE
S5_MIN_NEG = 1e-4
GLA_HEADS, GLA_DK, GLA_DV, GLA_RANK = 4, 48, 96, 16
GLA_KW = GLA_HEADS * GLA_DK
GLA_GATE_TEMP = 16.0
D_FF = 2816
CONV_W = 3
ROPE_BASE = 10000.0
CHUNK = 64
EPS = 1e-6
PAST_LEN = 16384
IN_COLS = 4 * RET_W + S5_W + 2 * GLA_KW + 2 * GLA_W + GLA_RANK
COL_RET, COL_S5, COL_GLA, COL_LR = 0, 4 * RET_W, 4 * RET_W + S5_W, IN_COLS - GLA_RANK
GLA_KP = 256
GLA_Q0, GLA_LR0, GLA_K0, GLA_V0, GLA_G0 = 0, GLA_KW, GLA_KP, 2 * GLA_KP, 2 * GLA_KP + GLA_W
GLA_COLS = 2 * GLA_KP + 2 * GLA_W
MIX_COLS = COL_GLA + GLA_COLS

V7X_SUBLANES = 8
V7X_LANES = 128
VMEM_LIMIT = 58 * 1024 * 1024

ROW_TILE = 512
MIX_IN_TILE = 1024
MIXER_TILE = 2048
S5_TIME_TILE = 256
S5_SUB_STEPS = 64
ACT_COLS = 256


def _dot(a, b):
    return jnp.dot(a, b, preferred_element_type=F32)


def _dot_nt(a, b):
    return lax.dot_general(a, b, (((1,), (1,)), ((), ())), preferred_element_type=F32)


def _dot_tn(a, b):
    return lax.dot_general(a, b, (((0,), (0,)), ((), ())), preferred_element_type=F32)


def _hi_lo(x):
    hi = x.astype(BF16)
    return hi, (x - hi.astype(F32)).astype(BF16)


def _dot_exact_lhs(m, x):
    hi, lo = _hi_lo(x)
    return _dot(m, hi) + _dot(m, lo)


def _dot_row_halves(a, w):
    half = a.shape[0] // 2
    return jnp.concatenate([_dot(a[:half], w), _dot(a[half:], w)], axis=0)


def _pad_gate(a, dtype):
    return jnp.pad(a.astype(dtype), [(0, 0)] * (a.ndim - 1) + [(0, GLA_KP - GLA_KW)])


def _sigmoid(x):
    return 1.0 / (1.0 + jnp.exp(-x))


def _log_sigmoid(z):
    return jnp.minimum(z, 0.0) - jnp.log(1.0 + jnp.exp(-jnp.abs(z)))


def _gelu_tanh(x):
    return 0.5 * x * (1.0 + jnp.tanh(0.7978845608028654 * (x + 0.044715 * (x * x * x))))


def _rmsnorm_rows(x, g):
    return x * lax.rsqrt(jnp.mean(x * x, axis=-1, keepdims=True) + EPS) * g


def _const_spec(shape):
    nd = len(shape)
    return pl.BlockSpec(shape, lambda *_: (0,) * nd)


def _layer_spec(stacked_shape, layer, single_buffer=False):
    nd = len(stacked_shape) - 1
    mode = dict(pipeline_mode=pl.Buffered(1)) if single_buffer else {}
    return pl.BlockSpec((None,) + tuple(stacked_shape[1:]), lambda *_: (layer,) + (0,) * nd, **mode)


def _specs(arrays, layer):
    return [_layer_spec(a.shape, layer) if stacked else _const_spec(a.shape) for a, stacked in arrays]


def _carried_output(buffer, n_before, out_index):
    if buffer is None:
        return [], [], {}
    return [buffer], [pl.BlockSpec(memory_space=pl.ANY)], {n_before: out_index}


def _held_head(slot, head, heads):
    return jnp.where(slot == 0, head, heads - 1)


def _layer_slot_body(*refs, head_fn, n_in, **kw):
    @pl.when(pl.program_id(0) == 0)
    def _():
        head_fn(pl.program_id(1), *refs, n_in=n_in, **kw)

    @pl.when(pl.program_id(0) > 0)
    def _():
        refs[n_in + 1][...] = jnp.zeros_like(refs[n_in + 1])


def _params():
    return pltpu.CompilerParams(vmem_limit_bytes=VMEM_LIMIT)


def _to_lanes(x):
    w = x.shape[1]
    pad = -w % V7X_LANES
    if pad:
        x = jnp.concatenate([x, jnp.zeros((x.shape[0], pad), F32)], axis=1)
    return x.T[0:w]


def _mix_in_body(x_ref, g_ref, win_ref, cos_ref, sin_ref, ret_ref, su_ref, gla_ref, glr_ref, w_ref):
    @pl.when(pl.program_id(0) == 0)
    def _():
        k0, v0 = COL_GLA + GLA_KW, COL_GLA + 2 * GLA_KW
        zeros = lambda n: jnp.zeros((D_MODEL, n), BF16)
        w_ref[...] = jnp.concatenate(
            [win_ref[:, 0:k0], win_ref[:, COL_LR:IN_COLS], zeros(GLA_KP - GLA_KW - GLA_RANK),
             win_ref[:, k0:v0], zeros(GLA_KP - GLA_KW), win_ref[:, v0:COL_LR]], axis=1)

    h = _rmsnorm_rows(x_ref[...], g_ref[...]).astype(BF16)
    ret = _dot(h, w_ref[:, COL_RET:COL_S5])
    cos = cos_ref[...]
    sin = sin_ref[...]
    lane = lax.broadcasted_iota(jnp.int32, cos.shape, 1)
    first_half = (lane & (RET_HD // 2)) == 0

    def rotary(z):
        swapped = jnp.where(first_half, pltpu.roll(z, RET_W - RET_HD // 2, 1), pltpu.roll(z, RET_HD // 2, 1))
        return z * cos + swapped * sin

    ret_ref[:, 0:RET_W] = rotary(ret[:, 0:RET_W]).astype(BF16)
    ret_ref[:, RET_W:2 * RET_W] = (rotary(ret[:, RET_W:2 * RET_W]) * RET_HD ** -0.5).astype(BF16)
    ret_ref[:, 2 * RET_W:] = ret[:, 2 * RET_W:].astype(BF16)
    su_ref[...] = _dot(h, w_ref[:, COL_S5:COL_GLA]).astype(BF16)
    gla = _dot(h, w_ref[:, COL_GLA:MIX_COLS])
    glane = lax.broadcasted_iota(jnp.int32, (1, GLA_COLS), 1)
    scale = jnp.where(glane < GLA_KW, GLA_DK ** -0.5, jnp.where(glane < GLA_KP, 0.0, 1.0))
    gla_ref[...] = (gla * scale).astype(BF16)
    glr_ref[...] = gla[:, GLA_LR0:GLA_LR0 + GLA_RANK]


def _mix_in(x2d, norm_g, w_in_bf, layer, cos_tab, sin_tab, tm):
    rows = x2d.shape[0]
    n_tab = cos_tab.shape[0] // tm
    row_spec = lambda w: pl.BlockSpec((tm, w), lambda i: (i, 0))
    tab_spec = pl.BlockSpec((tm, RET_W), lambda i: (i % n_tab, 0))
    return pl.pallas_call(
        _mix_in_body,
        grid=(rows // tm,),
        in_specs=[row_spec(D_MODEL), _layer_spec(norm_g.shape, layer), _layer_spec(w_in_bf.shape, layer, True),
                  tab_spec, tab_spec],
        out_specs=[row_spec(4 * RET_W), row_spec(S5_W), row_spec(GLA_COLS), row_spec(GLA_RANK)],
        out_shape=[jax.ShapeDtypeStruct((rows, 4 * RET_W), BF16), jax.ShapeDtypeStruct((rows, S5_W), BF16),
                   jax.ShapeDtypeStruct((rows, GLA_COLS), BF16), jax.ShapeDtypeStruct((rows, GLA_RANK), F32)],
        scratch_shapes=[pltpu.VMEM((D_MODEL, MIX_COLS), BF16)],
        compiler_params=_params(),
        name="mix_in",
    )(x2d, norm_g, w_in_bf, cos_tab, sin_tab)


def _retention_gammas():
    return 1.0 - 2.0 ** (-5.0 - np.arange(RET_HEADS))


def _retention_consts():
    t = np.arange(CHUNK)
    gam = _retention_gammas()
    causal = t[:, None] >= t[None, :]
    diff = np.maximum(t[:, None] - t[None, :], 0)
    dmask = np.zeros((RET_PAIRS, CHUNK, 2 * CHUNK))
    cdec = np.zeros((RET_PAIRS, 2 * RET_HD, 2 * RET_HD))
    for p in range(RET_PAIRS):
        for s in range(2):
            g = gam[2 * p + s]
            dmask[p, :, s * CHUNK:(s + 1) * CHUNK] = np.where(causal, g ** diff, 0.0)
            cdec[p, s * RET_HD:(s + 1) * RET_HD, s * RET_HD:(s + 1) * RET_HD] = g ** CHUNK
    lane_gam = np.repeat(gam, RET_HD)[None, :]
    qdec = lane_gam ** (t[:, None] + 1.0)
    kdec = lane_gam ** (CHUNK - 1.0 - t[:, None])
    bd = (cdec[0] > 0).astype(np.float32)
    head_mask = np.stack([np.arange(2 * RET_HD) < RET_HD, np.arange(2 * RET_HD) >= RET_HD]).astype(np.float32)
    ones_blk = np.kron(np.eye(RET_HEADS), np.ones((RET_HD, RET_HD)))
    f = lambda a: jnp.asarray(a, F32)
    return dict(dmask=f(dmask), cdec=f(cdec), qdec=qdec, kdec=kdec, bd=f(bd),
                head_mask=jnp.asarray(head_mask, BF16), ones_blk=jnp.asarray(ones_blk, BF16))


def _retention_body(ret_ref, dmask_ref, cdec_ref, qdec_ref, kdec_ref, bd_ref, hm_ref, ones_ref, lng_ref, lnb_ref,
                    o_ref, sout_ref, s_scr, o_scr, *, tl):
    @pl.when(pl.program_id(1) == 0)
    def _():
        s_scr[...] = jnp.zeros_like(s_scr)

    bd = bd_ref[...]
    m0 = hm_ref[0:1, :]
    m1 = hm_ref[1:2, :]
    q = ret_ref[:, 0:RET_W]
    k = ret_ref[:, RET_W:2 * RET_W]
    v = ret_ref[:, 2 * RET_W:3 * RET_W]
    q_start = (q.astype(F32) * qdec_ref[...]).astype(BF16)
    k_end = (k.astype(F32) * kdec_ref[...]).astype(BF16)
    blocks = [(n, p) for n in range(tl // CHUNK) for p in range(RET_PAIRS)]

    def part(a, n, p):
        return a[n * CHUNK:(n + 1) * CHUNK, 2 * RET_HD * p:2 * RET_HD * (p + 1)]

    scores, o, kv = {}, {}, {}
    for n, p in blocks:
        kp = part(k, n, p)
        kk = jnp.concatenate([kp * m0, kp * m1], axis=0)
        scores[n, p] = (_dot_nt(part(q, n, p), kk) * dmask_ref[p]).astype(BF16)
    for n, p in blocks:
        vp = part(v, n, p)
        o[n, p] = _dot(scores[n, p], jnp.concatenate([vp * m0, vp * m1], axis=0))
    for n, p in blocks:
        kv[n, p] = _dot_tn(part(k_end, n, p), part(v, n, p)) * bd
    for n, p in blocks:
        s_prev = s_scr[p]
        o_scr[n * CHUNK:(n + 1) * CHUNK, 2 * RET_HD * p:2 * RET_HD * (p + 1)] = (
            o[n, p] + _dot(part(q_start, n, p), s_prev.astype(BF16)))
        s_scr[p] = s_prev * cdec_ref[p] + kv[n, p]

    o_all = o_scr[...]
    ones_blk = ones_ref[...]
    mu = _dot_row_halves(o_all.astype(BF16), ones_blk) * (1.0 / RET_HD)
    cen = o_all - mu
    var = _dot_row_halves((cen * cen).astype(BF16), ones_blk) * (1.0 / RET_HD)
    y = cen * lax.rsqrt(var + EPS) * lng_ref[...] + lnb_ref[...]
    gate = ret_ref[:, 3 * RET_W:4 * RET_W].astype(F32)
    o_ref[...] = (y * gate * _sigmoid(gate)).astype(BF16)
    for p in range(RET_PAIRS):
        s = s_scr[p]
        sout_ref[0, 2 * p] = s[0:RET_HD, 0:RET_HD]
        sout_ref[0, 2 * p + 1] = s[RET_HD:2 * RET_HD, RET_HD:2 * RET_HD]


def _retention(ret2d, ln_g, ln_b, layer, batch, seq_len):
    assert seq_len % CHUNK == 0
    c = _retention_consts()
    tl = min(MIXER_TILE, seq_len)
    nl = seq_len // tl
    tile_rows = lambda a: jnp.asarray(np.tile(a, (tl // CHUNK, 1)), F32)
    consts = [(c["dmask"], False), (c["cdec"], False), (tile_rows(c["qdec"]), False), (tile_rows(c["kdec"]), False),
              (c["bd"], False), (c["head_mask"], False), (c["ones_blk"], False), (ln_g, True), (ln_b, True)]
    state_shape = (batch, RET_HEADS, RET_HD, RET_HD)
    return pl.pallas_call(
        functools.partial(_retention_body, tl=tl),
        grid=(batch, nl),
        in_specs=[pl.BlockSpec((tl, 4 * RET_W), lambda b, l: (b * nl + l, 0))] + _specs(consts, layer),
        out_specs=[pl.BlockSpec((tl, RET_W), lambda b, l: (b * nl + l, 0)),
                   pl.BlockSpec((1,) + state_shape[1:], lambda b, l: (b, 0, 0, 0))],
        out_shape=[jax.ShapeDtypeStruct((batch * seq_len, RET_W), BF16), jax.ShapeDtypeStruct(state_shape, F32)],
        scratch_shapes=[pltpu.VMEM((RET_PAIRS, 2 * RET_HD, 2 * RET_HD), F32), pltpu.VMEM((tl, RET_W), F32)],
        compiler_params=_params(),
        name="retention",
    )(ret2d, *[a for a, _ in consts])


def _retention_lanes_head(h, *refs, seq_len, n_in):
    ret_ref, tab_ref, lng_ref, lnb_ref, s_ref = refs[:5]
    o_ref, sout_ref, q_scr, k_scr, v_scr, g_scr, o_scr = refs[n_in:]
    nb = V7X_LANES
    steps = range(seq_len)

    @pl.when(h == 0)
    def _():
        for t in steps:
            blk = ret_ref[t * nb:(t + 1) * nb, :].astype(F32)
            for scr, off in ((q_scr, 0), (k_scr, RET_W), (v_scr, 2 * RET_W), (g_scr, 3 * RET_W)):
                tr = _to_lanes(blk[:, off:off + RET_W])
                for hh in range(RET_HEADS):
                    scr[t, hh] = tr[hh * RET_HD:(hh + 1) * RET_HD]

    tab = tab_ref[h]
    row = lambda r: tab[r:r + 1, :]
    q = [q_scr[t, h] for t in steps]
    k = [k_scr[t, h] for t in steps]
    v = [v_scr[t, h] for t in steps]
    o = []
    for i in steps:
        acc = None
        for j in range(i + 1):
            s = jnp.sum(q[i] * k[j], axis=0, keepdims=True)
            if i > j:
                s = s * row(i - j - 1)
            acc = s * v[j] if acc is None else acc + s * v[j]
        o.append(acc)
    q_start = [q[t] * row(t) for t in steps]
    k_end = [k[t] * row(seq_len + t) for t in steps]
    chunk_decay = row(2 * seq_len)
    for d in range(RET_HD):
        s_d = s_ref[d]
        new = s_d * chunk_decay
        for t in steps:
            o[t] = o[t] + q_start[t][d:d + 1, :] * s_d
            new = new + k_end[t][d:d + 1, :] * v[t]
        sout_ref[d] = new
    for t in steps:
        mu = jnp.mean(o[t], axis=0, keepdims=True)
        cen = o[t] - mu
        var = jnp.mean(cen * cen, axis=0, keepdims=True)
        gate = g_scr[t, h]
        o_scr[t, h] = (cen * lax.rsqrt(var + EPS) * lng_ref[h] + lnb_ref[h]) * gate * _sigmoid(gate)

    @pl.when(h == RET_HEADS - 1)
    def _():
        for t in steps:
            o_ref[t * nb:(t + 1) * nb, :] = o_scr[t].reshape(RET_W, nb).T.astype(BF16)


def _retention_lanes(ret2d, ln_g, ln_b, state, new_state, layer, seq_len):
    nb = V7X_LANES
    gam = _retention_gammas()[:, None]
    t = np.arange(seq_len)[None, :]
    rows = np.concatenate([gam ** (t + 1.0), gam ** (seq_len - 1.0 - t), gam ** (seq_len + 0.0 * t[:, :1])], axis=1)
    n_rows = -(-rows.shape[1] // V7X_SUBLANES) * V7X_SUBLANES
    tab = np.zeros((RET_HEADS, n_rows, nb))
    tab[:, :rows.shape[1], :] = rows[:, :, None]
    consts = [(jnp.asarray(tab, F32), False), (ln_g, True), (ln_b, True)]
    tile = pltpu.VMEM((seq_len, RET_HEADS, RET_HD, nb), F32)
    block = (None, None, RET_HD, RET_HD, nb)
    prev, prev_specs, alias = _carried_output(new_state, n_before=5, out_index=1)
    slots = 1 if prev else state.shape[0] - layer
    return pl.pallas_call(
        functools.partial(_layer_slot_body, head_fn=_retention_lanes_head, n_in=5 + len(prev), seq_len=seq_len),
        grid=(slots, RET_HEADS),
        in_specs=[_const_spec(ret2d.shape)] + _specs(consts, layer)
                 + [pl.BlockSpec(block, lambda g, h: (layer, _held_head(g, h, RET_HEADS), 0, 0, 0))] + prev_specs,
        out_specs=[_const_spec((seq_len * nb, RET_W)), pl.BlockSpec(block, lambda g, h: (layer + g, h, 0, 0, 0))],
        out_shape=[jax.ShapeDtypeStruct((seq_len * nb, RET_W), BF16), jax.ShapeDtypeStruct(state.shape, F32)],
        scratch_shapes=[tile] * 5,
        input_output_aliases=alias,
        compiler_params=_params(),
        name="retention_lanes",
    )(ret2d, *[a for a, _ in consts], state, *prev)


def _gla_consts():
    t = np.arange(CHUNK)
    causal = t[:, None] >= t[None, :]
    pad = GLA_KP - GLA_KW
    kmask = np.pad(np.kron(np.eye(GLA_HEADS), np.ones((1, GLA_DK))), ((0, 0), (0, pad)))
    vmask = np.kron(np.eye(GLA_HEADS), np.ones((1, GLA_DV)))
    bd = np.pad(np.kron(np.eye(GLA_HEADS), np.ones((GLA_DK, GLA_DV))), ((0, pad), (0, 0)))
    ones_blk = np.kron(np.eye(GLA_HEADS), np.ones((GLA_DV, GLA_DV)))
    b = lambda a: jnp.asarray(a, BF16)
    return dict(tril=b(causal), causal4=jnp.asarray(np.tile(causal, (1, GLA_HEADS)), F32), kmask=b(kmask),
                vmask=b(vmask), bd=jnp.asarray(bd, F32), ones_blk=b(ones_blk))


def _gla_body(gla_ref, glr_ref, gw_ref, gb_ref, tril_ref, causal_ref, kmask_ref, vmask_ref, bd_ref, ones_blk_ref,
              block_ref, ng_ref, o_ref, sout_ref, s_scr, o_scr, *, tl):
    @pl.when(pl.program_id(1) == 0)
    def _():
        s_scr[...] = jnp.zeros_like(s_scr)

    bd = bd_ref[...]
    z = _dot_row_halves(glr_ref[...].astype(BF16), gw_ref[...]) + gb_ref[...]
    lg = _log_sigmoid(z) * (1.0 / GLA_GATE_TEMP)
    q0, k0, v0, g0 = GLA_Q0, GLA_K0, GLA_V0, GLA_G0
    chunks = range(tl // CHUNK)

    def rows(a, n):
        return a[n * CHUNK:(n + 1) * CHUNK]

    cums = [_dot_exact_lhs(tril_ref[...], rows(lg, n)) for n in chunks]
    b_cum = jnp.concatenate(cums, axis=0)
    b_end = jnp.concatenate([jnp.broadcast_to(c[CHUNK - 1:CHUNK, :], (CHUNK, GLA_KP)) for c in cums], axis=0)
    q_in = (gla_ref[:, q0:q0 + GLA_KP].astype(F32) * jnp.exp(b_cum)).astype(BF16)
    kf = gla_ref[:, k0:k0 + GLA_KP].astype(F32)
    k_in = (kf * jnp.exp(-b_cum)).astype(BF16)
    k_dec = (kf * jnp.exp(b_end - b_cum)).astype(BF16)
    v = gla_ref[:, v0:g0]
    lg_hi, lg_lo = _hi_lo(lg)
    block_decay = jnp.exp(_dot_tn(lg_hi, block_ref[...]) + _dot_tn(lg_lo, block_ref[...]))

    scores, o, kv = {}, {}, {}
    for n in chunks:
        kn = rows(k_in, n)
        kk = jnp.concatenate([kn * kmask_ref[h:h + 1, :] for h in range(GLA_HEADS)], axis=0)
        scores[n] = (_dot_nt(rows(q_in, n), kk) * causal_ref[...]).astype(BF16)
    for n in chunks:
        vn = rows(v, n)
        vv = jnp.concatenate([vn * vmask_ref[h:h + 1, :] for h in range(GLA_HEADS)], axis=0)
        o[n] = _dot(scores[n], vv)
    for n in chunks:
        kv[n] = _dot_tn(rows(k_dec, n), rows(v, n)) * bd
    for n in chunks:
        s_prev = s_scr[...]
        o_scr[n * CHUNK:(n + 1) * CHUNK, :] = o[n] + _dot(rows(q_in, n), s_prev.astype(BF16))
        s_scr[...] = s_prev * jnp.broadcast_to(block_decay[:, n:n + 1], (GLA_KP, GLA_W)) + kv[n]

    o_all = o_scr[...]
    ms = _dot_row_halves((o_all * o_all).astype(BF16), ones_blk_ref[...]) * (1.0 / GLA_DV)
    gate = gla_ref[:, g0:g0 + GLA_W].astype(F32)
    o_ref[...] = (o_all * lax.rsqrt(ms + EPS) * ng_ref[...] * gate * _sigmoid(gate)).astype(BF16)
    s = s_scr[...]
    for h in range(GLA_HEADS):
        sout_ref[0, h] = s[h * GLA_DK:(h + 1) * GLA_DK, h * GLA_DV:(h + 1) * GLA_DV]


def _gla(gla2d, glr2d, gate_w, gate_b, norm_g, layer, batch, seq_len):
    assert seq_len % CHUNK == 0
    c = _gla_consts()
    tl = min(MIXER_TILE, seq_len)
    nl = seq_len // tl
    assert tl // CHUNK <= V7X_LANES
    block = (np.arange(tl) // CHUNK)[:, None] == np.arange(V7X_LANES)[None, :]
    consts = [(gate_w, True), (gate_b, True), (c["tril"], False), (c["causal4"], False), (c["kmask"], False),
              (c["vmask"], False), (c["bd"], False), (c["ones_blk"], False), (jnp.asarray(block, BF16), False),
              (norm_g, True)]
    row_spec = lambda w: pl.BlockSpec((tl, w), lambda b, l: (b * nl + l, 0))
    state_shape = (batch, GLA_HEADS, GLA_DK, GLA_DV)
    return pl.pallas_call(
        functools.partial(_gla_body, tl=tl),
        grid=(batch, nl),
        in_specs=[row_spec(GLA_COLS), row_spec(GLA_RANK)] + _specs(consts, layer),
        out_specs=[row_spec(GLA_W), pl.BlockSpec((1,) + state_shape[1:], lambda b, l: (b, 0, 0, 0))],
        out_shape=[jax.ShapeDtypeStruct((batch * seq_len, GLA_W), BF16), jax.ShapeDtypeStruct(state_shape, F32)],
        scratch_shapes=[pltpu.VMEM((GLA_KP, GLA_W), F32), pltpu.VMEM((tl, GLA_W), F32)],
        compiler_params=_params(),
        name="gla",
    )(gla2d, glr2d, *[a for a, _ in consts])


def _gla_lanes_head(h, *refs, seq_len, n_in):
    gla_ref, glr_ref, gw_ref, gb_ref, ng_ref, s_ref = refs[:6]
    o_ref, sout_ref, q_scr, k_scr, kd_scr, v_scr, g_scr, dec_scr, o_scr = refs[n_in:]
    nb = V7X_LANES
    steps = range(seq_len)
    q0, k0, v0, g0 = GLA_Q0, GLA_K0, GLA_V0, GLA_G0
    kw = slice(0, GLA_KW)

    def split(scr, t, tr, width):
        for hh in range(GLA_HEADS):
            scr[t, hh] = tr[hh * width:(hh + 1) * width]

    @pl.when(h == 0)
    def _():
        z = _dot_row_halves(glr_ref[...].astype(BF16), gw_ref[...]) + gb_ref[...]
        lg = (_log_sigmoid(z) * (1.0 / GLA_GATE_TEMP))[:, kw]
        cums = []
        for t in steps:
            lg_t = lg[t * nb:(t + 1) * nb]
            cums.append(lg_t if t == 0 else cums[-1] + lg_t)
        b_end = cums[-1]
        dec = _to_lanes(jnp.exp(b_end))
        for hh in range(GLA_HEADS):
            dec_scr[hh] = dec[hh * GLA_DK:(hh + 1) * GLA_DK]
        for t in steps:
            blk = gla_ref[t * nb:(t + 1) * nb, :].astype(F32)
            kf = blk[:, k0:k0 + GLA_KW]
            split(q_scr, t, _to_lanes(blk[:, q0:q0 + GLA_KW] * jnp.exp(cums[t])), GLA_DK)
            split(k_scr, t, _to_lanes(kf * jnp.exp(-cums[t])), GLA_DK)
            split(kd_scr, t, _to_lanes(kf * jnp.exp(b_end - cums[t])), GLA_DK)
            split(v_scr, t, _to_lanes(blk[:, v0:g0]), GLA_DV)
            split(g_scr, t, _to_lanes(blk[:, g0:g0 + GLA_W]), GLA_DV)

    q = [q_scr[t, h] for t in steps]
    k = [k_scr[t, h] for t in steps]
    kd = [kd_scr[t, h] for t in steps]
    v = [v_scr[t, h] for t in steps]
    dec = dec_scr[h]
    o = []
    for i in steps:
        acc = None
        for j in range(i + 1):
            s = jnp.sum(q[i] * k[j], axis=0, keepdims=True)
            acc = s * v[j] if acc is None else acc + s * v[j]
        o.append(acc)
    for i in range(GLA_DK):
        s_i = s_ref[i]
        new = s_i * dec[i:i + 1, :]
        for t in steps:
            o[t] = o[t] + q[t][i:i + 1, :] * s_i
            new = new + kd[t][i:i + 1, :] * v[t]
        sout_ref[i] = new
    for t in steps:
        ms = jnp.mean(o[t] * o[t], axis=0, keepdims=True)
        gate = g_scr[t, h]
        o_scr[t, h] = o[t] * lax.rsqrt(ms + EPS) * ng_ref[...] * gate * _sigmoid(gate)

    @pl.when(h == GLA_HEADS - 1)
    def _():
        for t in steps:
            o_ref[t * nb:(t + 1) * nb, :] = o_scr[t].reshape(GLA_W, nb).T.astype(BF16)


def _gla_lanes(gla2d, glr2d, gate_w, gate_b, norm_g, state, new_state, layer, seq_len):
    nb = V7X_LANES
    consts = [(gate_w, True), (gate_b, True), (norm_g, True)]
    k_tile = pltpu.VMEM((seq_len, GLA_HEADS, GLA_DK, nb), F32)
    v_tile = pltpu.VMEM((seq_len, GLA_HEADS, GLA_DV, nb), F32)
    block = (None, None, GLA_DK, GLA_DV, nb)
    prev, prev_specs, alias = _carried_output(new_state, n_before=6, out_index=1)
    slots = 1 if prev else state.shape[0] - layer
    return pl.pallas_call(
        functools.partial(_layer_slot_body, head_fn=_gla_lanes_head, n_in=6 + len(prev), seq_len=seq_len),
        grid=(slots, GLA_HEADS),
        in_specs=[_const_spec(gla2d.shape), _const_spec(glr2d.shape)] + _specs(consts, layer)
                 + [pl.BlockSpec(block, lambda g, h: (layer, _held_head(g, h, GLA_HEADS), 0, 0, 0))] + prev_specs,
        out_specs=[_const_spec((seq_len * nb, GLA_W)), pl.BlockSpec(block, lambda g, h: (layer + g, h, 0, 0, 0))],
        out_shape=[jax.ShapeDtypeStruct((seq_len * nb, GLA_W), BF16), jax.ShapeDtypeStruct(state.shape, F32)],
        scratch_shapes=[k_tile, k_tile, k_tile, v_tile, v_tile, pltpu.VMEM((GLA_HEADS, GLA_DK, nb), F32), v_tile],
        input_output_aliases=alias,
        compiler_params=_params(),
        name="gla_lanes",
    )(gla2d, glr2d, *[a for a, _ in consts], state, *prev)


def _s5_discretize(lam_re, lam_im, log_dt, b_re, b_im, c_re, c_im):
    lr = jnp.minimum(lam_re.astype(F32), -S5_MIN_NEG)
    li = lam_im.astype(F32)
    dt = jnp.exp(log_dt.astype(F32))[..., None]
    mag = jnp.exp(lr * dt)
    ar = mag * jnp.cos(li * dt)
    ai = mag * jnp.sin(li * dt)
    den = lr * lr + li * li
    cr = ((ar - 1.0) * lr + ai * li) / den
    ci = (ai * lr - (ar - 1.0) * li) / den
    b_re, b_im = b_re.astype(F32), b_im.astype(F32)
    bbar_re = cr[..., None] * b_re - ci[..., None] * b_im
    bbar_im = cr[..., None] * b_im + ci[..., None] * b_re
    eye = jnp.eye(S5_GROUPS, dtype=F32)
    depth = lr.shape[0]
    in_blk = lambda b: jnp.einsum("lgpi,gh->lgihp", b, eye).reshape(depth, S5_W, S5_LANES)
    out_blk = lambda c: jnp.einsum("lgop,gh->lgpho", c.astype(F32), eye).reshape(depth, S5_LANES, S5_W)
    bdb = jnp.concatenate([in_blk(bbar_re), in_blk(bbar_im)], axis=2).astype(BF16)
    bdc = jnp.concatenate([out_blk(c_re), -out_blk(c_im)], axis=1).astype(BF16)
    return ar.reshape(depth, 1, S5_LANES), ai.reshape(depth, 1, S5_LANES), bdb, bdc


def _s5_body(*refs, tl, batch, has_state, batch_major):
    u_ref, ar_ref, ai_ref, bdb_ref, bdc_ref, d_ref, gw_ref, gb_ref = refs[:8]
    n_in = 10 if has_state else 8
    o_ref, hr_ref, hi_ref, x_scr, h_scr = refs[n_in:]

    @pl.when(pl.program_id(0) == 0)
    def _():
        if has_state:
            h_scr[0] = refs[8][...]
            h_scr[1] = refs[9][...]
        else:
            h_scr[...] = jnp.zeros_like(h_scr)

    ar = jnp.broadcast_to(ar_ref[...], (batch, S5_LANES))
    ai = jnp.broadcast_to(ai_ref[...], (batch, S5_LANES))
    hr, hi = h_scr[0], h_scr[1]
    sub = min(tl, S5_SUB_STEPS)
    n_rows = sub * batch
    for s in range(tl // sub):
        rows = slice(s * n_rows, (s + 1) * n_rows)
        steps = slice(s * sub, (s + 1) * sub)
        if batch_major:
            u = jnp.swapaxes(u_ref[:, steps, :].astype(F32), 0, 1).reshape(n_rows, S5_W).astype(BF16)
        else:
            u = u_ref[rows, :]
        x_scr[rows, :] = _dot_row_halves(u, bdb_ref[...])
        for t in range(s * sub, (s + 1) * sub):
            r = slice(t * batch, (t + 1) * batch)
            hr, hi = (ar * hr - ai * hi + x_scr[r, 0:S5_LANES],
                      ar * hi + ai * hr + x_scr[r, S5_LANES:2 * S5_LANES])
            x_scr[r, 0:S5_LANES] = hr
            x_scr[r, S5_LANES:2 * S5_LANES] = hi
        y = _dot_row_halves(x_scr[rows, :].astype(BF16), bdc_ref[...]) + d_ref[...] * u.astype(F32)
        y = _gelu_tanh(y)
        o = y * _sigmoid(_dot_row_halves(y.astype(BF16), gw_ref[...]) + gb_ref[...])
        if batch_major:
            o_ref[:, steps, :] = jnp.swapaxes(o.reshape(sub, batch, S5_W), 0, 1).astype(BF16)
        else:
            o_ref[rows, :] = o.astype(BF16)
    h_scr[0] = hr
    h_scr[1] = hi
    hr_ref[...] = hr
    hi_ref[...] = hi


def _s5(u, disc, d, glu_w_bf, glu_b, layer, state, batch, seq_len):
    tl = min(S5_TIME_TILE, seq_len)
    has_state = state is not None
    batch_major = u.ndim == 3
    consts = list(disc) + [d, glu_w_bf, glu_b]
    h_spec = pl.BlockSpec((batch, S5_LANES), lambda i: (0, 0))
    if batch_major:
        u_spec = pl.BlockSpec((batch, tl, S5_W), lambda i: (0, i, 0))
    else:
        u_spec = pl.BlockSpec((tl * batch, S5_W), lambda i: (i, 0))
    in_specs = [u_spec] + [_layer_spec(a.shape, layer) for a in consts]
    args = [u] + consts
    if has_state:
        in_specs += [h_spec, h_spec]
        args += list(state)
    return pl.pallas_call(
        functools.partial(_s5_body, tl=tl, batch=batch, has_state=has_state, batch_major=batch_major),
        grid=(seq_len // tl,),
        in_specs=in_specs,
        out_specs=[u_spec, h_spec, h_spec],
        out_shape=[jax.ShapeDtypeStruct(u.shape, BF16),
                   jax.ShapeDtypeStruct((batch, S5_LANES), F32), jax.ShapeDtypeStruct((batch, S5_LANES), F32)],
        scratch_shapes=[pltpu.VMEM((tl * batch, 2 * S5_LANES), F32), pltpu.VMEM((2, batch, S5_LANES), F32)],
        compiler_params=_params(),
        name="s5",
    )(*args)


def _ffn_body(*refs, tm, time_major_batch, final):
    (x_ref, oret_ref, os5_ref, ogla_ref, wout_ref, gffn_ref, win_ref, cw_ref, cb_ref, wo_ref) = refs[:10]
    n = 10
    gfin_ref = None
    if final:
        gfin_ref, n = refs[n], n + 1
    if time_major_batch is not None:
        past_ref, n = refs[n], n + 1
    out_ref, conv_ref, x1_scr, h_scr, carry_scr, act_scr = refs[n:]
    pad = V7X_SUBLANES
    nb = time_major_batch

    if nb is None:
        @pl.when(pl.program_id(1) == 0)
        def _():
            carry_scr[...] = jnp.zeros_like(carry_scr)

    mix = jnp.concatenate([oret_ref[...], os5_ref[...], ogla_ref[...]], axis=1)
    x1_scr[...] = x_ref[...] + _dot(mix, wout_ref[...])
    h_scr[...] = _rmsnorm_rows(x1_scr[...], gffn_ref[...]).astype(BF16)
    h = h_scr[...]

    for c in range(D_FF // ACT_COLS):
        cols = slice(c * ACT_COLS, (c + 1) * ACT_COLS)
        a_c = _dot(h, win_ref[:, c * ACT_COLS:(c + 1) * ACT_COLS])
        gate_c = _dot(h, win_ref[:, D_FF + c * ACT_COLS:D_FF + (c + 1) * ACT_COLS])
        if nb is None:
            ext = jnp.concatenate([carry_scr[:, cols], a_c], axis=0)
            prev1 = ext[pad - 1:pad - 1 + tm]
            prev2 = ext[pad - 2:pad - 2 + tm]
            carry_scr[:, cols] = a_c[tm - pad:tm]
        else:
            prev1 = jnp.concatenate([past_ref[1, :, cols], a_c[0:tm - nb]], axis=0)
            prev2 = jnp.concatenate([past_ref[0, :, cols], past_ref[1, :, cols], a_c[0:tm - 2 * nb]], axis=0)
            conv_ref[:, cols] = a_c[tm - (CONV_W - 1) * nb:tm]
        conv = (cb_ref[:, cols] + prev2 * cw_ref[0:1, cols] + prev1 * cw_ref[1:2, cols]
                + a_c * cw_ref[2:3, cols])
        act_scr[:, cols] = (_gelu_tanh(conv) * gate_c).astype(BF16)

    x2 = x1_scr[...] + _dot(act_scr[...], wo_ref[...])
    out_ref[...] = _rmsnorm_rows(x2, gfin_ref[...]) if final else x2
    if nb is None:
        conv_ref[0] = carry_scr[pad - (CONV_W - 1):pad, :]


def _ffn(x2d, o_ret, o_s5, o_gla, w_out_bf, norm_g, w_in_bf, conv_w, conv_b, w_o_bf, layer, final_g, conv_past,
         batch, seq_len):
    rows = batch * seq_len
    time_major = conv_past is not None
    tm = rows if time_major else min(ROW_TILE, seq_len)
    final = final_g is not None
    consts = [w_out_bf, norm_g, w_in_bf, conv_w, conv_b, w_o_bf]
    stacked = [True] * len(consts)
    big = [True, False, True, False, False, True]
    if final:
        consts.append(final_g.reshape(1, D_MODEL))
        stacked.append(False)
        big.append(False)
    if time_major:
        assert seq_len >= CONV_W - 1 and batch % V7X_SUBLANES == 0
        consts.append(conv_past)
        stacked.append(False)
        big.append(False)
        grid = (1,)
        imap = lambda i: (0, 0)
        conv_rows = (CONV_W - 1) * batch
        conv_spec = pl.BlockSpec((conv_rows, D_FF), imap)
        conv_shape = jax.ShapeDtypeStruct((conv_rows, D_FF), F32)
    else:
        assert seq_len % tm == 0
        nl = seq_len // tm
        grid = (batch, nl)
        imap = lambda b, l: (b * nl + l, 0)
        conv_spec = pl.BlockSpec((1, CONV_W - 1, D_FF), lambda b, l: (b, 0, 0))
        conv_shape = jax.ShapeDtypeStruct((batch, CONV_W - 1, D_FF), F32)
    row_spec = lambda w: pl.BlockSpec((tm, w), imap)
    in_specs = ([row_spec(D_MODEL), row_spec(RET_W), row_spec(S5_W), row_spec(GLA_W)]
                + [_layer_spec(a.shape, layer, b) if s else _const_spec(a.shape)
                   for a, s, b in zip(consts, stacked, big)])
    return pl.pallas_call(
        functools.partial(_ffn_body, tm=tm, time_major_batch=batch if time_major else None, final=final),
        grid=grid,
        in_specs=in_specs,
        out_specs=[row_spec(D_MODEL), conv_spec],
        out_shape=[jax.ShapeDtypeStruct((rows, D_MODEL), F32), conv_shape],
        scratch_shapes=[pltpu.VMEM((tm, D_MODEL), F32), pltpu.VMEM((tm, D_MODEL), BF16),
                        pltpu.VMEM((V7X_SUBLANES, D_FF), F32), pltpu.VMEM((tm, D_FF), BF16)],
        compiler_params=_params(),
        name="ffn",
    )(x2d, o_ret, o_s5, o_gla, *consts)


def _rotary_tables(first_pos, n_pos, repeat=1):
    half = RET_HD // 2
    inv = ROPE_BASE ** (-(np.arange(half, dtype=np.float64) / half))
    ang = (first_pos + np.arange(n_pos)).astype(np.float64)[:, None] * inv[None, :]
    cos, sin = np.cos(ang), np.sin(ang)
    expand = lambda a, b: np.repeat(np.tile(np.concatenate([a, b], axis=1), (1, RET_HEADS)), repeat, axis=0)
    return jnp.asarray(expand(cos, cos), F32), jnp.asarray(expand(-sin, sin), F32)


def _run_prompt_group(x, prm, final_g):
    batch, seq_len, _ = x.shape
    rows = batch * seq_len
    depth = prm["w_in"].shape[0]
    tm = min(MIX_IN_TILE, seq_len)
    cos_tab, sin_tab = _rotary_tables(0, seq_len)
    x2d = x.reshape(rows, D_MODEL)
    outs = []
    for li in range(depth):
        ret, su, gla, glr = _mix_in(x2d, prm["norm_mix_g"], prm["w_in"], li, cos_tab, sin_tab, tm)
        o_ret, s_ret = _retention(ret, prm["ret_ln_g"], prm["ret_ln_b"], li, batch, seq_len)
        o_s5, s5r, s5i = _s5(su.reshape(batch, seq_len, S5_W), prm["s5_disc"], prm["s5_d"], prm["s5_glu_w"],
                             prm["s5_glu_b"], li, None, batch, seq_len)
        o_s5 = o_s5.reshape(rows, S5_W)
        o_gla, s_gla = _gla(gla, glr, prm["gla_gate_w"], prm["gla_gate_b"], prm["gla_norm_g"], li, batch, seq_len)
        x2d, conv_new = _ffn(x2d, o_ret, o_s5, o_gla, prm["w_out"], prm["norm_ffn_g"], prm["ffn_w_in"],
                             prm["ffn_conv_w"], prm["ffn_conv_b"], prm["ffn_w_out"], li,
                             final_g if li == depth - 1 else None, None, batch, seq_len)
        s5_shape = (batch, S5_GROUPS, S5_STATE)
        outs.append((s_ret, s5r.reshape(s5_shape), s5i.reshape(s5_shape), s_gla, conv_new))
    return [x2d.reshape(batch, seq_len, D_MODEL)] + [jnp.stack([o[i] for o in outs]) for i in range(5)]


def _run_sample_group(x, past_len, states, prm, final_g):
    batch, seq_len, _ = x.shape
    assert batch == V7X_LANES
    rows = batch * seq_len
    depth = prm["w_in"].shape[0]
    cos_tab, sin_tab = _rotary_tables(past_len, seq_len, repeat=batch)
    x2d = x.transpose(1, 0, 2).reshape(rows, D_MODEL)
    ret_state = states["ret"].transpose(0, 2, 3, 4, 1)
    gla_state = states["gla"].transpose(0, 2, 3, 4, 1)
    conv_state = states["conv"].transpose(0, 2, 1, 3)
    s5_flat = lambda s: s.reshape(depth, batch, S5_LANES)
    s5r_state, s5i_state = s5_flat(states["s5r"]), s5_flat(states["s5i"])
    outs = []
    s_ret = s_gla = None
    for li in range(depth):
        ret, su, gla, glr = _mix_in(x2d, prm["norm_mix_g"], prm["w_in"], li, cos_tab, sin_tab, rows)
        o_ret, s_ret = _retention_lanes(ret, prm["ret_ln_g_lanes"], prm["ret_ln_b_lanes"], ret_state, s_ret, li,
                                        seq_len)
        o_s5, s5r, s5i = _s5(su, prm["s5_disc"], prm["s5_d"], prm["s5_glu_w"], prm["s5_glu_b"], li,
                             (s5r_state[li], s5i_state[li]), batch, seq_len)
        o_gla, s_gla = _gla_lanes(gla, glr, prm["gla_gate_w"], prm["gla_gate_b"], prm["gla_norm_g_lanes"], gla_state,
                                  s_gla, li, seq_len)
        x2d, conv_new = _ffn(x2d, o_ret, o_s5, o_gla, prm["w_out"], prm["norm_ffn_g"], prm["ffn_w_in"],
                             prm["ffn_conv_w"], prm["ffn_conv_b"], prm["ffn_w_out"], li,
                             final_g if li == depth - 1 else None, conv_state[li], batch, seq_len)
        outs.append((s5r, s5i, conv_new.reshape(CONV_W - 1, batch, D_FF)))
    s5r, s5i, conv_new = (jnp.stack([o[i] for o in outs]) for i in range(3))
    s5_shape = (depth, batch, S5_GROUPS, S5_STATE)
    return [x2d.reshape(seq_len, batch, D_MODEL).transpose(1, 0, 2), s_ret.transpose(0, 4, 1, 2, 3),
            s5r.reshape(s5_shape), s5i.reshape(s5_shape), s_gla.transpose(0, 4, 1, 2, 3),
            conv_new.transpose(0, 2, 1, 3)]


def _prepare_params(norm_mix_g, w_in, ret_norm_g, ret_norm_b, s5_lambda_re, s5_lambda_im, s5_log_dt, s5_b_re, s5_b_im,
                    s5_c_re, s5_c_im, s5_d, s5_glu_w, s5_glu_b, gla_gate_w, gla_gate_b, gla_norm_g, w_out,
                    norm_ffn_g, ffn_w_in, ffn_conv_w, ffn_conv_b, ffn_w_out):
    depth = w_in.shape[0]
    nb = V7X_LANES
    row = lambda a: a.astype(F32).reshape(depth, 1, a.shape[-1])
    ret_lanes = lambda a: jnp.broadcast_to(a.astype(F32).reshape(depth, RET_HEADS, RET_HD, 1),
                                           (depth, RET_HEADS, RET_HD, nb))
    return dict(
        norm_mix_g=row(norm_mix_g), w_in=w_in.astype(BF16),
        ret_ln_g=row(ret_norm_g), ret_ln_b=row(ret_norm_b),
        ret_ln_g_lanes=ret_lanes(ret_norm_g), ret_ln_b_lanes=ret_lanes(ret_norm_b),
        s5_disc=_s5_discretize(s5_lambda_re, s5_lambda_im, s5_log_dt, s5_b_re, s5_b_im, s5_c_re, s5_c_im),
        s5_d=row(s5_d), s5_glu_w=s5_glu_w.astype(BF16), s5_glu_b=row(s5_glu_b),
        gla_gate_w=_pad_gate(gla_gate_w, BF16), gla_gate_b=_pad_gate(row(gla_gate_b), F32),
        gla_norm_g=row(jnp.tile(gla_norm_g, (1, GLA_HEADS))),
        gla_norm_g_lanes=jnp.broadcast_to(gla_norm_g.astype(F32).reshape(depth, GLA_DV, 1), (depth, GLA_DV, nb)),
        w_out=w_out.astype(BF16), norm_ffn_g=row(norm_ffn_g), ffn_w_in=ffn_w_in.astype(BF16),
        ffn_conv_w=jnp.pad(ffn_conv_w.astype(F32), ((0, 0), (0, V7X_SUBLANES - CONV_W), (0, 0))),
        ffn_conv_b=row(ffn_conv_b), ffn_w_out=ffn_w_out.astype(BF16))


def kernel(x_prompt, x_sample, state_ret, state_s5_re, state_s5_im, state_gla, state_ffn_conv, norm_mix_g, w_in, ret_norm_g, ret_norm_b, s5_lambda_re, s5_lambda_im, s5_log_dt, s5_b_re, s5_b_im, s5_c_re, s5_c_im, s5_d, s5_glu_w, s5_glu_b, gla_gate_w, gla_gate_b, gla_norm_g, w_out, norm_ffn_g, ffn_w_in, ffn_conv_w, ffn_conv_b, ffn_w_out, norm_final_g):
    prm = _prepare_params(norm_mix_g, w_in, ret_norm_g, ret_norm_b, s5_lambda_re, s5_lambda_im, s5_log_dt, s5_b_re,
                          s5_b_im, s5_c_re, s5_c_im, s5_d, s5_glu_w, s5_glu_b, gla_gate_w, gla_gate_b, gla_norm_g,
                          w_out, norm_ffn_g, ffn_w_in, ffn_conv_w, ffn_conv_b, ffn_w_out)
    sample_states = dict(ret=state_ret, s5r=state_s5_re, s5i=state_s5_im, gla=state_gla, conv=state_ffn_conv)
    yp, ret_p, s5r_p, s5i_p, gla_p, conv_p = _run_prompt_group(x_prompt, prm, norm_final_g)
    ys, ret_s, s5r_s, s5i_s, gla_s, conv_s = _run_sample_group(x_sample, PAST_LEN, sample_states, prm, norm_final_g)
    return (yp, ys, ret_p, ret_s, s5r_p, s5r_s, s5i_p, s5i_s, gla_p, gla_s, conv_p, conv_s)
```

```python
import functools
from typing import Callable, NamedTuple

import numpy as np
import jax
import jax.numpy as jnp
from jax import lax
from jax.experimental import pallas as pl
from jax.experimental.pallas import tpu as pltpu

F32, BF16 = jnp.float32, jnp.bfloat16

D_MODEL = 1024
RET_W, S5_W, GLA_W = 384, 256, 384
RET_HEADS, RET_HD = 6, 64
RET_PAIRS = RET_HEADS // 2
S5_GROUPS, S5_CH, S5_STATE = 16, 16, 64
S5_LANES = S5_GROUPS * S5_STATE
S5_MIN_NEG = 1e-4
GLA_HEADS, GLA_DK, GLA_DV, GLA_RANK = 4, 48, 96, 16
GLA_KW = GLA_HEADS * GLA_DK
GLA_GATE_TEMP = 16.0
D_FF = 2816
CONV_W = 3
ROPE_BASE = 10000.0
CHUNK = 64
EPS = 1e-6
PAST_LEN = 16384
IN_COLS = 4 * RET_W + S5_W + 2 * GLA_KW + 2 * GLA_W + GLA_RANK
COL_RET, COL_S5, COL_GLA, COL_LR = 0, 4 * RET_W, 4 * RET_W + S5_W, IN_COLS - GLA_RANK
GLA_KP = 256
GLA_Q0, GLA_LR0, GLA_K0, GLA_V0, GLA_G0 = 0, GLA_KW, GLA_KP, 2 * GLA_KP, 2 * GLA_KP + GLA_W
GLA_COLS = 2 * GLA_KP + 2 * GLA_W
MIX_COLS = COL_GLA + GLA_COLS

V7X_SUBLANES = 8
V7X_LANES = 128
VMEM_LIMIT = 58 * 1024 * 1024

ROW_TILE = 512
MIX_IN_TILE = 1024
MIXER_TILE = 2048
S5_TIME_TILE = 256
S5_SUB_STEPS = 64
ACT_COLS = 256


def _dot(a, b):
    return jnp.dot(a, b, preferred_element_type=F32)


def _dot_nt(a, b):
    return lax.dot_general(a, b, (((1,), (1,)), ((), ())), preferred_element_type=F32)


def _dot_tn(a, b):
    return lax.dot_general(a, b, (((0,), (0,)), ((), ())), preferred_element_type=F32)


def _hi_lo(x):
    hi = x.astype(BF16)
    return hi, (x - hi.astype(F32)).astype(BF16)


def _dot_exact_lhs(m, x):
    hi, lo = _hi_lo(x)
    return _dot(m, hi) + _dot(m, lo)


def _dot_row_halves(a, w):
    half = a.shape[0] // 2
    return jnp.concatenate([_dot(a[:half], w), _dot(a[half:], w)], axis=0)


def _pad_gate(a, dtype):
    return jnp.pad(a.astype(dtype), [(0, 0)] * (a.ndim - 1) + [(0, GLA_KP - GLA_KW)])


def _sigmoid(x):
    return 1.0 / (1.0 + jnp.exp(-x))


def _log_sigmoid(z):
    return jnp.minimum(z, 0.0) - jnp.log(1.0 + jnp.exp(-jnp.abs(z)))


def _gelu_tanh(x):
    return 0.5 * x * (1.0 + jnp.tanh(0.7978845608028654 * (x + 0.044715 * (x * x * x))))


def _rmsnorm_rows(x, g):
    return x * lax.rsqrt(jnp.mean(x * x, axis=-1, keepdims=True) + EPS) * g


def _const_spec(shape):
    nd = len(shape)
    return pl.BlockSpec(shape, lambda *_: (0,) * nd)


def _layer_spec(stacked_shape, layer, single_buffer=False):
    nd = len(stacked_shape) - 1
    mode = dict(pipeline_mode=pl.Buffered(1)) if single_buffer else {}
    return pl.BlockSpec((None,) + tuple(stacked_shape[1:]), lambda *_: (layer,) + (0,) * nd, **mode)


def _specs(arrays, layer):
    return [_layer_spec(a.shape, layer) if stacked else _const_spec(a.shape) for a, stacked in arrays]


def _carried_output(buffer, n_before, out_index):
    if buffer is None:
        return [], [], {}
    return [buffer], [pl.BlockSpec(memory_space=pl.ANY)], {n_before: out_index}


def _held_head(slot, head, heads):
    return jnp.where(slot == 0, head, heads - 1)


def _layer_slot_body(*refs, head_fn, n_in, **kw):
    @pl.when(pl.program_id(0) == 0)
    def _():
        head_fn(pl.program_id(1), *refs, n_in=n_in, **kw)

    @pl.when(pl.program_id(0) > 0)
    def _():
        refs[n_in + 1][...] = jnp.zeros_like(refs[n_in + 1])


def _params():
    return pltpu.CompilerParams(vmem_limit_bytes=VMEM_LIMIT)


class _Call(NamedTuple):
    body: Callable
    args: list
    in_specs: list
    out_specs: list
    out_shape: list
    scratch_shapes: list


def _run_together(calls, grid, name):
    counts = [[len(c.args) for c in calls], [len(c.out_shape) for c in calls], [len(c.scratch_shapes) for c in calls]]

    def body(*refs):
        groups, start = [], 0
        for kind in counts:
            per_call = []
            for n in kind:
                per_call.append(refs[start:start + n])
                start += n
            groups.append(per_call)
        for c, ins, outs, scratch in zip(calls, *groups):
            c.body(*ins, *outs, *scratch)

    outs = pl.pallas_call(
        body,
        grid=grid,
        in_specs=[sp for c in calls for sp in c.in_specs],
        out_specs=[sp for c in calls for sp in c.out_specs],
        out_shape=[sh for c in calls for sh in c.out_shape],
        scratch_shapes=[sh for c in calls for sh in c.scratch_shapes],
        compiler_params=_params(),
        name=name,
    )(*[a for c in calls for a in c.args])
    split, start = [], 0
    for n in counts[1]:
        split.append(outs[start:start + n])
        start += n
    return split


def _to_lanes(x):
    w = x.shape[1]
    pad = -w % V7X_LANES
    if pad:
        x = jnp.concatenate([x, jnp.zeros((x.shape[0], pad), F32)], axis=1)
    return x.T[0:w]


def _mix_in_body(x_ref, g_ref, win_ref, cos_ref, sin_ref, ret_ref, su_ref, gla_ref, glr_ref, w_ref):
    @pl.when(pl.program_id(0) == 0)
    def _():
        k0, v0 = COL_GLA + GLA_KW, COL_GLA + 2 * GLA_KW
        zeros = lambda n: jnp.zeros((D_MODEL, n), BF16)
        w_ref[...] = jnp.concatenate(
            [win_ref[:, 0:k0], win_ref[:, COL_LR:IN_COLS], zeros(GLA_KP - GLA_KW - GLA_RANK),
             win_ref[:, k0:v0], zeros(GLA_KP - GLA_KW), win_ref[:, v0:COL_LR]], axis=1)

    h = _rmsnorm_rows(x_ref[...], g_ref[...]).astype(BF16)
    ret = _dot(h, w_ref[:, COL_RET:COL_S5])
    cos = cos_ref[...]
    sin = sin_ref[...]
    lane = lax.broadcasted_iota(jnp.int32, cos.shape, 1)
    first_half = (lane & (RET_HD // 2)) == 0

    def rotary(z):
        swapped = jnp.where(first_half, pltpu.roll(z, RET_W - RET_HD // 2, 1), pltpu.roll(z, RET_HD // 2, 1))
        return z * cos + swapped * sin

    ret_ref[:, 0:RET_W] = rotary(ret[:, 0:RET_W]).astype(BF16)
    ret_ref[:, RET_W:2 * RET_W] = (rotary(ret[:, RET_W:2 * RET_W]) * RET_HD ** -0.5).astype(BF16)
    ret_ref[:, 2 * RET_W:] = ret[:, 2 * RET_W:].astype(BF16)
    su_ref[...] = _dot(h, w_ref[:, COL_S5:COL_GLA]).astype(BF16)
    gla = _dot(h, w_ref[:, COL_GLA:MIX_COLS])
    glane = lax.broadcasted_iota(jnp.int32, (1, GLA_COLS), 1)
    scale = jnp.where(glane < GLA_KW, GLA_DK ** -0.5, jnp.where(glane < GLA_KP, 0.0, 1.0))
    gla_ref[...] = (gla * scale).astype(BF16)
    glr_ref[...] = gla[:, GLA_LR0:GLA_LR0 + GLA_RANK]


def _mix_in(x2d, norm_g, w_in_bf, layer, cos_tab, sin_tab, tm):
    rows = x2d.shape[0]
    n_tab = cos_tab.shape[0] // tm
    row_spec = lambda w: pl.BlockSpec((tm, w), lambda i: (i, 0))
    tab_spec = pl.BlockSpec((tm, RET_W), lambda i: (i % n_tab, 0))
    return pl.pallas_call(
        _mix_in_body,
        grid=(rows // tm,),
        in_specs=[row_spec(D_MODEL), _layer_spec(norm_g.shape, layer), _layer_spec(w_in_bf.shape, layer, True),
                  tab_spec, tab_spec],
        out_specs=[row_spec(4 * RET_W), row_spec(S5_W), row_spec(GLA_COLS), row_spec(GLA_RANK)],
        out_shape=[jax.ShapeDtypeStruct((rows, 4 * RET_W), BF16), jax.ShapeDtypeStruct((rows, S5_W), BF16),
                   jax.ShapeDtypeStruct((rows, GLA_COLS), BF16), jax.ShapeDtypeStruct((rows, GLA_RANK), F32)],
        scratch_shapes=[pltpu.VMEM((D_MODEL, MIX_COLS), BF16)],
        compiler_params=_params(),
        name="mix_in",
    )(x2d, norm_g, w_in_bf, cos_tab, sin_tab)


def _retention_gammas():
    return 1.0 - 2.0 ** (-5.0 - np.arange(RET_HEADS))


def _retention_consts():
    t = np.arange(CHUNK)
    gam = _retention_gammas()
    causal = t[:, None] >= t[None, :]
    diff = np.maximum(t[:, None] - t[None, :], 0)
    dmask = np.zeros((RET_PAIRS, CHUNK, 2 * CHUNK))
    cdec = np.zeros((RET_PAIRS, 2 * RET_HD, 2 * RET_HD))
    for p in range(RET_PAIRS):
        for s in range(2):
            g = gam[2 * p + s]
            dmask[p, :, s * CHUNK:(s + 1) * CHUNK] = np.where(causal, g ** diff, 0.0)
            cdec[p, s * RET_HD:(s + 1) * RET_HD, s * RET_HD:(s + 1) * RET_HD] = g ** CHUNK
    lane_gam = np.repeat(gam, RET_HD)[None, :]
    qdec = lane_gam ** (t[:, None] + 1.0)
    kdec = lane_gam ** (CHUNK - 1.0 - t[:, None])
    bd = (cdec[0] > 0).astype(np.float32)
    head_mask = np.stack([np.arange(2 * RET_HD) < RET_HD, np.arange(2 * RET_HD) >= RET_HD]).astype(np.float32)
    ones_blk = np.kron(np.eye(RET_HEADS), np.ones((RET_HD, RET_HD)))
    f = lambda a: jnp.asarray(a, F32)
    return dict(dmask=f(dmask), cdec=f(cdec), qdec=qdec, kdec=kdec, bd=f(bd),
                head_mask=jnp.asarray(head_mask, BF16), ones_blk=jnp.asarray(ones_blk, BF16))


def _retention_body(ret_ref, dmask_ref, cdec_ref, qdec_ref, kdec_ref, bd_ref, hm_ref, ones_ref, lng_ref, lnb_ref,
                    o_ref, sout_ref, s_scr, o_scr, *, tl):
    @pl.when(pl.program_id(1) == 0)
    def _():
        s_scr[...] = jnp.zeros_like(s_scr)

    bd = bd_ref[...]
    m0 = hm_ref[0:1, :]
    m1 = hm_ref[1:2, :]
    q = ret_ref[:, 0:RET_W]
    k = ret_ref[:, RET_W:2 * RET_W]
    v = ret_ref[:, 2 * RET_W:3 * RET_W]
    q_start = (q.astype(F32) * qdec_ref[...]).astype(BF16)
    k_end = (k.astype(F32) * kdec_ref[...]).astype(BF16)
    blocks = [(n, p) for n in range(tl // CHUNK) for p in range(RET_PAIRS)]

    def part(a, n, p):
        return a[n * CHUNK:(n + 1) * CHUNK, 2 * RET_HD * p:2 * RET_HD * (p + 1)]

    scores, o, kv = {}, {}, {}
    for n, p in blocks:
        kp = part(k, n, p)
        kk = jnp.concatenate([kp * m0, kp * m1], axis=0)
        scores[n, p] = (_dot_nt(part(q, n, p), kk) * dmask_ref[p]).astype(BF16)
    for n, p in blocks:
        vp = part(v, n, p)
        o[n, p] = _dot(scores[n, p], jnp.concatenate([vp * m0, vp * m1], axis=0))
    for n, p in blocks:
        kv[n, p] = _dot_tn(part(k_end, n, p), part(v, n, p)) * bd
    for n, p in blocks:
        s_prev = s_scr[p]
        o_scr[n * CHUNK:(n + 1) * CHUNK, 2 * RET_HD * p:2 * RET_HD * (p + 1)] = (
            o[n, p] + _dot(part(q_start, n, p), s_prev.astype(BF16)))
        s_scr[p] = s_prev * cdec_ref[p] + kv[n, p]

    o_all = o_scr[...]
    ones_blk = ones_ref[...]
    mu = _dot_row_halves(o_all.astype(BF16), ones_blk) * (1.0 / RET_HD)
    cen = o_all - mu
    var = _dot_row_halves((cen * cen).astype(BF16), ones_blk) * (1.0 / RET_HD)
    y = cen * lax.rsqrt(var + EPS) * lng_ref[...] + lnb_ref[...]
    gate = ret_ref[:, 3 * RET_W:4 * RET_W].astype(F32)
    o_ref[...] = (y * gate * _sigmoid(gate)).astype(BF16)
    for p in range(RET_PAIRS):
        s = s_scr[p]
        sout_ref[0, 2 * p] = s[0:RET_HD, 0:RET_HD]
        sout_ref[0, 2 * p + 1] = s[RET_HD:2 * RET_HD, RET_HD:2 * RET_HD]


def _retention_call(ret2d, ln_g, ln_b, layer, batch, seq_len):
    assert seq_len % CHUNK == 0
    c = _retention_consts()
    tl = min(MIXER_TILE, seq_len)
    nl = seq_len // tl
    tile_rows = lambda a: jnp.asarray(np.tile(a, (tl // CHUNK, 1)), F32)
    consts = [(c["dmask"], False), (c["cdec"], False), (tile_rows(c["qdec"]), False), (tile_rows(c["kdec"]), False),
              (c["bd"], False), (c["head_mask"], False), (c["ones_blk"], False), (ln_g, True), (ln_b, True)]
    state_shape = (batch, RET_HEADS, RET_HD, RET_HD)
    return _Call(
        body=functools.partial(_retention_body, tl=tl),
        args=[ret2d] + [a for a, _ in consts],
        in_specs=[pl.BlockSpec((tl, 4 * RET_W), lambda b, l: (b * nl + l, 0))] + _specs(consts, layer),
        out_specs=[pl.BlockSpec((tl, RET_W), lambda b, l: (b * nl + l, 0)),
                   pl.BlockSpec((1,) + state_shape[1:], lambda b, l: (b, 0, 0, 0))],
        out_shape=[jax.ShapeDtypeStruct((batch * seq_len, RET_W), BF16), jax.ShapeDtypeStruct(state_shape, F32)],
        scratch_shapes=[pltpu.VMEM((RET_PAIRS, 2 * RET_HD, 2 * RET_HD), F32), pltpu.VMEM((tl, RET_W), F32)])


def _retention_lanes_head(h, *refs, seq_len, n_in):
    ret_ref, tab_ref, lng_ref, lnb_ref, s_ref = refs[:5]
    o_ref, sout_ref, q_scr, k_scr, v_scr, g_scr, o_scr = refs[n_in:]
    nb = V7X_LANES
    steps = range(seq_len)

    @pl.when(h == 0)
    def _():
        for t in steps:
            blk = ret_ref[t * nb:(t + 1) * nb, :].astype(F32)
            for scr, off in ((q_scr, 0), (k_scr, RET_W), (v_scr, 2 * RET_W), (g_scr, 3 * RET_W)):
                tr = _to_lanes(blk[:, off:off + RET_W])
                for hh in range(RET_HEADS):
                    scr[t, hh] = tr[hh * RET_HD:(hh + 1) * RET_HD]

    tab = tab_ref[h]
    row = lambda r: tab[r:r + 1, :]
    q = [q_scr[t, h] for t in steps]
    k = [k_scr[t, h] for t in steps]
    v = [v_scr[t, h] for t in steps]
    o = []
    for i in steps:
        acc = None
        for j in range(i + 1):
            s = jnp.sum(q[i] * k[j], axis=0, keepdims=True)
            if i > j:
                s = s * row(i - j - 1)
            acc = s * v[j] if acc is None else acc + s * v[j]
        o.append(acc)
    q_start = [q[t] * row(t) for t in steps]
    k_end = [k[t] * row(seq_len + t) for t in steps]
    chunk_decay = row(2 * seq_len)
    for d in range(RET_HD):
        s_d = s_ref[d]
        new = s_d * chunk_decay
        for t in steps:
            o[t] = o[t] + q_start[t][d:d + 1, :] * s_d
            new = new + k_end[t][d:d + 1, :] * v[t]
        sout_ref[d] = new
    for t in steps:
        mu = jnp.mean(o[t], axis=0, keepdims=True)
        cen = o[t] - mu
        var = jnp.mean(cen * cen, axis=0, keepdims=True)
        gate = g_scr[t, h]
        o_scr[t, h] = (cen * lax.rsqrt(var + EPS) * lng_ref[h] + lnb_ref[h]) * gate * _sigmoid(gate)

    @pl.when(h == RET_HEADS - 1)
    def _():
        for t in steps:
            o_ref[t * nb:(t + 1) * nb, :] = o_scr[t].reshape(RET_W, nb).T.astype(BF16)


def _retention_lanes(ret2d, ln_g, ln_b, state, new_state, layer, seq_len):
    nb = V7X_LANES
    gam = _retention_gammas()[:, None]
    t = np.arange(seq_len)[None, :]
    rows = np.concatenate([gam ** (t + 1.0), gam ** (seq_len - 1.0 - t), gam ** (seq_len + 0.0 * t[:, :1])], axis=1)
    n_rows = -(-rows.shape[1] // V7X_SUBLANES) * V7X_SUBLANES
    tab = np.zeros((RET_HEADS, n_rows, nb))
    tab[:, :rows.shape[1], :] = rows[:, :, None]
    consts = [(jnp.asarray(tab, F32), False), (ln_g, True), (ln_b, True)]
    tile = pltpu.VMEM((seq_len, RET_HEADS, RET_HD, nb), F32)
    block = (None, None, RET_HD, RET_HD, nb)
    prev, prev_specs, alias = _carried_output(new_state, n_before=5, out_index=1)
    slots = 1 if prev else state.shape[0] - layer
    return pl.pallas_call(
        functools.partial(_layer_slot_body, head_fn=_retention_lanes_head, n_in=5 + len(prev), seq_len=seq_len),
        grid=(slots, RET_HEADS),
        in_specs=[_const_spec(ret2d.shape)] + _specs(consts, layer)
                 + [pl.BlockSpec(block, lambda g, h: (layer, _held_head(g, h, RET_HEADS), 0, 0, 0))] + prev_specs,
        out_specs=[_const_spec((seq_len * nb, RET_W)), pl.BlockSpec(block, lambda g, h: (layer + g, h, 0, 0, 0))],
        out_shape=[jax.ShapeDtypeStruct((seq_len * nb, RET_W), BF16), jax.ShapeDtypeStruct(state.shape, F32)],
        scratch_shapes=[tile] * 5,
        input_output_aliases=alias,
        compiler_params=_params(),
        name="retention_lanes",
    )(ret2d, *[a for a, _ in consts], state, *prev)


def _gla_consts():
    t = np.arange(CHUNK)
    causal = t[:, None] >= t[None, :]
    pad = GLA_KP - GLA_KW
    kmask = np.pad(np.kron(np.eye(GLA_HEADS), np.ones((1, GLA_DK))), ((0, 0), (0, pad)))
    vmask = np.kron(np.eye(GLA_HEADS), np.ones((1, GLA_DV)))
    bd = np.pad(np.kron(np.eye(GLA_HEADS), np.ones((GLA_DK, GLA_DV))), ((0, pad), (0, 0)))
    ones_blk = np.kron(np.eye(GLA_HEADS), np.ones((GLA_DV, GLA_DV)))
    b = lambda a: jnp.asarray(a, BF16)
    return dict(tril=b(causal), causal4=jnp.asarray(np.tile(causal, (1, GLA_HEADS)), F32), kmask=b(kmask),
                vmask=b(vmask), bd=jnp.asarray(bd, F32), ones_blk=b(ones_blk))


def _gla_body(gla_ref, glr_ref, gw_ref, gb_ref, tril_ref, causal_ref, kmask_ref, vmask_ref, bd_ref, ones_blk_ref,
              block_ref, ng_ref, o_ref, sout_ref, s_scr, o_scr, *, tl):
    @pl.when(pl.program_id(1) == 0)
    def _():
        s_scr[...] = jnp.zeros_like(s_scr)

    bd = bd_ref[...]
    z = _dot_row_halves(glr_ref[...].astype(BF16), gw_ref[...]) + gb_ref[...]
    lg = _log_sigmoid(z) * (1.0 / GLA_GATE_TEMP)
    q0, k0, v0, g0 = GLA_Q0, GLA_K0, GLA_V0, GLA_G0
    chunks = range(tl // CHUNK)

    def rows(a, n):
        return a[n * CHUNK:(n + 1) * CHUNK]

    cums = [_dot_exact_lhs(tril_ref[...], rows(lg, n)) for n in chunks]
    b_cum = jnp.concatenate(cums, axis=0)
    b_end = jnp.concatenate([jnp.broadcast_to(c[CHUNK - 1:CHUNK, :], (CHUNK, GLA_KP)) for c in cums], axis=0)
    q_in = (gla_ref[:, q0:q0 + GLA_KP].astype(F32) * jnp.exp(b_cum)).astype(BF16)
    kf = gla_ref[:, k0:k0 + GLA_KP].astype(F32)
    k_in = (kf * jnp.exp(-b_cum)).astype(BF16)
    k_dec = (kf * jnp.exp(b_end - b_cum)).astype(BF16)
    v = gla_ref[:, v0:g0]
    lg_hi, lg_lo = _hi_lo(lg)
    block_decay = jnp.exp(_dot_tn(lg_hi, block_ref[...]) + _dot_tn(lg_lo, block_ref[...]))

    scores, o, kv = {}, {}, {}
    for n in chunks:
        kn = rows(k_in, n)
        kk = jnp.concatenate([kn * kmask_ref[h:h + 1, :] for h in range(GLA_HEADS)], axis=0)
        scores[n] = (_dot_nt(rows(q_in, n), kk) * causal_ref[...]).astype(BF16)
    for n in chunks:
        vn = rows(v, n)
        vv = jnp.concatenate([vn * vmask_ref[h:h + 1, :] for h in range(GLA_HEADS)], axis=0)
        o[n] = _dot(scores[n], vv)
    for n in chunks:
        kv[n] = _dot_tn(rows(k_dec, n), rows(v, n)) * bd
    for n in chunks:
        s_prev = s_scr[...]
        o_scr[n * CHUNK:(n + 1) * CHUNK, :] = o[n] + _dot(rows(q_in, n), s_prev.astype(BF16))
        s_scr[...] = s_prev * jnp.broadcast_to(block_decay[:, n:n + 1], (GLA_KP, GLA_W)) + kv[n]

    o_all = o_scr[...]
    ms = _dot_row_halves((o_all * o_all).astype(BF16), ones_blk_ref[...]) * (1.0 / GLA_DV)
    gate = gla_ref[:, g0:g0 + GLA_W].astype(F32)
    o_ref[...] = (o_all * lax.rsqrt(ms + EPS) * ng_ref[...] * gate * _sigmoid(gate)).astype(BF16)
    s = s_scr[...]
    for h in range(GLA_HEADS):
        sout_ref[0, h] = s[h * GLA_DK:(h + 1) * GLA_DK, h * GLA_DV:(h + 1) * GLA_DV]


def _gla_call(gla2d, glr2d, gate_w, gate_b, norm_g, layer, batch, seq_len):
    assert seq_len % CHUNK == 0
    c = _gla_consts()
    tl = min(MIXER_TILE, seq_len)
    nl = seq_len // tl
    assert tl // CHUNK <= V7X_LANES
    block = (np.arange(tl) // CHUNK)[:, None] == np.arange(V7X_LANES)[None, :]
    consts = [(gate_w, True), (gate_b, True), (c["tril"], False), (c["causal4"], False), (c["kmask"], False),
              (c["vmask"], False), (c["bd"], False), (c["ones_blk"], False), (jnp.asarray(block, BF16), False),
              (norm_g, True)]
    row_spec = lambda w: pl.BlockSpec((tl, w), lambda b, l: (b * nl + l, 0))
    state_shape = (batch, GLA_HEADS, GLA_DK, GLA_DV)
    return _Call(
        body=functools.partial(_gla_body, tl=tl),
        args=[gla2d, glr2d] + [a for a, _ in consts],
        in_specs=[row_spec(GLA_COLS), row_spec(GLA_RANK)] + _specs(consts, layer),
        out_specs=[row_spec(GLA_W), pl.BlockSpec((1,) + state_shape[1:], lambda b, l: (b, 0, 0, 0))],
        out_shape=[jax.ShapeDtypeStruct((batch * seq_len, GLA_W), BF16), jax.ShapeDtypeStruct(state_shape, F32)],
        scratch_shapes=[pltpu.VMEM((GLA_KP, GLA_W), F32), pltpu.VMEM((tl, GLA_W), F32)])


def _gla_lanes_head(h, *refs, seq_len, n_in):
    gla_ref, glr_ref, gw_ref, gb_ref, ng_ref, s_ref = refs[:6]
    o_ref, sout_ref, q_scr, k_scr, kd_scr, v_scr, g_scr, dec_scr, o_scr = refs[n_in:]
    nb = V7X_LANES
    steps = range(seq_len)
    q0, k0, v0, g0 = GLA_Q0, GLA_K0, GLA_V0, GLA_G0
    kw = slice(0, GLA_KW)

    def split(scr, t, tr, width):
        for hh in range(GLA_HEADS):
            scr[t, hh] = tr[hh * width:(hh + 1) * width]

    @pl.when(h == 0)
    def _():
        z = _dot_row_halves(glr_ref[...].astype(BF16), gw_ref[...]) + gb_ref[...]
        lg = (_log_sigmoid(z) * (1.0 / GLA_GATE_TEMP))[:, kw]
        cums = []
        for t in steps:
            lg_t = lg[t * nb:(t + 1) * nb]
            cums.append(lg_t if t == 0 else cums[-1] + lg_t)
        b_end = cums[-1]
        dec = _to_lanes(jnp.exp(b_end))
        for hh in range(GLA_HEADS):
            dec_scr[hh] = dec[hh * GLA_DK:(hh + 1) * GLA_DK]
        for t in steps:
            blk = gla_ref[t * nb:(t + 1) * nb, :].astype(F32)
            kf = blk[:, k0:k0 + GLA_KW]
            split(q_scr, t, _to_lanes(blk[:, q0:q0 + GLA_KW] * jnp.exp(cums[t])), GLA_DK)
            split(k_scr, t, _to_lanes(kf * jnp.exp(-cums[t])), GLA_DK)
            split(kd_scr, t, _to_lanes(kf * jnp.exp(b_end - cums[t])), GLA_DK)
            split(v_scr, t, _to_lanes(blk[:, v0:g0]), GLA_DV)
            split(g_scr, t, _to_lanes(blk[:, g0:g0 + GLA_W]), GLA_DV)

    q = [q_scr[t, h] for t in steps]
    k = [k_scr[t, h] for t in steps]
    kd = [kd_scr[t, h] for t in steps]
    v = [v_scr[t, h] for t in steps]
    dec = dec_scr[h]
    o = []
    for i in steps:
        acc = None
        for j in range(i + 1):
            s = jnp.sum(q[i] * k[j], axis=0, keepdims=True)
            acc = s * v[j] if acc is None else acc + s * v[j]
        o.append(acc)
    for i in range(GLA_DK):
        s_i = s_ref[i]
        new = s_i * dec[i:i + 1, :]
        for t in steps:
            o[t] = o[t] + q[t][i:i + 1, :] * s_i
            new = new + kd[t][i:i + 1, :] * v[t]
        sout_ref[i] = new
    for t in steps:
        ms = jnp.mean(o[t] * o[t], axis=0, keepdims=True)
        gate = g_scr[t, h]
        o_scr[t, h] = o[t] * lax.rsqrt(ms + EPS) * ng_ref[...] * gate * _sigmoid(gate)

    @pl.when(h == GLA_HEADS - 1)
    def _():
        for t in steps:
            o_ref[t * nb:(t + 1) * nb, :] = o_scr[t].reshape(GLA_W, nb).T.astype(BF16)


def _gla_lanes(gla2d, glr2d, gate_w, gate_b, norm_g, state, new_state, layer, seq_len):
    nb = V7X_LANES
    consts = [(gate_w, True), (gate_b, True), (norm_g, True)]
    k_tile = pltpu.VMEM((seq_len, GLA_HEADS, GLA_DK, nb), F32)
    v_tile = pltpu.VMEM((seq_len, GLA_HEADS, GLA_DV, nb), F32)
    block = (None, None, GLA_DK, GLA_DV, nb)
    prev, prev_specs, alias = _carried_output(new_state, n_before=6, out_index=1)
    slots = 1 if prev else state.shape[0] - layer
    return pl.pallas_call(
        functools.partial(_layer_slot_body, head_fn=_gla_lanes_head, n_in=6 + len(prev), seq_len=seq_len),
        grid=(slots, GLA_HEADS),
        in_specs=[_const_spec(gla2d.shape), _const_spec(glr2d.shape)] + _specs(consts, layer)
                 + [pl.BlockSpec(block, lambda g, h: (layer, _held_head(g, h, GLA_HEADS), 0, 0, 0))] + prev_specs,
        out_specs=[_const_spec((seq_len * nb, GLA_W)), pl.BlockSpec(block, lambda g, h: (layer + g, h, 0, 0, 0))],
        out_shape=[jax.ShapeDtypeStruct((seq_len * nb, GLA_W), BF16), jax.ShapeDtypeStruct(state.shape, F32)],
        scratch_shapes=[k_tile, k_tile, k_tile, v_tile, v_tile, pltpu.VMEM((GLA_HEADS, GLA_DK, nb), F32), v_tile],
        input_output_aliases=alias,
        compiler_params=_params(),
        name="gla_lanes",
    )(gla2d, glr2d, *[a for a, _ in consts], state, *prev)


def _s5_discretize(lam_re, lam_im, log_dt, b_re, b_im, c_re, c_im):
    lr = jnp.minimum(lam_re.astype(F32), -S5_MIN_NEG)
    li = lam_im.astype(F32)
    dt = jnp.exp(log_dt.astype(F32))[..., None]
    mag = jnp.exp(lr * dt)
    ar = mag * jnp.cos(li * dt)
    ai = mag * jnp.sin(li * dt)
    den = lr * lr + li * li
    cr = ((ar - 1.0) * lr + ai * li) / den
    ci = (ai * lr - (ar - 1.0) * li) / den
    b_re, b_im = b_re.astype(F32), b_im.astype(F32)
    bbar_re = cr[..., None] * b_re - ci[..., None] * b_im
    bbar_im = cr[..., None] * b_im + ci[..., None] * b_re
    eye = jnp.eye(S5_GROUPS, dtype=F32)
    depth = lr.shape[0]
    in_blk = lambda b: jnp.einsum("lgpi,gh->lgihp", b, eye).reshape(depth, S5_W, S5_LANES)
    out_blk = lambda c: jnp.einsum("lgop,gh->lgpho", c.astype(F32), eye).reshape(depth, S5_LANES, S5_W)
    bdb = jnp.concatenate([in_blk(bbar_re), in_blk(bbar_im)], axis=2).astype(BF16)
    bdc = jnp.concatenate([out_blk(c_re), -out_blk(c_im)], axis=1).astype(BF16)
    return ar.reshape(depth, 1, S5_LANES), ai.reshape(depth, 1, S5_LANES), bdb, bdc


def _s5_body(*refs, tl, batch, has_state, batch_major):
    u_ref, ar_ref, ai_ref, bdb_ref, bdc_ref, d_ref, gw_ref, gb_ref = refs[:8]
    n_in = 10 if has_state else 8
    o_ref, hr_ref, hi_ref, x_scr, h_scr = refs[n_in:]

    @pl.when(pl.program_id(0) == 0)
    def _():
        if has_state:
            h_scr[0] = refs[8][...]
            h_scr[1] = refs[9][...]
        else:
            h_scr[...] = jnp.zeros_like(h_scr)

    ar = jnp.broadcast_to(ar_ref[...], (batch, S5_LANES))
    ai = jnp.broadcast_to(ai_ref[...], (batch, S5_LANES))
    hr, hi = h_scr[0], h_scr[1]
    sub = min(tl, S5_SUB_STEPS)
    n_rows = sub * batch
    for s in range(tl // sub):
        rows = slice(s * n_rows, (s + 1) * n_rows)
        steps = slice(s * sub, (s + 1) * sub)
        if batch_major:
            u = jnp.swapaxes(u_ref[:, steps, :].astype(F32), 0, 1).reshape(n_rows, S5_W).astype(BF16)
        else:
            u = u_ref[rows, :]
        x_scr[rows, :] = _dot_row_halves(u, bdb_ref[...])
        for t in range(s * sub, (s + 1) * sub):
            r = slice(t * batch, (t + 1) * batch)
            hr, hi = (ar * hr - ai * hi + x_scr[r, 0:S5_LANES],
                      ar * hi + ai * hr + x_scr[r, S5_LANES:2 * S5_LANES])
            x_scr[r, 0:S5_LANES] = hr
            x_scr[r, S5_LANES:2 * S5_LANES] = hi
        y = _dot_row_halves(x_scr[rows, :].astype(BF16), bdc_ref[...]) + d_ref[...] * u.astype(F32)
        y = _gelu_tanh(y)
        o = y * _sigmoid(_dot_row_halves(y.astype(BF16), gw_ref[...]) + gb_ref[...])
        if batch_major:
            o_ref[:, steps, :] = jnp.swapaxes(o.reshape(sub, batch, S5_W), 0, 1).astype(BF16)
        else:
            o_ref[rows, :] = o.astype(BF16)
    h_scr[0] = hr
    h_scr[1] = hi
    hr_ref[...] = hr
    hi_ref[...] = hi


def _s5(u, disc, d, glu_w_bf, glu_b, layer, state, batch, seq_len):
    tl = min(S5_TIME_TILE, seq_len)
    has_state = state is not None
    batch_major = u.ndim == 3
    consts = list(disc) + [d, glu_w_bf, glu_b]
    h_spec = pl.BlockSpec((batch, S5_LANES), lambda i: (0, 0))
    if batch_major:
        u_spec = pl.BlockSpec((batch, tl, S5_W), lambda i: (0, i, 0))
    else:
        u_spec = pl.BlockSpec((tl * batch, S5_W), lambda i: (i, 0))
    in_specs = [u_spec] + [_layer_spec(a.shape, layer) for a in consts]
    args = [u] + consts
    if has_state:
        in_specs += [h_spec, h_spec]
        args += list(state)
    return pl.pallas_call(
        functools.partial(_s5_body, tl=tl, batch=batch, has_state=has_state, batch_major=batch_major),
        grid=(seq_len // tl,),
        in_specs=in_specs,
        out_specs=[u_spec, h_spec, h_spec],
        out_shape=[jax.ShapeDtypeStruct(u.shape, BF16),
                   jax.ShapeDtypeStruct((batch, S5_LANES), F32), jax.ShapeDtypeStruct((batch, S5_LANES), F32)],
        scratch_shapes=[pltpu.VMEM((tl * batch, 2 * S5_LANES), F32), pltpu.VMEM((2, batch, S5_LANES), F32)],
        compiler_params=_params(),
        name="s5",
    )(*args)


def _ffn_body(*refs, tm, time_major_batch, final):
    (x_ref, oret_ref, os5_ref, ogla_ref, wout_ref, gffn_ref, win_ref, cw_ref, cb_ref, wo_ref) = refs[:10]
    n = 10
    gfin_ref = None
    if final:
        gfin_ref, n = refs[n], n + 1
    if time_major_batch is not None:
        past_ref, n = refs[n], n + 1
    out_ref, conv_ref, x1_scr, h_scr, carry_scr, act_scr = refs[n:]
    pad = V7X_SUBLANES
    nb = time_major_batch

    if nb is None:
        @pl.when(pl.program_id(1) == 0)
        def _():
            carry_scr[...] = jnp.zeros_like(carry_scr)

    mix = jnp.concatenate([oret_ref[...], os5_ref[...], ogla_ref[...]], axis=1)
    x1_scr[...] = x_ref[...] + _dot(mix, wout_ref[...])
    h_scr[...] = _rmsnorm_rows(x1_scr[...], gffn_ref[...]).astype(BF16)
    h = h_scr[...]

    for c in range(D_FF // ACT_COLS):
        cols = slice(c * ACT_COLS, (c + 1) * ACT_COLS)
        a_c = _dot(h, win_ref[:, c * ACT_COLS:(c + 1) * ACT_COLS])
        gate_c = _dot(h, win_ref[:, D_FF + c * ACT_COLS:D_FF + (c + 1) * ACT_COLS])
        if nb is None:
            ext = jnp.concatenate([carry_scr[:, cols], a_c], axis=0)
            prev1 = ext[pad - 1:pad - 1 + tm]
            prev2 = ext[pad - 2:pad - 2 + tm]
            carry_scr[:, cols] = a_c[tm - pad:tm]
        else:
            prev1 = jnp.concatenate([past_ref[1, :, cols], a_c[0:tm - nb]], axis=0)
            prev2 = jnp.concatenate([past_ref[0, :, cols], past_ref[1, :, cols], a_c[0:tm - 2 * nb]], axis=0)
            conv_ref[:, cols] = a_c[tm - (CONV_W - 1) * nb:tm]
        conv = (cb_ref[:, cols] + prev2 * cw_ref[0:1, cols] + prev1 * cw_ref[1:2, cols]
                + a_c * cw_ref[2:3, cols])
        act_scr[:, cols] = (_gelu_tanh(conv) * gate_c).astype(BF16)

    x2 = x1_scr[...] + _dot(act_scr[...], wo_ref[...])
    out_ref[...] = _rmsnorm_rows(x2, gfin_ref[...]) if final else x2
    if nb is None:
        conv_ref[0] = carry_scr[pad - (CONV_W - 1):pad, :]


def _ffn(x2d, o_ret, o_s5, o_gla, w_out_bf, norm_g, w_in_bf, conv_w, conv_b, w_o_bf, layer, final_g, conv_past,
         batch, seq_len):
    rows = batch * seq_len
    time_major = conv_past is not None
    tm = rows if time_major else min(ROW_TILE, seq_len)
    final = final_g is not None
    consts = [w_out_bf, norm_g, w_in_bf, conv_w, conv_b, w_o_bf]
    stacked = [True] * len(consts)
    big = [True, False, True, False, False, True]
    if final:
        consts.append(final_g.reshape(1, D_MODEL))
        stacked.append(False)
        big.append(False)
    if time_major:
        assert seq_len >= CONV_W - 1 and batch % V7X_SUBLANES == 0
        consts.append(conv_past)
        stacked.append(False)
        big.append(False)
        grid = (1,)
        imap = lambda i: (0, 0)
        conv_rows = (CONV_W - 1) * batch
        conv_spec = pl.BlockSpec((conv_rows, D_FF), imap)
        conv_shape = jax.ShapeDtypeStruct((conv_rows, D_FF), F32)
    else:
        assert seq_len % tm == 0
        nl = seq_len // tm
        grid = (batch, nl)
        imap = lambda b, l: (b * nl + l, 0)
        conv_spec = pl.BlockSpec((1, CONV_W - 1, D_FF), lambda b, l: (b, 0, 0))
        conv_shape = jax.ShapeDtypeStruct((batch, CONV_W - 1, D_FF), F32)
    row_spec = lambda w: pl.BlockSpec((tm, w), imap)
    in_specs = ([row_spec(D_MODEL), row_spec(RET_W), row_spec(S5_W), row_spec(GLA_W)]
                + [_layer_spec(a.shape, layer, b) if s else _const_spec(a.shape)
                   for a, s, b in zip(consts, stacked, big)])
    return pl.pallas_call(
        functools.partial(_ffn_body, tm=tm, time_major_batch=batch if time_major else None, final=final),
        grid=grid,
        in_specs=in_specs,
        out_specs=[row_spec(D_MODEL), conv_spec],
        out_shape=[jax.ShapeDtypeStruct((rows, D_MODEL), F32), conv_shape],
        scratch_shapes=[pltpu.VMEM((tm, D_MODEL), F32), pltpu.VMEM((tm, D_MODEL), BF16),
                        pltpu.VMEM((V7X_SUBLANES, D_FF), F32), pltpu.VMEM((tm, D_FF), BF16)],
        compiler_params=_params(),
        name="ffn",
    )(x2d, o_ret, o_s5, o_gla, *consts)


def _rotary_tables(first_pos, n_pos, repeat=1):
    half = RET_HD // 2
    inv = ROPE_BASE ** (-(np.arange(half, dtype=np.float64) / half))
    ang = (first_pos + np.arange(n_pos)).astype(np.float64)[:, None] * inv[None, :]
    cos, sin = np.cos(ang), np.sin(ang)
    expand = lambda a, b: np.repeat(np.tile(np.concatenate([a, b], axis=1), (1, RET_HEADS)), repeat, axis=0)
    return jnp.asarray(expand(cos, cos), F32), jnp.asarray(expand(-sin, sin), F32)


def _run_prompt_group(x, prm, final_g):
    batch, seq_len, _ = x.shape
    rows = batch * seq_len
    depth = prm["w_in"].shape[0]
    tm = min(MIX_IN_TILE, seq_len)
    cos_tab, sin_tab = _rotary_tables(0, seq_len)
    x2d = x.reshape(rows, D_MODEL)
    outs = []
    for li in range(depth):
        ret, su, gla, glr = _mix_in(x2d, prm["norm_mix_g"], prm["w_in"], li, cos_tab, sin_tab, tm)
        o_s5, s5r, s5i = _s5(su.reshape(batch, seq_len, S5_W), prm["s5_disc"], prm["s5_d"], prm["s5_glu_w"],
                             prm["s5_glu_b"], li, None, batch, seq_len)
        o_s5 = o_s5.reshape(rows, S5_W)
        (o_ret, s_ret), (o_gla, s_gla) = _run_together(
            [_retention_call(ret, prm["ret_ln_g"], prm["ret_ln_b"], li, batch, seq_len),
             _gla_call(gla, glr, prm["gla_gate_w"], prm["gla_gate_b"], prm["gla_norm_g"], li, batch, seq_len)],
            (batch, seq_len // min(MIXER_TILE, seq_len)), "retention_gla")
        x2d, conv_new = _ffn(x2d, o_ret, o_s5, o_gla, prm["w_out"], prm["norm_ffn_g"], prm["ffn_w_in"],
                             prm["ffn_conv_w"], prm["ffn_conv_b"], prm["ffn_w_out"], li,
                             final_g if li == depth - 1 else None, None, batch, seq_len)
        s5_shape = (batch, S5_GROUPS, S5_STATE)
        outs.append((s_ret, s5r.reshape(s5_shape), s5i.reshape(s5_shape), s_gla, conv_new))
    return [x2d.reshape(batch, seq_len, D_MODEL)] + [jnp.stack([o[i] for o in outs]) for i in range(5)]


def _run_sample_group(x, past_len, states, prm, final_g):
    batch, seq_len, _ = x.shape
    assert batch == V7X_LANES
    rows = batch * seq_len
    depth = prm["w_in"].shape[0]
    cos_tab, sin_tab = _rotary_tables(past_len, seq_len, repeat=batch)
    x2d = x.transpose(1, 0, 2).reshape(rows, D_MODEL)
    ret_state = states["ret"].transpose(0, 2, 3, 4, 1)
    gla_state = states["gla"].transpose(0, 2, 3, 4, 1)
    conv_state = states["conv"].transpose(0, 2, 1, 3)
    s5_flat = lambda s: s.reshape(depth, batch, S5_LANES)
    s5r_state, s5i_state = s5_flat(states["s5r"]), s5_flat(states["s5i"])
    outs = []
    s_ret = s_gla = None
    for li in range(depth):
        ret, su, gla, glr = _mix_in(x2d, prm["norm_mix_g"], prm["w_in"], li, cos_tab, sin_tab, rows)
        o_ret, s_ret = _retention_lanes(ret, prm["ret_ln_g_lanes"], prm["ret_ln_b_lanes"], ret_state, s_ret, li,
                                        seq_len)
        o_s5, s5r, s5i = _s5(su, prm["s5_disc"], prm["s5_d"], prm["s5_glu_w"], prm["s5_glu_b"], li,
                             (s5r_state[li], s5i_state[li]), batch, seq_len)
        o_gla, s_gla = _gla_lanes(gla, glr, prm["gla_gate_w"], prm["gla_gate_b"], prm["gla_norm_g_lanes"], gla_state,
                                  s_gla, li, seq_len)
        x2d, conv_new = _ffn(x2d, o_ret, o_s5, o_gla, prm["w_out"], prm["norm_ffn_g"], prm["ffn_w_in"],
                             prm["ffn_conv_w"], prm["ffn_conv_b"], prm["ffn_w_out"], li,
                             final_g if li == depth - 1 else None, conv_state[li], batch, seq_len)
        outs.append((s5r, s5i, conv_new.reshape(CONV_W - 1, batch, D_FF)))
    s5r, s5i, conv_new = (jnp.stack([o[i] for o in outs]) for i in range(3))
    s5_shape = (depth, batch, S5_GROUPS, S5_STATE)
    return [x2d.reshape(seq_len, batch, D_MODEL).transpose(1, 0, 2), s_ret.transpose(0, 4, 1, 2, 3),
            s5r.reshape(s5_shape), s5i.reshape(s5_shape), s_gla.transpose(0, 4, 1, 2, 3),
            conv_new.transpose(0, 2, 1, 3)]


def _prepare_params(norm_mix_g, w_in, ret_norm_g, ret_norm_b, s5_lambda_re, s5_lambda_im, s5_log_dt, s5_b_re, s5_b_im,
                    s5_c_re, s5_c_im, s5_d, s5_glu_w, s5_glu_b, gla_gate_w, gla_gate_b, gla_norm_g, w_out,
                    norm_ffn_g, ffn_w_in, ffn_conv_w, ffn_conv_b, ffn_w_out):
    depth = w_in.shape[0]
    nb = V7X_LANES
    row = lambda a: a.astype(F32).reshape(depth, 1, a.shape[-1])
    ret_lanes = lambda a: jnp.broadcast_to(a.astype(F32).reshape(depth, RET_HEADS, RET_HD, 1),
                                           (depth, RET_HEADS, RET_HD, nb))
    return dict(
        norm_mix_g=row(norm_mix_g), w_in=w_in.astype(BF16),
        ret_ln_g=row(ret_norm_g), ret_ln_b=row(ret_norm_b),
        ret_ln_g_lanes=ret_lanes(ret_norm_g), ret_ln_b_lanes=ret_lanes(ret_norm_b),
        s5_disc=_s5_discretize(s5_lambda_re, s5_lambda_im, s5_log_dt, s5_b_re, s5_b_im, s5_c_re, s5_c_im),
        s5_d=row(s5_d), s5_glu_w=s5_glu_w.astype(BF16), s5_glu_b=row(s5_glu_b),
        gla_gate_w=_pad_gate(gla_gate_w, BF16), gla_gate_b=_pad_gate(row(gla_gate_b), F32),
        gla_norm_g=row(jnp.tile(gla_norm_g, (1, GLA_HEADS))),
        gla_norm_g_lanes=jnp.broadcast_to(gla_norm_g.astype(F32).reshape(depth, GLA_DV, 1), (depth, GLA_DV, nb)),
        w_out=w_out.astype(BF16), norm_ffn_g=row(norm_ffn_g), ffn_w_in=ffn_w_in.astype(BF16),
        ffn_conv_w=jnp.pad(ffn_conv_w.astype(F32), ((0, 0), (0, V7X_SUBLANES - CONV_W), (0, 0))),
        ffn_conv_b=row(ffn_conv_b), ffn_w_out=ffn_w_out.astype(BF16))


def kernel(x_prompt, x_sample, state_ret, state_s5_re, state_s5_im, state_gla, state_ffn_conv, norm_mix_g, w_in, ret_norm_g, ret_norm_b, s5_lambda_re, s5_lambda_im, s5_log_dt, s5_b_re, s5_b_im, s5_c_re, s5_c_im, s5_d, s5_glu_w, s5_glu_b, gla_gate_w, gla_gate_b, gla_norm_g, w_out, norm_ffn_g, ffn_w_in, ffn_conv_w, ffn_conv_b, ffn_w_out, norm_final_g):
    prm = _prepare_params(norm_mix_g, w_in, ret_norm_g, ret_norm_b, s5_lambda_re, s5_lambda_im, s5_log_dt, s5_b_re,
                          s5_b_im, s5_c_re, s5_c_im, s5_d, s5_glu_w, s5_glu_b, gla_gate_w, gla_gate_b, gla_norm_g,
                          w_out, norm_ffn_g, ffn_w_in, ffn_conv_w, ffn_conv_b, ffn_w_out)
    sample_states = dict(ret=state_ret, s5r=state_s5_re, s5i=state_s5_im, gla=state_gla, conv=state_ffn_conv)
    yp, ret_p, s5r_p, s5i_p, gla_p, conv_p = _run_prompt_group(x_prompt, prm, norm_final_g)
    ys, ret_s, s5r_s, s5i_s, gla_s, conv_s = _run_sample_group(x_sample, PAST_LEN, sample_states, prm, norm_final_g)
    return (yp, ys, ret_p, ret_s, s5r_p, s5r_s, s5i_p, s5i_s, gla_p, gla_s, conv_p, conv_s)
```

```python
import functools

import numpy as np
import jax
import jax.numpy as jnp
from jax import lax
from jax.experimental import pallas as pl
from jax.experimental.pallas import tpu as pltpu

F32, BF16 = jnp.float32, jnp.bfloat16

D_MODEL = 1024
RET_W, S5_W, GLA_W = 384, 256, 384
RET_HEADS, RET_HD = 6, 64
RET_PAIRS = RET_HEADS // 2
S5_GROUPS, S5_CH, S5_STATE = 16, 16, 64
S5_LANES = S5_GROUPS * S5_STATE
S5_MIN_NEG = 1e-4
GLA_HEADS, GLA_DK, GLA_DV, GLA_RANK = 4, 48, 96, 16
GLA_KW = GLA_HEADS * GLA_DK
GLA_GATE_TEMP = 16.0
D_FF = 2816
CONV_W = 3
ROPE_BASE = 10000.0
CHUNK = 64
EPS = 1e-6
PAST_LEN = 16384
IN_COLS = 4 * RET_W + S5_W + 2 * GLA_KW + 2 * GLA_W + GLA_RANK
COL_RET, COL_S5, COL_GLA, COL_LR = 0, 4 * RET_W, 4 * RET_W + S5_W, IN_COLS - GLA_RANK
GLA_KP = 256
GLA_Q0, GLA_LR0, GLA_K0, GLA_V0, GLA_G0 = 0, GLA_KW, GLA_KP, 2 * GLA_KP, 2 * GLA_KP + GLA_W
GLA_COLS = 2 * GLA_KP + 2 * GLA_W
MIX_COLS = COL_GLA + GLA_COLS

V7X_SUBLANES = 8
V7X_LANES = 128
VMEM_LIMIT = 58 * 1024 * 1024

ROW_TILE = 512
MIX_IN_TILE = 1024
MIXER_TILE = 2048
S5_TIME_TILE = 256
S5_SUB_STEPS = 64
INPUT_RING = 3
ACT_COLS = 256


def _dot(a, b):
    return jnp.dot(a, b, preferred_element_type=F32)


def _dot_nt(a, b):
    return lax.dot_general(a, b, (((1,), (1,)), ((), ())), preferred_element_type=F32)


def _dot_tn(a, b):
    return lax.dot_general(a, b, (((0,), (0,)), ((), ())), preferred_element_type=F32)


def _hi_lo(x):
    hi = x.astype(BF16)
    return hi, (x - hi.astype(F32)).astype(BF16)


def _dot_exact_lhs(m, x):
    hi, lo = _hi_lo(x)
    return _dot(m, hi) + _dot(m, lo)


def _dot_row_halves(a, w):
    half = a.shape[0] // 2
    return jnp.concatenate([_dot(a[:half], w), _dot(a[half:], w)], axis=0)


def _pad_gate(a, dtype):
    return jnp.pad(a.astype(dtype), [(0, 0)] * (a.ndim - 1) + [(0, GLA_KP - GLA_KW)])


def _sigmoid(x):
    return 1.0 / (1.0 + jnp.exp(-x))


def _log_sigmoid(z):
    return jnp.minimum(z, 0.0) - jnp.log(1.0 + jnp.exp(-jnp.abs(z)))


def _gelu_tanh(x):
    return 0.5 * x * (1.0 + jnp.tanh(0.7978845608028654 * (x + 0.044715 * (x * x * x))))


def _rmsnorm_rows(x, g):
    return x * lax.rsqrt(jnp.mean(x * x, axis=-1, keepdims=True) + EPS) * g


def _const_spec(shape):
    nd = len(shape)
    return pl.BlockSpec(shape, lambda *_: (0,) * nd)


def _layer_spec(stacked_shape, layer, single_buffer=False):
    nd = len(stacked_shape) - 1
    mode = dict(pipeline_mode=pl.Buffered(1)) if single_buffer else {}
    return pl.BlockSpec((None,) + tuple(stacked_shape[1:]), lambda *_: (layer,) + (0,) * nd, **mode)


def _specs(arrays, layer):
    return [_layer_spec(a.shape, layer) if stacked else _const_spec(a.shape) for a, stacked in arrays]


def _carried_output(buffer, n_before, out_index):
    if buffer is None:
        return [], [], {}
    return [buffer], [pl.BlockSpec(memory_space=pl.ANY)], {n_before: out_index}


def _held_head(slot, head, heads):
    return jnp.where(slot == 0, head, heads - 1)


def _layer_slot_body(*refs, head_fn, n_in, **kw):
    @pl.when(pl.program_id(0) == 0)
    def _():
        head_fn(pl.program_id(1), *refs, n_in=n_in, **kw)

    @pl.when(pl.program_id(0) > 0)
    def _():
        refs[n_in + 1][...] = jnp.zeros_like(refs[n_in + 1])


def _params():
    return pltpu.CompilerParams(vmem_limit_bytes=VMEM_LIMIT)


def _ring_params():
    return pltpu.CompilerParams(vmem_limit_bytes=VMEM_LIMIT, dimension_semantics=("arbitrary", "arbitrary"))


def _ring_rows(hbm, ring, sem, step, tl, n_steps):
    def fetch(s):
        slot = s % INPUT_RING
        return pltpu.make_async_copy(hbm.at[pl.ds(pl.multiple_of(s * tl, tl), tl), :], ring.at[slot], sem.at[slot])

    @pl.when(step == 0)
    def _():
        for s in range(min(INPUT_RING - 1, n_steps)):
            fetch(s).start()

    @pl.when(step + INPUT_RING - 1 < n_steps)
    def _():
        fetch(step + INPUT_RING - 1).start()

    fetch(step).wait()
    return ring.at[step % INPUT_RING]


def _to_lanes(x):
    w = x.shape[1]
    pad = -w % V7X_LANES
    if pad:
        x = jnp.concatenate([x, jnp.zeros((x.shape[0], pad), F32)], axis=1)
    return x.T[0:w]


def _mix_in_body(x_ref, g_ref, win_ref, cos_ref, sin_ref, ret_ref, su_ref, gla_ref, glr_ref, w_ref):
    @pl.when(pl.program_id(0) == 0)
    def _():
        k0, v0 = COL_GLA + GLA_KW, COL_GLA + 2 * GLA_KW
        zeros = lambda n: jnp.zeros((D_MODEL, n), BF16)
        w_ref[...] = jnp.concatenate(
            [win_ref[:, 0:k0], win_ref[:, COL_LR:IN_COLS], zeros(GLA_KP - GLA_KW - GLA_RANK),
             win_ref[:, k0:v0], zeros(GLA_KP - GLA_KW), win_ref[:, v0:COL_LR]], axis=1)

    h = _rmsnorm_rows(x_ref[...], g_ref[...]).astype(BF16)
    ret = _dot(h, w_ref[:, COL_RET:COL_S5])
    cos = cos_ref[...]
    sin = sin_ref[...]
    lane = lax.broadcasted_iota(jnp.int32, cos.shape, 1)
    first_half = (lane & (RET_HD // 2)) == 0

    def rotary(z):
        swapped = jnp.where(first_half, pltpu.roll(z, RET_W - RET_HD // 2, 1), pltpu.roll(z, RET_HD // 2, 1))
        return z * cos + swapped * sin

    ret_ref[:, 0:RET_W] = rotary(ret[:, 0:RET_W]).astype(BF16)
    ret_ref[:, RET_W:2 * RET_W] = (rotary(ret[:, RET_W:2 * RET_W]) * RET_HD ** -0.5).astype(BF16)
    ret_ref[:, 2 * RET_W:] = ret[:, 2 * RET_W:].astype(BF16)
    su_ref[...] = _dot(h, w_ref[:, COL_S5:COL_GLA]).astype(BF16)
    gla = _dot(h, w_ref[:, COL_GLA:MIX_COLS])
    glane = lax.broadcasted_iota(jnp.int32, (1, GLA_COLS), 1)
    scale = jnp.where(glane < GLA_KW, GLA_DK ** -0.5, jnp.where(glane < GLA_KP, 0.0, 1.0))
    gla_ref[...] = (gla * scale).astype(BF16)
    glr_ref[...] = gla[:, GLA_LR0:GLA_LR0 + GLA_RANK]


def _mix_in(x2d, norm_g, w_in_bf, layer, cos_tab, sin_tab, tm):
    rows = x2d.shape[0]
    n_tab = cos_tab.shape[0] // tm
    row_spec = lambda w: pl.BlockSpec((tm, w), lambda i: (i, 0))
    tab_spec = pl.BlockSpec((tm, RET_W), lambda i: (i % n_tab, 0))
    return pl.pallas_call(
        _mix_in_body,
        grid=(rows // tm,),
        in_specs=[row_spec(D_MODEL), _layer_spec(norm_g.shape, layer), _layer_spec(w_in_bf.shape, layer, True),
                  tab_spec, tab_spec],
        out_specs=[row_spec(4 * RET_W), row_spec(S5_W), row_spec(GLA_COLS), row_spec(GLA_RANK)],
        out_shape=[jax.ShapeDtypeStruct((rows, 4 * RET_W), BF16), jax.ShapeDtypeStruct((rows, S5_W), BF16),
                   jax.ShapeDtypeStruct((rows, GLA_COLS), BF16), jax.ShapeDtypeStruct((rows, GLA_RANK), F32)],
        scratch_shapes=[pltpu.VMEM((D_MODEL, MIX_COLS), BF16)],
        compiler_params=_params(),
        name="mix_in",
    )(x2d, norm_g, w_in_bf, cos_tab, sin_tab)


def _retention_gammas():
    return 1.0 - 2.0 ** (-5.0 - np.arange(RET_HEADS))


def _retention_consts():
    t = np.arange(CHUNK)
    gam = _retention_gammas()
    causal = t[:, None] >= t[None, :]
    diff = np.maximum(t[:, None] - t[None, :], 0)
    dmask = np.zeros((RET_PAIRS, CHUNK, 2 * CHUNK))
    cdec = np.zeros((RET_PAIRS, 2 * RET_HD, 2 * RET_HD))
    for p in range(RET_PAIRS):
        for s in range(2):
            g = gam[2 * p + s]
            dmask[p, :, s * CHUNK:(s + 1) * CHUNK] = np.where(causal, g ** diff, 0.0)
            cdec[p, s * RET_HD:(s + 1) * RET_HD, s * RET_HD:(s + 1) * RET_HD] = g ** CHUNK
    lane_gam = np.repeat(gam, RET_HD)[None, :]
    qdec = lane_gam ** (t[:, None] + 1.0)
    kdec = lane_gam ** (CHUNK - 1.0 - t[:, None])
    bd = (cdec[0] > 0).astype(np.float32)
    head_mask = np.stack([np.arange(2 * RET_HD) < RET_HD, np.arange(2 * RET_HD) >= RET_HD]).astype(np.float32)
    ones_blk = np.kron(np.eye(RET_HEADS), np.ones((RET_HD, RET_HD)))
    f = lambda a: jnp.asarray(a, F32)
    return dict(dmask=f(dmask), cdec=f(cdec), qdec=qdec, kdec=kdec, bd=f(bd),
                head_mask=jnp.asarray(head_mask, BF16), ones_blk=jnp.asarray(ones_blk, BF16))


def _retention_body(ret_hbm, dmask_ref, cdec_ref, qdec_ref, kdec_ref, bd_ref, hm_ref, ones_ref, lng_ref, lnb_ref,
                    o_ref, sout_ref, s_scr, o_scr, ring, sem, *, tl, nl, n_steps):
    ret_ref = _ring_rows(ret_hbm, ring, sem, pl.program_id(0) * nl + pl.program_id(1), tl, n_steps)

    @pl.when(pl.program_id(1) == 0)
    def _():
        s_scr[...] = jnp.zeros_like(s_scr)

    bd = bd_ref[...]
    m0 = hm_ref[0:1, :]
    m1 = hm_ref[1:2, :]
    q = ret_ref[:, 0:RET_W]
    k = ret_ref[:, RET_W:2 * RET_W]
    v = ret_ref[:, 2 * RET_W:3 * RET_W]
    q_start = (q.astype(F32) * qdec_ref[...]).astype(BF16)
    k_end = (k.astype(F32) * kdec_ref[...]).astype(BF16)
    blocks = [(n, p) for n in range(tl // CHUNK) for p in range(RET_PAIRS)]

    def part(a, n, p):
        return a[n * CHUNK:(n + 1) * CHUNK, 2 * RET_HD * p:2 * RET_HD * (p + 1)]

    scores, o, kv = {}, {}, {}
    for n, p in blocks:
        kp = part(k, n, p)
        kk = jnp.concatenate([kp * m0, kp * m1], axis=0)
        scores[n, p] = (_dot_nt(part(q, n, p), kk) * dmask_ref[p]).astype(BF16)
    for n, p in blocks:
        vp = part(v, n, p)
        o[n, p] = _dot(scores[n, p], jnp.concatenate([vp * m0, vp * m1], axis=0))
    for n, p in blocks:
        kv[n, p] = _dot_tn(part(k_end, n, p), part(v, n, p)) * bd
    for n, p in blocks:
        s_prev = s_scr[p]
        o_scr[n * CHUNK:(n + 1) * CHUNK, 2 * RET_HD * p:2 * RET_HD * (p + 1)] = (
            o[n, p] + _dot(part(q_start, n, p), s_prev.astype(BF16)))
        s_scr[p] = s_prev * cdec_ref[p] + kv[n, p]

    o_all = o_scr[...]
    ones_blk = ones_ref[...]
    mu = _dot_row_halves(o_all.astype(BF16), ones_blk) * (1.0 / RET_HD)
    cen = o_all - mu
    var = _dot_row_halves((cen * cen).astype(BF16), ones_blk) * (1.0 / RET_HD)
    y = cen * lax.rsqrt(var + EPS) * lng_ref[...] + lnb_ref[...]
    gate = ret_ref[:, 3 * RET_W:4 * RET_W].astype(F32)
    o_ref[...] = (y * gate * _sigmoid(gate)).astype(BF16)
    for p in range(RET_PAIRS):
        s = s_scr[p]
        sout_ref[0, 2 * p] = s[0:RET_HD, 0:RET_HD]
        sout_ref[0, 2 * p + 1] = s[RET_HD:2 * RET_HD, RET_HD:2 * RET_HD]


def _retention(ret2d, ln_g, ln_b, layer, batch, seq_len):
    assert seq_len % CHUNK == 0
    c = _retention_consts()
    tl = min(MIXER_TILE, seq_len)
    nl = seq_len // tl
    tile_rows = lambda a: jnp.asarray(np.tile(a, (tl // CHUNK, 1)), F32)
    consts = [(c["dmask"], False), (c["cdec"], False), (tile_rows(c["qdec"]), False), (tile_rows(c["kdec"]), False),
              (c["bd"], False), (c["head_mask"], False), (c["ones_blk"], False), (ln_g, True), (ln_b, True)]
    state_shape = (batch, RET_HEADS, RET_HD, RET_HD)
    return pl.pallas_call(
        functools.partial(_retention_body, tl=tl, nl=nl, n_steps=batch * nl),
        grid=(batch, nl),
        in_specs=[pl.BlockSpec(memory_space=pl.ANY)] + _specs(consts, layer),
        out_specs=[pl.BlockSpec((tl, RET_W), lambda b, l: (b * nl + l, 0)),
                   pl.BlockSpec((1,) + state_shape[1:], lambda b, l: (b, 0, 0, 0))],
        out_shape=[jax.ShapeDtypeStruct((batch * seq_len, RET_W), BF16), jax.ShapeDtypeStruct(state_shape, F32)],
        scratch_shapes=[pltpu.VMEM((RET_PAIRS, 2 * RET_HD, 2 * RET_HD), F32), pltpu.VMEM((tl, RET_W), F32),
                        pltpu.VMEM((INPUT_RING, tl, 4 * RET_W), BF16), pltpu.SemaphoreType.DMA((INPUT_RING,))],
        compiler_params=_ring_params(),
        name="retention",
    )(ret2d, *[a for a, _ in consts])


def _retention_lanes_head(h, *refs, seq_len, n_in):
    ret_ref, tab_ref, lng_ref, lnb_ref, s_ref = refs[:5]
    o_ref, sout_ref, q_scr, k_scr, v_scr, g_scr, o_scr = refs[n_in:]
    nb = V7X_LANES
    steps = range(seq_len)

    @pl.when(h == 0)
    def _():
        for t in steps:
            blk = ret_ref[t * nb:(t + 1) * nb, :].astype(F32)
            for scr, off in ((q_scr, 0), (k_scr, RET_W), (v_scr, 2 * RET_W), (g_scr, 3 * RET_W)):
                tr = _to_lanes(blk[:, off:off + RET_W])
                for hh in range(RET_HEADS):
                    scr[t, hh] = tr[hh * RET_HD:(hh + 1) * RET_HD]

    tab = tab_ref[h]
    row = lambda r: tab[r:r + 1, :]
    q = [q_scr[t, h] for t in steps]
    k = [k_scr[t, h] for t in steps]
    v = [v_scr[t, h] for t in steps]
    o = []
    for i in steps:
        acc = None
        for j in range(i + 1):
            s = jnp.sum(q[i] * k[j], axis=0, keepdims=True)
            if i > j:
                s = s * row(i - j - 1)
            acc = s * v[j] if acc is None else acc + s * v[j]
        o.append(acc)
    q_start = [q[t] * row(t) for t in steps]
    k_end = [k[t] * row(seq_len + t) for t in steps]
    chunk_decay = row(2 * seq_len)
    for d in range(RET_HD):
        s_d = s_ref[d]
        new = s_d * chunk_decay
        for t in steps:
            o[t] = o[t] + q_start[t][d:d + 1, :] * s_d
            new = new + k_end[t][d:d + 1, :] * v[t]
        sout_ref[d] = new
    for t in steps:
        mu = jnp.mean(o[t], axis=0, keepdims=True)
        cen = o[t] - mu
        var = jnp.mean(cen * cen, axis=0, keepdims=True)
        gate = g_scr[t, h]
        o_scr[t, h] = (cen * lax.rsqrt(var + EPS) * lng_ref[h] + lnb_ref[h]) * gate * _sigmoid(gate)

    @pl.when(h == RET_HEADS - 1)
    def _():
        for t in steps:
            o_ref[t * nb:(t + 1) * nb, :] = o_scr[t].reshape(RET_W, nb).T.astype(BF16)


def _retention_lanes(ret2d, ln_g, ln_b, state, new_state, layer, seq_len):
    nb = V7X_LANES
    gam = _retention_gammas()[:, None]
    t = np.arange(seq_len)[None, :]
    rows = np.concatenate([gam ** (t + 1.0), gam ** (seq_len - 1.0 - t), gam ** (seq_len + 0.0 * t[:, :1])], axis=1)
    n_rows = -(-rows.shape[1] // V7X_SUBLANES) * V7X_SUBLANES
    tab = np.zeros((RET_HEADS, n_rows, nb))
    tab[:, :rows.shape[1], :] = rows[:, :, None]
    consts = [(jnp.asarray(tab, F32), False), (ln_g, True), (ln_b, True)]
    tile = pltpu.VMEM((seq_len, RET_HEADS, RET_HD, nb), F32)
    block = (None, None, RET_HD, RET_HD, nb)
    prev, prev_specs, alias = _carried_output(new_state, n_before=5, out_index=1)
    slots = 1 if prev else state.shape[0] - layer
    return pl.pallas_call(
        functools.partial(_layer_slot_body, head_fn=_retention_lanes_head, n_in=5 + len(prev), seq_len=seq_len),
        grid=(slots, RET_HEADS),
        in_specs=[_const_spec(ret2d.shape)] + _specs(consts, layer)
                 + [pl.BlockSpec(block, lambda g, h: (layer, _held_head(g, h, RET_HEADS), 0, 0, 0))] + prev_specs,
        out_specs=[_const_spec((seq_len * nb, RET_W)), pl.BlockSpec(block, lambda g, h: (layer + g, h, 0, 0, 0))],
        out_shape=[jax.ShapeDtypeStruct((seq_len * nb, RET_W), BF16), jax.ShapeDtypeStruct(state.shape, F32)],
        scratch_shapes=[tile] * 5,
        input_output_aliases=alias,
        compiler_params=_params(),
        name="retention_lanes",
    )(ret2d, *[a for a, _ in consts], state, *prev)


def _gla_consts():
    t = np.arange(CHUNK)
    causal = t[:, None] >= t[None, :]
    pad = GLA_KP - GLA_KW
    kmask = np.pad(np.kron(np.eye(GLA_HEADS), np.ones((1, GLA_DK))), ((0, 0), (0, pad)))
    vmask = np.kron(np.eye(GLA_HEADS), np.ones((1, GLA_DV)))
    bd = np.pad(np.kron(np.eye(GLA_HEADS), np.ones((GLA_DK, GLA_DV))), ((0, pad), (0, 0)))
    ones_blk = np.kron(np.eye(GLA_HEADS), np.ones((GLA_DV, GLA_DV)))
    b = lambda a: jnp.asarray(a, BF16)
    return dict(tril=b(causal), causal4=jnp.asarray(np.tile(causal, (1, GLA_HEADS)), F32), kmask=b(kmask),
                vmask=b(vmask), bd=jnp.asarray(bd, F32), ones_blk=b(ones_blk))


def _gla_body(gla_hbm, glr_ref, gw_ref, gb_ref, tril_ref, causal_ref, kmask_ref, vmask_ref, bd_ref, ones_blk_ref,
              block_ref, ng_ref, o_ref, sout_ref, s_scr, o_scr, ring, sem, *, tl, nl, n_steps):
    gla_ref = _ring_rows(gla_hbm, ring, sem, pl.program_id(0) * nl + pl.program_id(1), tl, n_steps)

    @pl.when(pl.program_id(1) == 0)
    def _():
        s_scr[...] = jnp.zeros_like(s_scr)

    bd = bd_ref[...]
    z = _dot_row_halves(glr_ref[...].astype(BF16), gw_ref[...]) + gb_ref[...]
    lg = _log_sigmoid(z) * (1.0 / GLA_GATE_TEMP)
    q0, k0, v0, g0 = GLA_Q0, GLA_K0, GLA_V0, GLA_G0
    chunks = range(tl // CHUNK)

    def rows(a, n):
        return a[n * CHUNK:(n + 1) * CHUNK]

    cums = [_dot_exact_lhs(tril_ref[...], rows(lg, n)) for n in chunks]
    b_cum = jnp.concatenate(cums, axis=0)
    b_end = jnp.concatenate([jnp.broadcast_to(c[CHUNK - 1:CHUNK, :], (CHUNK, GLA_KP)) for c in cums], axis=0)
    q_in = (gla_ref[:, q0:q0 + GLA_KP].astype(F32) * jnp.exp(b_cum)).astype(BF16)
    kf = gla_ref[:, k0:k0 + GLA_KP].astype(F32)
    k_in = (kf * jnp.exp(-b_cum)).astype(BF16)
    k_dec = (kf * jnp.exp(b_end - b_cum)).astype(BF16)
    v = gla_ref[:, v0:g0]
    lg_hi, lg_lo = _hi_lo(lg)
    block_decay = jnp.exp(_dot_tn(lg_hi, block_ref[...]) + _dot_tn(lg_lo, block_ref[...]))

    scores, o, kv = {}, {}, {}
    for n in chunks:
        kn = rows(k_in, n)
        kk = jnp.concatenate([kn * kmask_ref[h:h + 1, :] for h in range(GLA_HEADS)], axis=0)
        scores[n] = (_dot_nt(rows(q_in, n), kk) * causal_ref[...]).astype(BF16)
    for n in chunks:
        vn = rows(v, n)
        vv = jnp.concatenate([vn * vmask_ref[h:h + 1, :] for h in range(GLA_HEADS)], axis=0)
        o[n] = _dot(scores[n], vv)
    for n in chunks:
        kv[n] = _dot_tn(rows(k_dec, n), rows(v, n)) * bd
    for n in chunks:
        s_prev = s_scr[...]
        o_scr[n * CHUNK:(n + 1) * CHUNK, :] = o[n] + _dot(rows(q_in, n), s_prev.astype(BF16))
        s_scr[...] = s_prev * jnp.broadcast_to(block_decay[:, n:n + 1], (GLA_KP, GLA_W)) + kv[n]

    o_all = o_scr[...]
    ms = _dot_row_halves((o_all * o_all).astype(BF16), ones_blk_ref[...]) * (1.0 / GLA_DV)
    gate = gla_ref[:, g0:g0 + GLA_W].astype(F32)
    o_ref[...] = (o_all * lax.rsqrt(ms + EPS) * ng_ref[...] * gate * _sigmoid(gate)).astype(BF16)
    s = s_scr[...]
    for h in range(GLA_HEADS):
        sout_ref[0, h] = s[h * GLA_DK:(h + 1) * GLA_DK, h * GLA_DV:(h + 1) * GLA_DV]


def _gla(gla2d, glr2d, gate_w, gate_b, norm_g, layer, batch, seq_len):
    assert seq_len % CHUNK == 0
    c = _gla_consts()
    tl = min(MIXER_TILE, seq_len)
    nl = seq_len // tl
    assert tl // CHUNK <= V7X_LANES
    block = (np.arange(tl) // CHUNK)[:, None] == np.arange(V7X_LANES)[None, :]
    consts = [(gate_w, True), (gate_b, True), (c["tril"], False), (c["causal4"], False), (c["kmask"], False),
              (c["vmask"], False), (c["bd"], False), (c["ones_blk"], False), (jnp.asarray(block, BF16), False),
              (norm_g, True)]
    row_spec = lambda w: pl.BlockSpec((tl, w), lambda b, l: (b * nl + l, 0))
    state_shape = (batch, GLA_HEADS, GLA_DK, GLA_DV)
    return pl.pallas_call(
        functools.partial(_gla_body, tl=tl, nl=nl, n_steps=batch * nl),
        grid=(batch, nl),
        in_specs=[pl.BlockSpec(memory_space=pl.ANY), row_spec(GLA_RANK)] + _specs(consts, layer),
        out_specs=[row_spec(GLA_W), pl.BlockSpec((1,) + state_shape[1:], lambda b, l: (b, 0, 0, 0))],
        out_shape=[jax.ShapeDtypeStruct((batch * seq_len, GLA_W), BF16), jax.ShapeDtypeStruct(state_shape, F32)],
        scratch_shapes=[pltpu.VMEM((GLA_KP, GLA_W), F32), pltpu.VMEM((tl, GLA_W), F32),
                        pltpu.VMEM((INPUT_RING, tl, GLA_COLS), BF16), pltpu.SemaphoreType.DMA((INPUT_RING,))],
        compiler_params=_ring_params(),
        name="gla",
    )(gla2d, glr2d, *[a for a, _ in consts])


def _gla_lanes_head(h, *refs, seq_len, n_in):
    gla_ref, glr_ref, gw_ref, gb_ref, ng_ref, s_ref = refs[:6]
    o_ref, sout_ref, q_scr, k_scr, kd_scr, v_scr, g_scr, dec_scr, o_scr = refs[n_in:]
    nb = V7X_LANES
    steps = range(seq_len)
    q0, k0, v0, g0 = GLA_Q0, GLA_K0, GLA_V0, GLA_G0
    kw = slice(0, GLA_KW)

    def split(scr, t, tr, width):
        for hh in range(GLA_HEADS):
            scr[t, hh] = tr[hh * width:(hh + 1) * width]

    @pl.when(h == 0)
    def _():
        z = _dot_row_halves(glr_ref[...].astype(BF16), gw_ref[...]) + gb_ref[...]
        lg = (_log_sigmoid(z) * (1.0 / GLA_GATE_TEMP))[:, kw]
        cums = []
        for t in steps:
            lg_t = lg[t * nb:(t + 1) * nb]
            cums.append(lg_t if t == 0 else cums[-1] + lg_t)
        b_end = cums[-1]
        dec = _to_lanes(jnp.exp(b_end))
        for hh in range(GLA_HEADS):
            dec_scr[hh] = dec[hh * GLA_DK:(hh + 1) * GLA_DK]
        for t in steps:
            blk = gla_ref[t * nb:(t + 1) * nb, :].astype(F32)
            kf = blk[:, k0:k0 + GLA_KW]
            split(q_scr, t, _to_lanes(blk[:, q0:q0 + GLA_KW] * jnp.exp(cums[t])), GLA_DK)
            split(k_scr, t, _to_lanes(kf * jnp.exp(-cums[t])), GLA_DK)
            split(kd_scr, t, _to_lanes(kf * jnp.exp(b_end - cums[t])), GLA_DK)
            split(v_scr, t, _to_lanes(blk[:, v0:g0]), GLA_DV)
            split(g_scr, t, _to_lanes(blk[:, g0:g0 + GLA_W]), GLA_DV)

    q = [q_scr[t, h] for t in steps]
    k = [k_scr[t, h] for t in steps]
    kd = [kd_scr[t, h] for t in steps]
    v = [v_scr[t, h] for t in steps]
    dec = dec_scr[h]
    o = []
    for i in steps:
        acc = None
        for j in range(i + 1):
            s = jnp.sum(q[i] * k[j], axis=0, keepdims=True)
            acc = s * v[j] if acc is None else acc + s * v[j]
        o.append(acc)
    for i in range(GLA_DK):
        s_i = s_ref[i]
        new = s_i * dec[i:i + 1, :]
        for t in steps:
            o[t] = o[t] + q[t][i:i + 1, :] * s_i
            new = new + kd[t][i:i + 1, :] * v[t]
        sout_ref[i] = new
    for t in steps:
        ms = jnp.mean(o[t] * o[t], axis=0, keepdims=True)
        gate = g_scr[t, h]
        o_scr[t, h] = o[t] * lax.rsqrt(ms + EPS) * ng_ref[...] * gate * _sigmoid(gate)

    @pl.when(h == GLA_HEADS - 1)
    def _():
        for t in steps:
            o_ref[t * nb:(t + 1) * nb, :] = o_scr[t].reshape(GLA_W, nb).T.astype(BF16)


def _gla_lanes(gla2d, glr2d, gate_w, gate_b, norm_g, state, new_state, layer, seq_len):
    nb = V7X_LANES
    consts = [(gate_w, True), (gate_b, True), (norm_g, True)]
    k_tile = pltpu.VMEM((seq_len, GLA_HEADS, GLA_DK, nb), F32)
    v_tile = pltpu.VMEM((seq_len, GLA_HEADS, GLA_DV, nb), F32)
    block = (None, None, GLA_DK, GLA_DV, nb)
    prev, prev_specs, alias = _carried_output(new_state, n_before=6, out_index=1)
    slots = 1 if prev else state.shape[0] - layer
    return pl.pallas_call(
        functools.partial(_layer_slot_body, head_fn=_gla_lanes_head, n_in=6 + len(prev), seq_len=seq_len),
        grid=(slots, GLA_HEADS),
        in_specs=[_const_spec(gla2d.shape), _const_spec(glr2d.shape)] + _specs(consts, layer)
                 + [pl.BlockSpec(block, lambda g, h: (layer, _held_head(g, h, GLA_HEADS), 0, 0, 0))] + prev_specs,
        out_specs=[_const_spec((seq_len * nb, GLA_W)), pl.BlockSpec(block, lambda g, h: (layer + g, h, 0, 0, 0))],
        out_shape=[jax.ShapeDtypeStruct((seq_len * nb, GLA_W), BF16), jax.ShapeDtypeStruct(state.shape, F32)],
        scratch_shapes=[k_tile, k_tile, k_tile, v_tile, v_tile, pltpu.VMEM((GLA_HEADS, GLA_DK, nb), F32), v_tile],
        input_output_aliases=alias,
        compiler_params=_params(),
        name="gla_lanes",
    )(gla2d, glr2d, *[a for a, _ in consts], state, *prev)


def _s5_discretize(lam_re, lam_im, log_dt, b_re, b_im, c_re, c_im):
    lr = jnp.minimum(lam_re.astype(F32), -S5_MIN_NEG)
    li = lam_im.astype(F32)
    dt = jnp.exp(log_dt.astype(F32))[..., None]
    mag = jnp.exp(lr * dt)
    ar = mag * jnp.cos(li * dt)
    ai = mag * jnp.sin(li * dt)
    den = lr * lr + li * li
    cr = ((ar - 1.0) * lr + ai * li) / den
    ci = (ai * lr - (ar - 1.0) * li) / den
    b_re, b_im = b_re.astype(F32), b_im.astype(F32)
    bbar_re = cr[..., None] * b_re - ci[..., None] * b_im
    bbar_im = cr[..., None] * b_im + ci[..., None] * b_re
    eye = jnp.eye(S5_GROUPS, dtype=F32)
    depth = lr.shape[0]
    in_blk = lambda b: jnp.einsum("lgpi,gh->lgihp", b, eye).reshape(depth, S5_W, S5_LANES)
    out_blk = lambda c: jnp.einsum("lgop,gh->lgpho", c.astype(F32), eye).reshape(depth, S5_LANES, S5_W)
    bdb = jnp.concatenate([in_blk(bbar_re), in_blk(bbar_im)], axis=2).astype(BF16)
    bdc = jnp.concatenate([out_blk(c_re), -out_blk(c_im)], axis=1).astype(BF16)
    return ar.reshape(depth, 1, S5_LANES), ai.reshape(depth, 1, S5_LANES), bdb, bdc


def _s5_body(*refs, tl, batch, has_state, batch_major):
    u_ref, ar_ref, ai_ref, bdb_ref, bdc_ref, d_ref, gw_ref, gb_ref = refs[:8]
    n_in = 10 if has_state else 8
    o_ref, hr_ref, hi_ref, x_scr, h_scr = refs[n_in:]

    @pl.when(pl.program_id(0) == 0)
    def _():
        if has_state:
            h_scr[0] = refs[8][...]
            h_scr[1] = refs[9][...]
        else:
            h_scr[...] = jnp.zeros_like(h_scr)

    ar = jnp.broadcast_to(ar_ref[...], (batch, S5_LANES))
    ai = jnp.broadcast_to(ai_ref[...], (batch, S5_LANES))
    hr, hi = h_scr[0], h_scr[1]
    sub = min(tl, S5_SUB_STEPS)
    n_rows = sub * batch
    for s in range(tl // sub):
        rows = slice(s * n_rows, (s + 1) * n_rows)
        steps = slice(s * sub, (s + 1) * sub)
        if batch_major:
            u = jnp.swapaxes(u_ref[:, steps, :].astype(F32), 0, 1).reshape(n_rows, S5_W).astype(BF16)
        else:
            u = u_ref[rows, :]
        x_scr[rows, :] = _dot_row_halves(u, bdb_ref[...])
        for t in range(s * sub, (s + 1) * sub):
            r = slice(t * batch, (t + 1) * batch)
            hr, hi = (ar * hr - ai * hi + x_scr[r, 0:S5_LANES],
                      ar * hi + ai * hr + x_scr[r, S5_LANES:2 * S5_LANES])
            x_scr[r, 0:S5_LANES] = hr
            x_scr[r, S5_LANES:2 * S5_LANES] = hi
        y = _dot_row_halves(x_scr[rows, :].astype(BF16), bdc_ref[...]) + d_ref[...] * u.astype(F32)
        y = _gelu_tanh(y)
        o = y * _sigmoid(_dot_row_halves(y.astype(BF16), gw_ref[...]) + gb_ref[...])
        if batch_major:
            o_ref[:, steps, :] = jnp.swapaxes(o.reshape(sub, batch, S5_W), 0, 1).astype(BF16)
        else:
            o_ref[rows, :] = o.astype(BF16)
    h_scr[0] = hr
    h_scr[1] = hi
    hr_ref[...] = hr
    hi_ref[...] = hi


def _s5(u, disc, d, glu_w_bf, glu_b, layer, state, batch, seq_len):
    tl = min(S5_TIME_TILE, seq_len)
    has_state = state is not None
    batch_major = u.ndim == 3
    consts = list(disc) + [d, glu_w_bf, glu_b]
    h_spec = pl.BlockSpec((batch, S5_LANES), lambda i: (0, 0))
    if batch_major:
        u_spec = pl.BlockSpec((batch, tl, S5_W), lambda i: (0, i, 0))
    else:
        u_spec = pl.BlockSpec((tl * batch, S5_W), lambda i: (i, 0))
    in_specs = [u_spec] + [_layer_spec(a.shape, layer) for a in consts]
    args = [u] + consts
    if has_state:
        in_specs += [h_spec, h_spec]
        args += list(state)
    return pl.pallas_call(
        functools.partial(_s5_body, tl=tl, batch=batch, has_state=has_state, batch_major=batch_major),
        grid=(seq_len // tl,),
        in_specs=in_specs,
        out_specs=[u_spec, h_spec, h_spec],
        out_shape=[jax.ShapeDtypeStruct(u.shape, BF16),
                   jax.ShapeDtypeStruct((batch, S5_LANES), F32), jax.ShapeDtypeStruct((batch, S5_LANES), F32)],
        scratch_shapes=[pltpu.VMEM((tl * batch, 2 * S5_LANES), F32), pltpu.VMEM((2, batch, S5_LANES), F32)],
        compiler_params=_params(),
        name="s5",
    )(*args)


def _ffn_body(*refs, tm, time_major_batch, final):
    (x_ref, oret_ref, os5_ref, ogla_ref, wout_ref, gffn_ref, win_ref, cw_ref, cb_ref, wo_ref) = refs[:10]
    n = 10
    gfin_ref = None
    if final:
        gfin_ref, n = refs[n], n + 1
    if time_major_batch is not None:
        past_ref, n = refs[n], n + 1
    out_ref, conv_ref, x1_scr, h_scr, carry_scr, act_scr = refs[n:]
    pad = V7X_SUBLANES
    nb = time_major_batch

    if nb is None:
        @pl.when(pl.program_id(1) == 0)
        def _():
            carry_scr[...] = jnp.zeros_like(carry_scr)

    mix = jnp.concatenate([oret_ref[...], os5_ref[...], ogla_ref[...]], axis=1)
    x1_scr[...] = x_ref[...] + _dot(mix, wout_ref[...])
    h_scr[...] = _rmsnorm_rows(x1_scr[...], gffn_ref[...]).astype(BF16)
    h = h_scr[...]

    for c in range(D_FF // ACT_COLS):
        cols = slice(c * ACT_COLS, (c + 1) * ACT_COLS)
        a_c = _dot(h, win_ref[:, c * ACT_COLS:(c + 1) * ACT_COLS])
        gate_c = _dot(h, win_ref[:, D_FF + c * ACT_COLS:D_FF + (c + 1) * ACT_COLS])
        if nb is None:
            ext = jnp.concatenate([carry_scr[:, cols], a_c], axis=0)
            prev1 = ext[pad - 1:pad - 1 + tm]
            prev2 = ext[pad - 2:pad - 2 + tm]
            carry_scr[:, cols] = a_c[tm - pad:tm]
        else:
            prev1 = jnp.concatenate([past_ref[1, :, cols], a_c[0:tm - nb]], axis=0)
            prev2 = jnp.concatenate([past_ref[0, :, cols], past_ref[1, :, cols], a_c[0:tm - 2 * nb]], axis=0)
            conv_ref[:, cols] = a_c[tm - (CONV_W - 1) * nb:tm]
        conv = (cb_ref[:, cols] + prev2 * cw_ref[0:1, cols] + prev1 * cw_ref[1:2, cols]
                + a_c * cw_ref[2:3, cols])
        act_scr[:, cols] = (_gelu_tanh(conv) * gate_c).astype(BF16)

    x2 = x1_scr[...] + _dot(act_scr[...], wo_ref[...])
    out_ref[...] = _rmsnorm_rows(x2, gfin_ref[...]) if final else x2
    if nb is None:
        conv_ref[0] = carry_scr[pad - (CONV_W - 1):pad, :]


def _ffn(x2d, o_ret, o_s5, o_gla, w_out_bf, norm_g, w_in_bf, conv_w, conv_b, w_o_bf, layer, final_g, conv_past,
         batch, seq_len):
    rows = batch * seq_len
    time_major = conv_past is not None
    tm = rows if time_major else min(ROW_TILE, seq_len)
    final = final_g is not None
    consts = [w_out_bf, norm_g, w_in_bf, conv_w, conv_b, w_o_bf]
    stacked = [True] * len(consts)
    big = [True, False, True, False, False, True]
    if final:
        consts.append(final_g.reshape(1, D_MODEL))
        stacked.append(False)
        big.append(False)
    if time_major:
        assert seq_len >= CONV_W - 1 and batch % V7X_SUBLANES == 0
        consts.append(conv_past)
        stacked.append(False)
        big.append(False)
        grid = (1,)
        imap = lambda i: (0, 0)
        conv_rows = (CONV_W - 1) * batch
        conv_spec = pl.BlockSpec((conv_rows, D_FF), imap)
        conv_shape = jax.ShapeDtypeStruct((conv_rows, D_FF), F32)
    else:
        assert seq_len % tm == 0
        nl = seq_len // tm
        grid = (batch, nl)
        imap = lambda b, l: (b * nl + l, 0)
        conv_spec = pl.BlockSpec((1, CONV_W - 1, D_FF), lambda b, l: (b, 0, 0))
        conv_shape = jax.ShapeDtypeStruct((batch, CONV_W - 1, D_FF), F32)
    row_spec = lambda w: pl.BlockSpec((tm, w), imap)
    in_specs = ([row_spec(D_MODEL), row_spec(RET_W), row_spec(S5_W), row_spec(GLA_W)]
                + [_layer_spec(a.shape, layer, b) if s else _const_spec(a.shape)
                   for a, s, b in zip(consts, stacked, big)])
    return pl.pallas_call(
        functools.partial(_ffn_body, tm=tm, time_major_batch=batch if time_major else None, final=final),
        grid=grid,
        in_specs=in_specs,
        out_specs=[row_spec(D_MODEL), conv_spec],
        out_shape=[jax.ShapeDtypeStruct((rows, D_MODEL), F32), conv_shape],
        scratch_shapes=[pltpu.VMEM((tm, D_MODEL), F32), pltpu.VMEM((tm, D_MODEL), BF16),
                        pltpu.VMEM((V7X_SUBLANES, D_FF), F32), pltpu.VMEM((tm, D_FF), BF16)],
        compiler_params=_params(),
        name="ffn",
    )(x2d, o_ret, o_s5, o_gla, *consts)


def _rotary_tables(first_pos, n_pos, repeat=1):
    half = RET_HD // 2
    inv = ROPE_BASE ** (-(np.arange(half, dtype=np.float64) / half))
    ang = (first_pos + np.arange(n_pos)).astype(np.float64)[:, None] * inv[None, :]
    cos, sin = np.cos(ang), np.sin(ang)
    expand = lambda a, b: np.repeat(np.tile(np.concatenate([a, b], axis=1), (1, RET_HEADS)), repeat, axis=0)
    return jnp.asarray(expand(cos, cos), F32), jnp.asarray(expand(-sin, sin), F32)


def _run_prompt_group(x, prm, final_g):
    batch, seq_len, _ = x.shape
    rows = batch * seq_len
    depth = prm["w_in"].shape[0]
    tm = min(MIX_IN_TILE, seq_len)
    cos_tab, sin_tab = _rotary_tables(0, seq_len)
    x2d = x.reshape(rows, D_MODEL)
    outs = []
    for li in range(depth):
        ret, su, gla, glr = _mix_in(x2d, prm["norm_mix_g"], prm["w_in"], li, cos_tab, sin_tab, tm)
        o_ret, s_ret = _retention(ret, prm["ret_ln_g"], prm["ret_ln_b"], li, batch, seq_len)
        o_s5, s5r, s5i = _s5(su.reshape(batch, seq_len, S5_W), prm["s5_disc"], prm["s5_d"], prm["s5_glu_w"],
                             prm["s5_glu_b"], li, None, batch, seq_len)
        o_s5 = o_s5.reshape(rows, S5_W)
        o_gla, s_gla = _gla(gla, glr, prm["gla_gate_w"], prm["gla_gate_b"], prm["gla_norm_g"], li, batch, seq_len)
        x2d, conv_new = _ffn(x2d, o_ret, o_s5, o_gla, prm["w_out"], prm["norm_ffn_g"], prm["ffn_w_in"],
                             prm["ffn_conv_w"], prm["ffn_conv_b"], prm["ffn_w_out"], li,
                             final_g if li == depth - 1 else None, None, batch, seq_len)
        s5_shape = (batch, S5_GROUPS, S5_STATE)
        outs.append((s_ret, s5r.reshape(s5_shape), s5i.reshape(s5_shape), s_gla, conv_new))
    return [x2d.reshape(batch, seq_len, D_MODEL)] + [jnp.stack([o[i] for o in outs]) for i in range(5)]


def _run_sample_group(x, past_len, states, prm, final_g):
    batch, seq_len, _ = x.shape
    assert batch == V7X_LANES
    rows = batch * seq_len
    depth = prm["w_in"].shape[0]
    cos_tab, sin_tab = _rotary_tables(past_len, seq_len, repeat=batch)
    x2d = x.transpose(1, 0, 2).reshape(rows, D_MODEL)
    ret_state = states["ret"].transpose(0, 2, 3, 4, 1)
    gla_state = states["gla"].transpose(0, 2, 3, 4, 1)
    conv_state = states["conv"].transpose(0, 2, 1, 3)
    s5_flat = lambda s: s.reshape(depth, batch, S5_LANES)
    s5r_state, s5i_state = s5_flat(states["s5r"]), s5_flat(states["s5i"])
    outs = []
    s_ret = s_gla = None
    for li in range(depth):
        ret, su, gla, glr = _mix_in(x2d, prm["norm_mix_g"], prm["w_in"], li, cos_tab, sin_tab, rows)
        o_ret, s_ret = _retention_lanes(ret, prm["ret_ln_g_lanes"], prm["ret_ln_b_lanes"], ret_state, s_ret, li,
                                        seq_len)
        o_s5, s5r, s5i = _s5(su, prm["s5_disc"], prm["s5_d"], prm["s5_glu_w"], prm["s5_glu_b"], li,
                             (s5r_state[li], s5i_state[li]), batch, seq_len)
        o_gla, s_gla = _gla_lanes(gla, glr, prm["gla_gate_w"], prm["gla_gate_b"], prm["gla_norm_g_lanes"], gla_state,
                                  s_gla, li, seq_len)
        x2d, conv_new = _ffn(x2d, o_ret, o_s5, o_gla, prm["w_out"], prm["norm_ffn_g"], prm["ffn_w_in"],
                             prm["ffn_conv_w"], prm["ffn_conv_b"], prm["ffn_w_out"], li,
                             final_g if li == depth - 1 else None, conv_state[li], batch, seq_len)
        outs.append((s5r, s5i, conv_new.reshape(CONV_W - 1, batch, D_FF)))
    s5r, s5i, conv_new = (jnp.stack([o[i] for o in outs]) for i in range(3))
    s5_shape = (depth, batch, S5_GROUPS, S5_STATE)
    return [x2d.reshape(seq_len, batch, D_MODEL).transpose(1, 0, 2), s_ret.transpose(0, 4, 1, 2, 3),
            s5r.reshape(s5_shape), s5i.reshape(s5_shape), s_gla.transpose(0, 4, 1, 2, 3),
            conv_new.transpose(0, 2, 1, 3)]


def _prepare_params(norm_mix_g, w_in, ret_norm_g, ret_norm_b, s5_lambda_re, s5_lambda_im, s5_log_dt, s5_b_re, s5_b_im,
                    s5_c_re, s5_c_im, s5_d, s5_glu_w, s5_glu_b, gla_gate_w, gla_gate_b, gla_norm_g, w_out,
                    norm_ffn_g, ffn_w_in, ffn_conv_w, ffn_conv_b, ffn_w_out):
    depth = w_in.shape[0]
    nb = V7X_LANES
    row = lambda a: a.astype(F32).reshape(depth, 1, a.shape[-1])
    ret_lanes = lambda a: jnp.broadcast_to(a.astype(F32).reshape(depth, RET_HEADS, RET_HD, 1),
                                           (depth, RET_HEADS, RET_HD, nb))
    return dict(
        norm_mix_g=row(norm_mix_g), w_in=w_in.astype(BF16),
        ret_ln_g=row(ret_norm_g), ret_ln_b=row(ret_norm_b),
        ret_ln_g_lanes=ret_lanes(ret_norm_g), ret_ln_b_lanes=ret_lanes(ret_norm_b),
        s5_disc=_s5_discretize(s5_lambda_re, s5_lambda_im, s5_log_dt, s5_b_re, s5_b_im, s5_c_re, s5_c_im),
        s5_d=row(s5_d), s5_glu_w=s5_glu_w.astype(BF16), s5_glu_b=row(s5_glu_b),
        gla_gate_w=_pad_gate(gla_gate_w, BF16), gla_gate_b=_pad_gate(row(gla_gate_b), F32),
        gla_norm_g=row(jnp.tile(gla_norm_g, (1, GLA_HEADS))),
        gla_norm_g_lanes=jnp.broadcast_to(gla_norm_g.astype(F32).reshape(depth, GLA_DV, 1), (depth, GLA_DV, nb)),
        w_out=w_out.astype(BF16), norm_ffn_g=row(norm_ffn_g), ffn_w_in=ffn_w_in.astype(BF16),
        ffn_conv_w=jnp.pad(ffn_conv_w.astype(F32), ((0, 0), (0, V7X_SUBLANES - CONV_W), (0, 0))),
        ffn_conv_b=row(ffn_conv_b), ffn_w_out=ffn_w_out.astype(BF16))


def kernel(x_prompt, x_sample, state_ret, state_s5_re, state_s5_im, state_gla, state_ffn_conv, norm_mix_g, w_in, ret_norm_g, ret_norm_b, s5_lambda_re, s5_lambda_im, s5_log_dt, s5_b_re, s5_b_im, s5_c_re, s5_c_im, s5_d, s5_glu_w, s5_glu_b, gla_gate_w, gla_gate_b, gla_norm_g, w_out, norm_ffn_g, ffn_w_in, ffn_conv_w, ffn_conv_b, ffn_w_out, norm_final_g):
    prm = _prepare_params(norm_mix_g, w_in, ret_norm_g, ret_norm_b, s5_lambda_re, s5_lambda_im, s5_log_dt, s5_b_re,
                          s5_b_im, s5_c_re, s5_c_im, s5_d, s5_glu_w, s5_glu_b, gla_gate_w, gla_gate_b, gla_norm_g,
                          w_out, norm_ffn_g, ffn_w_in, ffn_conv_w, ffn_conv_b, ffn_w_out)
    sample_states = dict(ret=state_ret, s5r=state_s5_re, s5i=state_s5_im, gla=state_gla, conv=state_ffn_conv)
    yp, ret_p, s5r_p, s5i_p, gla_p, conv_p = _run_prompt_group(x_prompt, prm, norm_final_g)
    ys, ret_s, s5r_s, s5i_s, gla_s, conv_s = _run_sample_group(x_sample, PAST_LEN, sample_states, prm, norm_final_g)
    return (yp, ys, ret_p, ret_s, s5r_p, s5r_s, s5i_p, s5i_s, gla_p, gla_s, conv_p, conv_s)
```
